```python
import jax, jax.numpy as jnp
from jax import lax
import numpy as np

D_MODEL = 1024
BATCH = 2
SEQ = 8192
DEPTH = 2

HEAD_DIM = 64
GROUP_WIDTH = D_MODEL // 4
N_HEADS = GROUP_WIDTH // HEAD_DIM
D_MIX = 4 * GROUP_WIDTH
ROPE_THETA = 500000.0
ROPE_DIM = HEAD_DIM // 4
Q_CHUNK = 128
MOBA_BLOCK = 256
MOBA_TOPK = 3
NSA_KV_DIM = HEAD_DIM
CMP_BLOCK = 32
CMP_STRIDE = 16
CMP_HIDDEN = 128
SLC_BLOCK = 64
SLC_TOPN = 16
WIN = 512
FORCE_SCORE = 1e9
POOL_WINDOWS = (2, 4, 8, 16)
POOL_GROUPS = len(POOL_WINDOWS)
POOL_GDIM = GROUP_WIDTH // POOL_GROUPS
GLA_CHUNK = 64
GLA_LOWRANK = 16
GLA_TAU = 16.0
N_EXPERTS = 256
TOP_K = 8
N_EXPERT_GROUPS = 8
TOPK_GROUPS = 4
D_EXPERT = D_MODEL // 4
D_SHARED = D_EXPERT
ROUTED_SCALE = 2.5
MOE_BLOCK = 128
LN_EPS = 1e-5
DEEPNORM_ALPHA = (2 * DEPTH) ** 0.25
DEEPNORM_BETA = (8 * DEPTH) ** -0.25
IN_LAYOUT = (
    ('moba_q', GROUP_WIDTH), ('moba_k', GROUP_WIDTH), ('moba_v', GROUP_WIDTH),
    ('nsa_q', GROUP_WIDTH),
    ('nsa_k_cmp', NSA_KV_DIM), ('nsa_v_cmp', NSA_KV_DIM),
    ('nsa_k_slc', NSA_KV_DIM), ('nsa_v_slc', NSA_KV_DIM),
    ('nsa_k_win', NSA_KV_DIM), ('nsa_v_win', NSA_KV_DIM),
    ('nsa_gate', 3 * N_HEADS),
    ('pool', GROUP_WIDTH),
    ('gla_q', GROUP_WIDTH), ('gla_k', GROUP_WIDTH), ('gla_v', GROUP_WIDTH),
    ('gla_a', GLA_LOWRANK), ('gla_g', GROUP_WIDTH),
)
IN_COLS = sum(w for _, w in IN_LAYOUT)

kernel_name = 'hybrid_headgroup_moe_deepnorm_block'


def split_columns(z):
    out, start = {}, 0
    for name, width in IN_LAYOUT:
        out[name] = z[..., start:start + width]
        start += width
    return out


def to_heads(t):
    B, T, W = t.shape
    return t.reshape(B, T, W // HEAD_DIM, HEAD_DIM).transpose(0, 2, 1, 3)


def from_heads(t):
    B, H, T, d = t.shape
    return t.transpose(0, 2, 1, 3).reshape(B, T, H * d)


def layer_norm(x, g, b):
    xf = x.astype(jnp.float32)
    mu = jnp.mean(xf, -1, keepdims=True)
    var = jnp.mean(jnp.square(xf - mu), -1, keepdims=True)
    return ((xf - mu) * lax.rsqrt(var + LN_EPS) * g + b).astype(x.dtype)


def partial_rope(x, pos):
    half = ROPE_DIM // 2
    inv_freq = ROPE_THETA ** (-jnp.arange(half, dtype=jnp.float32) / half)
    ang = pos.astype(jnp.float32)[:, None] * inv_freq[None, :]
    cos, sin = jnp.cos(ang), jnp.sin(ang)
    xr = x[..., :ROPE_DIM].astype(jnp.float32)
    x1, x2 = xr[..., :half], xr[..., half:]
    rot = jnp.concatenate([x1 * cos - x2 * sin, x1 * sin + x2 * cos], axis=-1).astype(x.dtype)
    return jnp.concatenate([rot, x[..., ROPE_DIM:]], axis=-1)


def masked_softmax(s, mask):
    s = jnp.where(mask, s, -jnp.inf)
    m = jnp.max(s, -1, keepdims=True)
    m = jnp.where(jnp.isfinite(m), m, 0.0)
    e = jnp.where(mask, jnp.exp(s - m), 0.0)
    return e / jnp.maximum(jnp.sum(e, -1, keepdims=True), 1e-30)


def moba_attention(q, k, v):
    B, H, T, d = q.shape
    n_blk = -(-T // MOBA_BLOCK)
    pad = ((0, 0), (0, 0), (0, n_blk * MOBA_BLOCK - T), (0, 0))
    kb = jnp.pad(k, pad).reshape(B, H, n_blk, MOBA_BLOCK, d)
    vb = jnp.pad(v, pad).reshape(B, H, n_blk, MOBA_BLOCK, d)
    k_mean = jnp.mean(kb.astype(jnp.float32), axis=3)
    k_sel = min(MOBA_TOPK, n_blk)
    n_chunk = T // Q_CHUNK
    scale = d ** -0.5
    gather = jax.vmap(jax.vmap(lambda blocks, idx: blocks[idx]))
    q_chunks = q.reshape(B, H, n_chunk, Q_CHUNK, d).transpose(2, 0, 1, 3, 4)

    def step(args):
        ci, qc = args
        q_pos = ci * Q_CHUNK + jnp.arange(Q_CHUNK)
        own = (ci * Q_CHUNK) // MOBA_BLOCK
        gate = jnp.einsum('bhqd,bhnd->bhqn', qc.astype(jnp.float32), k_mean)
        gate = jnp.where(jnp.arange(n_blk) < own, gate, -jnp.inf)
        _, gidx = lax.top_k(gate, k_sel)
        sel_ok = gidx < own
        k_g = gather(kb, gidx)
        v_g = gather(vb, gidx)
        s_sel = jnp.einsum('bhqd,bhqkpd->bhqkp', qc, k_g).astype(jnp.float32) * scale
        s_sel = jnp.where(sel_ok[..., None], s_sel, -jnp.inf).reshape(B, H, Q_CHUNK, k_sel * MOBA_BLOCK)
        k_own = lax.dynamic_index_in_dim(kb, own, axis=2, keepdims=False)
        v_own = lax.dynamic_index_in_dim(vb, own, axis=2, keepdims=False)
        s_own = jnp.einsum('bhqd,bhpd->bhqp', qc, k_own).astype(jnp.float32) * scale
        k_pos = own * MOBA_BLOCK + jnp.arange(MOBA_BLOCK)
        s_own = jnp.where(k_pos[None, :] <= q_pos[:, None], s_own, -jnp.inf)
        p = jax.nn.softmax(jnp.concatenate([s_sel, s_own], axis=-1), axis=-1).astype(v.dtype)
        p_sel = p[..., :k_sel * MOBA_BLOCK].reshape(B, H, Q_CHUNK, k_sel, MOBA_BLOCK)
        p_own = p[..., k_sel * MOBA_BLOCK:]
        return (jnp.einsum('bhqkp,bhqkpd->bhqd', p_sel, v_g)
                + jnp.einsum('bhqp,bhpd->bhqd', p_own, v_own)).astype(v.dtype)

    out = lax.map(step, (jnp.arange(n_chunk), q_chunks))
    return out.transpose(1, 2, 0, 3, 4).reshape(B, H, T, d)


def compress_blocks(blocks, pe, w1, w2):
    B, n, L, d = blocks.shape
    hid = jax.nn.gelu((blocks + pe).reshape(B, n, L * d) @ w1)
    return hid @ w2


def nsa_attention(q, q_rot, k_cmp, v_cmp, k_slc, v_slc, k_win, v_win, gates, cmp_pe, cmp_w1, cmp_w2):
    B, H, T, d = q.shape
    scale = d ** -0.5
    n_cmp = (T - CMP_BLOCK) // CMP_STRIDE + 1
    cmp_start = jnp.arange(n_cmp) * CMP_STRIDE
    cmp_idx = cmp_start[:, None] + jnp.arange(CMP_BLOCK)[None, :]
    kc = compress_blocks(k_cmp[:, cmp_idx], cmp_pe[0], cmp_w1[0], cmp_w2[0])
    vc = compress_blocks(v_cmp[:, cmp_idx], cmp_pe[1], cmp_w1[1], cmp_w2[1])
    cmp_end = cmp_start + CMP_BLOCK - 1
    n_slc = T // SLC_BLOCK
    n_sel = min(SLC_TOPN, n_slc)
    slc_start = jnp.arange(n_slc) * SLC_BLOCK
    overlap = ((cmp_start[:, None] < slc_start[None, :] + SLC_BLOCK)
               & (cmp_start[:, None] + CMP_BLOCK > slc_start[None, :])).astype(jnp.float32)
    ks_blk = k_slc.reshape(B, n_slc, SLC_BLOCK, d)
    vs_blk = v_slc.reshape(B, n_slc, SLC_BLOCK, d)
    kw_pad = jnp.pad(k_win, ((0, 0), (WIN, 0), (0, 0)))
    vw_pad = jnp.pad(v_win, ((0, 0), (WIN, 0), (0, 0)))
    gather = jax.vmap(lambda blocks, idx: blocks[idx])
    n_chunk = T // Q_CHUNK

    def chunks(t):
        return t.reshape(B, H, n_chunk, Q_CHUNK, t.shape[-1]).transpose(2, 0, 1, 3, 4)

    def step(args):
        ci, qc, qrc, gc = args
        q_pos = ci * Q_CHUNK + jnp.arange(Q_CHUNK)
        s_c = jnp.einsum('bhqd,bnd->bhqn', qc, kc).astype(jnp.float32) * scale
        p_c = masked_softmax(s_c, cmp_end[None, :] <= q_pos[:, None])
        o_c = jnp.einsum('bhqn,bnd->bhqd', p_c.astype(vc.dtype), vc)
        imp = jnp.einsum('bhqn,nj->bqj', p_c, overlap)
        cur = q_pos // SLC_BLOCK
        j = jnp.arange(n_slc)
        forced = (j[None, :] == 0) | (j[None, :] == cur[:, None]) | (j[None, :] == cur[:, None] - 1)
        valid = slc_start[None, :] <= q_pos[:, None]
        imp = jnp.where(valid, jnp.where(forced, FORCE_SCORE, imp), -jnp.inf)
        _, sidx = lax.top_k(imp, n_sel)
        k_g = gather(ks_blk, sidx)
        v_g = gather(vs_blk, sidx)
        k_pos = sidx[..., None] * SLC_BLOCK + jnp.arange(SLC_BLOCK)
        s_s = jnp.einsum('bhqd,bqkpd->bhqkp', qrc, k_g).astype(jnp.float32) * scale
        s_s = jnp.where((k_pos <= q_pos[None, :, None, None])[:, None], s_s, -jnp.inf)
        p_s = jax.nn.softmax(s_s.reshape(B, H, Q_CHUNK, -1), axis=-1).reshape(s_s.shape)
        o_s = jnp.einsum('bhqkp,bqkpd->bhqd', p_s.astype(v_g.dtype), v_g)
        kw = lax.dynamic_slice_in_dim(kw_pad, ci * Q_CHUNK, Q_CHUNK + WIN, axis=1)
        vw = lax.dynamic_slice_in_dim(vw_pad, ci * Q_CHUNK, Q_CHUNK + WIN, axis=1)
        w_pos = ci * Q_CHUNK - WIN + jnp.arange(Q_CHUNK + WIN)
        w_mask = ((w_pos[None, :] <= q_pos[:, None]) & (w_pos[None, :] > q_pos[:, None] - WIN)
                  & (w_pos[None, :] >= 0))
        s_w = jnp.einsum('bhqd,bkd->bhqk', qrc, kw).astype(jnp.float32) * scale
        p_w = jax.nn.softmax(jnp.where(w_mask, s_w, -jnp.inf), axis=-1)
        o_w = jnp.einsum('bhqk,bkd->bhqd', p_w.astype(vw.dtype), vw)
        out = gc[..., 0:1] * o_c + gc[..., 1:2] * o_s + gc[..., 2:3] * o_w
        return out.astype(q.dtype)

    out = lax.map(step, (jnp.arange(n_chunk), chunks(q), chunks(q_rot), chunks(gates)))
    return out.transpose(1, 2, 0, 3, 4).reshape(B, H, T, d)


def pool_mixer(u, pool_w, pool_scale):
    B, T, _ = u.shape
    ug = u.astype(jnp.float32).reshape(B, T, POOL_GROUPS, POOL_GDIM)
    cs = jnp.cumsum(ug, axis=1)
    t1 = jnp.arange(1, T + 1, dtype=jnp.float32)
    pooled = []
    for g, w in enumerate(POOL_WINDOWS):
        c_g = cs[:, :, g]
        lagged = jnp.pad(c_g, ((0, 0), (w, 0), (0, 0)))[:, :T]
        count = jnp.minimum(t1, float(w))[None, :, None]
        pooled.append((c_g - lagged) / count)
    pooled = jnp.stack(pooled, axis=2) - ug
    mixed = jnp.einsum('btgc,gcd->btgd', pooled, pool_w.astype(jnp.float32))
    return (mixed.reshape(B, T, GROUP_WIDTH) * pool_scale).astype(u.dtype)


def gla_chunked(q, k, v, log_a):
    B, H, T, dk = q.shape
    dv = v.shape[-1]
    n = T // GLA_CHUNK

    def chunks(t):
        return t.reshape(B, H, n, GLA_CHUNK, t.shape[-1]).transpose(2, 0, 1, 3, 4)

    causal = jnp.tril(jnp.ones((GLA_CHUNK, GLA_CHUNK), bool))[:, :, None]

    def step(S, inp):
        qc, kc, vc, ac = inp
        b = jnp.cumsum(ac, axis=2)
        diff = b[:, :, :, None, :] - b[:, :, None, :, :]
        decay = jnp.where(causal, jnp.exp(jnp.where(causal, diff, 0.0)), 0.0)
        attn = jnp.einsum('bhid,bhjd,bhijd->bhij', qc, kc, decay)
        o = jnp.einsum('bhij,bhjv->bhiv', attn, vc) + jnp.einsum('bhid,bhdv->bhiv', qc * jnp.exp(b), S)
        b_end = b[:, :, -1, :]
        S = jnp.exp(b_end)[..., None] * S + jnp.einsum('bhjd,bhjv->bhdv', kc * jnp.exp(b_end[:, :, None, :] - b), vc)
        return S, o

    S0 = jnp.zeros((B, H, dk, dv), jnp.float32)
    _, o = lax.scan(step, S0, (chunks(q * dk ** -0.5), chunks(k), chunks(v), chunks(log_a)))
    return o.transpose(1, 2, 0, 3, 4).reshape(B, H, T, dv)


def token_mixers(h, w_in, cmp_pe, cmp_w1, cmp_w2, pool_w, pool_scale, gla_wa, gla_ba, gla_norm, w_out):
    B, T, _ = h.shape
    f32 = jnp.float32
    pos = jnp.arange(T)
    p = split_columns(h @ w_in)
    qa = partial_rope(to_heads(p['moba_q']), pos)
    ka = partial_rope(to_heads(p['moba_k']), pos)
    o_a = from_heads(moba_attention(qa, ka, to_heads(p['moba_v'])))
    qb = to_heads(p['nsa_q'])
    gates = jax.nn.sigmoid(p['nsa_gate'].astype(f32)).reshape(B, T, N_HEADS, 3).transpose(0, 2, 1, 3)
    o_b = from_heads(nsa_attention(qb, partial_rope(qb, pos), p['nsa_k_cmp'], p['nsa_v_cmp'],
                                   partial_rope(p['nsa_k_slc'], pos), p['nsa_v_slc'],
                                   partial_rope(p['nsa_k_win'], pos), p['nsa_v_win'],
                                   gates, cmp_pe, cmp_w1, cmp_w2))
    o_c = pool_mixer(p['pool'], pool_w, pool_scale)
    log_a = jax.nn.log_sigmoid((p['gla_a'] @ gla_wa + gla_ba).astype(f32)) / GLA_TAU
    o = gla_chunked(to_heads(p['gla_q']).astype(f32), to_heads(p['gla_k']).astype(f32),
                    to_heads(p['gla_v']).astype(f32), to_heads(log_a))
    o = (o * lax.rsqrt(jnp.mean(o * o, -1, keepdims=True) + LN_EPS) * gla_norm.astype(f32)
         * jax.nn.silu(to_heads(p['gla_g']).astype(f32)))
    o_d = from_heads(o).astype(h.dtype)
    return jnp.concatenate([o_a, o_b, o_c, o_d], axis=-1) @ w_out


def moe_ffn(h, w_router, router_bias, exp_gate, exp_up, exp_down, sh_gate, sh_up, sh_down):
    B, T, D = h.shape
    n_tok = B * T
    xt = h.reshape(n_tok, D)
    s = jax.nn.sigmoid(xt.astype(jnp.float32) @ w_router.astype(jnp.float32))
    s_sel = s + router_bias.astype(jnp.float32)
    grp = s_sel.reshape(n_tok, N_EXPERT_GROUPS, N_EXPERTS // N_EXPERT_GROUPS)
    grp_score = jnp.sum(lax.top_k(grp, 2)[0], axis=-1)
    _, gidx = lax.top_k(grp_score, TOPK_GROUPS)
    gmask = jnp.any(gidx[:, :, None] == jnp.arange(N_EXPERT_GROUPS)[None, None, :], axis=1)
    emask = jnp.repeat(gmask, N_EXPERTS // N_EXPERT_GROUPS, axis=1)
    _, eidx = lax.top_k(jnp.where(emask, s_sel, -jnp.inf), TOP_K)
    wts = jnp.take_along_axis(s, eidx, axis=-1)
    wts = wts / jnp.sum(wts, -1, keepdims=True) * ROUTED_SCALE
    n_asg = n_tok * TOP_K
    flat_e = eidx.reshape(n_asg)
    order = jnp.argsort(flat_e)
    e_sorted = flat_e[order]
    tok_sorted = (order // TOP_K).astype(jnp.int32)
    w_sorted = wts.reshape(n_asg)[order]
    counts = jnp.bincount(flat_e, length=N_EXPERTS)
    starts = jnp.cumsum(counts) - counts
    padded = (counts + MOE_BLOCK - 1) // MOE_BLOCK * MOE_BLOCK
    pstarts = jnp.cumsum(padded) - padded
    dest = pstarts[e_sorted] + jnp.arange(n_asg) - starts[e_sorted]
    n_blocks = -(-n_asg // MOE_BLOCK) + N_EXPERTS
    row_tok = jnp.zeros((n_blocks * MOE_BLOCK,), jnp.int32).at[dest].set(tok_sorted)
    row_w = jnp.zeros((n_blocks * MOE_BLOCK,), jnp.float32).at[dest].set(w_sorted)
    blk_e = jnp.clip(jnp.searchsorted(pstarts, jnp.arange(n_blocks) * MOE_BLOCK, side='right') - 1,
                     0, N_EXPERTS - 1)

    def expert_block(acc, inp):
        e, toks, w = inp
        xb = xt[toks]
        yb = (jax.nn.silu(xb @ exp_gate[e]) * (xb @ exp_up[e])) @ exp_down[e]
        return acc.at[toks].add(yb.astype(jnp.float32) * w[:, None]), None

    routed, _ = lax.scan(expert_block, jnp.zeros((n_tok, D), jnp.float32),
                         (blk_e, row_tok.reshape(n_blocks, MOE_BLOCK), row_w.reshape(n_blocks, MOE_BLOCK)))
    shared = (jax.nn.silu(xt @ sh_gate) * (xt @ sh_up)) @ sh_down
    return (shared.astype(jnp.float32) + routed).astype(h.dtype).reshape(B, T, D)


def setup_inputs(seed: int = 0) -> dict:
    key = jax.random.key(seed)
    ks = jax.random.split(key, 24)
    L = DEPTH

    def nrm(k, shape, scale):
        return jax.random.normal(k, shape, jnp.float32) * scale

    return {
        'x': nrm(ks[0], (BATCH, SEQ, D_MODEL), 1.0),
        'c': nrm(ks[1], (BATCH, D_MODEL), 1.0),
        'w_ada': nrm(ks[2], (L, D_MODEL, 6 * D_MODEL), 0.5 * D_MODEL ** -0.5),
        'b_ada': nrm(ks[3], (L, 6 * D_MODEL), 0.02),
        'w_in': nrm(ks[4], (L, D_MODEL, IN_COLS), D_MODEL ** -0.5),
        'cmp_pe': nrm(ks[5], (L, 2, CMP_BLOCK, HEAD_DIM), 0.1),
        'cmp_w1': nrm(ks[6], (L, 2, CMP_BLOCK * HEAD_DIM, CMP_HIDDEN), (CMP_BLOCK * HEAD_DIM) ** -0.5),
        'cmp_w2': nrm(ks[7], (L, 2, CMP_HIDDEN, HEAD_DIM), CMP_HIDDEN ** -0.5),
        'pool_w': nrm(ks[8], (L, POOL_GROUPS, POOL_GDIM, POOL_GDIM), POOL_GDIM ** -0.5),
        'pool_scale': 1.0 + nrm(ks[9], (L, GROUP_WIDTH), 0.1),
        'gla_wa': nrm(ks[10], (L, GLA_LOWRANK, GROUP_WIDTH), GLA_LOWRANK ** -0.5),
        'gla_ba': nrm(ks[11], (L, GROUP_WIDTH), 0.1),
        'gla_norm': 1.0 + nrm(ks[12], (L, HEAD_DIM), 0.1),
        'w_out': nrm(ks[13], (L, D_MIX, D_MODEL), D_MIX ** -0.5 * DEEPNORM_BETA),
        'ln_g': 1.0 + nrm(ks[14], (L, 2, D_MODEL), 0.1),
        'ln_b': nrm(ks[15], (L, 2, D_MODEL), 0.02),
        'w_router': nrm(ks[16], (L, D_MODEL, N_EXPERTS), D_MODEL ** -0.5),
        'router_bias': nrm(ks[17], (L, N_EXPERTS), 0.01),
        'exp_gate': nrm(ks[18], (L, N_EXPERTS, D_MODEL, D_EXPERT), D_MODEL ** -0.5),
        'exp_up': nrm(ks[19], (L, N_EXPERTS, D_MODEL, D_EXPERT), D_MODEL ** -0.5),
        'exp_down': nrm(ks[20], (L, N_EXPERTS, D_EXPERT, D_MODEL), D_EXPERT ** -0.5 * DEEPNORM_BETA),
        'sh_gate': nrm(ks[21], (L, D_MODEL, D_SHARED), D_MODEL ** -0.5),
        'sh_up': nrm(ks[22], (L, D_MODEL, D_SHARED), D_MODEL ** -0.5),
        'sh_down': nrm(ks[23], (L, D_SHARED, D_MODEL), D_SHARED ** -0.5 * DEEPNORM_BETA),
    }


def reference(x, c, w_ada, b_ada, w_in, cmp_pe, cmp_w1, cmp_w2, pool_w, pool_scale, gla_wa, gla_ba, gla_norm,
              w_out, ln_g, ln_b, w_router, router_bias, exp_gate, exp_up, exp_down, sh_gate, sh_up, sh_down):
    for l in range(DEPTH):
        mod = jax.nn.silu(c) @ w_ada[l] + b_ada[l]
        sh_m, sc_m, g_m, sh_f, sc_f, g_f = [m[:, None, :] for m in jnp.split(mod, 6, axis=-1)]
        h = x * (1.0 + sc_m) + sh_m
        y = token_mixers(h, w_in[l], cmp_pe[l], cmp_w1[l], cmp_w2[l], pool_w[l], pool_scale[l],
                         gla_wa[l], gla_ba[l], gla_norm[l], w_out[l])
        x = layer_norm(DEEPNORM_ALPHA * x + g_m * y, ln_g[l, 0], ln_b[l, 0])
        h = x * (1.0 + sc_f) + sh_f
        y = moe_ffn(h, w_router[l], router_bias[l], exp_gate[l], exp_up[l], exp_down[l],
                    sh_gate[l], sh_up[l], sh_down[l])
        x = layer_norm(DEEPNORM_ALPHA * x + g_f * y, ln_g[l, 1], ln_b[l, 1])
    return x
```

```python
import functools

import jax
import jax.numpy as jnp
from jax import lax
from jax.experimental import pallas as pl
from jax.experimental.pallas import tpu as pltpu

f32, bf16, i32 = jnp.float32, jnp.bfloat16, jnp.int32

D_MODEL = 1024
HEAD_DIM = 64
N_HEADS = 4
GROUP_WIDTH = 256
ROPE_THETA = 500000.0
ROPE_DIM = 16
MOBA_BLOCK = 256
MOBA_TOPK = 3
CMP_BLOCK = 32
CMP_STRIDE = 16
CMP_HIDDEN = 128
SLC_BLOCK = 64
SLC_TOPN = 16
WIN = 512
FORCE_SCORE = 1e9
POOL_WINDOWS = (2, 4, 8, 16)
GLA_SUB = 16
GLA_LOWRANK = 16
GLA_TAU = 16.0
N_EXPERTS = 256
TOP_K = 8
N_EXPERT_GROUPS = 8
TOPK_GROUPS = 4
D_EXPERT = 256
ROUTED_SCALE = 2.5
LN_EPS = 1e-5
QK_SCALE = HEAD_DIM ** -0.5

LANES = 128
MASKED = -1e30
M_INIT = -3e38
BIG_IDX = 1 << 20
MOE_ROWS = 256
VMEM_LIMIT = 56 * 1024 * 1024

_OFF = dict(moba_q=0, moba_k=256, moba_v=512, nsa_q=768, k_cmp=1024, v_cmp=1088, k_slc=1152, v_slc=1216,
            k_win=1280, v_win=1344, nsa_gate=1408, pool=1420, gla_q=1676, gla_k=1932, gla_v=2188,
            gla_a=2444, gla_g=2460)
_S = dict(mq=0, mk=256, mv=512, nq=768, sw=1024, vsw=1152, kvc=1280, pool=1408, gq=1664, gk=1920, gv=2176,
          gg=2432, misc=2688)
IN_COLS_PACKED = 2816
MISC_GATE0 = 0
MISC_A0 = 12

NT = (((1,), (1,)), ((), ()))
TN = (((0,), (0,)), ((), ()))


def _cparams(*sem):
    return pltpu.CompilerParams(dimension_semantics=sem, vmem_limit_bytes=VMEM_LIMIT)


def _silu(x):
    return x * jax.nn.sigmoid(x)


def _split_bf16(x):
    hi = x.astype(bf16)
    lo = (x - hi.astype(f32)).astype(bf16)
    return hi, lo


def _layer_norm(z, g, b):
    mu = jnp.mean(z, axis=-1, keepdims=True)
    zc = z - mu
    var = jnp.mean(zc * zc, axis=-1, keepdims=True)
    return zc * lax.rsqrt(var + LN_EPS) * g + b


def _argmax_rounds(score, lane, rounds):
    picked = jnp.zeros(score.shape, f32)
    for _ in range(rounds):
        mx = jnp.max(score, axis=1, keepdims=True)
        idx = jnp.min(jnp.where(score == mx, lane, BIG_IDX), axis=1, keepdims=True)
        hit = lane == idx
        picked = jnp.where(hit, 1.0, picked)
        score = jnp.where(hit, -jnp.inf, score)
    return picked


def _softmax_step(s, m, l, acc, vb):
    m_new = jnp.maximum(m, jnp.max(s, axis=1, keepdims=True))
    alpha = jnp.exp(m - m_new)
    p = jnp.exp(s - m_new)
    l = alpha * l + jnp.sum(p, axis=1, keepdims=True)
    acc = alpha * acc + jnp.dot(p.astype(bf16), vb, preferred_element_type=f32)
    return m_new, l, acc


def _ada_kernel(c_ref, w_ref, b_ref, o_ref):
    o_ref[...] = jnp.dot(_silu(c_ref[...]), w_ref[...], preferred_element_type=f32,
                         precision=lax.Precision.HIGHEST) + b_ref[...]


def _ada(c8, w, b):
    n = w.shape[1] // D_MODEL
    return pl.pallas_call(
        _ada_kernel, grid=(n,),
        in_specs=[pl.BlockSpec((8, D_MODEL), lambda j: (0, 0)),
                  pl.BlockSpec((D_MODEL, D_MODEL), lambda j: (0, j)),
                  pl.BlockSpec((1, D_MODEL), lambda j: (0, j))],
        out_specs=pl.BlockSpec((8, D_MODEL), lambda j: (0, j)),
        out_shape=jax.ShapeDtypeStruct((8, w.shape[1]), f32),
        compiler_params=_cparams("arbitrary"), name="ada")(c8, w, b)


IN_TM = MOBA_BLOCK


def _inproj_kernel(x_ref, mod_ref, w_ref, ct_ref, st_ref,
                   mq_ref, mk_ref, mv_ref, km_ref, nq_ref, nqr_ref, ks_ref, kw_ref, vs_ref, vw_ref,
                   kvc_ref, pool_ref, gq_ref, gk_ref, gv_ref, gg_ref, misc_ref, *, nt):
    tb = pl.program_id(0) % nt
    h = (x_ref[...] * (1.0 + mod_ref[0, 1:2, :]) + mod_ref[0, 0:1, :]).astype(bf16)

    def seg(name, width):
        a = _S[name]
        return jnp.dot(h, w_ref[:, a:a + width], preferred_element_type=f32)

    ct, st = ct_ref[...], st_ref[...]
    lane = lax.broadcasted_iota(i32, (IN_TM, LANES), 1)
    first8 = (lane % HEAD_DIM) < ROPE_DIM // 2
    half = lane < HEAD_DIM

    def rope128(y):
        partner = jnp.where(first8, pltpu.roll(y, LANES - 8, axis=1), pltpu.roll(y, 8, axis=1))
        return y * ct + partner * st

    def rope(y):
        return jnp.concatenate([rope128(y[:, c * LANES:(c + 1) * LANES]) for c in range(y.shape[1] // LANES)],
                               axis=1)

    def lo_half(y):
        return jnp.where(half, y, 0.0)

    def hi_half(y):
        return jnp.where(half, pltpu.roll(y, HEAD_DIM, axis=1), 0.0)

    def per_head(y):
        parts = []
        for c in range(2):
            yc = y[:, c * LANES:(c + 1) * LANES]
            parts += [lo_half(yc), hi_half(yc)]
        return jnp.concatenate(parts, axis=1)

    mq_ref[...] = per_head(rope(seg("mq", 256)) * QK_SCALE).astype(bf16)
    k = rope(seg("mk", 256))
    km_ref[0] = jnp.mean(k, axis=0, keepdims=True)
    lane4 = lax.broadcasted_iota(i32, (IN_TM, 4 * LANES), 1)
    mk_ref[...] = jnp.where((lane4 % LANES) == HEAD_DIM + tb, 1.0, per_head(k)).astype(bf16)
    mv_ref[...] = per_head(seg("mv", 256)).astype(bf16)
    q = seg("nq", 256) * QK_SCALE
    nq_ref[...] = per_head(q).astype(bf16)
    nqr_ref[...] = per_head(rope(q)).astype(bf16)
    sw = rope128(seg("sw", 128))
    row = lax.broadcasted_iota(i32, (IN_TM, LANES), 0)
    slc_id = tb * (IN_TM // SLC_BLOCK) + row // SLC_BLOCK
    ks_ref[...] = jnp.concatenate([lo_half(sw), jnp.where(lane == slc_id, 1.0, 0.0)], axis=1).astype(bf16)
    kw_ref[...] = hi_half(sw).astype(bf16)
    vsw = seg("vsw", 128)
    vs_ref[...] = lo_half(vsw).astype(bf16)
    vw_ref[...] = hi_half(vsw).astype(bf16)
    kvc_ref[...] = seg("kvc", 128).astype(bf16)
    pool_ref[...] = seg("pool", 256)
    gq_ref[...] = seg("gq", 256)
    gk_ref[...] = seg("gk", 256)
    gv_ref[...] = seg("gv", 256)
    gg_ref[...] = seg("gg", 256)
    misc_ref[...] = seg("misc", 128)


def _inproj(x2, mod, w, ct, st, B, T):
    N = B * T
    nt = T // IN_TM
    assert T % IN_TM == 0 and nt <= 32 and T // SLC_BLOCK <= LANES
    row = lambda w_, dt: (jax.ShapeDtypeStruct((N, w_), dt), pl.BlockSpec((IN_TM, w_), lambda i: (i, 0)))
    outs = [row(512, bf16), row(512, bf16), row(512, bf16),
            (jax.ShapeDtypeStruct((N // IN_TM, 1, 256), f32), pl.BlockSpec((1, 1, 256), lambda i: (i, 0, 0))),
            row(512, bf16), row(512, bf16), row(256, bf16), row(128, bf16), row(128, bf16), row(128, bf16),
            row(128, bf16), row(256, f32), row(256, f32), row(256, f32), row(256, f32), row(256, f32),
            row(128, f32)]
    return pl.pallas_call(
        functools.partial(_inproj_kernel, nt=nt), grid=(N // IN_TM,),
        in_specs=[pl.BlockSpec((IN_TM, D_MODEL), lambda i: (i, 0)),
                  pl.BlockSpec((1, 6, D_MODEL), lambda i: (i // nt, 0, 0)),
                  pl.BlockSpec((D_MODEL, IN_COLS_PACKED), lambda i: (0, 0), pipeline_mode=pl.Buffered(1)),
                  pl.BlockSpec((IN_TM, LANES), lambda i: (i % nt, 0)),
                  pl.BlockSpec((IN_TM, LANES), lambda i: (i % nt, 0))],
        out_specs=[o[1] for o in outs], out_shape=[o[0] for o in outs],
        compiler_params=_cparams("arbitrary"), name="inproj")(x2, mod, w, ct, st)


def _cmp_kernel(x_ref, wa_ref, wb_ref, pea_ref, peb_ref, w2k_ref, w2v_ref, kc_ref, vc_ref):
    x = x_ref[0].astype(f32)
    a = jnp.dot((x + pea_ref[...]).astype(bf16), wa_ref[...], preferred_element_type=f32)
    b = jnp.dot((x + peb_ref[...]).astype(bf16), wb_ref[...], preferred_element_type=f32)
    hid = a + pltpu.roll(b, x.shape[0] - 1, axis=0)
    g = jax.nn.gelu(hid).astype(bf16)
    kc_ref[0] = jnp.dot(g, w2k_ref[...], preferred_element_type=f32).astype(bf16)
    vc_ref[0] = jnp.dot(g, w2v_ref[...], preferred_element_type=f32).astype(bf16)


def _cmp(xseg, wa, wb, pea, peb, w2k, w2v):
    B, ns, wd = xseg.shape
    full = lambda a: pl.BlockSpec(a.shape, lambda b: (0,) * a.ndim)
    return pl.pallas_call(
        _cmp_kernel, grid=(B,),
        in_specs=[pl.BlockSpec((1, ns, wd), lambda b: (b, 0, 0))] + [full(a) for a in (wa, wb, pea, peb, w2k, w2v)],
        out_specs=[pl.BlockSpec((1, ns, LANES), lambda b: (b, 0, 0))] * 2,
        out_shape=[jax.ShapeDtypeStruct((B, ns, LANES), bf16)] * 2,
        compiler_params=_cparams("arbitrary"), name="nsa_compress")(xseg, wa, wb, pea, peb, w2k, w2v)


def _moba_kernel(q_ref, k_ref, v_ref, kmf_ref, o_ref):
    own = pl.program_id(2)
    q = q_ref[...]
    hi, lo = _split_bf16(kmf_ref[0, 0])
    gate = (lax.dot_general(q, hi, NT, preferred_element_type=f32)
            + lax.dot_general(q, lo, NT, preferred_element_type=f32))
    lane = lax.broadcasted_iota(i32, gate.shape, 1)
    blk = lane - HEAD_DIM
    past = (blk >= 0) & (blk < own)
    picked = _argmax_rounds(jnp.where(past, gate, -jnp.inf), lane, MOBA_TOPK)
    allowed = ((picked > 0.0) & past) | (blk == own)
    is_blk_lane = (blk >= 0) & (blk < 32)
    qf = q + jnp.where(is_blk_lane & jnp.logical_not(allowed), MASKED, 0.0).astype(bf16)

    def body(j, carry):
        off = pl.multiple_of(j * MOBA_BLOCK, MOBA_BLOCK)
        s = lax.dot_general(qf, k_ref[pl.ds(off, MOBA_BLOCK), :], NT, preferred_element_type=f32)
        return _softmax_step(s, *carry, v_ref[pl.ds(off, MOBA_BLOCK), :])

    init = (jnp.full((MOBA_BLOCK, 1), M_INIT, f32), jnp.zeros((MOBA_BLOCK, 1), f32),
            jnp.zeros((MOBA_BLOCK, LANES), f32))
    carry = lax.fori_loop(0, own, body, init)
    off = pl.multiple_of(own * MOBA_BLOCK, MOBA_BLOCK)
    s = lax.dot_general(qf, k_ref[pl.ds(off, MOBA_BLOCK), :], NT, preferred_element_type=f32)
    r = lax.broadcasted_iota(i32, s.shape, 0)
    c = lax.broadcasted_iota(i32, s.shape, 1)
    _, l, acc = _softmax_step(jnp.where(c <= r, s, MASKED), *carry, v_ref[pl.ds(off, MOBA_BLOCK), :])
    o_ref[...] = (acc / l).astype(bf16)


def _moba(mq, mk, mv, kmf, B, T):
    N = B * T
    nt = T // MOBA_BLOCK
    return pl.pallas_call(
        _moba_kernel, grid=(B, N_HEADS, nt),
        in_specs=[pl.BlockSpec((MOBA_BLOCK, LANES), lambda b, h, i: (b * nt + i, h)),
                  pl.BlockSpec((T, LANES), lambda b, h, i: (b, h)),
                  pl.BlockSpec((T, LANES), lambda b, h, i: (b, h)),
                  pl.BlockSpec((1, 1, LANES, LANES), lambda b, h, i: (b, h, 0, 0))],
        out_specs=pl.BlockSpec((MOBA_BLOCK, LANES), lambda b, h, i: (b * nt + i, h)),
        out_shape=jax.ShapeDtypeStruct((N, N_HEADS * LANES), bf16),
        compiler_params=_cparams("arbitrary", "arbitrary", "arbitrary"), name="moba")(mq, mk, mv, kmf)


NSA_TQ = 128
NSA_KB = 256


def _nsa_kernel(nq_ref, nqr_ref, misc_ref, ks_ref, vs_ref, kw_ref, vw_ref, kc_ref, vc_ref, ov_ref, o_ref, *,
                n_cmp):
    c = pl.program_id(1)
    rows = N_HEADS * NSA_TQ
    q4 = jnp.concatenate([nq_ref[:, h * LANES:(h + 1) * LANES] for h in range(N_HEADS)], axis=0)
    q4r = jnp.concatenate([nqr_ref[:, h * LANES:(h + 1) * LANES] for h in range(N_HEADS)], axis=0)
    qpos4 = c * NSA_TQ + lax.broadcasted_iota(i32, (rows, 1), 0) % NSA_TQ

    kc = kc_ref[0]
    s = lax.dot_general(q4, kc, NT, preferred_element_type=f32)
    n = lax.broadcasted_iota(i32, s.shape, 1)
    ok = (n * CMP_STRIDE + (CMP_BLOCK - 1) <= qpos4) & (n < n_cmp)
    s = jnp.where(ok, s, -jnp.inf)
    m = jnp.max(s, axis=1, keepdims=True)
    m = jnp.where(m > -jnp.inf, m, 0.0)
    e = jnp.where(ok, jnp.exp(s - m), 0.0)
    p_c = e / jnp.maximum(jnp.sum(e, axis=1, keepdims=True), 1e-30)
    o_c = jnp.dot(p_c.astype(bf16), vc_ref[0], preferred_element_type=f32)

    p_sum = (p_c[0:NSA_TQ] + p_c[NSA_TQ:2 * NSA_TQ]) + (p_c[2 * NSA_TQ:3 * NSA_TQ] + p_c[3 * NSA_TQ:])
    hi, lo = _split_bf16(p_sum)
    imp = jnp.dot(hi, ov_ref[...], preferred_element_type=f32) + jnp.dot(lo, ov_ref[...], preferred_element_type=f32)
    j = lax.broadcasted_iota(i32, imp.shape, 1)
    qpos = c * NSA_TQ + lax.broadcasted_iota(i32, (NSA_TQ, 1), 0)
    cur = qpos // SLC_BLOCK
    forced = (j == 0) | (j == cur) | (j == cur - 1)
    valid = j <= cur
    score = jnp.where(valid, jnp.where(forced, FORCE_SCORE, imp), -jnp.inf)
    chosen = (_argmax_rounds(score, j, SLC_TOPN) > 0.0) & valid
    selb = jnp.where(chosen, 0.0, MASKED).astype(bf16)
    lhs = jnp.concatenate([jnp.concatenate([q4r[h * NSA_TQ:(h + 1) * NSA_TQ], selb], axis=1)
                           for h in range(N_HEADS)], axis=0)

    def body(jb, carry):
        off = pl.multiple_of(jb * NSA_KB, NSA_KB)
        s_ = lax.dot_general(lhs, ks_ref[pl.ds(off, NSA_KB), :], NT, preferred_element_type=f32)
        return _softmax_step(s_, *carry, vs_ref[pl.ds(off, NSA_KB), :])

    init = (jnp.full((rows, 1), M_INIT, f32), jnp.zeros((rows, 1), f32), jnp.zeros((rows, LANES), f32))
    diag = (c * NSA_TQ) // NSA_KB
    carry = lax.fori_loop(0, diag, body, init)
    off = pl.multiple_of(diag * NSA_KB, NSA_KB)
    s_ = lax.dot_general(lhs, ks_ref[pl.ds(off, NSA_KB), :], NT, preferred_element_type=f32)
    kpos = off + lax.broadcasted_iota(i32, s_.shape, 1)
    _, l, acc = _softmax_step(jnp.where(kpos <= qpos4, s_, MASKED), *carry, vs_ref[pl.ds(off, NSA_KB), :])
    o_s = acc / l

    wlen = NSA_TQ + WIN
    start = pl.multiple_of(jnp.maximum(c * NSA_TQ - WIN, 0), NSA_TQ)
    s_w = lax.dot_general(q4r, kw_ref[pl.ds(start, wlen), :], NT, preferred_element_type=f32)
    wpos = start + lax.broadcasted_iota(i32, s_w.shape, 1)
    s_w = jnp.where((wpos <= qpos4) & (wpos > qpos4 - WIN), s_w, -jnp.inf)
    e_w = jnp.exp(s_w - jnp.max(s_w, axis=1, keepdims=True))
    p_w = e_w / jnp.sum(e_w, axis=1, keepdims=True)
    o_w = jnp.dot(p_w.astype(bf16), vw_ref[pl.ds(start, wlen), :], preferred_element_type=f32)

    gates = jax.nn.sigmoid(misc_ref[...])
    gl = lax.broadcasted_iota(i32, gates.shape, 1)

    def gate_col(g):
        return jnp.concatenate([jnp.sum(jnp.where(gl == MISC_GATE0 + 3 * h + g, gates, 0.0), axis=1, keepdims=True)
                                for h in range(N_HEADS)], axis=0)

    out = gate_col(0) * o_c + gate_col(1) * o_s + gate_col(2) * o_w
    pair = lambda a, b: a + pltpu.roll(b, HEAD_DIM, axis=1)
    o_ref[...] = jnp.concatenate([pair(out[0:NSA_TQ], out[NSA_TQ:2 * NSA_TQ]),
                                  pair(out[2 * NSA_TQ:3 * NSA_TQ], out[3 * NSA_TQ:])], axis=1).astype(bf16)


def _nsa(nq, nqr, misc, ks, vs, kw, vw, kc, vc, ov, B, T):
    N = B * T
    nc = T // NSA_TQ
    ns = kc.shape[1]
    assert T >= NSA_TQ + WIN and T % NSA_KB == 0
    tok = lambda w_: pl.BlockSpec((NSA_TQ, w_), lambda b, c: (b * nc + c, 0))
    seq = lambda w_: pl.BlockSpec((T, w_), lambda b, c: (b, 0))
    return pl.pallas_call(
        functools.partial(_nsa_kernel, n_cmp=(T - CMP_BLOCK) // CMP_STRIDE + 1), grid=(B, nc),
        in_specs=[tok(512), tok(512), tok(128), seq(256), seq(128), seq(128), seq(128),
                  pl.BlockSpec((1, ns, LANES), lambda b, c: (b, 0, 0)),
                  pl.BlockSpec((1, ns, LANES), lambda b, c: (b, 0, 0)),
                  pl.BlockSpec(ov.shape, lambda b, c: (0, 0))],
        out_specs=tok(256), out_shape=jax.ShapeDtypeStruct((N, 256), bf16),
        compiler_params=_cparams("arbitrary", "arbitrary"), name="nsa")(nq, nqr, misc, ks, vs, kw, vw, kc, vc, ov)


POOL_TM = 512
POOL_HALO = 16


def _pool_kernel(u_ref, halo_ref, w_ref, sc_ref, o_ref):
    t = pl.program_id(1)
    halo = jnp.where(t == 0, 0.0, halo_ref[...])
    ext = jnp.concatenate([halo, u_ref[...]], axis=0)
    s2 = ext + pltpu.roll(ext, 1, axis=0)
    s4 = s2 + pltpu.roll(s2, 2, axis=0)
    s8 = s4 + pltpu.roll(s4, 4, axis=0)
    s16 = s8 + pltpu.roll(s8, 8, axis=0)
    pos1 = jnp.maximum(t * POOL_TM - POOL_HALO + 1 + lax.broadcasted_iota(i32, ext.shape, 0), 1).astype(f32)
    grp = lax.broadcasted_iota(i32, ext.shape, 1) // HEAD_DIM
    mean = jnp.where(grp == 0, s2 / jnp.minimum(pos1, 2.0),
                     jnp.where(grp == 1, s4 / jnp.minimum(pos1, 4.0),
                               jnp.where(grp == 2, s8 / jnp.minimum(pos1, 8.0), s16 / jnp.minimum(pos1, 16.0))))
    pooled = (mean - ext)[POOL_HALO:, :]
    o_ref[...] = (jnp.dot(pooled.astype(bf16), w_ref[...], preferred_element_type=f32) * sc_ref[...]).astype(bf16)


def _pool(u, wbd, scale, B, T):
    N = B * T
    tm = min(POOL_TM, T)
    assert tm == POOL_TM and T % POOL_TM == 0
    nt = T // tm
    hb = tm // POOL_HALO
    return pl.pallas_call(
        _pool_kernel, grid=(B, nt),
        in_specs=[pl.BlockSpec((tm, 256), lambda b, t: (b * nt + t, 0)),
                  pl.BlockSpec((POOL_HALO, 256), lambda b, t: (jnp.maximum((b * nt + t) * hb - 1, 0), 0)),
                  pl.BlockSpec((256, 256), lambda b, t: (0, 0)),
                  pl.BlockSpec((1, 256), lambda b, t: (0, 0))],
        out_specs=pl.BlockSpec((tm, 256), lambda b, t: (b * nt + t, 0)),
        out_shape=jax.ShapeDtypeStruct((N, 256), bf16),
        compiler_params=_cparams("arbitrary", "arbitrary"), name="pool")(u, u, wbd, scale)


GLA_TM = 256


def _gla_kernel(q_ref, k_ref, v_ref, g_ref, misc_ref, wa_ref, ba_ref, gn_ref, bd_ref, o_ref,
                st_ref, q_s, k_s, v_s, b_s, qe_s, ke_s, gam_s, o_s):
    @pl.when(pl.program_id(1) == 0)
    def _():
        st_ref[...] = jnp.zeros_like(st_ref)

    x = jnp.dot(misc_ref[...], wa_ref[...], preferred_element_type=f32, precision=lax.Precision.HIGHEST) + ba_ref[...]
    log_a = (jnp.minimum(x, 0.0) - jnp.log1p(jnp.exp(-jnp.abs(x)))) / GLA_TAU
    r16 = lax.broadcasted_iota(i32, log_a.shape, 0) % GLA_SUB
    b = log_a
    for s in (1, 2, 4, 8):
        b = b + jnp.where(r16 >= s, pltpu.roll(b, s, axis=0), 0.0)
    b_end = jnp.where(r16 == GLA_SUB - 1, b, 0.0)
    for s in (1, 2, 4, 8):
        b_end = b_end + pltpu.roll(b_end, GLA_TM - s, axis=0)
    q = q_ref[...] * QK_SCALE
    k = k_ref[...]
    q_s[...] = q
    k_s[...] = k
    v_s[...] = v_ref[...]
    b_s[...] = b
    qe_s[...] = (q * jnp.exp(b)).astype(bf16)
    ke_s[...] = (k * jnp.exp(b_end - b)).astype(bf16)
    gam_s[...] = jnp.exp(b_end)
    bd = bd_ref[...]
    shape3 = (GLA_SUB, GLA_SUB, 256)
    causal = lax.broadcasted_iota(i32, shape3, 0) <= lax.broadcasted_iota(i32, shape3, 1)

    def block(n, _):
        r0 = pl.multiple_of(n * GLA_SUB, GLA_SUB)
        rows = pl.ds(r0, GLA_SUB)
        qi, ki, vi, bi = q_s[rows, :], k_s[rows, :], v_s[rows, :], b_s[rows, :]
        diff = jnp.where(causal, bi[None, :, :] - bi[:, None, :], 0.0)
        w3 = jnp.where(causal, qi[None, :, :] * ki[:, None, :] * jnp.exp(diff), 0.0)
        a3 = jnp.dot(w3.reshape(GLA_SUB * GLA_SUB, 256).astype(bf16), bd, preferred_element_type=f32)
        intra = jnp.sum(a3.reshape(shape3) * vi[:, None, :], axis=0)
        st = st_ref[...]
        inter = lax.dot_general(qe_s[rows, :], st.astype(bf16), NT, preferred_element_type=f32)
        o_s[rows, :] = intra + inter
        upd = lax.dot_general(vi.astype(bf16), ke_s[rows, :], TN, preferred_element_type=f32)
        st_ref[...] = st * gam_s[pl.ds(r0, 1), :] + jnp.where(bd > 0, upd, 0.0)
        return 0

    lax.fori_loop(0, GLA_TM // GLA_SUB, block, 0)
    o = o_s[...]
    ms = jnp.dot(o * o, bd.astype(f32), preferred_element_type=f32, precision=lax.Precision.HIGHEST) / HEAD_DIM
    o_ref[...] = (o * lax.rsqrt(ms + LN_EPS) * gn_ref[...] * _silu(g_ref[...])).astype(bf16)


def _gla(gq, gk, gv, gg, misc, wa, ba, gn, bd, B, T):
    N = B * T
    nt = T // GLA_TM
    assert T % GLA_TM == 0
    tok = lambda w_: pl.BlockSpec((GLA_TM, w_), lambda b, t: (b * nt + t, 0))
    full = lambda a: pl.BlockSpec(a.shape, lambda b, t: (0, 0))
    v = lambda dt: pltpu.VMEM((GLA_TM, 256), dt)
    return pl.pallas_call(
        _gla_kernel, grid=(B, nt),
        in_specs=[tok(256), tok(256), tok(256), tok(256), tok(128), full(wa), full(ba), full(gn), full(bd)],
        out_specs=tok(256), out_shape=jax.ShapeDtypeStruct((N, 256), bf16),
        scratch_shapes=[pltpu.VMEM((256, 256), f32), v(f32), v(f32), v(f32), v(f32), v(bf16), v(bf16), v(f32), v(f32)],
        compiler_params=_cparams("arbitrary", "arbitrary"), name="gla")(gq, gk, gv, gg, misc, wa, ba, gn, bd)


OUT_TM = 256


def _outproj_kernel(mo_ref, no_ref, po_ref, go_ref, x_ref, mod_ref, w_ref, lng_ref, lnb_ref, wrh_ref, wrl_ref,
                    x1_ref, h2_ref, lg_ref, *, alpha):
    a = jnp.concatenate([mo_ref[...], no_ref[...], po_ref[...], go_ref[...]], axis=1)
    y = jnp.dot(a, w_ref[...], preferred_element_type=f32)
    x1 = _layer_norm(alpha * x_ref[...] + mod_ref[0, 2:3, :] * y, lng_ref[...], lnb_ref[...])
    x1_ref[...] = x1
    h2 = x1 * (1.0 + mod_ref[0, 4:5, :]) + mod_ref[0, 3:4, :]
    h2_ref[...] = h2
    hi, lo = _split_bf16(h2)
    wrh = wrh_ref[...]
    lg_ref[...] = (jnp.dot(hi, wrh, preferred_element_type=f32) + jnp.dot(lo, wrh, preferred_element_type=f32)
                   + jnp.dot(hi, wrl_ref[...], preferred_element_type=f32))


def _outproj(mo, no, po, go, x2, mod, w, lng, lnb, wrh, wrl, alpha, B, T):
    N = B * T
    nt = T // OUT_TM
    tok = lambda w_: pl.BlockSpec((OUT_TM, w_), lambda i: (i, 0))
    full = lambda a: pl.BlockSpec(a.shape, lambda i: (0,) * a.ndim)
    return pl.pallas_call(
        functools.partial(_outproj_kernel, alpha=alpha), grid=(N // OUT_TM,),
        in_specs=[tok(512), tok(256), tok(256), tok(256), tok(D_MODEL),
                  pl.BlockSpec((1, 6, D_MODEL), lambda i: (i // nt, 0, 0)),
                  full(w), full(lng), full(lnb), full(wrh), full(wrl)],
        out_specs=[tok(D_MODEL), tok(D_MODEL), tok(N_EXPERTS)],
        out_shape=[jax.ShapeDtypeStruct((N, D_MODEL), f32), jax.ShapeDtypeStruct((N, D_MODEL), f32),
                   jax.ShapeDtypeStruct((N, N_EXPERTS), f32)],
        compiler_params=_cparams("arbitrary"), name="outproj")(mo, no, po, go, x2, mod, w, lng, lnb, wrh, wrl)


ROUTE_TM = 256
GROUP_SIZE = N_EXPERTS // N_EXPERT_GROUPS


def _route_kernel(lg_ref, rb_ref, ei_ref, wt_ref):
    s = jax.nn.sigmoid(lg_ref[...])
    ssel = s + rb_ref[...]
    lane = lax.broadcasted_iota(i32, s.shape, 1)

    def seg_reduce(x, op):
        k = 1
        while k < GROUP_SIZE:
            partner = jnp.where((lane & k) == 0, pltpu.roll(x, N_EXPERTS - k, axis=1), pltpu.roll(x, k, axis=1))
            x = op(x, partner)
            k *= 2
        return x

    m1 = seg_reduce(ssel, jnp.maximum)
    first = seg_reduce(jnp.where(ssel == m1, lane, BIG_IDX), jnp.minimum)
    m2 = seg_reduce(jnp.where(lane == first, -jnp.inf, ssel), jnp.maximum)
    gscore = m1 + m2
    gid = lane // GROUP_SIZE
    rank = jnp.zeros(s.shape, i32)
    for k in range(1, N_EXPERT_GROUPS):
        other = pltpu.roll(gscore, GROUP_SIZE * k, axis=1)
        rank = rank + jnp.where((other > gscore) | ((other == gscore) & (gid >= k)), 1, 0)
    x = jnp.where(rank < TOPK_GROUPS, ssel, -jnp.inf)
    lane_o = lax.broadcasted_iota(i32, (ROUTE_TM, LANES), 1)
    idx_out = jnp.zeros((ROUTE_TM, LANES), i32)
    w_out = jnp.zeros((ROUTE_TM, LANES), f32)
    wsum = jnp.zeros((ROUTE_TM, 1), f32)
    for r in range(TOP_K):
        mx = jnp.max(x, axis=1, keepdims=True)
        idx = jnp.min(jnp.where(x == mx, lane, BIG_IDX), axis=1, keepdims=True)
        hit = lane == idx
        w = jnp.sum(jnp.where(hit, s, 0.0), axis=1, keepdims=True)
        x = jnp.where(hit, -jnp.inf, x)
        idx_out = jnp.where(lane_o == r, idx, idx_out)
        w_out = jnp.where(lane_o == r, w, w_out)
        wsum = wsum + w
    ei_ref[...] = idx_out
    wt_ref[...] = w_out / wsum * ROUTED_SCALE


def _route(logits, rb):
    N = logits.shape[0]
    return pl.pallas_call(
        _route_kernel, grid=(N // ROUTE_TM,),
        in_specs=[pl.BlockSpec((ROUTE_TM, N_EXPERTS), lambda i: (i, 0)), pl.BlockSpec((1, N_EXPERTS), lambda i: (0, 0))],
        out_specs=[pl.BlockSpec((ROUTE_TM, LANES), lambda i: (i, 0))] * 2,
        out_shape=[jax.ShapeDtypeStruct((N, LANES), i32), jax.ShapeDtypeStruct((N, LANES), f32)],
        compiler_params=_cparams("arbitrary"), name="route")(logits, rb)


def _row_gather(idx_hbm, src_hbm, idx_s, buf, isem, rsem, step, n_steps, n_idx):
    def idx_copy(s):
        return pltpu.make_async_copy(idx_hbm.at[s], idx_s.at[s % 2], isem.at[s % 2])

    def row_copy(slot, r, tok):
        return pltpu.make_async_copy(src_hbm.at[pl.ds(tok, 1), :], buf.at[slot, pl.ds(r, 1), :], rsem.at[slot])

    def start_rows(s):
        slot = s % 2

        def body(r, _):
            row_copy(slot, r, idx_s[slot, r]).start()
            return 0
        lax.fori_loop(0, n_idx, body, 0)

    @pl.when(step == 0)
    def _():
        idx_copy(step).start()
        idx_copy(step).wait()
        start_rows(step)

        @pl.when(n_steps > 1)
        def _():
            idx_copy(step + 1).start()

    @pl.when(step + 1 < n_steps)
    def _():
        idx_copy(step + 1).wait()
        start_rows(step + 1)

        @pl.when(step + 2 < n_steps)
        def _():
            idx_copy(step + 2).start()

    slot = step % 2

    def wait_body(r, _):
        row_copy(slot, r, 0).wait()
        return 0
    lax.fori_loop(0, n_idx, wait_body, 0)


def _moe_kernel(be_ref, nu_ref, tok_hbm, h_hbm, wg_ref, wu_ref, wd_ref, ys_ref, idx_s, xbuf, isem, rsem):
    i = pl.program_id(0)
    nu = nu_ref[0]

    @pl.when(i < nu)
    def _():
        _row_gather(tok_hbm, h_hbm, idx_s, xbuf, isem, rsem, i, nu, MOE_ROWS)
        xb = xbuf[i % 2].astype(bf16)
        g = jnp.dot(xb, wg_ref[0].astype(bf16), preferred_element_type=f32)
        u = jnp.dot(xb, wu_ref[0].astype(bf16), preferred_element_type=f32)
        ys_ref[...] = jnp.dot((_silu(g) * u).astype(bf16), wd_ref[0].astype(bf16), preferred_element_type=f32)

    @pl.when(i >= nu)
    def _():
        ys_ref[...] = jnp.zeros_like(ys_ref)


def _moe(blk_e, n_used, row_tok, h2, wg, wu, wd):
    nb = row_tok.shape[0]
    gs = pltpu.PrefetchScalarGridSpec(
        num_scalar_prefetch=2, grid=(nb,),
        in_specs=[pl.BlockSpec(memory_space=pl.ANY), pl.BlockSpec(memory_space=pl.ANY),
                  pl.BlockSpec((1, D_MODEL, D_EXPERT), lambda i, be, nu: (be[i], 0, 0)),
                  pl.BlockSpec((1, D_MODEL, D_EXPERT), lambda i, be, nu: (be[i], 0, 0)),
                  pl.BlockSpec((1, D_EXPERT, D_MODEL), lambda i, be, nu: (be[i], 0, 0))],
        out_specs=pl.BlockSpec((MOE_ROWS, D_MODEL), lambda i, be, nu: (i, 0)),
        scratch_shapes=[pltpu.SMEM((2, MOE_ROWS), i32), pltpu.VMEM((2, MOE_ROWS, D_MODEL), f32),
                        pltpu.SemaphoreType.DMA((2,)), pltpu.SemaphoreType.DMA((2,))])
    return pl.pallas_call(
        _moe_kernel, grid_spec=gs, out_shape=jax.ShapeDtypeStruct((nb * MOE_ROWS, D_MODEL), f32),
        compiler_params=_cparams("arbitrary"), name="moe_experts")(blk_e, n_used, row_tok, h2, wg, wu, wd)


FIN_TM = 128


def _fin_kernel(pos_hbm, ys_hbm, h2_ref, x1_ref, wt_ref, mod_ref, sg_ref, su_ref, sd_ref, lng_ref, lnb_ref, o_ref,
                idx_s, gbuf, isem, rsem, *, alpha):
    i = pl.program_id(0)
    _row_gather(pos_hbm, ys_hbm, idx_s, gbuf, isem, rsem, i, pl.num_programs(0), TOP_K * FIN_TM)
    hb = h2_ref[...].astype(bf16)
    g = jnp.dot(hb, sg_ref[...], preferred_element_type=f32)
    u = jnp.dot(hb, su_ref[...], preferred_element_type=f32)
    shared = jnp.dot((_silu(g) * u).astype(bf16), sd_ref[...], preferred_element_type=f32)
    wt = wt_ref[...]
    lane = lax.broadcasted_iota(i32, wt.shape, 1)
    slot = i % 2
    routed = jnp.zeros((FIN_TM, D_MODEL), f32)
    for k in range(TOP_K):
        wk = jnp.sum(jnp.where(lane == k, wt, 0.0), axis=1, keepdims=True)
        routed = routed + gbuf[slot, k * FIN_TM:(k + 1) * FIN_TM, :] * wk
    y = shared + routed
    o_ref[...] = _layer_norm(alpha * x1_ref[...] + mod_ref[0, 5:6, :] * y, lng_ref[...], lnb_ref[...])


def _fin(pos, ys, h2, x1, wts, mod, sg, su, sd, lng, lnb, alpha, B, T):
    N = B * T
    nt = T // FIN_TM
    tok = lambda w_: pl.BlockSpec((FIN_TM, w_), lambda i: (i, 0))
    full = lambda a: pl.BlockSpec(a.shape, lambda i: (0,) * a.ndim)
    return pl.pallas_call(
        functools.partial(_fin_kernel, alpha=alpha), grid=(N // FIN_TM,),
        in_specs=[pl.BlockSpec(memory_space=pl.ANY), pl.BlockSpec(memory_space=pl.ANY),
                  tok(D_MODEL), tok(D_MODEL), tok(LANES),
                  pl.BlockSpec((1, 6, D_MODEL), lambda i: (i // nt, 0, 0)),
                  full(sg), full(su), full(sd), full(lng), full(lnb)],
        out_specs=tok(D_MODEL), out_shape=jax.ShapeDtypeStruct((N, D_MODEL), f32),
        scratch_shapes=[pltpu.SMEM((2, TOP_K * FIN_TM), i32), pltpu.VMEM((2, TOP_K * FIN_TM, D_MODEL), f32),
                        pltpu.SemaphoreType.DMA((2,)), pltpu.SemaphoreType.DMA((2,))],
        compiler_params=_cparams("arbitrary"), name="combine")(pos, ys, h2, x1, wts, mod, sg, su, sd, lng, lnb)


def _pack_w_in(w):
    c = lambda name, width: w[:, _OFF[name]:_OFF[name] + width]
    cols = [c("moba_q", 256), c("moba_k", 256), c("moba_v", 256), c("nsa_q", 256),
            c("k_slc", 64), c("k_win", 64), c("v_slc", 64), c("v_win", 64), c("k_cmp", 64), c("v_cmp", 64),
            c("pool", 256), c("gla_q", 256), c("gla_k", 256), c("gla_v", 256), c("gla_g", 256),
            c("nsa_gate", 12), c("gla_a", 16), jnp.zeros((w.shape[0], LANES - 28), w.dtype)]
    return jnp.concatenate(cols, axis=1).astype(bf16)


def _rope_tables(T):
    half = ROPE_DIM // 2
    inv_freq = ROPE_THETA ** (-jnp.arange(half, dtype=f32) / half)
    ang = jnp.arange(T).astype(f32)[:, None] * inv_freq[None, :]
    cos, sin = jnp.cos(ang), jnp.sin(ang)
    one = jnp.ones((T, HEAD_DIM - ROPE_DIM), f32)
    zero = jnp.zeros((T, HEAD_DIM - ROPE_DIM), f32)
    ct = jnp.concatenate([cos, cos, one], axis=1)
    st = jnp.concatenate([-sin, sin, zero], axis=1)
    return jnp.tile(ct, (1, 2)), jnp.tile(st, (1, 2))


def _cmp_weights(pe, w1, w2):
    half = CMP_BLOCK // 2
    z = jnp.zeros((half, HEAD_DIM, CMP_HIDDEN), f32)

    def arrange(lo):
        wk = w1[0].reshape(CMP_BLOCK, HEAD_DIM, CMP_HIDDEN)[lo:lo + half]
        wv = w1[1].reshape(CMP_BLOCK, HEAD_DIM, CMP_HIDDEN)[lo:lo + half]
        top = jnp.concatenate([wk, z], axis=2)
        bot = jnp.concatenate([z, wv], axis=2)
        return jnp.concatenate([top, bot], axis=1).reshape(half * 2 * HEAD_DIM, 2 * CMP_HIDDEN).astype(bf16)

    def pe_row(lo):
        return jnp.concatenate([pe[0, lo:lo + half], pe[1, lo:lo + half]], axis=1).reshape(1, half * 2 * HEAD_DIM)

    zc = jnp.zeros((CMP_HIDDEN, LANES - HEAD_DIM), f32)
    zr = jnp.zeros((CMP_HIDDEN, LANES), f32)
    w2k = jnp.concatenate([jnp.concatenate([w2[0], zc], axis=1), zr], axis=0).astype(bf16)
    w2v = jnp.concatenate([zr, jnp.concatenate([w2[1], zc], axis=1)], axis=0).astype(bf16)
    return arrange(0), arrange(half), pe_row(0), pe_row(half), w2k, w2v


def _overlap_matrix(ns):
    n = jnp.arange(ns)[:, None] * CMP_STRIDE
    j = jnp.arange(LANES)[None, :] * SLC_BLOCK
    ov = (n < j + SLC_BLOCK) & (n + CMP_BLOCK > j) & (jnp.arange(ns)[:, None] < ns - 1)
    return ov.astype(bf16)


def _block_diag_ones():
    h = jnp.arange(256) // HEAD_DIM
    return (h[:, None] == h[None, :]).astype(bf16)


def _routing_tables(eidx, n_tok):
    n_asg = n_tok * TOP_K
    flat_e = eidx.reshape(n_asg)
    order = jnp.argsort(flat_e)
    e_sorted = flat_e[order]
    tok_sorted = (order // TOP_K).astype(i32)
    counts = jnp.bincount(flat_e, length=N_EXPERTS)
    starts = jnp.cumsum(counts) - counts
    padded = (counts + MOE_ROWS - 1) // MOE_ROWS * MOE_ROWS
    pstarts = jnp.cumsum(padded) - padded
    dest = (pstarts[e_sorted] + jnp.arange(n_asg) - starts[e_sorted]).astype(i32)
    n_blocks = n_asg // MOE_ROWS + N_EXPERTS
    row_tok = jnp.zeros((n_blocks * MOE_ROWS,), i32).at[dest].set(tok_sorted).reshape(n_blocks, MOE_ROWS)
    pos = jnp.zeros((n_asg,), i32).at[order].set(dest).reshape(n_tok, TOP_K)
    blk_e = jnp.clip(jnp.searchsorted(pstarts, jnp.arange(n_blocks) * MOE_ROWS, side='right') - 1,
                     0, N_EXPERTS - 1).astype(i32)
    n_used = (jnp.sum(padded) // MOE_ROWS).astype(i32).reshape(1)
    pos_tiles = pos.reshape(n_tok // FIN_TM, FIN_TM, TOP_K).transpose(0, 2, 1).reshape(n_tok // FIN_TM, TOP_K * FIN_TM)
    return blk_e, n_used, row_tok, pos_tiles


def _mixer_inputs(x2, mod, w_in, B, T):
    ct, st = _rope_tables(T)
    return _inproj(x2, mod, _pack_w_in(w_in), ct, st, B, T)


def _token_mixers(x2, mod, w_in, cmp_pe, cmp_w1, cmp_w2, pool_w, pool_scale, gla_wa, gla_ba, gla_norm, B, T):
    N = B * T
    (mq, mk, mv, km, nq, nqr, ks, kw, vs, vw, kvc, pool_u, gq, gk, gv, gg, misc) = _mixer_inputs(x2, mod, w_in, B, T)
    nt = T // MOBA_BLOCK
    kmh = km.reshape(B, nt, N_HEADS, HEAD_DIM).transpose(0, 2, 1, 3)
    kmf = jnp.pad(kmh, ((0, 0), (0, 0), (HEAD_DIM, LANES - HEAD_DIM - nt), (0, LANES - HEAD_DIM)))
    mo = _moba(mq, mk, mv, kmf, B, T)
    ns = T // CMP_STRIDE
    kc, vc = _cmp(kvc.reshape(B, ns, CMP_STRIDE * LANES), *_cmp_weights(cmp_pe, cmp_w1, cmp_w2))
    no = _nsa(nq, nqr, misc, ks, vs, kw, vw, kc, vc, _overlap_matrix(ns), B, T)
    wbd = jax.scipy.linalg.block_diag(*[pool_w[g] for g in range(len(POOL_WINDOWS))]).astype(bf16)
    po = _pool(pool_u, wbd, pool_scale.reshape(1, 256), B, T)
    wa = jnp.zeros((LANES, 256), f32).at[MISC_A0:MISC_A0 + GLA_LOWRANK].set(gla_wa)
    go = _gla(gq, gk, gv, gg, misc, wa, gla_ba.reshape(1, 256), jnp.tile(gla_norm, N_HEADS).reshape(1, 256),
              _block_diag_ones(), B, T)
    return mo, no, po, go


def _pad_w_out(w_out):
    wm = w_out[:GROUP_WIDTH].reshape(N_HEADS, HEAD_DIM, D_MODEL)
    wm = jnp.pad(wm, ((0, 0), (0, LANES - HEAD_DIM), (0, 0))).reshape(N_HEADS * LANES, D_MODEL)
    return jnp.concatenate([wm, w_out[GROUP_WIDTH:]], axis=0).astype(bf16)


def kernel(x, c, w_ada, b_ada, w_in, cmp_pe, cmp_w1, cmp_w2, pool_w, pool_scale, gla_wa, gla_ba, gla_norm, w_out,
           ln_g, ln_b, w_router, router_bias, exp_gate, exp_up, exp_down, sh_gate, sh_up, sh_down):
    B, T, D = x.shape
    N = B * T
    depth = w_ada.shape[0]
    alpha = float((2 * depth) ** 0.25)
    x2 = x.reshape(N, D)
    c8 = jnp.zeros((8, D), f32).at[:B].set(c)
    for l in range(depth):
        mod = _ada(c8, w_ada[l], b_ada[l].reshape(1, -1))[:B].reshape(B, 6, D)
        mo, no, po, go = _token_mixers(x2, mod, w_in[l], cmp_pe[l], cmp_w1[l], cmp_w2[l], pool_w[l], pool_scale[l],
                                       gla_wa[l], gla_ba[l], gla_norm[l], B, T)
        wrh, wrl = _split_bf16(w_router[l])
        x1, h2, logits = _outproj(mo, no, po, go, x2, mod, _pad_w_out(w_out[l]), ln_g[l, 0].reshape(1, D),
                                  ln_b[l, 0].reshape(1, D), wrh, wrl, alpha, B, T)
        eidx, wts = _route(logits, router_bias[l].reshape(1, N_EXPERTS))
        blk_e, n_used, row_tok, pos_tiles = _routing_tables(eidx[:, :TOP_K], N)
        ys = _moe(blk_e, n_used, row_tok, h2, exp_gate[l], exp_up[l], exp_down[l])
        x2 = _fin(pos_tiles, ys, h2, x1, wts, mod, sh_gate[l].astype(bf16), sh_up[l].astype(bf16),
                  sh_down[l].astype(bf16), ln_g[l, 1].reshape(1, D), ln_b[l, 1].reshape(1, D), alpha, B, T)
    return x2.reshape(B, T, D)
```

```python
import functools

import jax
import jax.numpy as jnp
from jax import lax
from jax.experimental import pallas as pl
from jax.experimental.pallas import tpu as pltpu

f32, bf16, i32 = jnp.float32, jnp.bfloat16, jnp.int32

D_MODEL = 1024
HEAD_DIM = 64
N_HEADS = 4
GROUP_WIDTH = 256
ROPE_THETA = 500000.0
ROPE_DIM = 16
MOBA_BLOCK = 256
MOBA_TOPK = 3
CMP_BLOCK = 32
CMP_STRIDE = 16
CMP_HIDDEN = 128
SLC_BLOCK = 64
SLC_TOPN = 16
WIN = 512
FORCE_SCORE = 1e9
POOL_WINDOWS = (2, 4, 8, 16)
GLA_SUB = 16
GLA_LOWRANK = 16
GLA_TAU = 16.0
N_EXPERTS = 256
TOP_K = 8
N_EXPERT_GROUPS = 8
TOPK_GROUPS = 4
D_EXPERT = 256
ROUTED_SCALE = 2.5
LN_EPS = 1e-5
QK_SCALE = HEAD_DIM ** -0.5

LANES = 128
MASKED = -1e30
M_INIT = -3e38
BIG_IDX = 1 << 20
MOE_ROWS = 256
ROW_TILE = D_MODEL // LANES
VMEM_LIMIT = 56 * 1024 * 1024

_OFF = dict(moba_q=0, moba_k=256, moba_v=512, nsa_q=768, k_cmp=1024, v_cmp=1088, k_slc=1152, v_slc=1216,
            k_win=1280, v_win=1344, nsa_gate=1408, pool=1420, gla_q=1676, gla_k=1932, gla_v=2188,
            gla_a=2444, gla_g=2460)
_S = dict(mq=0, mk=256, mv=512, nq=768, sw=1024, vsw=1152, kvc=1280, pool=1408, gq=1664, gk=1920, gv=2176,
          gg=2432, misc=2688)
IN_COLS_PACKED = 2816
MISC_GATE0 = 0
MISC_A0 = 12

NT = (((1,), (1,)), ((), ()))
TN = (((0,), (0,)), ((), ()))


def _cparams(*sem):
    return pltpu.CompilerParams(dimension_semantics=sem, vmem_limit_bytes=VMEM_LIMIT)


def _silu(x):
    return x * jax.nn.sigmoid(x)


def _split_bf16(x):
    hi = x.astype(bf16)
    lo = (x - hi.astype(f32)).astype(bf16)
    return hi, lo


def _layer_norm(z, g, b):
    mu = jnp.mean(z, axis=-1, keepdims=True)
    zc = z - mu
    var = jnp.mean(zc * zc, axis=-1, keepdims=True)
    return zc * lax.rsqrt(var + LN_EPS) * g + b


def _argmax_rounds(score, lane, rounds):
    picked = jnp.zeros(score.shape, f32)
    for _ in range(rounds):
        mx = jnp.max(score, axis=1, keepdims=True)
        idx = jnp.min(jnp.where(score == mx, lane, BIG_IDX), axis=1, keepdims=True)
        hit = lane == idx
        picked = jnp.where(hit, 1.0, picked)
        score = jnp.where(hit, -jnp.inf, score)
    return picked


def _softmax_step(s, m, l, acc, vb):
    m_new = jnp.maximum(m, jnp.max(s, axis=1, keepdims=True))
    alpha = jnp.exp(m - m_new)
    p = jnp.exp(s - m_new)
    l = alpha * l + jnp.sum(p, axis=1, keepdims=True)
    acc = alpha * acc + jnp.dot(p.astype(bf16), vb, preferred_element_type=f32)
    return m_new, l, acc


def _ada_kernel(c_ref, w_ref, b_ref, o_ref):
    o_ref[...] = jnp.dot(_silu(c_ref[...]), w_ref[...], preferred_element_type=f32,
                         precision=lax.Precision.HIGHEST) + b_ref[...]


def _ada(c8, w, b):
    n = w.shape[1] // D_MODEL
    return pl.pallas_call(
        _ada_kernel, grid=(n,),
        in_specs=[pl.BlockSpec((8, D_MODEL), lambda j: (0, 0)),
                  pl.BlockSpec((D_MODEL, D_MODEL), lambda j: (0, j)),
                  pl.BlockSpec((1, D_MODEL), lambda j: (0, j))],
        out_specs=pl.BlockSpec((8, D_MODEL), lambda j: (0, j)),
        out_shape=jax.ShapeDtypeStruct((8, w.shape[1]), f32),
        compiler_params=_cparams("arbitrary"), name="ada")(c8, w, b)


IN_TM = MOBA_BLOCK


def _inproj_kernel(x_ref, mod_ref, w_ref, ct_ref, st_ref,
                   mq_ref, mk_ref, mv_ref, km_ref, nq_ref, nqr_ref, ks_ref, kw_ref, vs_ref, vw_ref,
                   kvc_ref, pool_ref, gq_ref, gk_ref, gv_ref, gg_ref, misc_ref, *, nt):
    tb = pl.program_id(0) % nt
    h = (x_ref[...] * (1.0 + mod_ref[0, 1:2, :]) + mod_ref[0, 0:1, :]).astype(bf16)

    def seg(name, width):
        a = _S[name]
        return jnp.dot(h, w_ref[:, a:a + width], preferred_element_type=f32)

    ct, st = ct_ref[...], st_ref[...]
    lane = lax.broadcasted_iota(i32, (IN_TM, LANES), 1)
    first8 = (lane % HEAD_DIM) < ROPE_DIM // 2
    half = lane < HEAD_DIM

    def rope128(y):
        partner = jnp.where(first8, pltpu.roll(y, LANES - 8, axis=1), pltpu.roll(y, 8, axis=1))
        return y * ct + partner * st

    def rope(y):
        return jnp.concatenate([rope128(y[:, c * LANES:(c + 1) * LANES]) for c in range(y.shape[1] // LANES)],
                               axis=1)

    def lo_half(y):
        return jnp.where(half, y, 0.0)

    def hi_half(y):
        return jnp.where(half, pltpu.roll(y, HEAD_DIM, axis=1), 0.0)

    def per_head(y):
        parts = []
        for c in range(2):
            yc = y[:, c * LANES:(c + 1) * LANES]
            parts += [lo_half(yc), hi_half(yc)]
        return jnp.concatenate(parts, axis=1)

    mq_ref[...] = per_head(rope(seg("mq", 256)) * QK_SCALE).astype(bf16)
    k = rope(seg("mk", 256))
    km_ref[0] = jnp.mean(k, axis=0, keepdims=True)
    lane4 = lax.broadcasted_iota(i32, (IN_TM, 4 * LANES), 1)
    mk_ref[...] = jnp.where((lane4 % LANES) == HEAD_DIM + tb, 1.0, per_head(k)).astype(bf16)
    mv_ref[...] = per_head(seg("mv", 256)).astype(bf16)
    q = seg("nq", 256) * QK_SCALE
    nq_ref[...] = per_head(q).astype(bf16)
    nqr_ref[...] = per_head(rope(q)).astype(bf16)
    sw = rope128(seg("sw", 128))
    row = lax.broadcasted_iota(i32, (IN_TM, LANES), 0)
    slc_id = tb * (IN_TM // SLC_BLOCK) + row // SLC_BLOCK
    ks_ref[...] = jnp.concatenate([lo_half(sw), jnp.where(lane == slc_id, 1.0, 0.0)], axis=1).astype(bf16)
    kw_ref[...] = hi_half(sw).astype(bf16)
    vsw = seg("vsw", 128)
    vs_ref[...] = lo_half(vsw).astype(bf16)
    vw_ref[...] = hi_half(vsw).astype(bf16)
    kvc_ref[...] = seg("kvc", 128).astype(bf16)
    pool_ref[...] = seg("pool", 256)
    gq_ref[...] = seg("gq", 256)
    gk_ref[...] = seg("gk", 256)
    gv_ref[...] = seg("gv", 256)
    gg_ref[...] = seg("gg", 256)
    misc_ref[...] = seg("misc", 128)


def _inproj(x2, mod, w, ct, st, B, T):
    N = B * T
    nt = T // IN_TM
    assert T % IN_TM == 0 and nt <= 32 and T // SLC_BLOCK <= LANES
    row = lambda w_, dt: (jax.ShapeDtypeStruct((N, w_), dt), pl.BlockSpec((IN_TM, w_), lambda i: (i, 0)))
    outs = [row(512, bf16), row(512, bf16), row(512, bf16),
            (jax.ShapeDtypeStruct((N // IN_TM, 1, 256), f32), pl.BlockSpec((1, 1, 256), lambda i: (i, 0, 0))),
            row(512, bf16), row(512, bf16), row(256, bf16), row(128, bf16), row(128, bf16), row(128, bf16),
            row(128, bf16), row(256, f32), row(256, f32), row(256, f32), row(256, f32), row(256, f32),
            row(128, f32)]
    return pl.pallas_call(
        functools.partial(_inproj_kernel, nt=nt), grid=(N // IN_TM,),
        in_specs=[pl.BlockSpec((IN_TM, D_MODEL), lambda i: (i, 0)),
                  pl.BlockSpec((1, 6, D_MODEL), lambda i: (i // nt, 0, 0)),
                  pl.BlockSpec((D_MODEL, IN_COLS_PACKED), lambda i: (0, 0), pipeline_mode=pl.Buffered(1)),
                  pl.BlockSpec((IN_TM, LANES), lambda i: (i % nt, 0)),
                  pl.BlockSpec((IN_TM, LANES), lambda i: (i % nt, 0))],
        out_specs=[o[1] for o in outs], out_shape=[o[0] for o in outs],
        compiler_params=_cparams("arbitrary"), name="inproj")(x2, mod, w, ct, st)


def _cmp_kernel(x_ref, wa_ref, wb_ref, pea_ref, peb_ref, w2k_ref, w2v_ref, kc_ref, vc_ref):
    x = x_ref[0].astype(f32)
    a = jnp.dot((x + pea_ref[...]).astype(bf16), wa_ref[...], preferred_element_type=f32)
    b = jnp.dot((x + peb_ref[...]).astype(bf16), wb_ref[...], preferred_element_type=f32)
    hid = a + pltpu.roll(b, x.shape[0] - 1, axis=0)
    g = jax.nn.gelu(hid).astype(bf16)
    kc_ref[0] = jnp.dot(g, w2k_ref[...], preferred_element_type=f32).astype(bf16)
    vc_ref[0] = jnp.dot(g, w2v_ref[...], preferred_element_type=f32).astype(bf16)


def _cmp(xseg, wa, wb, pea, peb, w2k, w2v):
    B, ns, wd = xseg.shape
    full = lambda a: pl.BlockSpec(a.shape, lambda b: (0,) * a.ndim)
    return pl.pallas_call(
        _cmp_kernel, grid=(B,),
        in_specs=[pl.BlockSpec((1, ns, wd), lambda b: (b, 0, 0))] + [full(a) for a in (wa, wb, pea, peb, w2k, w2v)],
        out_specs=[pl.BlockSpec((1, ns, LANES), lambda b: (b, 0, 0))] * 2,
        out_shape=[jax.ShapeDtypeStruct((B, ns, LANES), bf16)] * 2,
        compiler_params=_cparams("arbitrary"), name="nsa_compress")(xseg, wa, wb, pea, peb, w2k, w2v)


def _moba_kernel(q_ref, k_ref, v_ref, kmf_ref, o_ref):
    own = pl.program_id(1)
    heads = range(N_HEADS)
    hl = lambda h: slice(h * LANES, (h + 1) * LANES)
    q4 = jnp.concatenate([q_ref[:, hl(h)] for h in heads], axis=0)
    gates = []
    for h in heads:
        hi, lo = _split_bf16(kmf_ref[0, h])
        qh = q_ref[:, hl(h)]
        gates.append(lax.dot_general(qh, hi, NT, preferred_element_type=f32)
                     + lax.dot_general(qh, lo, NT, preferred_element_type=f32))
    gate = jnp.concatenate(gates, axis=0)
    lane = lax.broadcasted_iota(i32, gate.shape, 1)
    blk = lane - HEAD_DIM
    past = (blk >= 0) & (blk < own)
    picked = _argmax_rounds(jnp.where(past, gate, -jnp.inf), lane, MOBA_TOPK)
    allowed = ((picked > 0.0) & past) | (blk == own)
    is_blk_lane = (blk >= 0) & (blk < 32)
    qf4 = q4 + jnp.where(is_blk_lane & jnp.logical_not(allowed), MASKED, 0.0).astype(bf16)
    qf = [qf4[h * MOBA_BLOCK:(h + 1) * MOBA_BLOCK] for h in heads]

    def scores(h, off):
        return lax.dot_general(qf[h], k_ref[pl.ds(off, MOBA_BLOCK), hl(h)], NT, preferred_element_type=f32)

    def body(j, carry):
        off = pl.multiple_of(j * MOBA_BLOCK, MOBA_BLOCK)
        return tuple(_softmax_step(scores(h, off), *carry[h], v_ref[pl.ds(off, MOBA_BLOCK), hl(h)]) for h in heads)

    init = tuple((jnp.full((MOBA_BLOCK, 1), M_INIT, f32), jnp.zeros((MOBA_BLOCK, 1), f32),
                  jnp.zeros((MOBA_BLOCK, LANES), f32)) for _ in heads)
    carry = lax.fori_loop(0, own, body, init)
    off = pl.multiple_of(own * MOBA_BLOCK, MOBA_BLOCK)
    r = lax.broadcasted_iota(i32, (MOBA_BLOCK, MOBA_BLOCK), 0)
    c = lax.broadcasted_iota(i32, (MOBA_BLOCK, MOBA_BLOCK), 1)
    for h in heads:
        _, l, acc = _softmax_step(jnp.where(c <= r, scores(h, off), MASKED), *carry[h],
                                  v_ref[pl.ds(off, MOBA_BLOCK), hl(h)])
        o_ref[:, hl(h)] = (acc / l).astype(bf16)


def _moba(mq, mk, mv, kmf, B, T):
    N = B * T
    nt = T // MOBA_BLOCK
    wd = N_HEADS * LANES
    seq = pl.BlockSpec((T, wd), lambda b, i: (b, 0), pipeline_mode=pl.Buffered(1))
    return pl.pallas_call(
        _moba_kernel, grid=(B, nt),
        in_specs=[pl.BlockSpec((MOBA_BLOCK, wd), lambda b, i: (b * nt + i, 0)), seq, seq,
                  pl.BlockSpec((1, N_HEADS, LANES, LANES), lambda b, i: (b, 0, 0, 0))],
        out_specs=pl.BlockSpec((MOBA_BLOCK, wd), lambda b, i: (b * nt + i, 0)),
        out_shape=jax.ShapeDtypeStruct((N, wd), bf16),
        compiler_params=_cparams("arbitrary", "arbitrary"), name="moba")(mq, mk, mv, kmf)


NSA_TQ = 256
NSA_KB = 256


def _stack_heads(ref):
    return jnp.concatenate([ref[:, h * LANES:(h + 1) * LANES] for h in range(N_HEADS)], axis=0)


def _nsa_select_kernel(nq_ref, kc_ref, vc_ref, ov_ref, oc_ref, selb_ref, *, n_cmp):
    c = pl.program_id(1)
    q4 = _stack_heads(nq_ref)
    qpos4 = c * NSA_TQ + lax.broadcasted_iota(i32, (N_HEADS * NSA_TQ, 1), 0) % NSA_TQ
    s = lax.dot_general(q4, kc_ref[0], NT, preferred_element_type=f32)
    n = lax.broadcasted_iota(i32, s.shape, 1)
    ok = (n * CMP_STRIDE + (CMP_BLOCK - 1) <= qpos4) & (n < n_cmp)
    s = jnp.where(ok, s, -jnp.inf)
    m = jnp.max(s, axis=1, keepdims=True)
    m = jnp.where(m > -jnp.inf, m, 0.0)
    e = jnp.where(ok, jnp.exp(s - m), 0.0)
    p_c = e / jnp.maximum(jnp.sum(e, axis=1, keepdims=True), 1e-30)
    o_c = jnp.dot(p_c.astype(bf16), vc_ref[0], preferred_element_type=f32)
    for h in range(N_HEADS):
        oc_ref[:, h * LANES:(h + 1) * LANES] = o_c[h * NSA_TQ:(h + 1) * NSA_TQ].astype(bf16)
    p_sum = (p_c[0:NSA_TQ] + p_c[NSA_TQ:2 * NSA_TQ]) + (p_c[2 * NSA_TQ:3 * NSA_TQ] + p_c[3 * NSA_TQ:])
    hi, lo = _split_bf16(p_sum)
    imp = jnp.dot(hi, ov_ref[...], preferred_element_type=f32) + jnp.dot(lo, ov_ref[...], preferred_element_type=f32)
    j = lax.broadcasted_iota(i32, imp.shape, 1)
    cur = (c * NSA_TQ + lax.broadcasted_iota(i32, (NSA_TQ, 1), 0)) // SLC_BLOCK
    forced = (j == 0) | (j == cur) | (j == cur - 1)
    valid = j <= cur
    score = jnp.where(valid, jnp.where(forced, FORCE_SCORE, imp), -jnp.inf)
    chosen = (_argmax_rounds(score, j, SLC_TOPN) > 0.0) & valid
    selb_ref[...] = jnp.where(chosen, 0.0, MASKED).astype(bf16)


def _nsa_attend_kernel(nqr_ref, selb_ref, oc_ref, misc_ref, ks_ref, vs_ref, kw_ref, vw_ref, o_ref):
    c = pl.program_id(1)
    rows = N_HEADS * NSA_TQ
    q4r = _stack_heads(nqr_ref)
    qpos4 = c * NSA_TQ + lax.broadcasted_iota(i32, (rows, 1), 0) % NSA_TQ
    selb = selb_ref[...]
    lhs = jnp.concatenate([jnp.concatenate([q4r[h * NSA_TQ:(h + 1) * NSA_TQ], selb], axis=1)
                           for h in range(N_HEADS)], axis=0)

    def body(jb, carry):
        off = pl.multiple_of(jb * NSA_KB, NSA_KB)
        s_ = lax.dot_general(lhs, ks_ref[pl.ds(off, NSA_KB), :], NT, preferred_element_type=f32)
        return _softmax_step(s_, *carry, vs_ref[pl.ds(off, NSA_KB), :])

    init = (jnp.full((rows, 1), M_INIT, f32), jnp.zeros((rows, 1), f32), jnp.zeros((rows, LANES), f32))
    carry = lax.fori_loop(0, c, body, init)
    off = pl.multiple_of(c * NSA_KB, NSA_KB)
    s_ = lax.dot_general(lhs, ks_ref[pl.ds(off, NSA_KB), :], NT, preferred_element_type=f32)
    kpos = off + lax.broadcasted_iota(i32, s_.shape, 1)
    _, l, acc = _softmax_step(jnp.where(kpos <= qpos4, s_, MASKED), *carry, vs_ref[pl.ds(off, NSA_KB), :])
    o_s = acc / l

    wlen = NSA_TQ + WIN
    start = pl.multiple_of(jnp.maximum(c * NSA_TQ - WIN, 0), NSA_TQ)
    s_w = lax.dot_general(q4r, kw_ref[pl.ds(start, wlen), :], NT, preferred_element_type=f32)
    wpos = start + lax.broadcasted_iota(i32, s_w.shape, 1)
    s_w = jnp.where((wpos <= qpos4) & (wpos > qpos4 - WIN), s_w, -jnp.inf)
    e_w = jnp.exp(s_w - jnp.max(s_w, axis=1, keepdims=True))
    p_w = e_w / jnp.sum(e_w, axis=1, keepdims=True)
    o_w = jnp.dot(p_w.astype(bf16), vw_ref[pl.ds(start, wlen), :], preferred_element_type=f32)

    gates = jax.nn.sigmoid(misc_ref[...])
    gl = lax.broadcasted_iota(i32, gates.shape, 1)

    def gate_col(g):
        return jnp.concatenate([jnp.sum(jnp.where(gl == MISC_GATE0 + 3 * h + g, gates, 0.0), axis=1, keepdims=True)
                                for h in range(N_HEADS)], axis=0)

    out = gate_col(0) * _stack_heads(oc_ref).astype(f32) + gate_col(1) * o_s + gate_col(2) * o_w
    pair = lambda a, b: a + pltpu.roll(b, HEAD_DIM, axis=1)
    o_ref[...] = jnp.concatenate([pair(out[0:NSA_TQ], out[NSA_TQ:2 * NSA_TQ]),
                                  pair(out[2 * NSA_TQ:3 * NSA_TQ], out[3 * NSA_TQ:])], axis=1).astype(bf16)


def _nsa(nq, nqr, misc, ks, vs, kw, vw, kc, vc, ov, B, T):
    N = B * T
    nc = T // NSA_TQ
    ns = kc.shape[1]
    assert T >= NSA_TQ + WIN and T % NSA_KB == 0 and NSA_TQ == NSA_KB
    tok = lambda w_: pl.BlockSpec((NSA_TQ, w_), lambda b, c: (b * nc + c, 0))
    seq = lambda w_: pl.BlockSpec((T, w_), lambda b, c: (b, 0), pipeline_mode=pl.Buffered(1))
    cmp_spec = pl.BlockSpec((1, ns, LANES), lambda b, c: (b, 0, 0))
    o_c, selb = pl.pallas_call(
        functools.partial(_nsa_select_kernel, n_cmp=(T - CMP_BLOCK) // CMP_STRIDE + 1), grid=(B, nc),
        in_specs=[tok(512), cmp_spec, cmp_spec, pl.BlockSpec(ov.shape, lambda b, c: (0, 0))],
        out_specs=[tok(512), tok(LANES)],
        out_shape=[jax.ShapeDtypeStruct((N, 512), bf16), jax.ShapeDtypeStruct((N, LANES), bf16)],
        compiler_params=_cparams("arbitrary", "arbitrary"), name="nsa_select")(nq, kc, vc, ov)
    return pl.pallas_call(
        _nsa_attend_kernel, grid=(B, nc),
        in_specs=[tok(512), tok(LANES), tok(512), tok(LANES), seq(256), seq(LANES), seq(LANES), seq(LANES)],
        out_specs=tok(256), out_shape=jax.ShapeDtypeStruct((N, 256), bf16),
        compiler_params=_cparams("arbitrary", "arbitrary"), name="nsa_attend")(nqr, selb, o_c, misc, ks, vs, kw, vw)


POOL_TM = 512
POOL_HALO = 16


def _pool_kernel(u_ref, halo_ref, w_ref, sc_ref, o_ref):
    t = pl.program_id(1)
    halo = jnp.where(t == 0, 0.0, halo_ref[...])
    ext = jnp.concatenate([halo, u_ref[...]], axis=0)
    s2 = ext + pltpu.roll(ext, 1, axis=0)
    s4 = s2 + pltpu.roll(s2, 2, axis=0)
    s8 = s4 + pltpu.roll(s4, 4, axis=0)
    s16 = s8 + pltpu.roll(s8, 8, axis=0)
    pos1 = jnp.maximum(t * POOL_TM - POOL_HALO + 1 + lax.broadcasted_iota(i32, ext.shape, 0), 1).astype(f32)
    grp = lax.broadcasted_iota(i32, ext.shape, 1) // HEAD_DIM
    mean = jnp.where(grp == 0, s2 / jnp.minimum(pos1, 2.0),
                     jnp.where(grp == 1, s4 / jnp.minimum(pos1, 4.0),
                               jnp.where(grp == 2, s8 / jnp.minimum(pos1, 8.0), s16 / jnp.minimum(pos1, 16.0))))
    pooled = (mean - ext)[POOL_HALO:, :]
    o_ref[...] = (jnp.dot(pooled.astype(bf16), w_ref[...], preferred_element_type=f32) * sc_ref[...]).astype(bf16)


def _pool(u, wbd, scale, B, T):
    N = B * T
    tm = min(POOL_TM, T)
    assert tm == POOL_TM and T % POOL_TM == 0
    nt = T // tm
    hb = tm // POOL_HALO
    return pl.pallas_call(
        _pool_kernel, grid=(B, nt),
        in_specs=[pl.BlockSpec((tm, 256), lambda b, t: (b * nt + t, 0)),
                  pl.BlockSpec((POOL_HALO, 256), lambda b, t: (jnp.maximum((b * nt + t) * hb - 1, 0), 0)),
                  pl.BlockSpec((256, 256), lambda b, t: (0, 0)),
                  pl.BlockSpec((1, 256), lambda b, t: (0, 0))],
        out_specs=pl.BlockSpec((tm, 256), lambda b, t: (b * nt + t, 0)),
        out_shape=jax.ShapeDtypeStruct((N, 256), bf16),
        compiler_params=_cparams("arbitrary", "arbitrary"), name="pool")(u, u, wbd, scale)


GLA_TM = 256


def _gla_kernel(q_ref, k_ref, v_ref, g_ref, misc_ref, wa_ref, ba_ref, gn_ref, bd_ref, o_ref,
                st_ref, q_s, k_s, v_s, b_s, qe_s, ke_s, gam_s, o_s):
    @pl.when(pl.program_id(1) == 0)
    def _():
        st_ref[...] = jnp.zeros_like(st_ref)

    x = jnp.dot(misc_ref[...], wa_ref[...], preferred_element_type=f32, precision=lax.Precision.HIGHEST) + ba_ref[...]
    log_a = (jnp.minimum(x, 0.0) - jnp.log1p(jnp.exp(-jnp.abs(x)))) / GLA_TAU
    r16 = lax.broadcasted_iota(i32, log_a.shape, 0) % GLA_SUB
    b = log_a
    for s in (1, 2, 4, 8):
        b = b + jnp.where(r16 >= s, pltpu.roll(b, s, axis=0), 0.0)
    b_end = jnp.where(r16 == GLA_SUB - 1, b, 0.0)
    for s in (1, 2, 4, 8):
        b_end = b_end + pltpu.roll(b_end, GLA_TM - s, axis=0)
    q = q_ref[...] * QK_SCALE
    k = k_ref[...]
    q_s[...] = q
    k_s[...] = k
    v_s[...] = v_ref[...]
    b_s[...] = b
    qe_s[...] = (q * jnp.exp(b)).astype(bf16)
    ke_s[...] = (k * jnp.exp(b_end - b)).astype(bf16)
    gam_s[...] = jnp.exp(b_end)
    bd = bd_ref[...]
    shape3 = (GLA_SUB, GLA_SUB, 256)
    causal = lax.broadcasted_iota(i32, shape3, 0) <= lax.broadcasted_iota(i32, shape3, 1)

    def block(n, _):
        r0 = pl.multiple_of(n * GLA_SUB, GLA_SUB)
        rows = pl.ds(r0, GLA_SUB)
        qi, ki, vi, bi = q_s[rows, :], k_s[rows, :], v_s[rows, :], b_s[rows, :]
        diff = jnp.where(causal, bi[None, :, :] - bi[:, None, :], 0.0)
        w3 = jnp.where(causal, qi[None, :, :] * ki[:, None, :] * jnp.exp(diff), 0.0)
        a3 = jnp.dot(w3.reshape(GLA_SUB * GLA_SUB, 256).astype(bf16), bd, preferred_element_type=f32)
        intra = jnp.sum(a3.reshape(shape3) * vi[:, None, :], axis=0)
        st = st_ref[...]
        inter = lax.dot_general(qe_s[rows, :], st.astype(bf16), NT, preferred_element_type=f32)
        o_s[rows, :] = intra + inter
        upd = lax.dot_general(vi.astype(bf16), ke_s[rows, :], TN, preferred_element_type=f32)
        st_ref[...] = st * gam_s[pl.ds(r0, 1), :] + jnp.where(bd > 0, upd, 0.0)
        return 0

    lax.fori_loop(0, GLA_TM // GLA_SUB, block, 0)
    o = o_s[...]
    ms = jnp.dot(o * o, bd.astype(f32), preferred_element_type=f32, precision=lax.Precision.HIGHEST) / HEAD_DIM
    o_ref[...] = (o * lax.rsqrt(ms + LN_EPS) * gn_ref[...] * _silu(g_ref[...])).astype(bf16)


def _gla(gq, gk, gv, gg, misc, wa, ba, gn, bd, B, T):
    N = B * T
    nt = T // GLA_TM
    assert T % GLA_TM == 0
    tok = lambda w_: pl.BlockSpec((GLA_TM, w_), lambda b, t: (b * nt + t, 0))
    full = lambda a: pl.BlockSpec(a.shape, lambda b, t: (0, 0))
    v = lambda dt: pltpu.VMEM((GLA_TM, 256), dt)
    return pl.pallas_call(
        _gla_kernel, grid=(B, nt),
        in_specs=[tok(256), tok(256), tok(256), tok(256), tok(128), full(wa), full(ba), full(gn), full(bd)],
        out_specs=tok(256), out_shape=jax.ShapeDtypeStruct((N, 256), bf16),
        scratch_shapes=[pltpu.VMEM((256, 256), f32), v(f32), v(f32), v(f32), v(f32), v(bf16), v(bf16), v(f32), v(f32)],
        compiler_params=_cparams("arbitrary", "arbitrary"), name="gla")(gq, gk, gv, gg, misc, wa, ba, gn, bd)


OUT_TM = 256


def _outproj_kernel(mo_ref, no_ref, po_ref, go_ref, x_ref, mod_ref, w_ref, lng_ref, lnb_ref, wrh_ref, wrl_ref,
                    x1_ref, h2t_ref, h2b_ref, lg_ref, *, alpha):
    a = jnp.concatenate([mo_ref[...], no_ref[...], po_ref[...], go_ref[...]], axis=1)
    y = jnp.dot(a, w_ref[...], preferred_element_type=f32)
    x1 = _layer_norm(alpha * x_ref[...] + mod_ref[0, 2:3, :] * y, lng_ref[...], lnb_ref[...])
    x1_ref[...] = x1
    h2 = x1 * (1.0 + mod_ref[0, 4:5, :]) + mod_ref[0, 3:4, :]
    h2b_ref[...] = h2.astype(bf16)
    for s in range(ROW_TILE):
        h2t_ref[pl.ds(s, OUT_TM, stride=ROW_TILE), :] = h2[:, s * LANES:(s + 1) * LANES]
    hi, lo = _split_bf16(h2)
    wrh = wrh_ref[...]
    lg_ref[...] = (jnp.dot(hi, wrh, preferred_element_type=f32) + jnp.dot(lo, wrh, preferred_element_type=f32)
                   + jnp.dot(hi, wrl_ref[...], preferred_element_type=f32))


def _outproj(mo, no, po, go, x2, mod, w, lng, lnb, wrh, wrl, alpha, B, T):
    N = B * T
    nt = T // OUT_TM
    tok = lambda w_: pl.BlockSpec((OUT_TM, w_), lambda i: (i, 0))
    full = lambda a: pl.BlockSpec(a.shape, lambda i: (0,) * a.ndim)
    return pl.pallas_call(
        functools.partial(_outproj_kernel, alpha=alpha), grid=(N // OUT_TM,),
        in_specs=[tok(512), tok(256), tok(256), tok(256), tok(D_MODEL),
                  pl.BlockSpec((1, 6, D_MODEL), lambda i: (i // nt, 0, 0)),
                  full(w), full(lng), full(lnb), full(wrh), full(wrl)],
        out_specs=[tok(D_MODEL), pl.BlockSpec((OUT_TM * ROW_TILE, LANES), lambda i: (i, 0)), tok(D_MODEL),
                   tok(N_EXPERTS)],
        out_shape=[jax.ShapeDtypeStruct((N, D_MODEL), f32), jax.ShapeDtypeStruct((N * ROW_TILE, LANES), f32),
                   jax.ShapeDtypeStruct((N, D_MODEL), bf16), jax.ShapeDtypeStruct((N, N_EXPERTS), f32)],
        compiler_params=_cparams("arbitrary"), name="outproj")(mo, no, po, go, x2, mod, w, lng, lnb, wrh, wrl)


ROUTE_TM = 256
GROUP_SIZE = N_EXPERTS // N_EXPERT_GROUPS


def _route_kernel(lg_ref, rb_ref, tri_ref, ei_ref, wt_ref, rk_ref, cnt_ref, base_ref):
    @pl.when(pl.program_id(0) == 0)
    def _():
        base_ref[...] = jnp.zeros_like(base_ref)

    s = jax.nn.sigmoid(lg_ref[...])
    ssel = s + rb_ref[...]
    lane = lax.broadcasted_iota(i32, s.shape, 1)

    def seg_reduce(x, op):
        k = 1
        while k < GROUP_SIZE:
            partner = jnp.where((lane & k) == 0, pltpu.roll(x, N_EXPERTS - k, axis=1), pltpu.roll(x, k, axis=1))
            x = op(x, partner)
            k *= 2
        return x

    m1 = seg_reduce(ssel, jnp.maximum)
    first = seg_reduce(jnp.where(ssel == m1, lane, BIG_IDX), jnp.minimum)
    m2 = seg_reduce(jnp.where(lane == first, -jnp.inf, ssel), jnp.maximum)
    gscore = m1 + m2
    gid = lane // GROUP_SIZE
    rank = jnp.zeros(s.shape, i32)
    for k in range(1, N_EXPERT_GROUPS):
        other = pltpu.roll(gscore, GROUP_SIZE * k, axis=1)
        rank = rank + jnp.where((other > gscore) | ((other == gscore) & (gid >= k)), 1, 0)
    x = jnp.where(rank < TOPK_GROUPS, ssel, -jnp.inf)
    lane_o = lax.broadcasted_iota(i32, (ROUTE_TM, LANES), 1)
    idx_out = jnp.zeros((ROUTE_TM, LANES), i32)
    w_out = jnp.zeros((ROUTE_TM, LANES), f32)
    wsum = jnp.zeros((ROUTE_TM, 1), f32)
    hits = []
    for r in range(TOP_K):
        mx = jnp.max(x, axis=1, keepdims=True)
        idx = jnp.min(jnp.where(x == mx, lane, BIG_IDX), axis=1, keepdims=True)
        hit = lane == idx
        hits.append(hit)
        w = jnp.sum(jnp.where(hit, s, 0.0), axis=1, keepdims=True)
        x = jnp.where(hit, -jnp.inf, x)
        idx_out = jnp.where(lane_o == r, idx, idx_out)
        w_out = jnp.where(lane_o == r, w, w_out)
        wsum = wsum + w
    ei_ref[...] = idx_out
    wt_ref[...] = w_out / wsum * ROUTED_SCALE
    chosen = jnp.zeros(s.shape, f32)
    for hit in hits:
        chosen = jnp.where(hit, 1.0, chosen)
    before = base_ref[...] + jnp.dot(tri_ref[...], chosen.astype(bf16), preferred_element_type=f32)
    rk_out = jnp.zeros((ROUTE_TM, LANES), i32)
    for r in range(TOP_K):
        rk = jnp.sum(jnp.where(hits[r], before, 0.0), axis=1, keepdims=True)
        rk_out = jnp.where(lane_o == r, rk.astype(i32), rk_out)
    rk_ref[...] = rk_out
    total = base_ref[...] + jnp.sum(chosen, axis=0, keepdims=True)
    base_ref[...] = total
    cnt_ref[...] = total.astype(i32)


def _route(logits, rb):
    N = logits.shape[0]
    r_, c_ = jnp.arange(ROUTE_TM)[:, None], jnp.arange(ROUTE_TM)[None, :]
    tri = (c_ < r_).astype(bf16)
    tok = pl.BlockSpec((ROUTE_TM, LANES), lambda i: (i, 0))
    return pl.pallas_call(
        _route_kernel, grid=(N // ROUTE_TM,),
        in_specs=[pl.BlockSpec((ROUTE_TM, N_EXPERTS), lambda i: (i, 0)), pl.BlockSpec((1, N_EXPERTS), lambda i: (0, 0)),
                  pl.BlockSpec((ROUTE_TM, ROUTE_TM), lambda i: (0, 0))],
        out_specs=[tok, tok, tok, pl.BlockSpec((1, N_EXPERTS), lambda i: (0, 0))],
        out_shape=[jax.ShapeDtypeStruct((N, LANES), i32), jax.ShapeDtypeStruct((N, LANES), f32),
                   jax.ShapeDtypeStruct((N, LANES), i32), jax.ShapeDtypeStruct((1, N_EXPERTS), i32)],
        scratch_shapes=[pltpu.VMEM((1, N_EXPERTS), f32)],
        compiler_params=_cparams("arbitrary"), name="route")(logits, rb, tri)


DISP_TM = 128


def _dispatch_kernel(h_ref, ei_ref, rk_ref, ps_ref, xs_in, pos_ref, xs_hbm, pos_s, csem, rsem):
    del xs_in
    ei = ei_ref[...].astype(f32)
    lane = lax.broadcasted_iota(i32, ei.shape, 1)
    lane_e = lax.broadcasted_iota(i32, (DISP_TM, N_EXPERTS), 1).astype(f32)
    starts = ps_ref[...].astype(f32)
    pos = jnp.zeros(ei.shape, f32)
    for k in range(TOP_K):
        e_k = jnp.sum(jnp.where(lane == k, ei, 0.0), axis=1, keepdims=True)
        st_k = jnp.sum(jnp.where(lane_e == e_k, starts, 0.0), axis=1, keepdims=True)
        pos = jnp.where(lane == k, st_k, pos)
    pos_ref[...] = jnp.where(lane < TOP_K, pos.astype(i32) + rk_ref[...], 0)
    cp = pltpu.make_async_copy(pos_ref, pos_s, csem)
    cp.start()
    cp.wait()

    def tile(i):
        return pl.ds(pl.multiple_of(i * ROW_TILE, ROW_TILE), ROW_TILE)

    def row_copy(t, k, p):
        return pltpu.make_async_copy(h_ref.at[tile(t)], xs_hbm.at[tile(p)], rsem)

    def issue(t, _):
        for k in range(TOP_K):
            row_copy(t, k, pos_s[t, k]).start()
        return 0
    lax.fori_loop(0, DISP_TM, issue, 0)

    def drain(t, _):
        for k in range(TOP_K):
            row_copy(t, k, 0).wait()
        return 0
    lax.fori_loop(0, DISP_TM, drain, 0)


def _dispatch(h2t, eidx, rank, pstarts, n_rows):
    N = h2t.shape[0] // ROW_TILE
    tok = pl.BlockSpec((DISP_TM, LANES), lambda i: (i, 0))
    xs0 = jnp.zeros((n_rows * ROW_TILE, LANES), f32)
    pos, xs = pl.pallas_call(
        _dispatch_kernel, grid=(N // DISP_TM,),
        in_specs=[pl.BlockSpec((DISP_TM * ROW_TILE, LANES), lambda i: (i, 0)), tok, tok,
                  pl.BlockSpec((1, N_EXPERTS), lambda i: (0, 0)), pl.BlockSpec(memory_space=pl.ANY)],
        out_specs=[tok, pl.BlockSpec(memory_space=pl.ANY)],
        out_shape=[jax.ShapeDtypeStruct((N, LANES), i32), jax.ShapeDtypeStruct(xs0.shape, f32)],
        scratch_shapes=[pltpu.SMEM((DISP_TM, LANES), i32), pltpu.SemaphoreType.DMA, pltpu.SemaphoreType.DMA],
        input_output_aliases={4: 1},
        compiler_params=_cparams("arbitrary"), name="moe_dispatch")(h2t, eidx, rank, pstarts, xs0)
    return pos, xs


def _moe_kernel(be_ref, nu_ref, xs_ref, wg_ref, wu_ref, wd_ref, ys_ref):
    i = pl.program_id(0)

    @pl.when(i < nu_ref[0])
    def _():
        def cols(s):
            return xs_ref[pl.ds(s, MOE_ROWS, stride=ROW_TILE), :]

        def lhs(c):
            return jnp.concatenate([cols(2 * c), cols(2 * c + 1)], axis=1).astype(bf16)

        g = jnp.zeros((MOE_ROWS, D_EXPERT), f32)
        u = jnp.zeros((MOE_ROWS, D_EXPERT), f32)
        for c in range(ROW_TILE // 2):
            xc = lhs(c)
            rows = slice(2 * c * LANES, (2 * c + 2) * LANES)
            g = g + jnp.dot(xc, wg_ref[0, 0, rows, :].astype(bf16), preferred_element_type=f32)
            u = u + jnp.dot(xc, wu_ref[0, 0, rows, :].astype(bf16), preferred_element_type=f32)
        y = jnp.dot((_silu(g) * u).astype(bf16), wd_ref[0, 0].astype(bf16), preferred_element_type=f32)
        for s in range(ROW_TILE):
            ys_ref[pl.ds(s, MOE_ROWS, stride=ROW_TILE), :] = y[:, s * LANES:(s + 1) * LANES]

    @pl.when(i >= nu_ref[0])
    def _():
        ys_ref[...] = jnp.zeros_like(ys_ref)


def _moe(layer, blk_e, n_used, xs, wg, wu, wd):
    blk = MOE_ROWS * ROW_TILE
    nb = xs.shape[0] // blk
    wspec = lambda a: pl.BlockSpec((1, 1) + a.shape[2:], lambda i, be, nu: (layer, be[i], 0, 0))
    gs = pltpu.PrefetchScalarGridSpec(
        num_scalar_prefetch=2, grid=(nb,),
        in_specs=[pl.BlockSpec((blk, LANES), lambda i, be, nu: (jnp.minimum(i, nu[0] - 1), 0)),
                  wspec(wg), wspec(wu), wspec(wd)],
        out_specs=pl.BlockSpec((blk, LANES), lambda i, be, nu: (i, 0)))
    return pl.pallas_call(
        _moe_kernel, grid_spec=gs, out_shape=jax.ShapeDtypeStruct(xs.shape, f32),
        compiler_params=_cparams("arbitrary"), name="moe_experts")(blk_e, n_used, xs, wg, wu, wd)


FIN_TM = 128


def _tile_gather(pos_hbm, ys_hbm, pos_s, gbuf, isem, rsem, step, n_steps):
    def idx_copy(s):
        return pltpu.make_async_copy(pos_hbm.at[pl.ds(s * FIN_TM, FIN_TM)], pos_s.at[s % 2], isem.at[s % 2])

    def tile(i):
        return pl.ds(pl.multiple_of(i * ROW_TILE, ROW_TILE), ROW_TILE)

    def row_copy(slot, t, k, p):
        return pltpu.make_async_copy(ys_hbm.at[tile(p)], gbuf.at[slot, k, tile(t)], rsem.at[slot])

    def start_rows(s):
        slot = s % 2

        def body(t, _):
            for k in range(TOP_K):
                row_copy(slot, t, k, pos_s[slot, t, k]).start()
            return 0
        lax.fori_loop(0, FIN_TM, body, 0)

    @pl.when(step == 0)
    def _():
        idx_copy(step).start()
        idx_copy(step).wait()
        start_rows(step)

        @pl.when(n_steps > 1)
        def _():
            idx_copy(step + 1).start()

    @pl.when(step + 1 < n_steps)
    def _():
        idx_copy(step + 1).wait()
        start_rows(step + 1)

        @pl.when(step + 2 < n_steps)
        def _():
            idx_copy(step + 2).start()

    slot = step % 2

    def wait_body(t, _):
        for k in range(TOP_K):
            row_copy(slot, t, k, 0).wait()
        return 0
    lax.fori_loop(0, FIN_TM, wait_body, 0)


def _fin_kernel(pos_hbm, ys_hbm, h2_ref, x1_ref, wt_ref, mod_ref, sg_ref, su_ref, sd_ref, lng_ref, lnb_ref, o_ref,
                pos_s, gbuf, isem, rsem, *, alpha):
    i = pl.program_id(0)
    _tile_gather(pos_hbm, ys_hbm, pos_s, gbuf, isem, rsem, i, pl.num_programs(0))
    hb = h2_ref[...]
    g = jnp.dot(hb, sg_ref[...], preferred_element_type=f32)
    u = jnp.dot(hb, su_ref[...], preferred_element_type=f32)
    shared = jnp.dot((_silu(g) * u).astype(bf16), sd_ref[...], preferred_element_type=f32)
    wt = wt_ref[...]
    lane = lax.broadcasted_iota(i32, wt.shape, 1)
    wk = [jnp.sum(jnp.where(lane == k, wt, 0.0), axis=1, keepdims=True) for k in range(TOP_K)]
    slot = i % 2
    cols = []
    for s in range(ROW_TILE):
        sub = pl.ds(s, FIN_TM, stride=ROW_TILE)
        acc = gbuf[slot, 0, sub, :] * wk[0]
        for k in range(1, TOP_K):
            acc = acc + gbuf[slot, k, sub, :] * wk[k]
        cols.append(acc)
    y = shared + jnp.concatenate(cols, axis=1)
    o_ref[...] = _layer_norm(alpha * x1_ref[...] + mod_ref[0, 5:6, :] * y, lng_ref[...], lnb_ref[...])


def _fin(pos, ys, h2b, x1, wts, mod, sg, su, sd, lng, lnb, alpha, B, T):
    N = B * T
    nt = T // FIN_TM
    tok = lambda w_: pl.BlockSpec((FIN_TM, w_), lambda i: (i, 0))
    full = lambda a: pl.BlockSpec(a.shape, lambda i: (0,) * a.ndim)
    return pl.pallas_call(
        functools.partial(_fin_kernel, alpha=alpha), grid=(N // FIN_TM,),
        in_specs=[pl.BlockSpec(memory_space=pl.ANY), pl.BlockSpec(memory_space=pl.ANY),
                  tok(D_MODEL), tok(D_MODEL), tok(LANES),
                  pl.BlockSpec((1, 6, D_MODEL), lambda i: (i // nt, 0, 0)),
                  full(sg), full(su), full(sd), full(lng), full(lnb)],
        out_specs=tok(D_MODEL), out_shape=jax.ShapeDtypeStruct((N, D_MODEL), f32),
        scratch_shapes=[pltpu.SMEM((2, FIN_TM, LANES), i32), pltpu.VMEM((2, TOP_K, FIN_TM * ROW_TILE, LANES), f32),
                        pltpu.SemaphoreType.DMA((2,)), pltpu.SemaphoreType.DMA((2,))],
        compiler_params=_cparams("arbitrary"), name="combine")(pos, ys, h2b, x1, wts, mod, sg, su, sd, lng, lnb)


def _pack_w_in(w):
    c = lambda name, width: w[:, _OFF[name]:_OFF[name] + width]
    cols = [c("moba_q", 256), c("moba_k", 256), c("moba_v", 256), c("nsa_q", 256),
            c("k_slc", 64), c("k_win", 64), c("v_slc", 64), c("v_win", 64), c("k_cmp", 64), c("v_cmp", 64),
            c("pool", 256), c("gla_q", 256), c("gla_k", 256), c("gla_v", 256), c("gla_g", 256),
            c("nsa_gate", 12), c("gla_a", 16), jnp.zeros((w.shape[0], LANES - 28), w.dtype)]
    return jnp.concatenate(cols, axis=1).astype(bf16)


def _rope_tables(T):
    half = ROPE_DIM // 2
    inv_freq = ROPE_THETA ** (-jnp.arange(half, dtype=f32) / half)
    ang = jnp.arange(T).astype(f32)[:, None] * inv_freq[None, :]
    cos, sin = jnp.cos(ang), jnp.sin(ang)
    one = jnp.ones((T, HEAD_DIM - ROPE_DIM), f32)
    zero = jnp.zeros((T, HEAD_DIM - ROPE_DIM), f32)
    ct = jnp.concatenate([cos, cos, one], axis=1)
    st = jnp.concatenate([-sin, sin, zero], axis=1)
    return jnp.tile(ct, (1, 2)), jnp.tile(st, (1, 2))


def _cmp_weights(pe, w1, w2):
    half = CMP_BLOCK // 2
    z = jnp.zeros((half, HEAD_DIM, CMP_HIDDEN), f32)

    def arrange(lo):
        wk = w1[0].reshape(CMP_BLOCK, HEAD_DIM, CMP_HIDDEN)[lo:lo + half]
        wv = w1[1].reshape(CMP_BLOCK, HEAD_DIM, CMP_HIDDEN)[lo:lo + half]
        top = jnp.concatenate([wk, z], axis=2)
        bot = jnp.concatenate([z, wv], axis=2)
        return jnp.concatenate([top, bot], axis=1).reshape(half * 2 * HEAD_DIM, 2 * CMP_HIDDEN).astype(bf16)

    def pe_row(lo):
        return jnp.concatenate([pe[0, lo:lo + half], pe[1, lo:lo + half]], axis=1).reshape(1, half * 2 * HEAD_DIM)

    zc = jnp.zeros((CMP_HIDDEN, LANES - HEAD_DIM), f32)
    zr = jnp.zeros((CMP_HIDDEN, LANES), f32)
    w2k = jnp.concatenate([jnp.concatenate([w2[0], zc], axis=1), zr], axis=0).astype(bf16)
    w2v = jnp.concatenate([zr, jnp.concatenate([w2[1], zc], axis=1)], axis=0).astype(bf16)
    return arrange(0), arrange(half), pe_row(0), pe_row(half), w2k, w2v


def _overlap_matrix(ns):
    n = jnp.arange(ns)[:, None] * CMP_STRIDE
    j = jnp.arange(LANES)[None, :] * SLC_BLOCK
    ov = (n < j + SLC_BLOCK) & (n + CMP_BLOCK > j) & (jnp.arange(ns)[:, None] < ns - 1)
    return ov.astype(bf16)


def _block_diag_ones():
    h = jnp.arange(256) // HEAD_DIM
    return (h[:, None] == h[None, :]).astype(bf16)


def _expert_layout(counts, n_tok):
    counts = counts.reshape(N_EXPERTS)
    padded = (counts + MOE_ROWS - 1) // MOE_ROWS * MOE_ROWS
    pstarts = (jnp.cumsum(padded) - padded).astype(i32)
    n_blocks = n_tok * TOP_K // MOE_ROWS + N_EXPERTS
    blk_e = jnp.clip(jnp.searchsorted(pstarts, jnp.arange(n_blocks) * MOE_ROWS, side='right') - 1,
                     0, N_EXPERTS - 1).astype(i32)
    n_used = (jnp.sum(padded) // MOE_ROWS).astype(i32).reshape(1)
    return pstarts.reshape(1, N_EXPERTS), blk_e, n_used, n_blocks * MOE_ROWS


def _mixer_inputs(x2, mod, w_in, B, T):
    ct, st = _rope_tables(T)
    return _inproj(x2, mod, _pack_w_in(w_in), ct, st, B, T)


def _token_mixers(x2, mod, w_in, cmp_pe, cmp_w1, cmp_w2, pool_w, pool_scale, gla_wa, gla_ba, gla_norm, B, T):
    N = B * T
    (mq, mk, mv, km, nq, nqr, ks, kw, vs, vw, kvc, pool_u, gq, gk, gv, gg, misc) = _mixer_inputs(x2, mod, w_in, B, T)
    nt = T // MOBA_BLOCK
    kmh = km.reshape(B, nt, N_HEADS, HEAD_DIM).transpose(0, 2, 1, 3)
    kmf = jnp.pad(kmh, ((0, 0), (0, 0), (HEAD_DIM, LANES - HEAD_DIM - nt), (0, LANES - HEAD_DIM)))
    mo = _moba(mq, mk, mv, kmf, B, T)
    ns = T // CMP_STRIDE
    kc, vc = _cmp(kvc.reshape(B, ns, CMP_STRIDE * LANES), *_cmp_weights(cmp_pe, cmp_w1, cmp_w2))
    no = _nsa(nq, nqr, misc, ks, vs, kw, vw, kc, vc, _overlap_matrix(ns), B, T)
    wbd = jax.scipy.linalg.block_diag(*[pool_w[g] for g in range(len(POOL_WINDOWS))]).astype(bf16)
    po = _pool(pool_u, wbd, pool_scale.reshape(1, 256), B, T)
    wa = jnp.zeros((LANES, 256), f32).at[MISC_A0:MISC_A0 + GLA_LOWRANK].set(gla_wa)
    go = _gla(gq, gk, gv, gg, misc, wa, gla_ba.reshape(1, 256), jnp.tile(gla_norm, N_HEADS).reshape(1, 256),
              _block_diag_ones(), B, T)
    return mo, no, po, go


def _pad_w_out(w_out):
    wm = w_out[:GROUP_WIDTH].reshape(N_HEADS, HEAD_DIM, D_MODEL)
    wm = jnp.pad(wm, ((0, 0), (0, LANES - HEAD_DIM), (0, 0))).reshape(N_HEADS * LANES, D_MODEL)
    return jnp.concatenate([wm, w_out[GROUP_WIDTH:]], axis=0).astype(bf16)


def kernel(x, c, w_ada, b_ada, w_in, cmp_pe, cmp_w1, cmp_w2, pool_w, pool_scale, gla_wa, gla_ba, gla_norm, w_out,
           ln_g, ln_b, w_router, router_bias, exp_gate, exp_up, exp_down, sh_gate, sh_up, sh_down):
    B, T, D = x.shape
    N = B * T
    depth = w_ada.shape[0]
    alpha = float((2 * depth) ** 0.25)
    x2 = x.reshape(N, D)
    c8 = jnp.zeros((8, D), f32).at[:B].set(c)
    for l in range(depth):
        mod = _ada(c8, w_ada[l], b_ada[l].reshape(1, -1))[:B].reshape(B, 6, D)
        mo, no, po, go = _token_mixers(x2, mod, w_in[l], cmp_pe[l], cmp_w1[l], cmp_w2[l], pool_w[l], pool_scale[l],
                                       gla_wa[l], gla_ba[l], gla_norm[l], B, T)
        wrh, wrl = _split_bf16(w_router[l])
        x1, h2t, h2b, logits = _outproj(mo, no, po, go, x2, mod, _pad_w_out(w_out[l]), ln_g[l, 0].reshape(1, D),
                                        ln_b[l, 0].reshape(1, D), wrh, wrl, alpha, B, T)
        eidx, wts, rank, counts = _route(logits, router_bias[l].reshape(1, N_EXPERTS))
        pstarts, blk_e, n_used, n_rows = _expert_layout(counts, N)
        pos, xs = _dispatch(h2t, eidx, rank, pstarts, n_rows)
        ys = _moe(l, blk_e, n_used, xs, exp_gate, exp_up, exp_down)
        x2 = _fin(pos, ys, h2b, x1, wts, mod, sh_gate[l].astype(bf16), sh_up[l].astype(bf16),
                  sh_down[l].astype(bf16), ln_g[l, 1].reshape(1, D), ln_b[l, 1].reshape(1, D), alpha, B, T)
    return x2.reshape(B, T, D)
```

```python
import functools

import jax
import jax.numpy as jnp
from jax import lax
from jax.experimental import pallas as pl
from jax.experimental.pallas import tpu as pltpu

f32, bf16, i32, u32 = jnp.float32, jnp.bfloat16, jnp.int32, jnp.uint32

D_MODEL = 1024
HEAD_DIM = 64
N_HEADS = 4
GROUP_WIDTH = 256
ROPE_THETA = 500000.0
ROPE_DIM = 16
MOBA_BLOCK = 256
MOBA_TOPK = 3
CMP_BLOCK = 32
CMP_STRIDE = 16
CMP_HIDDEN = 128
SLC_BLOCK = 64
SLC_TOPN = 16
WIN = 512
FORCE_SCORE = 1e9
POOL_WINDOWS = (2, 4, 8, 16)
GLA_SUB = 16
GLA_LOWRANK = 16
GLA_TAU = 16.0
N_EXPERTS = 256
TOP_K = 8
N_EXPERT_GROUPS = 8
TOPK_GROUPS = 4
D_EXPERT = 256
ROUTED_SCALE = 2.5
LN_EPS = 1e-5
QK_SCALE = HEAD_DIM ** -0.5
LOG2E = 1.4426950408889634
Q_SCALE = QK_SCALE * LOG2E

LANES = 128
MASKED = -1e30
M_INIT = -3e38
BIG_IDX = 1 << 20
MOE_ROWS = 256
ROW_TILE = D_MODEL // (2 * LANES)
VMEM_LIMIT = 56 * 1024 * 1024

_OFF = dict(moba_q=0, moba_k=256, moba_v=512, nsa_q=768, k_cmp=1024, v_cmp=1088, k_slc=1152, v_slc=1216,
            k_win=1280, v_win=1344, nsa_gate=1408, pool=1420, gla_q=1676, gla_k=1932, gla_v=2188,
            gla_a=2444, gla_g=2460)
_S = dict(mq=0, mk=256, mv=512, nq=768, sw=1024, vsw=1152, kvc=1280, pool=1408, gq=1664, gk=1920, gv=2176,
          gg=2432, misc=2688)
IN_COLS_PACKED = 2816
MISC_GATE0 = 0
MISC_A0 = 12

NT = (((1,), (1,)), ((), ()))
TN = (((0,), (0,)), ((), ()))


def _cparams(*sem):
    return pltpu.CompilerParams(dimension_semantics=sem, vmem_limit_bytes=VMEM_LIMIT)


def _silu(x):
    return x * jax.nn.sigmoid(x)


def _split_bf16(x):
    hi = x.astype(bf16)
    lo = (x - hi.astype(f32)).astype(bf16)
    return hi, lo


def _pack_rows(x):
    bits = lambda v: lax.bitcast_convert_type(v.astype(bf16).astype(f32), u32)
    return [(bits(x[:, (2 * s) * LANES:(2 * s + 1) * LANES]) >> 16) | bits(x[:, (2 * s + 1) * LANES:(2 * s + 2) * LANES])
            for s in range(ROW_TILE)]


def _unpack_word(w):
    return (lax.bitcast_convert_type(w << 16, f32), lax.bitcast_convert_type(w & jnp.uint32(0xFFFF0000), f32))


def _layer_norm(z, g, b):
    mu = jnp.mean(z, axis=-1, keepdims=True)
    zc = z - mu
    var = jnp.mean(zc * zc, axis=-1, keepdims=True)
    return zc * lax.rsqrt(var + LN_EPS) * g + b


def _argmax_rounds(score, index, rounds, axis=1):
    picked = jnp.zeros(score.shape, f32)
    for _ in range(rounds):
        mx = jnp.max(score, axis=axis, keepdims=True)
        first = jnp.min(jnp.where(score == mx, index, BIG_IDX), axis=axis, keepdims=True)
        hit = index == first
        picked = jnp.where(hit, 1.0, picked)
        score = jnp.where(hit, -jnp.inf, score)
    return picked


def _softmax_steps_t(s_list, carries, v_list):
    m_new = [jnp.maximum(c[0], jnp.max(s, axis=0, keepdims=True)) for s, c in zip(s_list, carries)]
    p = [jnp.exp2(s - m) for s, m in zip(s_list, m_new)]
    pv = [jnp.dot(v, pi.astype(bf16), preferred_element_type=f32) for v, pi in zip(v_list, p)]
    out = []
    for c, m, pi, pvi in zip(carries, m_new, p, pv):
        alpha = jnp.exp2(c[0] - m)
        out.append((m, alpha * c[1] + jnp.sum(pi, axis=0, keepdims=True), alpha * c[2] + pvi))
    return tuple(out)


def _softmax_init_t(groups, queries):
    return tuple((jnp.full((1, queries), M_INIT, f32), jnp.zeros((1, queries), f32),
                  jnp.zeros((LANES, queries), f32)) for _ in range(groups))


def _ada_kernel(c_ref, w_ref, b_ref, o_ref):
    o_ref[...] = jnp.dot(_silu(c_ref[...]), w_ref[...], preferred_element_type=f32,
                         precision=lax.Precision.HIGHEST) + b_ref[...]


def _ada(c8, w, b):
    n = w.shape[1] // D_MODEL
    return pl.pallas_call(
        _ada_kernel, grid=(n,),
        in_specs=[pl.BlockSpec((8, D_MODEL), lambda j: (0, 0)),
                  pl.BlockSpec((D_MODEL, D_MODEL), lambda j: (0, j)),
                  pl.BlockSpec((1, D_MODEL), lambda j: (0, j))],
        out_specs=pl.BlockSpec((8, D_MODEL), lambda j: (0, j)),
        out_shape=jax.ShapeDtypeStruct((8, w.shape[1]), f32),
        compiler_params=_cparams("arbitrary"), name="ada")(c8, w, b)


IN_TM = MOBA_BLOCK


def _inproj_kernel(x_ref, mod_ref, w_ref, ct_ref, st_ref,
                   mq_ref, mk_ref, mv_ref, km_ref, nq_ref, nqr_ref, ks_ref, kw_ref, vs_ref, vw_ref,
                   kvc_ref, pool_ref, gq_ref, gk_ref, gv_ref, gg_ref, misc_ref, *, nt):
    tb = pl.program_id(0) % nt
    h = (x_ref[...] * (1.0 + mod_ref[0, 1:2, :]) + mod_ref[0, 0:1, :]).astype(bf16)

    def seg(name, width):
        a = _S[name]
        return jnp.dot(h, w_ref[:, a:a + width], preferred_element_type=f32)

    ct, st = ct_ref[...], st_ref[...]
    lane = lax.broadcasted_iota(i32, (IN_TM, LANES), 1)
    first8 = (lane % HEAD_DIM) < ROPE_DIM // 2
    half = lane < HEAD_DIM

    def rope128(y):
        partner = jnp.where(first8, pltpu.roll(y, LANES - 8, axis=1), pltpu.roll(y, 8, axis=1))
        return y * ct + partner * st

    def rope(y):
        return jnp.concatenate([rope128(y[:, c * LANES:(c + 1) * LANES]) for c in range(y.shape[1] // LANES)],
                               axis=1)

    def lo_half(y):
        return jnp.where(half, y, 0.0)

    def hi_half(y):
        return jnp.where(half, pltpu.roll(y, HEAD_DIM, axis=1), 0.0)

    def per_head(y):
        parts = []
        for c in range(2):
            yc = y[:, c * LANES:(c + 1) * LANES]
            parts += [lo_half(yc), hi_half(yc)]
        return jnp.concatenate(parts, axis=1)

    mq_ref[...] = per_head(rope(seg("mq", 256)) * Q_SCALE).astype(bf16)
    k = rope(seg("mk", 256))
    km_ref[0] = jnp.mean(k, axis=0, keepdims=True)
    lane4 = lax.broadcasted_iota(i32, (IN_TM, 4 * LANES), 1)
    mk_ref[...] = jnp.where((lane4 % LANES) == HEAD_DIM + tb, 1.0, per_head(k)).astype(bf16)
    mv_ref[...] = per_head(seg("mv", 256)).astype(bf16)
    q = seg("nq", 256) * Q_SCALE
    nq_ref[...] = per_head(q).astype(bf16)
    nqr_ref[...] = per_head(rope(q)).astype(bf16)
    sw = rope128(seg("sw", 128))
    row = lax.broadcasted_iota(i32, (IN_TM, LANES), 0)
    slc_id = tb * (IN_TM // SLC_BLOCK) + row // SLC_BLOCK
    ks_ref[...] = jnp.concatenate([lo_half(sw), jnp.where(lane == slc_id, 1.0, 0.0)], axis=1).astype(bf16)
    kw_ref[...] = hi_half(sw).astype(bf16)
    vsw = seg("vsw", 128)
    vs_ref[...] = lo_half(vsw).astype(bf16)
    vw_ref[...] = hi_half(vsw).astype(bf16)
    kvc_ref[...] = seg("kvc", 128).astype(bf16)
    pool_ref[...] = seg("pool", 256)
    gq_ref[...] = seg("gq", 256)
    gk_ref[...] = seg("gk", 256)
    gv_ref[...] = seg("gv", 256)
    gg_ref[...] = seg("gg", 256)
    misc_ref[...] = seg("misc", 128)


def _inproj(x2, mod, w, ct, st, B, T):
    N = B * T
    nt = T // IN_TM
    assert T % IN_TM == 0 and nt <= 32 and T // SLC_BLOCK <= LANES
    row = lambda w_, dt: (jax.ShapeDtypeStruct((N, w_), dt), pl.BlockSpec((IN_TM, w_), lambda i: (i, 0)))
    outs = [row(512, bf16), row(512, bf16), row(512, bf16),
            (jax.ShapeDtypeStruct((N // IN_TM, 1, 256), f32), pl.BlockSpec((1, 1, 256), lambda i: (i, 0, 0))),
            row(512, bf16), row(512, bf16), row(256, bf16), row(128, bf16), row(128, bf16), row(128, bf16),
            row(128, bf16), row(256, f32), row(256, f32), row(256, f32), row(256, f32), row(256, f32),
            row(128, f32)]
    return pl.pallas_call(
        functools.partial(_inproj_kernel, nt=nt), grid=(N // IN_TM,),
        in_specs=[pl.BlockSpec((IN_TM, D_MODEL), lambda i: (i, 0)),
                  pl.BlockSpec((1, 6, D_MODEL), lambda i: (i // nt, 0, 0)),
                  pl.BlockSpec((D_MODEL, IN_COLS_PACKED), lambda i: (0, 0), pipeline_mode=pl.Buffered(1)),
                  pl.BlockSpec((IN_TM, LANES), lambda i: (i % nt, 0)),
                  pl.BlockSpec((IN_TM, LANES), lambda i: (i % nt, 0))],
        out_specs=[o[1] for o in outs], out_shape=[o[0] for o in outs],
        compiler_params=_cparams("arbitrary"), name="inproj")(x2, mod, w, ct, st)


def _cmp_kernel(x_ref, wa_ref, wb_ref, pea_ref, peb_ref, w2k_ref, w2v_ref, kc_ref, vc_ref):
    x = x_ref[0].astype(f32)
    a = jnp.dot((x + pea_ref[...]).astype(bf16), wa_ref[...], preferred_element_type=f32)
    b = jnp.dot((x + peb_ref[...]).astype(bf16), wb_ref[...], preferred_element_type=f32)
    hid = a + pltpu.roll(b, x.shape[0] - 1, axis=0)
    g = jax.nn.gelu(hid).astype(bf16)
    kc_ref[0] = jnp.dot(g, w2k_ref[...], preferred_element_type=f32).astype(bf16)
    vc_ref[0] = jnp.dot(g, w2v_ref[...], preferred_element_type=f32).astype(bf16)


def _cmp(xseg, wa, wb, pea, peb, w2k, w2v):
    B, ns, wd = xseg.shape
    full = lambda a: pl.BlockSpec(a.shape, lambda b: (0,) * a.ndim)
    return pl.pallas_call(
        _cmp_kernel, grid=(B,),
        in_specs=[pl.BlockSpec((1, ns, wd), lambda b: (b, 0, 0))] + [full(a) for a in (wa, wb, pea, peb, w2k, w2v)],
        out_specs=[pl.BlockSpec((1, ns, LANES), lambda b: (b, 0, 0))] * 2,
        out_shape=[jax.ShapeDtypeStruct((B, ns, LANES), bf16)] * 2,
        compiler_params=_cparams("arbitrary"), name="nsa_compress")(xseg, wa, wb, pea, peb, w2k, w2v)


MOBA_NBLK = 32
KEY_TILE = 512


def _moba_kernel(q_ref, k_ref, vt_ref, km_ref, o_ref):
    own = pl.program_id(1)
    heads = range(N_HEADS)
    hl = lambda h: slice(h * LANES, (h + 1) * LANES)
    blk = lax.broadcasted_iota(i32, (MOBA_NBLK, MOBA_BLOCK), 0)
    past = blk < own
    zeros = lambda n: jnp.zeros((n, MOBA_BLOCK), f32)
    qf = []
    for h in heads:
        qh = q_ref[:, hl(h)]
        hi, lo = _split_bf16(km_ref[0, h])
        gate_t = (lax.dot_general(hi, qh, NT, preferred_element_type=f32)
                  + lax.dot_general(lo, qh, NT, preferred_element_type=f32))
        picked = _argmax_rounds(jnp.where(past, gate_t, -jnp.inf), blk, MOBA_TOPK, axis=0)
        allowed = ((picked > 0.0) & past) | (blk == own)
        bias_t = jnp.concatenate([zeros(HEAD_DIM), jnp.where(allowed, 0.0, MASKED),
                                  zeros(LANES - HEAD_DIM - MOBA_NBLK)], axis=0)
        qf.append(qh + jnp.transpose(bias_t).astype(bf16))

    def scores_t(h, off):
        return lax.dot_general(k_ref[pl.ds(off, KEY_TILE), hl(h)], qf[h], NT, preferred_element_type=f32)

    def body(p, carry):
        off = pl.multiple_of(p * KEY_TILE, KEY_TILE)
        return _softmax_steps_t([scores_t(h, off) for h in heads], carry, [vt_ref[0, p, hl(h), :] for h in heads])

    last = own // (KEY_TILE // MOBA_BLOCK)
    carry = lax.fori_loop(0, last, body, _softmax_init_t(N_HEADS, MOBA_BLOCK))
    off = pl.multiple_of(last * KEY_TILE, KEY_TILE)
    kpos = off + lax.broadcasted_iota(i32, (KEY_TILE, MOBA_BLOCK), 0)
    qpos = own * MOBA_BLOCK + lax.broadcasted_iota(i32, (KEY_TILE, MOBA_BLOCK), 1)
    carry = _softmax_steps_t([jnp.where(kpos <= qpos, scores_t(h, off), MASKED) for h in heads], carry,
                             [vt_ref[0, last, hl(h), :] for h in heads])
    for h in heads:
        o_ref[:, hl(h)] = jnp.transpose(carry[h][2] / carry[h][1]).astype(bf16)


def _blocks_t(v, B, T, blk):
    return v.reshape(B, T // blk, blk, v.shape[1]).transpose(0, 1, 3, 2)


def _moba(mq, mk, mv, km, B, T):
    N = B * T
    nt = T // MOBA_BLOCK
    wd = N_HEADS * LANES
    assert T % KEY_TILE == 0 and nt <= MOBA_NBLK
    return pl.pallas_call(
        _moba_kernel, grid=(B, nt),
        in_specs=[pl.BlockSpec((MOBA_BLOCK, wd), lambda b, i: (b * nt + i, 0)),
                  pl.BlockSpec((T, wd), lambda b, i: (b, 0), pipeline_mode=pl.Buffered(1)),
                  pl.BlockSpec((1, T // KEY_TILE, wd, KEY_TILE), lambda b, i: (b, 0, 0, 0),
                               pipeline_mode=pl.Buffered(1)),
                  pl.BlockSpec((1, N_HEADS, MOBA_NBLK, LANES), lambda b, i: (b, 0, 0, 0))],
        out_specs=pl.BlockSpec((MOBA_BLOCK, wd), lambda b, i: (b * nt + i, 0)),
        out_shape=jax.ShapeDtypeStruct((N, wd), bf16),
        compiler_params=_cparams("arbitrary", "arbitrary"), name="moba")(mq, mk, _blocks_t(mv, B, T, KEY_TILE), km)


NSA_TQ = 256
NSA_KB = 256


def _stack_heads(ref):
    return jnp.concatenate([ref[:, h * LANES:(h + 1) * LANES] for h in range(N_HEADS)], axis=0)


def _nsa_select_kernel(nq_ref, kc_ref, vc_ref, ov_ref, oc_ref, selb_ref, *, n_cmp):
    c = pl.program_id(1)
    q4 = _stack_heads(nq_ref)
    qpos4 = c * NSA_TQ + lax.broadcasted_iota(i32, (N_HEADS * NSA_TQ, 1), 0) % NSA_TQ
    s = lax.dot_general(q4, kc_ref[0], NT, preferred_element_type=f32)
    n = lax.broadcasted_iota(i32, s.shape, 1)
    ok = (n * CMP_STRIDE + (CMP_BLOCK - 1) <= qpos4) & (n < n_cmp)
    s = jnp.where(ok, s, -jnp.inf)
    m = jnp.max(s, axis=1, keepdims=True)
    m = jnp.where(m > -jnp.inf, m, 0.0)
    e = jnp.where(ok, jnp.exp2(s - m), 0.0)
    p_c = e / jnp.maximum(jnp.sum(e, axis=1, keepdims=True), 1e-30)
    o_c = jnp.dot(p_c.astype(bf16), vc_ref[0], preferred_element_type=f32)
    for h in range(N_HEADS):
        oc_ref[:, h * LANES:(h + 1) * LANES] = o_c[h * NSA_TQ:(h + 1) * NSA_TQ].astype(bf16)
    p_sum = (p_c[0:NSA_TQ] + p_c[NSA_TQ:2 * NSA_TQ]) + (p_c[2 * NSA_TQ:3 * NSA_TQ] + p_c[3 * NSA_TQ:])
    hi, lo = _split_bf16(p_sum)
    imp = jnp.dot(hi, ov_ref[...], preferred_element_type=f32) + jnp.dot(lo, ov_ref[...], preferred_element_type=f32)
    j = lax.broadcasted_iota(i32, imp.shape, 1)
    cur = (c * NSA_TQ + lax.broadcasted_iota(i32, (NSA_TQ, 1), 0)) // SLC_BLOCK
    forced = (j == 0) | (j == cur) | (j == cur - 1)
    valid = j <= cur
    score = jnp.where(valid, jnp.where(forced, FORCE_SCORE, imp), -jnp.inf)
    chosen = (_argmax_rounds(score, j, SLC_TOPN) > 0.0) & valid
    selb_ref[...] = jnp.where(chosen, 0.0, MASKED).astype(bf16)


def _nsa_attend_kernel(nqr_ref, selb_ref, oc_ref, misc_ref, ks_ref, vst_ref, kw_ref, vwt_ref, o_ref):
    c = pl.program_id(1)
    rows = N_HEADS * NSA_TQ
    heads = range(N_HEADS)
    q4r = _stack_heads(nqr_ref)
    qpos_t = c * NSA_TQ + lax.broadcasted_iota(i32, (1, rows), 1) % NSA_TQ
    selb = selb_ref[...]
    lhs = [jnp.concatenate([q4r[h * NSA_TQ:(h + 1) * NSA_TQ], selb], axis=1) for h in heads]

    def scores_t(off):
        kb = ks_ref[pl.ds(off, KEY_TILE), :]
        return [lax.dot_general(kb, lhs[h], NT, preferred_element_type=f32) for h in heads]

    def body(p, carry):
        return _softmax_steps_t(scores_t(pl.multiple_of(p * KEY_TILE, KEY_TILE)), carry, [vst_ref[0, p]] * N_HEADS)

    last = (c * NSA_TQ) // KEY_TILE
    carry = lax.fori_loop(0, last, body, _softmax_init_t(N_HEADS, NSA_TQ))
    off = pl.multiple_of(last * KEY_TILE, KEY_TILE)
    kpos = off + lax.broadcasted_iota(i32, (KEY_TILE, NSA_TQ), 0)
    qpos = c * NSA_TQ + lax.broadcasted_iota(i32, (KEY_TILE, NSA_TQ), 1)
    carry = _softmax_steps_t([jnp.where(kpos <= qpos, s, MASKED) for s in scores_t(off)], carry,
                             [vst_ref[0, last]] * N_HEADS)
    o_s = jnp.concatenate([jnp.transpose(carry[h][2] / carry[h][1]) for h in heads], axis=0)

    nwb = WIN // NSA_KB + 1
    sb = jnp.maximum(c - (nwb - 1), 0)
    start = pl.multiple_of(sb * NSA_KB, NSA_KB)
    s_w = lax.dot_general(kw_ref[pl.ds(start, nwb * NSA_KB), :], q4r, NT, preferred_element_type=f32)
    wpos = start + lax.broadcasted_iota(i32, s_w.shape, 0)
    s_w = jnp.where((wpos <= qpos_t) & (wpos > qpos_t - WIN), s_w, -jnp.inf)
    e_w = jnp.exp2(s_w - jnp.max(s_w, axis=0, keepdims=True))
    p_w = (e_w / jnp.sum(e_w, axis=0, keepdims=True)).astype(bf16)
    o_w_t = jnp.dot(vwt_ref[0, sb], p_w[0:NSA_KB], preferred_element_type=f32)
    for i in range(1, nwb):
        o_w_t = o_w_t + jnp.dot(vwt_ref[0, sb + i], p_w[i * NSA_KB:(i + 1) * NSA_KB], preferred_element_type=f32)
    o_w = jnp.transpose(o_w_t)

    gates = jax.nn.sigmoid(misc_ref[...])
    gl = lax.broadcasted_iota(i32, gates.shape, 1)

    def gate_col(g):
        return jnp.concatenate([jnp.sum(jnp.where(gl == MISC_GATE0 + 3 * h + g, gates, 0.0), axis=1, keepdims=True)
                                for h in range(N_HEADS)], axis=0)

    out = gate_col(0) * _stack_heads(oc_ref).astype(f32) + gate_col(1) * o_s + gate_col(2) * o_w
    pair = lambda a, b: a + pltpu.roll(b, HEAD_DIM, axis=1)
    o_ref[...] = jnp.concatenate([pair(out[0:NSA_TQ], out[NSA_TQ:2 * NSA_TQ]),
                                  pair(out[2 * NSA_TQ:3 * NSA_TQ], out[3 * NSA_TQ:])], axis=1).astype(bf16)


def _nsa(nq, nqr, misc, ks, vs, kw, vw, kc, vc, ov, B, T):
    N = B * T
    nc = T // NSA_TQ
    ns = kc.shape[1]
    assert T >= NSA_TQ + WIN and T % KEY_TILE == 0 and KEY_TILE % NSA_TQ == 0 and NSA_TQ == NSA_KB
    tok = lambda w_: pl.BlockSpec((NSA_TQ, w_), lambda b, c: (b * nc + c, 0))
    seq = lambda w_: pl.BlockSpec((T, w_), lambda b, c: (b, 0), pipeline_mode=pl.Buffered(1))
    cmp_spec = pl.BlockSpec((1, ns, LANES), lambda b, c: (b, 0, 0))
    o_c, selb = pl.pallas_call(
        functools.partial(_nsa_select_kernel, n_cmp=(T - CMP_BLOCK) // CMP_STRIDE + 1), grid=(B, nc),
        in_specs=[tok(512), cmp_spec, cmp_spec, pl.BlockSpec(ov.shape, lambda b, c: (0, 0))],
        out_specs=[tok(512), tok(LANES)],
        out_shape=[jax.ShapeDtypeStruct((N, 512), bf16), jax.ShapeDtypeStruct((N, LANES), bf16)],
        compiler_params=_cparams("arbitrary", "arbitrary"), name="nsa_select")(nq, kc, vc, ov)
    seq_t = lambda kb: pl.BlockSpec((1, T // kb, LANES, kb), lambda b, c: (b, 0, 0, 0), pipeline_mode=pl.Buffered(1))
    return pl.pallas_call(
        _nsa_attend_kernel, grid=(B, nc),
        in_specs=[tok(512), tok(LANES), tok(512), tok(LANES), seq(256), seq_t(KEY_TILE), seq(LANES), seq_t(NSA_KB)],
        out_specs=tok(256), out_shape=jax.ShapeDtypeStruct((N, 256), bf16),
        compiler_params=_cparams("arbitrary", "arbitrary"), name="nsa_attend")(
            nqr, selb, o_c, misc, ks, _blocks_t(vs, B, T, KEY_TILE), kw, _blocks_t(vw, B, T, NSA_KB))


POOL_TM = 512
POOL_HALO = 16


def _pool_kernel(u_ref, halo_ref, w_ref, sc_ref, o_ref):
    t = pl.program_id(1)
    halo = jnp.where(t == 0, 0.0, halo_ref[...])
    ext = jnp.concatenate([halo, u_ref[...]], axis=0)
    s2 = ext + pltpu.roll(ext, 1, axis=0)
    s4 = s2 + pltpu.roll(s2, 2, axis=0)
    s8 = s4 + pltpu.roll(s4, 4, axis=0)
    s16 = s8 + pltpu.roll(s8, 8, axis=0)
    pos1 = jnp.maximum(t * POOL_TM - POOL_HALO + 1 + lax.broadcasted_iota(i32, ext.shape, 0), 1).astype(f32)
    grp = lax.broadcasted_iota(i32, ext.shape, 1) // HEAD_DIM
    mean = jnp.where(grp == 0, s2 / jnp.minimum(pos1, 2.0),
                     jnp.where(grp == 1, s4 / jnp.minimum(pos1, 4.0),
                               jnp.where(grp == 2, s8 / jnp.minimum(pos1, 8.0), s16 / jnp.minimum(pos1, 16.0))))
    pooled = (mean - ext)[POOL_HALO:, :]
    o_ref[...] = (jnp.dot(pooled.astype(bf16), w_ref[...], preferred_element_type=f32) * sc_ref[...]).astype(bf16)


def _pool(u, wbd, scale, B, T):
    N = B * T
    tm = min(POOL_TM, T)
    assert tm == POOL_TM and T % POOL_TM == 0
    nt = T // tm
    hb = tm // POOL_HALO
    return pl.pallas_call(
        _pool_kernel, grid=(B, nt),
        in_specs=[pl.BlockSpec((tm, 256), lambda b, t: (b * nt + t, 0)),
                  pl.BlockSpec((POOL_HALO, 256), lambda b, t: (jnp.maximum((b * nt + t) * hb - 1, 0), 0)),
                  pl.BlockSpec((256, 256), lambda b, t: (0, 0)),
                  pl.BlockSpec((1, 256), lambda b, t: (0, 0))],
        out_specs=pl.BlockSpec((tm, 256), lambda b, t: (b * nt + t, 0)),
        out_shape=jax.ShapeDtypeStruct((N, 256), bf16),
        compiler_params=_cparams("arbitrary", "arbitrary"), name="pool")(u, u, wbd, scale)


GLA_TM = 256


def _gla_kernel(q_ref, k_ref, v_ref, g_ref, misc_ref, wa_ref, ba_ref, gn_ref, bd_ref, o_ref,
                st_ref, q_s, k_s, v_s, b_s, qe_s, ke_s, gam_s, o_s):
    @pl.when(pl.program_id(1) == 0)
    def _():
        st_ref[...] = jnp.zeros_like(st_ref)

    x = jnp.dot(misc_ref[...], wa_ref[...], preferred_element_type=f32, precision=lax.Precision.HIGHEST) + ba_ref[...]
    log_a = (jnp.minimum(x, 0.0) - jnp.log1p(jnp.exp(-jnp.abs(x)))) / GLA_TAU
    r16 = lax.broadcasted_iota(i32, log_a.shape, 0) % GLA_SUB
    b = log_a
    for s in (1, 2, 4, 8):
        b = b + jnp.where(r16 >= s, pltpu.roll(b, s, axis=0), 0.0)
    b_end = jnp.where(r16 == GLA_SUB - 1, b, 0.0)
    for s in (1, 2, 4, 8):
        b_end = b_end + pltpu.roll(b_end, GLA_TM - s, axis=0)
    q = q_ref[...] * QK_SCALE
    k = k_ref[...]
    q_s[...] = q
    k_s[...] = k
    v_s[...] = v_ref[...]
    b_s[...] = b
    qe_s[...] = (q * jnp.exp(b)).astype(bf16)
    ke_s[...] = (k * jnp.exp(b_end - b)).astype(bf16)
    gam_s[...] = jnp.exp(b_end)
    bd = bd_ref[...]
    shape3 = (GLA_SUB, GLA_SUB, 256)
    causal = lax.broadcasted_iota(i32, shape3, 0) <= lax.broadcasted_iota(i32, shape3, 1)

    def block(n, _):
        r0 = pl.multiple_of(n * GLA_SUB, GLA_SUB)
        rows = pl.ds(r0, GLA_SUB)
        qi, ki, vi, bi = q_s[rows, :], k_s[rows, :], v_s[rows, :], b_s[rows, :]
        diff = jnp.where(causal, bi[None, :, :] - bi[:, None, :], 0.0)
        w3 = jnp.where(causal, qi[None, :, :] * ki[:, None, :] * jnp.exp(diff), 0.0)
        a3 = jnp.dot(w3.reshape(GLA_SUB * GLA_SUB, 256).astype(bf16), bd, preferred_element_type=f32)
        intra = jnp.sum(a3.reshape(shape3) * vi[:, None, :], axis=0)
        st = st_ref[...]
        inter = lax.dot_general(qe_s[rows, :], st.astype(bf16), NT, preferred_element_type=f32)
        o_s[rows, :] = intra + inter
        upd = lax.dot_general(vi.astype(bf16), ke_s[rows, :], TN, preferred_element_type=f32)
        st_ref[...] = st * gam_s[pl.ds(r0, 1), :] + jnp.where(bd > 0, upd, 0.0)
        return 0

    lax.fori_loop(0, GLA_TM // GLA_SUB, block, 0)
    o = o_s[...]
    ms = jnp.dot(o * o, bd.astype(f32), preferred_element_type=f32, precision=lax.Precision.HIGHEST) / HEAD_DIM
    o_ref[...] = (o * lax.rsqrt(ms + LN_EPS) * gn_ref[...] * _silu(g_ref[...])).astype(bf16)


def _gla(gq, gk, gv, gg, misc, wa, ba, gn, bd, B, T):
    N = B * T
    nt = T // GLA_TM
    assert T % GLA_TM == 0
    tok = lambda w_: pl.BlockSpec((GLA_TM, w_), lambda b, t: (b * nt + t, 0))
    full = lambda a: pl.BlockSpec(a.shape, lambda b, t: (0, 0))
    v = lambda dt: pltpu.VMEM((GLA_TM, 256), dt)
    return pl.pallas_call(
        _gla_kernel, grid=(B, nt),
        in_specs=[tok(256), tok(256), tok(256), tok(256), tok(128), full(wa), full(ba), full(gn), full(bd)],
        out_specs=tok(256), out_shape=jax.ShapeDtypeStruct((N, 256), bf16),
        scratch_shapes=[pltpu.VMEM((256, 256), f32), v(f32), v(f32), v(f32), v(f32), v(bf16), v(bf16), v(f32), v(f32)],
        compiler_params=_cparams("arbitrary", "arbitrary"), name="gla")(gq, gk, gv, gg, misc, wa, ba, gn, bd)


OUT_TM = 256


def _outproj_kernel(mo_ref, no_ref, po_ref, go_ref, x_ref, mod_ref, w_ref, lng_ref, lnb_ref, wrh_ref, wrl_ref,
                    x1_ref, h2t_ref, h2b_ref, lg_ref, *, alpha):
    a = jnp.concatenate([mo_ref[...], no_ref[...], po_ref[...], go_ref[...]], axis=1)
    y = jnp.dot(a, w_ref[...], preferred_element_type=f32)
    x1 = _layer_norm(alpha * x_ref[...] + mod_ref[0, 2:3, :] * y, lng_ref[...], lnb_ref[...])
    x1_ref[...] = x1
    h2 = x1 * (1.0 + mod_ref[0, 4:5, :]) + mod_ref[0, 3:4, :]
    h2b_ref[...] = h2.astype(bf16)
    for s, w in enumerate(_pack_rows(h2)):
        h2t_ref[pl.ds(s, OUT_TM, stride=ROW_TILE), :] = w
    hi, lo = _split_bf16(h2)
    wrh = wrh_ref[...]
    lg_ref[...] = (jnp.dot(hi, wrh, preferred_element_type=f32) + jnp.dot(lo, wrh, preferred_element_type=f32)
                   + jnp.dot(hi, wrl_ref[...], preferred_element_type=f32))


def _outproj(mo, no, po, go, x2, mod, w, lng, lnb, wrh, wrl, alpha, B, T):
    N = B * T
    nt = T // OUT_TM
    tok = lambda w_: pl.BlockSpec((OUT_TM, w_), lambda i: (i, 0))
    full = lambda a: pl.BlockSpec(a.shape, lambda i: (0,) * a.ndim)
    return pl.pallas_call(
        functools.partial(_outproj_kernel, alpha=alpha), grid=(N // OUT_TM,),
        in_specs=[tok(512), tok(256), tok(256), tok(256), tok(D_MODEL),
                  pl.BlockSpec((1, 6, D_MODEL), lambda i: (i // nt, 0, 0)),
                  full(w), full(lng), full(lnb), full(wrh), full(wrl)],
        out_specs=[tok(D_MODEL), pl.BlockSpec((OUT_TM * ROW_TILE, LANES), lambda i: (i, 0)), tok(D_MODEL),
                   tok(N_EXPERTS)],
        out_shape=[jax.ShapeDtypeStruct((N, D_MODEL), f32), jax.ShapeDtypeStruct((N * ROW_TILE, LANES), u32),
                   jax.ShapeDtypeStruct((N, D_MODEL), bf16), jax.ShapeDtypeStruct((N, N_EXPERTS), f32)],
        compiler_params=_cparams("arbitrary"), name="outproj")(mo, no, po, go, x2, mod, w, lng, lnb, wrh, wrl)


ROUTE_TM = 256
GROUP_SIZE = N_EXPERTS // N_EXPERT_GROUPS


def _route_kernel(lg_ref, rb_ref, tri_ref, ei_ref, wt_ref, rk_ref, cnt_ref, base_ref):
    @pl.when(pl.program_id(0) == 0)
    def _():
        base_ref[...] = jnp.zeros_like(base_ref)

    s = jax.nn.sigmoid(lg_ref[...])
    ssel = s + rb_ref[...]
    lane = lax.broadcasted_iota(i32, s.shape, 1)

    def seg_reduce(x, op):
        k = 1
        while k < GROUP_SIZE:
            partner = jnp.where((lane & k) == 0, pltpu.roll(x, N_EXPERTS - k, axis=1), pltpu.roll(x, k, axis=1))
            x = op(x, partner)
            k *= 2
        return x

    m1 = seg_reduce(ssel, jnp.maximum)
    first = seg_reduce(jnp.where(ssel == m1, lane, BIG_IDX), jnp.minimum)
    m2 = seg_reduce(jnp.where(lane == first, -jnp.inf, ssel), jnp.maximum)
    gscore = m1 + m2
    gid = lane // GROUP_SIZE
    rank = jnp.zeros(s.shape, i32)
    for k in range(1, N_EXPERT_GROUPS):
        other = pltpu.roll(gscore, GROUP_SIZE * k, axis=1)
        rank = rank + jnp.where((other > gscore) | ((other == gscore) & (gid >= k)), 1, 0)
    x = jnp.where(rank < TOPK_GROUPS, ssel, -jnp.inf)
    lane_o = lax.broadcasted_iota(i32, (ROUTE_TM, LANES), 1)
    idx_out = jnp.zeros((ROUTE_TM, LANES), i32)
    w_out = jnp.zeros((ROUTE_TM, LANES), f32)
    wsum = jnp.zeros((ROUTE_TM, 1), f32)
    hits = []
    for r in range(TOP_K):
        mx = jnp.max(x, axis=1, keepdims=True)
        idx = jnp.min(jnp.where(x == mx, lane, BIG_IDX), axis=1, keepdims=True)
        hit = lane == idx
        hits.append(hit)
        w = jnp.sum(jnp.where(hit, s, 0.0), axis=1, keepdims=True)
        x = jnp.where(hit, -jnp.inf, x)
        idx_out = jnp.where(lane_o == r, idx, idx_out)
        w_out = jnp.where(lane_o == r, w, w_out)
        wsum = wsum + w
    ei_ref[...] = idx_out
    wt_ref[...] = w_out / wsum * ROUTED_SCALE
    chosen = jnp.zeros(s.shape, f32)
    for hit in hits:
        chosen = jnp.where(hit, 1.0, chosen)
    before = base_ref[...] + jnp.dot(tri_ref[...], chosen.astype(bf16), preferred_element_type=f32)
    rk_out = jnp.zeros((ROUTE_TM, LANES), i32)
    for r in range(TOP_K):
        rk = jnp.sum(jnp.where(hits[r], before, 0.0), axis=1, keepdims=True)
        rk_out = jnp.where(lane_o == r, rk.astype(i32), rk_out)
    rk_ref[...] = rk_out
    total = base_ref[...] + jnp.sum(chosen, axis=0, keepdims=True)
    base_ref[...] = total
    cnt_ref[...] = total.astype(i32)


def _route(logits, rb):
    N = logits.shape[0]
    r_, c_ = jnp.arange(ROUTE_TM)[:, None], jnp.arange(ROUTE_TM)[None, :]
    tri = (c_ < r_).astype(bf16)
    tok = pl.BlockSpec((ROUTE_TM, LANES), lambda i: (i, 0))
    return pl.pallas_call(
        _route_kernel, grid=(N // ROUTE_TM,),
        in_specs=[pl.BlockSpec((ROUTE_TM, N_EXPERTS), lambda i: (i, 0)), pl.BlockSpec((1, N_EXPERTS), lambda i: (0, 0)),
                  pl.BlockSpec((ROUTE_TM, ROUTE_TM), lambda i: (0, 0))],
        out_specs=[tok, tok, tok, pl.BlockSpec((1, N_EXPERTS), lambda i: (0, 0))],
        out_shape=[jax.ShapeDtypeStruct((N, LANES), i32), jax.ShapeDtypeStruct((N, LANES), f32),
                   jax.ShapeDtypeStruct((N, LANES), i32), jax.ShapeDtypeStruct((1, N_EXPERTS), i32)],
        scratch_shapes=[pltpu.VMEM((1, N_EXPERTS), f32)],
        compiler_params=_cparams("arbitrary"), name="route")(logits, rb, tri)


DISP_TM = 128


def _dispatch_kernel(h_ref, ei_ref, rk_ref, ps_ref, xs_in, pos_ref, xs_hbm, pos_s, csem, rsem):
    del xs_in
    ei = ei_ref[...].astype(f32)
    lane = lax.broadcasted_iota(i32, ei.shape, 1)
    lane_e = lax.broadcasted_iota(i32, (DISP_TM, N_EXPERTS), 1).astype(f32)
    starts = ps_ref[...].astype(f32)
    pos = jnp.zeros(ei.shape, f32)
    for k in range(TOP_K):
        e_k = jnp.sum(jnp.where(lane == k, ei, 0.0), axis=1, keepdims=True)
        st_k = jnp.sum(jnp.where(lane_e == e_k, starts, 0.0), axis=1, keepdims=True)
        pos = jnp.where(lane == k, st_k, pos)
    pos_ref[...] = jnp.where(lane < TOP_K, pos.astype(i32) + rk_ref[...], 0)
    cp = pltpu.make_async_copy(pos_ref, pos_s, csem)
    cp.start()
    cp.wait()

    def tile(i):
        return pl.ds(pl.multiple_of(i * ROW_TILE, ROW_TILE), ROW_TILE)

    def row_copy(t, k, p):
        return pltpu.make_async_copy(h_ref.at[tile(t)], xs_hbm.at[tile(p)], rsem)

    def issue(t, _):
        for k in range(TOP_K):
            row_copy(t, k, pos_s[t, k]).start(priority=k % 2)
        return 0
    lax.fori_loop(0, DISP_TM, issue, 0)

    def drain(t, _):
        for k in range(TOP_K):
            row_copy(t, k, 0).wait()
        return 0
    lax.fori_loop(0, DISP_TM, drain, 0)


def _dispatch(h2t, eidx, rank, pstarts, n_rows):
    N = h2t.shape[0] // ROW_TILE
    tok = pl.BlockSpec((DISP_TM, LANES), lambda i: (i, 0))
    xs0 = jnp.zeros((n_rows * ROW_TILE, LANES), u32)
    pos, xs = pl.pallas_call(
        _dispatch_kernel, grid=(N // DISP_TM,),
        in_specs=[pl.BlockSpec((DISP_TM * ROW_TILE, LANES), lambda i: (i, 0)), tok, tok,
                  pl.BlockSpec((1, N_EXPERTS), lambda i: (0, 0)), pl.BlockSpec(memory_space=pl.ANY)],
        out_specs=[tok, pl.BlockSpec(memory_space=pl.ANY)],
        out_shape=[jax.ShapeDtypeStruct((N, LANES), i32), jax.ShapeDtypeStruct(xs0.shape, u32)],
        scratch_shapes=[pltpu.SMEM((DISP_TM, LANES), i32), pltpu.SemaphoreType.DMA, pltpu.SemaphoreType.DMA],
        input_output_aliases={4: 1},
        compiler_params=_cparams("arbitrary"), name="moe_dispatch")(h2t, eidx, rank, pstarts, xs0)
    return pos, xs


def _moe_kernel(be_ref, nu_ref, xs_ref, wg_ref, wu_ref, wd_ref, ys_ref):
    i = pl.program_id(0)

    @pl.when(i < nu_ref[0])
    def _():
        def lhs(s):
            return jnp.concatenate(_unpack_word(xs_ref[pl.ds(s, MOE_ROWS, stride=ROW_TILE), :]), axis=1).astype(bf16)

        g = jnp.zeros((MOE_ROWS, D_EXPERT), f32)
        u = jnp.zeros((MOE_ROWS, D_EXPERT), f32)
        for s in range(ROW_TILE):
            xc = lhs(s)
            rows = slice(2 * s * LANES, (2 * s + 2) * LANES)
            g = g + jnp.dot(xc, wg_ref[0, 0, rows, :].astype(bf16), preferred_element_type=f32)
            u = u + jnp.dot(xc, wu_ref[0, 0, rows, :].astype(bf16), preferred_element_type=f32)
        y = jnp.dot((_silu(g) * u).astype(bf16), wd_ref[0, 0].astype(bf16), preferred_element_type=f32)
        for s, w in enumerate(_pack_rows(y)):
            ys_ref[pl.ds(s, MOE_ROWS, stride=ROW_TILE), :] = w

    @pl.when(i >= nu_ref[0])
    def _():
        ys_ref[...] = jnp.zeros_like(ys_ref)


def _moe(layer, blk_e, n_used, xs, wg, wu, wd):
    blk = MOE_ROWS * ROW_TILE
    nb = xs.shape[0] // blk
    wspec = lambda a: pl.BlockSpec((1, 1) + a.shape[2:], lambda i, be, nu: (layer, be[i], 0, 0))
    gs = pltpu.PrefetchScalarGridSpec(
        num_scalar_prefetch=2, grid=(nb,),
        in_specs=[pl.BlockSpec((blk, LANES), lambda i, be, nu: (jnp.minimum(i, nu[0] - 1), 0)),
                  wspec(wg), wspec(wu), wspec(wd)],
        out_specs=pl.BlockSpec((blk, LANES), lambda i, be, nu: (i, 0)))
    return pl.pallas_call(
        _moe_kernel, grid_spec=gs, out_shape=jax.ShapeDtypeStruct(xs.shape, u32),
        compiler_params=_cparams("arbitrary"), name="moe_experts")(blk_e, n_used, xs, wg, wu, wd)


FIN_TM = 128


def _tile_gather(pos_hbm, ys_hbm, pos_s, gbuf, isem, rsem, step, n_steps):
    def idx_copy(s):
        return pltpu.make_async_copy(pos_hbm.at[pl.ds(s * FIN_TM, FIN_TM)], pos_s.at[s % 2], isem.at[s % 2])

    def tile(i):
        return pl.ds(pl.multiple_of(i * ROW_TILE, ROW_TILE), ROW_TILE)

    def row_copy(slot, t, k, p):
        return pltpu.make_async_copy(ys_hbm.at[tile(p)], gbuf.at[slot, k, tile(t)], rsem.at[slot])

    def start_rows(s):
        slot = s % 2

        def body(t, _):
            for k in range(TOP_K):
                row_copy(slot, t, k, pos_s[slot, t, k]).start(priority=k % 2)
            return 0
        lax.fori_loop(0, FIN_TM, body, 0)

    @pl.when(step == 0)
    def _():
        idx_copy(step).start()
        idx_copy(step).wait()
        start_rows(step)

        @pl.when(n_steps > 1)
        def _():
            idx_copy(step + 1).start()

    @pl.when(step + 1 < n_steps)
    def _():
        idx_copy(step + 1).wait()
        start_rows(step + 1)

        @pl.when(step + 2 < n_steps)
        def _():
            idx_copy(step + 2).start()

    slot = step % 2

    def wait_body(t, _):
        for k in range(TOP_K):
            row_copy(slot, t, k, 0).wait()
        return 0
    lax.fori_loop(0, FIN_TM, wait_body, 0)


def _fin_kernel(pos_hbm, ys_hbm, h2_ref, x1_ref, wt_ref, mod_ref, sg_ref, su_ref, sd_ref, lng_ref, lnb_ref, o_ref,
                pos_s, gbuf, isem, rsem, *, alpha):
    i = pl.program_id(0)
    _tile_gather(pos_hbm, ys_hbm, pos_s, gbuf, isem, rsem, i, pl.num_programs(0))
    hb = h2_ref[...]
    g = jnp.dot(hb, sg_ref[...], preferred_element_type=f32)
    u = jnp.dot(hb, su_ref[...], preferred_element_type=f32)
    shared = jnp.dot((_silu(g) * u).astype(bf16), sd_ref[...], preferred_element_type=f32)
    wt = wt_ref[...]
    lane = lax.broadcasted_iota(i32, wt.shape, 1)
    wk = [jnp.sum(jnp.where(lane == k, wt, 0.0), axis=1, keepdims=True) for k in range(TOP_K)]
    slot = i % 2
    cols = []
    for s in range(ROW_TILE):
        sub = pl.ds(s, FIN_TM, stride=ROW_TILE)
        lo, hi = _unpack_word(gbuf[slot, 0, sub, :])
        acc_lo, acc_hi = lo * wk[0], hi * wk[0]
        for k in range(1, TOP_K):
            lo, hi = _unpack_word(gbuf[slot, k, sub, :])
            acc_lo, acc_hi = acc_lo + lo * wk[k], acc_hi + hi * wk[k]
        cols += [acc_lo, acc_hi]
    y = shared + jnp.concatenate(cols, axis=1)
    o_ref[...] = _layer_norm(alpha * x1_ref[...] + mod_ref[0, 5:6, :] * y, lng_ref[...], lnb_ref[...])


def _fin(pos, ys, h2b, x1, wts, mod, sg, su, sd, lng, lnb, alpha, B, T):
    N = B * T
    nt = T // FIN_TM
    tok = lambda w_: pl.BlockSpec((FIN_TM, w_), lambda i: (i, 0))
    full = lambda a: pl.BlockSpec(a.shape, lambda i: (0,) * a.ndim)
    return pl.pallas_call(
        functools.partial(_fin_kernel, alpha=alpha), grid=(N // FIN_TM,),
        in_specs=[pl.BlockSpec(memory_space=pl.ANY), pl.BlockSpec(memory_space=pl.ANY),
                  tok(D_MODEL), tok(D_MODEL), tok(LANES),
                  pl.BlockSpec((1, 6, D_MODEL), lambda i: (i // nt, 0, 0)),
                  full(sg), full(su), full(sd), full(lng), full(lnb)],
        out_specs=tok(D_MODEL), out_shape=jax.ShapeDtypeStruct((N, D_MODEL), f32),
        scratch_shapes=[pltpu.SMEM((2, FIN_TM, LANES), i32), pltpu.VMEM((2, TOP_K, FIN_TM * ROW_TILE, LANES), u32),
                        pltpu.SemaphoreType.DMA((2,)), pltpu.SemaphoreType.DMA((2,))],
        compiler_params=_cparams("arbitrary"), name="combine")(pos, ys, h2b, x1, wts, mod, sg, su, sd, lng, lnb)


def _pack_w_in(w):
    c = lambda name, width: w[:, _OFF[name]:_OFF[name] + width]
    cols = [c("moba_q", 256), c("moba_k", 256), c("moba_v", 256), c("nsa_q", 256),
            c("k_slc", 64), c("k_win", 64), c("v_slc", 64), c("v_win", 64), c("k_cmp", 64), c("v_cmp", 64),
            c("pool", 256), c("gla_q", 256), c("gla_k", 256), c("gla_v", 256), c("gla_g", 256),
            c("nsa_gate", 12), c("gla_a", 16), jnp.zeros((w.shape[0], LANES - 28), w.dtype)]
    return jnp.concatenate(cols, axis=1).astype(bf16)


def _rope_tables(T):
    half = ROPE_DIM // 2
    inv_freq = ROPE_THETA ** (-jnp.arange(half, dtype=f32) / half)
    ang = jnp.arange(T).astype(f32)[:, None] * inv_freq[None, :]
    cos, sin = jnp.cos(ang), jnp.sin(ang)
    one = jnp.ones((T, HEAD_DIM - ROPE_DIM), f32)
    zero = jnp.zeros((T, HEAD_DIM - ROPE_DIM), f32)
    ct = jnp.concatenate([cos, cos, one], axis=1)
    st = jnp.concatenate([-sin, sin, zero], axis=1)
    return jnp.tile(ct, (1, 2)), jnp.tile(st, (1, 2))


def _cmp_weights(pe, w1, w2):
    half = CMP_BLOCK // 2
    z = jnp.zeros((half, HEAD_DIM, CMP_HIDDEN), f32)

    def arrange(lo):
        wk = w1[0].reshape(CMP_BLOCK, HEAD_DIM, CMP_HIDDEN)[lo:lo + half]
        wv = w1[1].reshape(CMP_BLOCK, HEAD_DIM, CMP_HIDDEN)[lo:lo + half]
        top = jnp.concatenate([wk, z], axis=2)
        bot = jnp.concatenate([z, wv], axis=2)
        return jnp.concatenate([top, bot], axis=1).reshape(half * 2 * HEAD_DIM, 2 * CMP_HIDDEN).astype(bf16)

    def pe_row(lo):
        return jnp.concatenate([pe[0, lo:lo + half], pe[1, lo:lo + half]], axis=1).reshape(1, half * 2 * HEAD_DIM)

    zc = jnp.zeros((CMP_HIDDEN, LANES - HEAD_DIM), f32)
    zr = jnp.zeros((CMP_HIDDEN, LANES), f32)
    w2k = jnp.concatenate([jnp.concatenate([w2[0], zc], axis=1), zr], axis=0).astype(bf16)
    w2v = jnp.concatenate([zr, jnp.concatenate([w2[1], zc], axis=1)], axis=0).astype(bf16)
    return arrange(0), arrange(half), pe_row(0), pe_row(half), w2k, w2v


def _overlap_matrix(ns):
    n = jnp.arange(ns)[:, None] * CMP_STRIDE
    j = jnp.arange(LANES)[None, :] * SLC_BLOCK
    ov = (n < j + SLC_BLOCK) & (n + CMP_BLOCK > j) & (jnp.arange(ns)[:, None] < ns - 1)
    return ov.astype(bf16)


def _block_diag_ones():
    h = jnp.arange(256) // HEAD_DIM
    return (h[:, None] == h[None, :]).astype(bf16)


def _expert_layout(counts, n_tok):
    counts = counts.reshape(N_EXPERTS)
    padded = (counts + MOE_ROWS - 1) // MOE_ROWS * MOE_ROWS
    pstarts = (jnp.cumsum(padded) - padded).astype(i32)
    n_blocks = n_tok * TOP_K // MOE_ROWS + N_EXPERTS
    blk_e = jnp.clip(jnp.searchsorted(pstarts, jnp.arange(n_blocks) * MOE_ROWS, side='right') - 1,
                     0, N_EXPERTS - 1).astype(i32)
    n_used = (jnp.sum(padded) // MOE_ROWS).astype(i32).reshape(1)
    return pstarts.reshape(1, N_EXPERTS), blk_e, n_used, n_blocks * MOE_ROWS


def _mixer_inputs(x2, mod, w_in, B, T):
    ct, st = _rope_tables(T)
    return _inproj(x2, mod, _pack_w_in(w_in), ct, st, B, T)


def _token_mixers(x2, mod, w_in, cmp_pe, cmp_w1, cmp_w2, pool_w, pool_scale, gla_wa, gla_ba, gla_norm, B, T):
    N = B * T
    (mq, mk, mv, km, nq, nqr, ks, kw, vs, vw, kvc, pool_u, gq, gk, gv, gg, misc) = _mixer_inputs(x2, mod, w_in, B, T)
    nt = T // MOBA_BLOCK
    kmh = km.reshape(B, nt, N_HEADS, HEAD_DIM).transpose(0, 2, 1, 3)
    mo = _moba(mq, mk, mv, jnp.pad(kmh, ((0, 0), (0, 0), (0, MOBA_NBLK - nt), (0, LANES - HEAD_DIM))), B, T)
    ns = T // CMP_STRIDE
    kc, vc = _cmp(kvc.reshape(B, ns, CMP_STRIDE * LANES), *_cmp_weights(cmp_pe, cmp_w1, cmp_w2))
    no = _nsa(nq, nqr, misc, ks, vs, kw, vw, kc, vc, _overlap_matrix(ns), B, T)
    wbd = jax.scipy.linalg.block_diag(*[pool_w[g] for g in range(len(POOL_WINDOWS))]).astype(bf16)
    po = _pool(pool_u, wbd, pool_scale.reshape(1, 256), B, T)
    wa = jnp.zeros((LANES, 256), f32).at[MISC_A0:MISC_A0 + GLA_LOWRANK].set(gla_wa)
    go = _gla(gq, gk, gv, gg, misc, wa, gla_ba.reshape(1, 256), jnp.tile(gla_norm, N_HEADS).reshape(1, 256),
              _block_diag_ones(), B, T)
    return mo, no, po, go


def _pad_w_out(w_out):
    wm = w_out[:GROUP_WIDTH].reshape(N_HEADS, HEAD_DIM, D_MODEL)
    wm = jnp.pad(wm, ((0, 0), (0, LANES - HEAD_DIM), (0, 0))).reshape(N_HEADS * LANES, D_MODEL)
    return jnp.concatenate([wm, w_out[GROUP_WIDTH:]], axis=0).astype(bf16)


def kernel(x, c, w_ada, b_ada, w_in, cmp_pe, cmp_w1, cmp_w2, pool_w, pool_scale, gla_wa, gla_ba, gla_norm, w_out,
           ln_g, ln_b, w_router, router_bias, exp_gate, exp_up, exp_down, sh_gate, sh_up, sh_down):
    B, T, D = x.shape
    N = B * T
    depth = w_ada.shape[0]
    alpha = float((2 * depth) ** 0.25)
    x2 = x.reshape(N, D)
    c8 = jnp.zeros((8, D), f32).at[:B].set(c)
    for l in range(depth):
        mod = _ada(c8, w_ada[l], b_ada[l].reshape(1, -1))[:B].reshape(B, 6, D)
        mo, no, po, go = _token_mixers(x2, mod, w_in[l], cmp_pe[l], cmp_w1[l], cmp_w2[l], pool_w[l], pool_scale[l],
                                       gla_wa[l], gla_ba[l], gla_norm[l], B, T)
        wrh, wrl = _split_bf16(w_router[l])
        x1, h2t, h2b, logits = _outproj(mo, no, po, go, x2, mod, _pad_w_out(w_out[l]), ln_g[l, 0].reshape(1, D),
                                        ln_b[l, 0].reshape(1, D), wrh, wrl, alpha, B, T)
        eidx, wts, rank, counts = _route(logits, router_bias[l].reshape(1, N_EXPERTS))
        pstarts, blk_e, n_used, n_rows = _expert_layout(counts, N)
        pos, xs = _dispatch(h2t, eidx, rank, pstarts, n_rows)
        ys = _moe(l, blk_e, n_used, xs, exp_gate, exp_up, exp_down)
        x2 = _fin(pos, ys, h2b, x1, wts, mod, sh_gate[l].astype(bf16), sh_up[l].astype(bf16),
                  sh_down[l].astype(bf16), ln_g[l, 1].reshape(1, D), ln_b[l, 1].reshape(1, D), alpha, B, T)
    return x2.reshape(B, T, D)
```

```python
import functools

import jax
import jax.numpy as jnp
from jax import lax
from jax.experimental import pallas as pl
from jax.experimental.pallas import tpu as pltpu

f32, bf16, i32, u32 = jnp.float32, jnp.bfloat16, jnp.int32, jnp.uint32

D_MODEL = 1024
HEAD_DIM = 64
N_HEADS = 4
GROUP_WIDTH = 256
ROPE_THETA = 500000.0
ROPE_DIM = 16
MOBA_BLOCK = 256
MOBA_TOPK = 3
CMP_BLOCK = 32
CMP_STRIDE = 16
CMP_HIDDEN = 128
SLC_BLOCK = 64
SLC_TOPN = 16
WIN = 512
FORCE_SCORE = 1e9
POOL_WINDOWS = (2, 4, 8, 16)
GLA_SUB = 16
GLA_LOWRANK = 16
GLA_TAU = 16.0
N_EXPERTS = 256
TOP_K = 8
N_EXPERT_GROUPS = 8
TOPK_GROUPS = 4
D_EXPERT = 256
ROUTED_SCALE = 2.5
LN_EPS = 1e-5
QK_SCALE = HEAD_DIM ** -0.5
LOG2E = 1.4426950408889634
Q_SCALE = QK_SCALE * LOG2E

LANES = 128
MASKED = -1e30
M_INIT = -3e38
BIG_IDX = 1 << 20
MOE_ROWS = 256
ROW_TILE = D_MODEL // (2 * LANES)
VMEM_LIMIT = 56 * 1024 * 1024

_OFF = dict(moba_q=0, moba_k=256, moba_v=512, nsa_q=768, k_cmp=1024, v_cmp=1088, k_slc=1152, v_slc=1216,
            k_win=1280, v_win=1344, nsa_gate=1408, pool=1420, gla_q=1676, gla_k=1932, gla_v=2188,
            gla_a=2444, gla_g=2460)
_S = dict(mq=0, mk=256, mv=512, nq=768, sw=1024, vsw=1152, kvc=1280, pool=1408, gq=1664, gk=1920, gv=2176,
          gg=2432, misc=2688)
IN_COLS_PACKED = 2816
MISC_GATE0 = 0
MISC_A0 = 12

NT = (((1,), (1,)), ((), ()))
TN = (((0,), (0,)), ((), ()))


def _cparams(*sem):
    return pltpu.CompilerParams(dimension_semantics=sem, vmem_limit_bytes=VMEM_LIMIT)


def _silu(x):
    return x * jax.nn.sigmoid(x)


def _split_bf16(x):
    hi = x.astype(bf16)
    lo = (x - hi.astype(f32)).astype(bf16)
    return hi, lo


def _pack_rows(x):
    bits = lambda v: lax.bitcast_convert_type(v.astype(bf16).astype(f32), u32)
    return [(bits(x[:, (2 * s) * LANES:(2 * s + 1) * LANES]) >> 16) | bits(x[:, (2 * s + 1) * LANES:(2 * s + 2) * LANES])
            for s in range(ROW_TILE)]


def _unpack_word(w):
    return (lax.bitcast_convert_type(w << 16, f32), lax.bitcast_convert_type(w & jnp.uint32(0xFFFF0000), f32))


def _layer_norm(z, g, b):
    mu = jnp.mean(z, axis=-1, keepdims=True)
    zc = z - mu
    var = jnp.mean(zc * zc, axis=-1, keepdims=True)
    return zc * lax.rsqrt(var + LN_EPS) * g + b


def _argmax_rounds(score, index, rounds, axis=1):
    picked = jnp.zeros(score.shape, f32)
    for _ in range(rounds):
        mx = jnp.max(score, axis=axis, keepdims=True)
        first = jnp.min(jnp.where(score == mx, index, BIG_IDX), axis=axis, keepdims=True)
        hit = index == first
        picked = jnp.where(hit, 1.0, picked)
        score = jnp.where(hit, -jnp.inf, score)
    return picked


def _softmax_steps_t(s_list, carries, v_list):
    m_new = [jnp.maximum(c[0], jnp.max(s, axis=0, keepdims=True)) for s, c in zip(s_list, carries)]
    p = [jnp.exp2(s - m) for s, m in zip(s_list, m_new)]
    pv = [jnp.dot(v, pi.astype(bf16), preferred_element_type=f32) for v, pi in zip(v_list, p)]
    out = []
    for c, m, pi, pvi in zip(carries, m_new, p, pv):
        alpha = jnp.exp2(c[0] - m)
        out.append((m, alpha * c[1] + jnp.sum(pi, axis=0, keepdims=True), alpha * c[2] + pvi))
    return tuple(out)


def _softmax_init_t(groups, queries):
    return tuple((jnp.full((1, queries), M_INIT, f32), jnp.zeros((1, queries), f32),
                  jnp.zeros((LANES, queries), f32)) for _ in range(groups))


def _ada_kernel(c_ref, w_ref, b_ref, o_ref):
    o_ref[...] = jnp.dot(_silu(c_ref[...]), w_ref[...], preferred_element_type=f32,
                         precision=lax.Precision.HIGHEST) + b_ref[...]


def _ada(c8, w, b):
    n = w.shape[1] // D_MODEL
    return pl.pallas_call(
        _ada_kernel, grid=(n,),
        in_specs=[pl.BlockSpec((8, D_MODEL), lambda j: (0, 0)),
                  pl.BlockSpec((D_MODEL, D_MODEL), lambda j: (0, j)),
                  pl.BlockSpec((1, D_MODEL), lambda j: (0, j))],
        out_specs=pl.BlockSpec((8, D_MODEL), lambda j: (0, j)),
        out_shape=jax.ShapeDtypeStruct((8, w.shape[1]), f32),
        compiler_params=_cparams("arbitrary"), name="ada")(c8, w, b)


IN_TM = MOBA_BLOCK


def _inproj_kernel(x_ref, mod_ref, w_ref, ct_ref, st_ref,
                   mq_ref, mk_ref, mv_ref, km_ref, nq_ref, nqr_ref, ks_ref, kw_ref, vs_ref, vw_ref,
                   kvc_ref, pool_ref, gq_ref, gk_ref, gv_ref, gg_ref, misc_ref, *, nt):
    tb = pl.program_id(0) % nt
    h = (x_ref[...] * (1.0 + mod_ref[0, 1:2, :]) + mod_ref[0, 0:1, :]).astype(bf16)

    def seg(name, width):
        a = _S[name]
        return jnp.dot(h, w_ref[:, a:a + width], preferred_element_type=f32)

    ct, st = ct_ref[...], st_ref[...]
    lane = lax.broadcasted_iota(i32, (IN_TM, LANES), 1)
    first8 = (lane % HEAD_DIM) < ROPE_DIM // 2
    half = lane < HEAD_DIM

    def rope128(y):
        partner = jnp.where(first8, pltpu.roll(y, LANES - 8, axis=1), pltpu.roll(y, 8, axis=1))
        return y * ct + partner * st

    def rope(y):
        return jnp.concatenate([rope128(y[:, c * LANES:(c + 1) * LANES]) for c in range(y.shape[1] // LANES)],
                               axis=1)

    def lo_half(y):
        return jnp.where(half, y, 0.0)

    def hi_half(y):
        return jnp.where(half, pltpu.roll(y, HEAD_DIM, axis=1), 0.0)

    def per_head(y):
        parts = []
        for c in range(2):
            yc = y[:, c * LANES:(c + 1) * LANES]
            parts += [lo_half(yc), hi_half(yc)]
        return jnp.concatenate(parts, axis=1)

    mq_ref[...] = per_head(rope(seg("mq", 256)) * Q_SCALE).astype(bf16)
    k = rope(seg("mk", 256))
    km_ref[0] = jnp.mean(k, axis=0, keepdims=True)
    lane4 = lax.broadcasted_iota(i32, (IN_TM, 4 * LANES), 1)
    mk_ref[...] = jnp.where((lane4 % LANES) == HEAD_DIM + tb, 1.0, per_head(k)).astype(bf16)
    mv_ref[...] = per_head(seg("mv", 256)).astype(bf16)
    q = seg("nq", 256) * Q_SCALE
    nq_ref[...] = per_head(q).astype(bf16)
    nqr_ref[...] = per_head(rope(q)).astype(bf16)
    sw = rope128(seg("sw", 128))
    row = lax.broadcasted_iota(i32, (IN_TM, LANES), 0)
    slc_id = tb * (IN_TM // SLC_BLOCK) + row // SLC_BLOCK
    ks_ref[...] = jnp.concatenate([lo_half(sw), jnp.where(lane == slc_id, 1.0, 0.0)], axis=1).astype(bf16)
    kw_ref[...] = hi_half(sw).astype(bf16)
    vsw = seg("vsw", 128)
    vs_ref[...] = lo_half(vsw).astype(bf16)
    vw_ref[...] = hi_half(vsw).astype(bf16)
    kvc_ref[...] = seg("kvc", 128).astype(bf16)
    pool_ref[...] = seg("pool", 256)
    gq_ref[...] = seg("gq", 256)
    gk_ref[...] = seg("gk", 256)
    gv_ref[...] = seg("gv", 256)
    gg_ref[...] = seg("gg", 256)
    misc_ref[...] = seg("misc", 128)


def _inproj(x2, mod, w, ct, st, B, T):
    N = B * T
    nt = T // IN_TM
    assert T % IN_TM == 0 and nt <= 32 and T // SLC_BLOCK <= LANES
    row = lambda w_, dt: (jax.ShapeDtypeStruct((N, w_), dt), pl.BlockSpec((IN_TM, w_), lambda i: (i, 0)))
    outs = [row(512, bf16), row(512, bf16), row(512, bf16),
            (jax.ShapeDtypeStruct((N // IN_TM, 1, 256), f32), pl.BlockSpec((1, 1, 256), lambda i: (i, 0, 0))),
            row(512, bf16), row(512, bf16), row(256, bf16), row(128, bf16), row(128, bf16), row(128, bf16),
            row(128, bf16), row(256, f32), row(256, f32), row(256, f32), row(256, f32), row(256, f32),
            row(128, f32)]
    return pl.pallas_call(
        functools.partial(_inproj_kernel, nt=nt), grid=(N // IN_TM,),
        in_specs=[pl.BlockSpec((IN_TM, D_MODEL), lambda i: (i, 0)),
                  pl.BlockSpec((1, 6, D_MODEL), lambda i: (i // nt, 0, 0)),
                  pl.BlockSpec((D_MODEL, IN_COLS_PACKED), lambda i: (0, 0), pipeline_mode=pl.Buffered(1)),
                  pl.BlockSpec((IN_TM, LANES), lambda i: (i % nt, 0)),
                  pl.BlockSpec((IN_TM, LANES), lambda i: (i % nt, 0))],
        out_specs=[o[1] for o in outs], out_shape=[o[0] for o in outs],
        compiler_params=_cparams("arbitrary"), name="inproj")(x2, mod, w, ct, st)


def _cmp_kernel(x_ref, wa_ref, wb_ref, pea_ref, peb_ref, w2k_ref, w2v_ref, kc_ref, vc_ref):
    x = x_ref[0].astype(f32)
    a = jnp.dot((x + pea_ref[...]).astype(bf16), wa_ref[...], preferred_element_type=f32)
    b = jnp.dot((x + peb_ref[...]).astype(bf16), wb_ref[...], preferred_element_type=f32)
    hid = a + pltpu.roll(b, x.shape[0] - 1, axis=0)
    g = jax.nn.gelu(hid).astype(bf16)
    kc_ref[0] = jnp.dot(g, w2k_ref[...], preferred_element_type=f32).astype(bf16)
    vc_ref[0] = jnp.dot(g, w2v_ref[...], preferred_element_type=f32).astype(bf16)


def _cmp(xseg, wa, wb, pea, peb, w2k, w2v):
    B, ns, wd = xseg.shape
    full = lambda a: pl.BlockSpec(a.shape, lambda b: (0,) * a.ndim)
    return pl.pallas_call(
        _cmp_kernel, grid=(B,),
        in_specs=[pl.BlockSpec((1, ns, wd), lambda b: (b, 0, 0))] + [full(a) for a in (wa, wb, pea, peb, w2k, w2v)],
        out_specs=[pl.BlockSpec((1, ns, LANES), lambda b: (b, 0, 0))] * 2,
        out_shape=[jax.ShapeDtypeStruct((B, ns, LANES), bf16)] * 2,
        compiler_params=_cparams("arbitrary"), name="nsa_compress")(xseg, wa, wb, pea, peb, w2k, w2v)


MOBA_NBLK = 32
KEY_TILE = 512


def _moba_kernel(q_ref, k_ref, vt_ref, km_ref, o_ref):
    own = pl.program_id(1)
    heads = range(N_HEADS)
    hl = lambda h: slice(h * LANES, (h + 1) * LANES)
    blk = lax.broadcasted_iota(i32, (MOBA_NBLK, MOBA_BLOCK), 0)
    past = blk < own
    zeros = lambda n: jnp.zeros((n, MOBA_BLOCK), f32)
    qf = []
    for h in heads:
        qh = q_ref[:, hl(h)]
        hi, lo = _split_bf16(km_ref[0, h])
        gate_t = (lax.dot_general(hi, qh, NT, preferred_element_type=f32)
                  + lax.dot_general(lo, qh, NT, preferred_element_type=f32))
        picked = _argmax_rounds(jnp.where(past, gate_t, -jnp.inf), blk, MOBA_TOPK, axis=0)
        allowed = ((picked > 0.0) & past) | (blk == own)
        bias_t = jnp.concatenate([zeros(HEAD_DIM), jnp.where(allowed, 0.0, MASKED),
                                  zeros(LANES - HEAD_DIM - MOBA_NBLK)], axis=0)
        qf.append(qh + jnp.transpose(bias_t).astype(bf16))

    def scores_t(h, off):
        return lax.dot_general(k_ref[pl.ds(off, KEY_TILE), hl(h)], qf[h], NT, preferred_element_type=f32)

    def body(p, carry):
        off = pl.multiple_of(p * KEY_TILE, KEY_TILE)
        return _softmax_steps_t([scores_t(h, off) for h in heads], carry, [vt_ref[0, p, hl(h), :] for h in heads])

    last = own // (KEY_TILE // MOBA_BLOCK)
    carry = lax.fori_loop(0, last, body, _softmax_init_t(N_HEADS, MOBA_BLOCK))
    off = pl.multiple_of(last * KEY_TILE, KEY_TILE)
    kpos = off + lax.broadcasted_iota(i32, (KEY_TILE, MOBA_BLOCK), 0)
    qpos = own * MOBA_BLOCK + lax.broadcasted_iota(i32, (KEY_TILE, MOBA_BLOCK), 1)
    carry = _softmax_steps_t([jnp.where(kpos <= qpos, scores_t(h, off), MASKED) for h in heads], carry,
                             [vt_ref[0, last, hl(h), :] for h in heads])
    for h in heads:
        o_ref[:, hl(h)] = jnp.transpose(carry[h][2] / carry[h][1]).astype(bf16)


def _blocks_t(v, B, T, blk):
    return v.reshape(B, T // blk, blk, v.shape[1]).transpose(0, 1, 3, 2)


def _moba(mq, mk, mv, km, B, T):
    N = B * T
    nt = T // MOBA_BLOCK
    wd = N_HEADS * LANES
    assert T % KEY_TILE == 0 and nt <= MOBA_NBLK
    return pl.pallas_call(
        _moba_kernel, grid=(B, nt),
        in_specs=[pl.BlockSpec((MOBA_BLOCK, wd), lambda b, i: (b * nt + i, 0)),
                  pl.BlockSpec((T, wd), lambda b, i: (b, 0), pipeline_mode=pl.Buffered(1)),
                  pl.BlockSpec((1, T // KEY_TILE, wd, KEY_TILE), lambda b, i: (b, 0, 0, 0),
                               pipeline_mode=pl.Buffered(1)),
                  pl.BlockSpec((1, N_HEADS, MOBA_NBLK, LANES), lambda b, i: (b, 0, 0, 0))],
        out_specs=pl.BlockSpec((MOBA_BLOCK, wd), lambda b, i: (b * nt + i, 0)),
        out_shape=jax.ShapeDtypeStruct((N, wd), bf16),
        compiler_params=_cparams("arbitrary", "arbitrary"), name="moba")(mq, mk, _blocks_t(mv, B, T, KEY_TILE), km)


NSA_TQ = 256
NSA_KB = 256


def _stack_heads(ref):
    return jnp.concatenate([ref[:, h * LANES:(h + 1) * LANES] for h in range(N_HEADS)], axis=0)


def _nsa_select_kernel(nq_ref, kc_ref, vct_ref, ovt_ref, oc_ref, selb_ref, *, n_cmp):
    c = pl.program_id(1)
    rows = N_HEADS * NSA_TQ
    s = lax.dot_general(kc_ref[0], _stack_heads(nq_ref), NT, preferred_element_type=f32)
    n = lax.broadcasted_iota(i32, s.shape, 0)
    qpos_t = c * NSA_TQ + lax.broadcasted_iota(i32, (1, rows), 1) % NSA_TQ
    ok = (n * CMP_STRIDE + (CMP_BLOCK - 1) <= qpos_t) & (n < n_cmp)
    s = jnp.where(ok, s, -jnp.inf)
    m = jnp.max(s, axis=0, keepdims=True)
    m = jnp.where(m > -jnp.inf, m, 0.0)
    e = jnp.where(ok, jnp.exp2(s - m), 0.0)
    p_c = e / jnp.maximum(jnp.sum(e, axis=0, keepdims=True), 1e-30)
    o_c_t = jnp.dot(vct_ref[0], p_c.astype(bf16), preferred_element_type=f32)
    hq = lambda h: slice(h * NSA_TQ, (h + 1) * NSA_TQ)
    for h in range(N_HEADS):
        oc_ref[:, h * LANES:(h + 1) * LANES] = jnp.transpose(o_c_t[:, hq(h)]).astype(bf16)
    hi, lo = _split_bf16((p_c[:, hq(0)] + p_c[:, hq(1)]) + (p_c[:, hq(2)] + p_c[:, hq(3)]))
    imp = (jnp.dot(ovt_ref[...], hi, preferred_element_type=f32)
           + jnp.dot(ovt_ref[...], lo, preferred_element_type=f32))
    j = lax.broadcasted_iota(i32, imp.shape, 0)
    cur = (c * NSA_TQ + lax.broadcasted_iota(i32, (1, NSA_TQ), 1)) // SLC_BLOCK
    forced = (j == 0) | (j == cur) | (j == cur - 1)
    valid = j <= cur
    score = jnp.where(valid, jnp.where(forced, FORCE_SCORE, imp), -jnp.inf)
    chosen = (_argmax_rounds(score, j, SLC_TOPN, axis=0) > 0.0) & valid
    selb_ref[...] = jnp.transpose(jnp.where(chosen, 0.0, MASKED)).astype(bf16)


def _nsa_attend_kernel(nqr_ref, selb_ref, oc_ref, misc_ref, ks_ref, vst_ref, kw_ref, vwt_ref, o_ref):
    c = pl.program_id(1)
    rows = N_HEADS * NSA_TQ
    heads = range(N_HEADS)
    q4r = _stack_heads(nqr_ref)
    qpos_t = c * NSA_TQ + lax.broadcasted_iota(i32, (1, rows), 1) % NSA_TQ
    selb = selb_ref[...]
    lhs = [jnp.concatenate([q4r[h * NSA_TQ:(h + 1) * NSA_TQ], selb], axis=1) for h in heads]

    def scores_t(off):
        kb = ks_ref[pl.ds(off, KEY_TILE), :]
        return [lax.dot_general(kb, lhs[h], NT, preferred_element_type=f32) for h in heads]

    def body(p, carry):
        return _softmax_steps_t(scores_t(pl.multiple_of(p * KEY_TILE, KEY_TILE)), carry, [vst_ref[0, p]] * N_HEADS)

    last = (c * NSA_TQ) // KEY_TILE
    carry = lax.fori_loop(0, last, body, _softmax_init_t(N_HEADS, NSA_TQ))
    off = pl.multiple_of(last * KEY_TILE, KEY_TILE)
    kpos = off + lax.broadcasted_iota(i32, (KEY_TILE, NSA_TQ), 0)
    qpos = c * NSA_TQ + lax.broadcasted_iota(i32, (KEY_TILE, NSA_TQ), 1)
    carry = _softmax_steps_t([jnp.where(kpos <= qpos, s, MASKED) for s in scores_t(off)], carry,
                             [vst_ref[0, last]] * N_HEADS)
    o_s = jnp.concatenate([jnp.transpose(carry[h][2] / carry[h][1]) for h in heads], axis=0)

    nwb = WIN // NSA_KB + 1
    sb = jnp.maximum(c - (nwb - 1), 0)
    start = pl.multiple_of(sb * NSA_KB, NSA_KB)
    s_w = lax.dot_general(kw_ref[pl.ds(start, nwb * NSA_KB), :], q4r, NT, preferred_element_type=f32)
    wpos = start + lax.broadcasted_iota(i32, s_w.shape, 0)
    s_w = jnp.where((wpos <= qpos_t) & (wpos > qpos_t - WIN), s_w, -jnp.inf)
    e_w = jnp.exp2(s_w - jnp.max(s_w, axis=0, keepdims=True))
    p_w = (e_w / jnp.sum(e_w, axis=0, keepdims=True)).astype(bf16)
    o_w_t = jnp.dot(vwt_ref[0, sb], p_w[0:NSA_KB], preferred_element_type=f32)
    for i in range(1, nwb):
        o_w_t = o_w_t + jnp.dot(vwt_ref[0, sb + i], p_w[i * NSA_KB:(i + 1) * NSA_KB], preferred_element_type=f32)
    o_w = jnp.transpose(o_w_t)

    gates = jax.nn.sigmoid(misc_ref[...])
    gl = lax.broadcasted_iota(i32, gates.shape, 1)

    def gate_col(g):
        return jnp.concatenate([jnp.sum(jnp.where(gl == MISC_GATE0 + 3 * h + g, gates, 0.0), axis=1, keepdims=True)
                                for h in range(N_HEADS)], axis=0)

    out = gate_col(0) * _stack_heads(oc_ref).astype(f32) + gate_col(1) * o_s + gate_col(2) * o_w
    pair = lambda a, b: a + pltpu.roll(b, HEAD_DIM, axis=1)
    o_ref[...] = jnp.concatenate([pair(out[0:NSA_TQ], out[NSA_TQ:2 * NSA_TQ]),
                                  pair(out[2 * NSA_TQ:3 * NSA_TQ], out[3 * NSA_TQ:])], axis=1).astype(bf16)


def _nsa(nq, nqr, misc, ks, vs, kw, vw, kc, vc, ov, B, T):
    N = B * T
    nc = T // NSA_TQ
    ns = kc.shape[1]
    assert T >= NSA_TQ + WIN and T % KEY_TILE == 0 and KEY_TILE % NSA_TQ == 0 and NSA_TQ == NSA_KB
    tok = lambda w_: pl.BlockSpec((NSA_TQ, w_), lambda b, c: (b * nc + c, 0))
    seq = lambda w_: pl.BlockSpec((T, w_), lambda b, c: (b, 0), pipeline_mode=pl.Buffered(1))
    o_c, selb = pl.pallas_call(
        functools.partial(_nsa_select_kernel, n_cmp=(T - CMP_BLOCK) // CMP_STRIDE + 1), grid=(B, nc),
        in_specs=[tok(512), pl.BlockSpec((1, ns, LANES), lambda b, c: (b, 0, 0)),
                  pl.BlockSpec((1, LANES, ns), lambda b, c: (b, 0, 0)), pl.BlockSpec((LANES, ns), lambda b, c: (0, 0))],
        out_specs=[tok(512), tok(LANES)],
        out_shape=[jax.ShapeDtypeStruct((N, 512), bf16), jax.ShapeDtypeStruct((N, LANES), bf16)],
        compiler_params=_cparams("arbitrary", "arbitrary"), name="nsa_select")(
            nq, kc, vc.transpose(0, 2, 1), ov.T)
    seq_t = lambda kb: pl.BlockSpec((1, T // kb, LANES, kb), lambda b, c: (b, 0, 0, 0), pipeline_mode=pl.Buffered(1))
    return pl.pallas_call(
        _nsa_attend_kernel, grid=(B, nc),
        in_specs=[tok(512), tok(LANES), tok(512), tok(LANES), seq(256), seq_t(KEY_TILE), seq(LANES), seq_t(NSA_KB)],
        out_specs=tok(256), out_shape=jax.ShapeDtypeStruct((N, 256), bf16),
        compiler_params=_cparams("arbitrary", "arbitrary"), name="nsa_attend")(
            nqr, selb, o_c, misc, ks, _blocks_t(vs, B, T, KEY_TILE), kw, _blocks_t(vw, B, T, NSA_KB))


POOL_TM = 512
POOL_HALO = 16


def _pool_kernel(u_ref, halo_ref, w_ref, sc_ref, o_ref):
    t = pl.program_id(1)
    halo = jnp.where(t == 0, 0.0, halo_ref[...])
    ext = jnp.concatenate([halo, u_ref[...]], axis=0)
    s2 = ext + pltpu.roll(ext, 1, axis=0)
    s4 = s2 + pltpu.roll(s2, 2, axis=0)
    s8 = s4 + pltpu.roll(s4, 4, axis=0)
    s16 = s8 + pltpu.roll(s8, 8, axis=0)
    pos1 = jnp.maximum(t * POOL_TM - POOL_HALO + 1 + lax.broadcasted_iota(i32, ext.shape, 0), 1).astype(f32)
    grp = lax.broadcasted_iota(i32, ext.shape, 1) // HEAD_DIM
    mean = jnp.where(grp == 0, s2 / jnp.minimum(pos1, 2.0),
                     jnp.where(grp == 1, s4 / jnp.minimum(pos1, 4.0),
                               jnp.where(grp == 2, s8 / jnp.minimum(pos1, 8.0), s16 / jnp.minimum(pos1, 16.0))))
    pooled = (mean - ext)[POOL_HALO:, :]
    o_ref[...] = (jnp.dot(pooled.astype(bf16), w_ref[...], preferred_element_type=f32) * sc_ref[...]).astype(bf16)


def _pool(u, wbd, scale, B, T):
    N = B * T
    tm = min(POOL_TM, T)
    assert tm == POOL_TM and T % POOL_TM == 0
    nt = T // tm
    hb = tm // POOL_HALO
    return pl.pallas_call(
        _pool_kernel, grid=(B, nt),
        in_specs=[pl.BlockSpec((tm, 256), lambda b, t: (b * nt + t, 0)),
                  pl.BlockSpec((POOL_HALO, 256), lambda b, t: (jnp.maximum((b * nt + t) * hb - 1, 0), 0)),
                  pl.BlockSpec((256, 256), lambda b, t: (0, 0)),
                  pl.BlockSpec((1, 256), lambda b, t: (0, 0))],
        out_specs=pl.BlockSpec((tm, 256), lambda b, t: (b * nt + t, 0)),
        out_shape=jax.ShapeDtypeStruct((N, 256), bf16),
        compiler_params=_cparams("arbitrary", "arbitrary"), name="pool")(u, u, wbd, scale)


GLA_TM = 256


def _gla_kernel(q_ref, k_ref, v_ref, g_ref, misc_ref, wa_ref, ba_ref, gn_ref, bd_ref, o_ref,
                st_ref, q_s, k_s, v_s, b_s, qe_s, ke_s, gam_s, o_s):
    @pl.when(pl.program_id(1) == 0)
    def _():
        st_ref[...] = jnp.zeros_like(st_ref)

    x = jnp.dot(misc_ref[...], wa_ref[...], preferred_element_type=f32, precision=lax.Precision.HIGHEST) + ba_ref[...]
    log_a = (jnp.minimum(x, 0.0) - jnp.log1p(jnp.exp(-jnp.abs(x)))) / GLA_TAU
    r16 = lax.broadcasted_iota(i32, log_a.shape, 0) % GLA_SUB
    b = log_a
    for s in (1, 2, 4, 8):
        b = b + jnp.where(r16 >= s, pltpu.roll(b, s, axis=0), 0.0)
    b_end = jnp.where(r16 == GLA_SUB - 1, b, 0.0)
    for s in (1, 2, 4, 8):
        b_end = b_end + pltpu.roll(b_end, GLA_TM - s, axis=0)
    q = q_ref[...] * QK_SCALE
    k = k_ref[...]
    q_s[...] = q
    k_s[...] = k
    v_s[...] = v_ref[...]
    b_s[...] = b
    qe_s[...] = (q * jnp.exp(b)).astype(bf16)
    ke_s[...] = (k * jnp.exp(b_end - b)).astype(bf16)
    gam_s[...] = jnp.exp(b_end)
    bd = bd_ref[...]
    shape3 = (GLA_SUB, GLA_SUB, 256)
    causal = lax.broadcasted_iota(i32, shape3, 0) <= lax.broadcasted_iota(i32, shape3, 1)

    def block(n, _):
        r0 = pl.multiple_of(n * GLA_SUB, GLA_SUB)
        rows = pl.ds(r0, GLA_SUB)
        qi, ki, vi, bi = q_s[rows, :], k_s[rows, :], v_s[rows, :], b_s[rows, :]
        diff = jnp.where(causal, bi[None, :, :] - bi[:, None, :], 0.0)
        w3 = jnp.where(causal, qi[None, :, :] * ki[:, None, :] * jnp.exp(diff), 0.0)
        a3 = jnp.dot(w3.reshape(GLA_SUB * GLA_SUB, 256).astype(bf16), bd, preferred_element_type=f32)
        intra = jnp.sum(a3.reshape(shape3) * vi[:, None, :], axis=0)
        st = st_ref[...]
        inter = lax.dot_general(qe_s[rows, :], st.astype(bf16), NT, preferred_element_type=f32)
        o_s[rows, :] = intra + inter
        upd = lax.dot_general(vi.astype(bf16), ke_s[rows, :], TN, preferred_element_type=f32)
        st_ref[...] = st * gam_s[pl.ds(r0, 1), :] + jnp.where(bd > 0, upd, 0.0)
        return 0

    lax.fori_loop(0, GLA_TM // GLA_SUB, block, 0)
    o = o_s[...]
    ms = jnp.dot(o * o, bd.astype(f32), preferred_element_type=f32, precision=lax.Precision.HIGHEST) / HEAD_DIM
    o_ref[...] = (o * lax.rsqrt(ms + LN_EPS) * gn_ref[...] * _silu(g_ref[...])).astype(bf16)


def _gla(gq, gk, gv, gg, misc, wa, ba, gn, bd, B, T):
    N = B * T
    nt = T // GLA_TM
    assert T % GLA_TM == 0
    tok = lambda w_: pl.BlockSpec((GLA_TM, w_), lambda b, t: (b * nt + t, 0))
    full = lambda a: pl.BlockSpec(a.shape, lambda b, t: (0, 0))
    v = lambda dt: pltpu.VMEM((GLA_TM, 256), dt)
    return pl.pallas_call(
        _gla_kernel, grid=(B, nt),
        in_specs=[tok(256), tok(256), tok(256), tok(256), tok(128), full(wa), full(ba), full(gn), full(bd)],
        out_specs=tok(256), out_shape=jax.ShapeDtypeStruct((N, 256), bf16),
        scratch_shapes=[pltpu.VMEM((256, 256), f32), v(f32), v(f32), v(f32), v(f32), v(bf16), v(bf16), v(f32), v(f32)],
        compiler_params=_cparams("arbitrary", "arbitrary"), name="gla")(gq, gk, gv, gg, misc, wa, ba, gn, bd)


OUT_TM = 256


def _outproj_kernel(mo_ref, no_ref, po_ref, go_ref, x_ref, mod_ref, w_ref, lng_ref, lnb_ref, wrh_ref, wrl_ref,
                    x1_ref, h2t_ref, h2b_ref, lg_ref, *, alpha):
    a = jnp.concatenate([mo_ref[...], no_ref[...], po_ref[...], go_ref[...]], axis=1)
    y = jnp.dot(a, w_ref[...], preferred_element_type=f32)
    x1 = _layer_norm(alpha * x_ref[...] + mod_ref[0, 2:3, :] * y, lng_ref[...], lnb_ref[...])
    x1_ref[...] = x1
    h2 = x1 * (1.0 + mod_ref[0, 4:5, :]) + mod_ref[0, 3:4, :]
    h2b_ref[...] = h2.astype(bf16)
    for s, w in enumerate(_pack_rows(h2)):
        h2t_ref[pl.ds(s, OUT_TM, stride=ROW_TILE), :] = w
    hi, lo = _split_bf16(h2)
    wrh = wrh_ref[...]
    lg_ref[...] = (jnp.dot(hi, wrh, preferred_element_type=f32) + jnp.dot(lo, wrh, preferred_element_type=f32)
                   + jnp.dot(hi, wrl_ref[...], preferred_element_type=f32))


def _outproj(mo, no, po, go, x2, mod, w, lng, lnb, wrh, wrl, alpha, B, T):
    N = B * T
    nt = T // OUT_TM
    tok = lambda w_: pl.BlockSpec((OUT_TM, w_), lambda i: (i, 0))
    full = lambda a: pl.BlockSpec(a.shape, lambda i: (0,) * a.ndim)
    return pl.pallas_call(
        functools.partial(_outproj_kernel, alpha=alpha), grid=(N // OUT_TM,),
        in_specs=[tok(512), tok(256), tok(256), tok(256), tok(D_MODEL),
                  pl.BlockSpec((1, 6, D_MODEL), lambda i: (i // nt, 0, 0)),
                  full(w), full(lng), full(lnb), full(wrh), full(wrl)],
        out_specs=[tok(D_MODEL), pl.BlockSpec((OUT_TM * ROW_TILE, LANES), lambda i: (i, 0)), tok(D_MODEL),
                   tok(N_EXPERTS)],
        out_shape=[jax.ShapeDtypeStruct((N, D_MODEL), f32), jax.ShapeDtypeStruct((N * ROW_TILE, LANES), u32),
                   jax.ShapeDtypeStruct((N, D_MODEL), bf16), jax.ShapeDtypeStruct((N, N_EXPERTS), f32)],
        compiler_params=_cparams("arbitrary"), name="outproj")(mo, no, po, go, x2, mod, w, lng, lnb, wrh, wrl)


ROUTE_TM = 256
GROUP_SIZE = N_EXPERTS // N_EXPERT_GROUPS


def _route_kernel(lg_ref, rb_ref, tri_ref, ei_ref, wt_ref, rk_ref, cnt_ref, base_ref):
    @pl.when(pl.program_id(0) == 0)
    def _():
        base_ref[...] = jnp.zeros_like(base_ref)

    s = jax.nn.sigmoid(jnp.transpose(lg_ref[...]))
    ssel = s + rb_ref[...]
    shape3 = (N_EXPERT_GROUPS, GROUP_SIZE, ROUTE_TM)
    x3 = ssel.reshape(shape3)
    i3 = lax.broadcasted_iota(i32, shape3, 1)
    m1 = jnp.max(x3, axis=1, keepdims=True)
    first = jnp.min(jnp.where(x3 == m1, i3, BIG_IDX), axis=1, keepdims=True)
    m2 = jnp.max(jnp.where(i3 == first, -jnp.inf, x3), axis=1, keepdims=True)
    gscore = (m1 + m2).reshape(N_EXPERT_GROUPS, ROUTE_TM)
    gid = lax.broadcasted_iota(i32, gscore.shape, 0)
    beaten = jnp.zeros(gscore.shape, i32)
    for g in range(N_EXPERT_GROUPS):
        other = gscore[g:g + 1, :]
        beaten = beaten + jnp.where((other > gscore) | ((other == gscore) & (g < gid)), 1, 0)
    keep = jnp.broadcast_to((beaten < TOPK_GROUPS)[:, None, :], shape3).reshape(ssel.shape)
    x = jnp.where(keep, ssel, -jnp.inf)
    eid = lax.broadcasted_iota(i32, x.shape, 0)
    hits, idx_rows, w_rows = [], [], []
    for _ in range(TOP_K):
        mx = jnp.max(x, axis=0, keepdims=True)
        idx = jnp.min(jnp.where(x == mx, eid, BIG_IDX), axis=0, keepdims=True)
        hit = eid == idx
        hits.append(hit)
        idx_rows.append(idx.astype(f32))
        w_rows.append(jnp.sum(jnp.where(hit, s, 0.0), axis=0, keepdims=True))
        x = jnp.where(hit, -jnp.inf, x)
    wsum = w_rows[0]
    for w in w_rows[1:]:
        wsum = wsum + w
    chosen = jnp.zeros(s.shape, f32)
    for hit in hits:
        chosen = jnp.where(hit, 1.0, chosen)
    before = base_ref[...] + jnp.dot(chosen.astype(bf16), tri_ref[...], preferred_element_type=f32)
    rk_rows = [jnp.sum(jnp.where(hit, before, 0.0), axis=0, keepdims=True) for hit in hits]
    total = base_ref[...] + jnp.sum(chosen, axis=1, keepdims=True)
    base_ref[...] = total
    cnt_ref[...] = total.astype(i32)

    def per_token(rows):
        return jnp.transpose(jnp.concatenate(rows + [jnp.zeros((LANES - TOP_K, ROUTE_TM), f32)], axis=0))

    ei_ref[...] = per_token(idx_rows).astype(i32)
    wt_ref[...] = per_token([w / wsum * ROUTED_SCALE for w in w_rows])
    rk_ref[...] = per_token(rk_rows).astype(i32)


def _route(logits, rb):
    N = logits.shape[0]
    r_, c_ = jnp.arange(ROUTE_TM)[:, None], jnp.arange(ROUTE_TM)[None, :]
    tri = (r_ < c_).astype(bf16)
    tok = pl.BlockSpec((ROUTE_TM, LANES), lambda i: (i, 0))
    col = pl.BlockSpec((N_EXPERTS, 1), lambda i: (0, 0))
    return pl.pallas_call(
        _route_kernel, grid=(N // ROUTE_TM,),
        in_specs=[pl.BlockSpec((ROUTE_TM, N_EXPERTS), lambda i: (i, 0)), col,
                  pl.BlockSpec((ROUTE_TM, ROUTE_TM), lambda i: (0, 0))],
        out_specs=[tok, tok, tok, col],
        out_shape=[jax.ShapeDtypeStruct((N, LANES), i32), jax.ShapeDtypeStruct((N, LANES), f32),
                   jax.ShapeDtypeStruct((N, LANES), i32), jax.ShapeDtypeStruct((N_EXPERTS, 1), i32)],
        scratch_shapes=[pltpu.VMEM((N_EXPERTS, 1), f32)],
        compiler_params=_cparams("arbitrary"), name="route")(logits, rb.reshape(N_EXPERTS, 1), tri)


DISP_TM = 128


def _dispatch_kernel(h_ref, ei_ref, rk_ref, ps_ref, xs_in, pos_ref, xs_hbm, pos_s, csem, rsem):
    del xs_in
    ei = ei_ref[...].astype(f32)
    lane = lax.broadcasted_iota(i32, ei.shape, 1)
    lane_e = lax.broadcasted_iota(i32, (DISP_TM, N_EXPERTS), 1).astype(f32)
    starts = ps_ref[...].astype(f32)
    pos = jnp.zeros(ei.shape, f32)
    for k in range(TOP_K):
        e_k = jnp.sum(jnp.where(lane == k, ei, 0.0), axis=1, keepdims=True)
        st_k = jnp.sum(jnp.where(lane_e == e_k, starts, 0.0), axis=1, keepdims=True)
        pos = jnp.where(lane == k, st_k, pos)
    pos_ref[...] = jnp.where(lane < TOP_K, pos.astype(i32) + rk_ref[...], 0)
    cp = pltpu.make_async_copy(pos_ref, pos_s, csem)
    cp.start()
    cp.wait()

    def tile(i):
        return pl.ds(pl.multiple_of(i * ROW_TILE, ROW_TILE), ROW_TILE)

    def row_copy(t, k, p):
        return pltpu.make_async_copy(h_ref.at[tile(t)], xs_hbm.at[tile(p)], rsem)

    def issue(t, _):
        for k in range(TOP_K):
            row_copy(t, k, pos_s[t, k]).start(priority=k % 2)
        return 0
    lax.fori_loop(0, DISP_TM, issue, 0)

    def drain(t, _):
        for k in range(TOP_K):
            row_copy(t, k, 0).wait()
        return 0
    lax.fori_loop(0, DISP_TM, drain, 0)


def _dispatch(h2t, eidx, rank, pstarts, n_rows):
    N = h2t.shape[0] // ROW_TILE
    tok = pl.BlockSpec((DISP_TM, LANES), lambda i: (i, 0))
    xs0 = jnp.zeros((n_rows * ROW_TILE, LANES), u32)
    pos, xs = pl.pallas_call(
        _dispatch_kernel, grid=(N // DISP_TM,),
        in_specs=[pl.BlockSpec((DISP_TM * ROW_TILE, LANES), lambda i: (i, 0)), tok, tok,
                  pl.BlockSpec((1, N_EXPERTS), lambda i: (0, 0)), pl.BlockSpec(memory_space=pl.ANY)],
        out_specs=[tok, pl.BlockSpec(memory_space=pl.ANY)],
        out_shape=[jax.ShapeDtypeStruct((N, LANES), i32), jax.ShapeDtypeStruct(xs0.shape, u32)],
        scratch_shapes=[pltpu.SMEM((DISP_TM, LANES), i32), pltpu.SemaphoreType.DMA, pltpu.SemaphoreType.DMA],
        input_output_aliases={4: 1},
        compiler_params=_cparams("arbitrary"), name="moe_dispatch")(h2t, eidx, rank, pstarts, xs0)
    return pos, xs


MOE_BLK = MOE_ROWS * ROW_TILE


def _moe_kernel(b0_ref, nb_ref, nu_ref, xs_in, wg_ref, wu_ref, wd_ref, xs_out, xbuf, ybuf, isem, osem):
    e = pl.program_id(0)
    nu = nu_ref[0]

    def in_copy(b):
        return pltpu.make_async_copy(xs_in.at[pl.ds(pl.multiple_of(b * MOE_BLK, MOE_BLK), MOE_BLK)],
                                     xbuf.at[b % 2], isem.at[b % 2])

    def out_copy(b):
        return pltpu.make_async_copy(ybuf.at[b % 2],
                                     xs_out.at[pl.ds(pl.multiple_of(b * MOE_BLK, MOE_BLK), MOE_BLK)], osem.at[b % 2])

    @pl.when(e == 0)
    def _():
        in_copy(0).start()

    wg = wg_ref[0, 0].astype(bf16)
    wu = wu_ref[0, 0].astype(bf16)
    wd = wd_ref[0, 0].astype(bf16)

    def block(i, _):
        b = b0_ref[e] + i
        slot = b % 2
        in_copy(b).wait()

        @pl.when(b + 1 < nu)
        def _():
            in_copy(b + 1).start()

        @pl.when(b >= 2)
        def _():
            out_copy(b - 2).wait()

        g = jnp.zeros((MOE_ROWS, D_EXPERT), f32)
        u = jnp.zeros((MOE_ROWS, D_EXPERT), f32)
        for s in range(ROW_TILE):
            xc = jnp.concatenate(_unpack_word(xbuf[slot, pl.ds(s, MOE_ROWS, stride=ROW_TILE), :]), axis=1).astype(bf16)
            rows = slice(2 * s * LANES, (2 * s + 2) * LANES)
            g = g + jnp.dot(xc, wg[rows], preferred_element_type=f32)
            u = u + jnp.dot(xc, wu[rows], preferred_element_type=f32)
        y = jnp.dot((_silu(g) * u).astype(bf16), wd, preferred_element_type=f32)
        for s, w in enumerate(_pack_rows(y)):
            ybuf[slot, pl.ds(s, MOE_ROWS, stride=ROW_TILE), :] = w
        out_copy(b).start()
        return 0

    lax.fori_loop(0, nb_ref[e], block, 0)

    @pl.when(e == pl.num_programs(0) - 1)
    def _():
        @pl.when(nu >= 2)
        def _():
            out_copy(nu - 2).wait()
        out_copy(nu - 1).wait()


def _moe(layer, blk0, nblk, n_used, xs, wg, wu, wd):
    wspec = lambda a: pl.BlockSpec((1, 1) + a.shape[2:], lambda e, b0, nb, nu: (layer, e, 0, 0))
    gs = pltpu.PrefetchScalarGridSpec(
        num_scalar_prefetch=3, grid=(N_EXPERTS,),
        in_specs=[pl.BlockSpec(memory_space=pl.ANY), wspec(wg), wspec(wu), wspec(wd)],
        out_specs=pl.BlockSpec(memory_space=pl.ANY),
        scratch_shapes=[pltpu.VMEM((2, MOE_BLK, LANES), u32), pltpu.VMEM((2, MOE_BLK, LANES), u32),
                        pltpu.SemaphoreType.DMA((2,)), pltpu.SemaphoreType.DMA((2,))])
    return pl.pallas_call(
        _moe_kernel, grid_spec=gs, out_shape=jax.ShapeDtypeStruct(xs.shape, u32),
        input_output_aliases={3: 0},
        compiler_params=_cparams("arbitrary"), name="moe_experts")(blk0, nblk, n_used, xs, wg, wu, wd)


FIN_TM = 128


def _tile_gather(pos_hbm, ys_hbm, pos_s, gbuf, isem, rsem, step, n_steps):
    def idx_copy(s):
        return pltpu.make_async_copy(pos_hbm.at[pl.ds(s * FIN_TM, FIN_TM)], pos_s.at[s % 2], isem.at[s % 2])

    def tile(i):
        return pl.ds(pl.multiple_of(i * ROW_TILE, ROW_TILE), ROW_TILE)

    def row_copy(slot, t, k, p):
        return pltpu.make_async_copy(ys_hbm.at[tile(p)], gbuf.at[slot, k, tile(t)], rsem.at[slot])

    def start_rows(s):
        slot = s % 2

        def body(t, _):
            for k in range(TOP_K):
                row_copy(slot, t, k, pos_s[slot, t, k]).start(priority=k % 2)
            return 0
        lax.fori_loop(0, FIN_TM, body, 0)

    @pl.when(step == 0)
    def _():
        idx_copy(step).start()
        idx_copy(step).wait()
        start_rows(step)

        @pl.when(n_steps > 1)
        def _():
            idx_copy(step + 1).start()

    @pl.when(step + 1 < n_steps)
    def _():
        idx_copy(step + 1).wait()
        start_rows(step + 1)

        @pl.when(step + 2 < n_steps)
        def _():
            idx_copy(step + 2).start()

    slot = step % 2

    def wait_body(t, _):
        for k in range(TOP_K):
            row_copy(slot, t, k, 0).wait()
        return 0
    lax.fori_loop(0, FIN_TM, wait_body, 0)


def _fin_kernel(pos_hbm, ys_hbm, h2_ref, x1_ref, wt_ref, mod_ref, sg_ref, su_ref, sd_ref, lng_ref, lnb_ref, o_ref,
                pos_s, gbuf, isem, rsem, *, alpha):
    i = pl.program_id(0)
    _tile_gather(pos_hbm, ys_hbm, pos_s, gbuf, isem, rsem, i, pl.num_programs(0))
    hb = h2_ref[...]
    g = jnp.dot(hb, sg_ref[...], preferred_element_type=f32)
    u = jnp.dot(hb, su_ref[...], preferred_element_type=f32)
    shared = jnp.dot((_silu(g) * u).astype(bf16), sd_ref[...], preferred_element_type=f32)
    wt = wt_ref[...]
    lane = lax.broadcasted_iota(i32, wt.shape, 1)
    wk = [jnp.sum(jnp.where(lane == k, wt, 0.0), axis=1, keepdims=True) for k in range(TOP_K)]
    slot = i % 2
    cols = []
    for s in range(ROW_TILE):
        sub = pl.ds(s, FIN_TM, stride=ROW_TILE)
        lo, hi = _unpack_word(gbuf[slot, 0, sub, :])
        acc_lo, acc_hi = lo * wk[0], hi * wk[0]
        for k in range(1, TOP_K):
            lo, hi = _unpack_word(gbuf[slot, k, sub, :])
            acc_lo, acc_hi = acc_lo + lo * wk[k], acc_hi + hi * wk[k]
        cols += [acc_lo, acc_hi]
    y = shared + jnp.concatenate(cols, axis=1)
    o_ref[...] = _layer_norm(alpha * x1_ref[...] + mod_ref[0, 5:6, :] * y, lng_ref[...], lnb_ref[...])


def _fin(pos, ys, h2b, x1, wts, mod, sg, su, sd, lng, lnb, alpha, B, T):
    N = B * T
    nt = T // FIN_TM
    tok = lambda w_: pl.BlockSpec((FIN_TM, w_), lambda i: (i, 0))
    full = lambda a: pl.BlockSpec(a.shape, lambda i: (0,) * a.ndim)
    return pl.pallas_call(
        functools.partial(_fin_kernel, alpha=alpha), grid=(N // FIN_TM,),
        in_specs=[pl.BlockSpec(memory_space=pl.ANY), pl.BlockSpec(memory_space=pl.ANY),
                  tok(D_MODEL), tok(D_MODEL), tok(LANES),
                  pl.BlockSpec((1, 6, D_MODEL), lambda i: (i // nt, 0, 0)),
                  full(sg), full(su), full(sd), full(lng), full(lnb)],
        out_specs=tok(D_MODEL), out_shape=jax.ShapeDtypeStruct((N, D_MODEL), f32),
        scratch_shapes=[pltpu.SMEM((2, FIN_TM, LANES), i32), pltpu.VMEM((2, TOP_K, FIN_TM * ROW_TILE, LANES), u32),
                        pltpu.SemaphoreType.DMA((2,)), pltpu.SemaphoreType.DMA((2,))],
        compiler_params=_cparams("arbitrary"), name="combine")(pos, ys, h2b, x1, wts, mod, sg, su, sd, lng, lnb)


def _pack_w_in(w):
    c = lambda name, width: w[:, _OFF[name]:_OFF[name] + width]
    cols = [c("moba_q", 256), c("moba_k", 256), c("moba_v", 256), c("nsa_q", 256),
            c("k_slc", 64), c("k_win", 64), c("v_slc", 64), c("v_win", 64), c("k_cmp", 64), c("v_cmp", 64),
            c("pool", 256), c("gla_q", 256), c("gla_k", 256), c("gla_v", 256), c("gla_g", 256),
            c("nsa_gate", 12), c("gla_a", 16), jnp.zeros((w.shape[0], LANES - 28), w.dtype)]
    return jnp.concatenate(cols, axis=1).astype(bf16)


def _rope_tables(T):
    half = ROPE_DIM // 2
    inv_freq = ROPE_THETA ** (-jnp.arange(half, dtype=f32) / half)
    ang = jnp.arange(T).astype(f32)[:, None] * inv_freq[None, :]
    cos, sin = jnp.cos(ang), jnp.sin(ang)
    one = jnp.ones((T, HEAD_DIM - ROPE_DIM), f32)
    zero = jnp.zeros((T, HEAD_DIM - ROPE_DIM), f32)
    ct = jnp.concatenate([cos, cos, one], axis=1)
    st = jnp.concatenate([-sin, sin, zero], axis=1)
    return jnp.tile(ct, (1, 2)), jnp.tile(st, (1, 2))


def _cmp_weights(pe, w1, w2):
    half = CMP_BLOCK // 2
    z = jnp.zeros((half, HEAD_DIM, CMP_HIDDEN), f32)

    def arrange(lo):
        wk = w1[0].reshape(CMP_BLOCK, HEAD_DIM, CMP_HIDDEN)[lo:lo + half]
        wv = w1[1].reshape(CMP_BLOCK, HEAD_DIM, CMP_HIDDEN)[lo:lo + half]
        top = jnp.concatenate([wk, z], axis=2)
        bot = jnp.concatenate([z, wv], axis=2)
        return jnp.concatenate([top, bot], axis=1).reshape(half * 2 * HEAD_DIM, 2 * CMP_HIDDEN).astype(bf16)

    def pe_row(lo):
        return jnp.concatenate([pe[0, lo:lo + half], pe[1, lo:lo + half]], axis=1).reshape(1, half * 2 * HEAD_DIM)

    zc = jnp.zeros((CMP_HIDDEN, LANES - HEAD_DIM), f32)
    zr = jnp.zeros((CMP_HIDDEN, LANES), f32)
    w2k = jnp.concatenate([jnp.concatenate([w2[0], zc], axis=1), zr], axis=0).astype(bf16)
    w2v = jnp.concatenate([zr, jnp.concatenate([w2[1], zc], axis=1)], axis=0).astype(bf16)
    return arrange(0), arrange(half), pe_row(0), pe_row(half), w2k, w2v


def _overlap_matrix(ns):
    n = jnp.arange(ns)[:, None] * CMP_STRIDE
    j = jnp.arange(LANES)[None, :] * SLC_BLOCK
    ov = (n < j + SLC_BLOCK) & (n + CMP_BLOCK > j) & (jnp.arange(ns)[:, None] < ns - 1)
    return ov.astype(bf16)


def _block_diag_ones():
    h = jnp.arange(256) // HEAD_DIM
    return (h[:, None] == h[None, :]).astype(bf16)


def _expert_layout(counts, n_tok):
    counts = counts.reshape(N_EXPERTS)
    padded = (counts + MOE_ROWS - 1) // MOE_ROWS * MOE_ROWS
    pstarts = (jnp.cumsum(padded) - padded).astype(i32)
    n_blocks = n_tok * TOP_K // MOE_ROWS + N_EXPERTS
    n_used = (jnp.sum(padded) // MOE_ROWS).astype(i32).reshape(1)
    return (pstarts.reshape(1, N_EXPERTS), pstarts // MOE_ROWS, (padded // MOE_ROWS).astype(i32), n_used,
            n_blocks * MOE_ROWS)


def _mixer_inputs(x2, mod, w_in, B, T):
    ct, st = _rope_tables(T)
    return _inproj(x2, mod, _pack_w_in(w_in), ct, st, B, T)


def _token_mixers(x2, mod, w_in, cmp_pe, cmp_w1, cmp_w2, pool_w, pool_scale, gla_wa, gla_ba, gla_norm, B, T):
    N = B * T
    (mq, mk, mv, km, nq, nqr, ks, kw, vs, vw, kvc, pool_u, gq, gk, gv, gg, misc) = _mixer_inputs(x2, mod, w_in, B, T)
    nt = T // MOBA_BLOCK
    kmh = km.reshape(B, nt, N_HEADS, HEAD_DIM).transpose(0, 2, 1, 3)
    mo = _moba(mq, mk, mv, jnp.pad(kmh, ((0, 0), (0, 0), (0, MOBA_NBLK - nt), (0, LANES - HEAD_DIM))), B, T)
    ns = T // CMP_STRIDE
    kc, vc = _cmp(kvc.reshape(B, ns, CMP_STRIDE * LANES), *_cmp_weights(cmp_pe, cmp_w1, cmp_w2))
    no = _nsa(nq, nqr, misc, ks, vs, kw, vw, kc, vc, _overlap_matrix(ns), B, T)
    wbd = jax.scipy.linalg.block_diag(*[pool_w[g] for g in range(len(POOL_WINDOWS))]).astype(bf16)
    po = _pool(pool_u, wbd, pool_scale.reshape(1, 256), B, T)
    wa = jnp.zeros((LANES, 256), f32).at[MISC_A0:MISC_A0 + GLA_LOWRANK].set(gla_wa)
    go = _gla(gq, gk, gv, gg, misc, wa, gla_ba.reshape(1, 256), jnp.tile(gla_norm, N_HEADS).reshape(1, 256),
              _block_diag_ones(), B, T)
    return mo, no, po, go


def _pad_w_out(w_out):
    wm = w_out[:GROUP_WIDTH].reshape(N_HEADS, HEAD_DIM, D_MODEL)
    wm = jnp.pad(wm, ((0, 0), (0, LANES - HEAD_DIM), (0, 0))).reshape(N_HEADS * LANES, D_MODEL)
    return jnp.concatenate([wm, w_out[GROUP_WIDTH:]], axis=0).astype(bf16)


def kernel(x, c, w_ada, b_ada, w_in, cmp_pe, cmp_w1, cmp_w2, pool_w, pool_scale, gla_wa, gla_ba, gla_norm, w_out,
           ln_g, ln_b, w_router, router_bias, exp_gate, exp_up, exp_down, sh_gate, sh_up, sh_down):
    B, T, D = x.shape
    N = B * T
    depth = w_ada.shape[0]
    alpha = float((2 * depth) ** 0.25)
    x2 = x.reshape(N, D)
    c8 = jnp.zeros((8, D), f32).at[:B].set(c)
    for l in range(depth):
        mod = _ada(c8, w_ada[l], b_ada[l].reshape(1, -1))[:B].reshape(B, 6, D)
        mo, no, po, go = _token_mixers(x2, mod, w_in[l], cmp_pe[l], cmp_w1[l], cmp_w2[l], pool_w[l], pool_scale[l],
                                       gla_wa[l], gla_ba[l], gla_norm[l], B, T)
        wrh, wrl = _split_bf16(w_router[l])
        x1, h2t, h2b, logits = _outproj(mo, no, po, go, x2, mod, _pad_w_out(w_out[l]), ln_g[l, 0].reshape(1, D),
                                        ln_b[l, 0].reshape(1, D), wrh, wrl, alpha, B, T)
        eidx, wts, rank, counts = _route(logits, router_bias[l].reshape(1, N_EXPERTS))
        pstarts, blk0, nblk, n_used, n_rows = _expert_layout(counts, N)
        pos, xs = _dispatch(h2t, eidx, rank, pstarts, n_rows)
        ys = _moe(l, blk0, nblk, n_used, xs, exp_gate, exp_up, exp_down)
        x2 = _fin(pos, ys, h2b, x1, wts, mod, sh_gate[l].astype(bf16), sh_up[l].astype(bf16),
                  sh_down[l].astype(bf16), ln_g[l, 1].reshape(1, D), ln_b[l, 1].reshape(1, D), alpha, B, T)
    return x2.reshape(B, T, D)
```

```python
import functools

import jax
import jax.numpy as jnp
from jax import lax
from jax.experimental import pallas as pl
from jax.experimental.pallas import tpu as pltpu

f32, bf16, i32, u32 = jnp.float32, jnp.bfloat16, jnp.int32, jnp.uint32

D_MODEL = 1024
HEAD_DIM = 64
N_HEADS = 4
GROUP_WIDTH = 256
ROPE_THETA = 500000.0
ROPE_DIM = 16
MOBA_BLOCK = 256
MOBA_TOPK = 3
CMP_BLOCK = 32
CMP_STRIDE = 16
CMP_HIDDEN = 128
SLC_BLOCK = 64
SLC_TOPN = 16
WIN = 512
FORCE_SCORE = 1e9
POOL_WINDOWS = (2, 4, 8, 16)
GLA_SUB = 16
GLA_LOWRANK = 16
GLA_TAU = 16.0
N_EXPERTS = 256
TOP_K = 8
N_EXPERT_GROUPS = 8
TOPK_GROUPS = 4
D_EXPERT = 256
ROUTED_SCALE = 2.5
LN_EPS = 1e-5
QK_SCALE = HEAD_DIM ** -0.5
LOG2E = 1.4426950408889634
Q_SCALE = QK_SCALE * LOG2E

LANES = 128
MASKED = -1e30
M_INIT = -3e38
BIG_IDX = 1 << 20
MOE_ROWS = 256
ROW_TILE = D_MODEL // (2 * LANES)
VMEM_LIMIT = 56 * 1024 * 1024

_OFF = dict(moba_q=0, moba_k=256, moba_v=512, nsa_q=768, k_cmp=1024, v_cmp=1088, k_slc=1152, v_slc=1216,
            k_win=1280, v_win=1344, nsa_gate=1408, pool=1420, gla_q=1676, gla_k=1932, gla_v=2188,
            gla_a=2444, gla_g=2460)
_S = dict(mq=0, mk=256, mv=512, nq=768, sw=1024, vsw=1152, kvc=1280, pool=1408, gq=1664, gk=1920, gv=2176,
          gg=2432, misc=2688)
IN_COLS_PACKED = 2816
MISC_GATE0 = 0
MISC_A0 = 12

NT = (((1,), (1,)), ((), ()))
TN = (((0,), (0,)), ((), ()))


def _cparams(*sem):
    return pltpu.CompilerParams(dimension_semantics=sem, vmem_limit_bytes=VMEM_LIMIT)


def _silu(x):
    return x * jax.nn.sigmoid(x)


def _split_bf16(x):
    hi = x.astype(bf16)
    lo = (x - hi.astype(f32)).astype(bf16)
    return hi, lo


def _pack_rows(x):
    bits = lambda v: lax.bitcast_convert_type(v.astype(bf16).astype(f32), u32)
    return [(bits(x[:, (2 * s) * LANES:(2 * s + 1) * LANES]) >> 16) | bits(x[:, (2 * s + 1) * LANES:(2 * s + 2) * LANES])
            for s in range(ROW_TILE)]


def _unpack_word(w):
    return (lax.bitcast_convert_type(w << 16, f32), lax.bitcast_convert_type(w & jnp.uint32(0xFFFF0000), f32))


def _layer_norm(z, g, b):
    mu = jnp.mean(z, axis=-1, keepdims=True)
    zc = z - mu
    var = jnp.mean(zc * zc, axis=-1, keepdims=True)
    return zc * lax.rsqrt(var + LN_EPS) * g + b


def _argmax_rounds(score, index, rounds, axis=1):
    picked = jnp.zeros(score.shape, f32)
    for _ in range(rounds):
        mx = jnp.max(score, axis=axis, keepdims=True)
        first = jnp.min(jnp.where(score == mx, index, BIG_IDX), axis=axis, keepdims=True)
        hit = index == first
        picked = jnp.where(hit, 1.0, picked)
        score = jnp.where(hit, -jnp.inf, score)
    return picked


def _softmax_steps_t(s_list, carries, v_list):
    m_new = [jnp.maximum(c[0], jnp.max(s, axis=0, keepdims=True)) for s, c in zip(s_list, carries)]
    p = [jnp.exp2(s - m) for s, m in zip(s_list, m_new)]
    pv = [jnp.dot(v, pi.astype(bf16), preferred_element_type=f32) for v, pi in zip(v_list, p)]
    out = []
    for c, m, pi, pvi in zip(carries, m_new, p, pv):
        alpha = jnp.exp2(c[0] - m)
        out.append((m, alpha * c[1] + jnp.sum(pi, axis=0, keepdims=True), alpha * c[2] + pvi))
    return tuple(out)


def _softmax_init_t(groups, queries):
    return tuple((jnp.full((1, queries), M_INIT, f32), jnp.zeros((1, queries), f32),
                  jnp.zeros((LANES, queries), f32)) for _ in range(groups))


def _ada_kernel(c_ref, w_ref, b_ref, o_ref):
    o_ref[...] = jnp.dot(_silu(c_ref[...]), w_ref[...], preferred_element_type=f32,
                         precision=lax.Precision.HIGHEST) + b_ref[...]


def _ada(c8, w, b):
    n = w.shape[1] // D_MODEL
    return pl.pallas_call(
        _ada_kernel, grid=(n,),
        in_specs=[pl.BlockSpec((8, D_MODEL), lambda j: (0, 0)),
                  pl.BlockSpec((D_MODEL, D_MODEL), lambda j: (0, j)),
                  pl.BlockSpec((1, D_MODEL), lambda j: (0, j))],
        out_specs=pl.BlockSpec((8, D_MODEL), lambda j: (0, j)),
        out_shape=jax.ShapeDtypeStruct((8, w.shape[1]), f32),
        compiler_params=_cparams("arbitrary"), name="ada")(c8, w, b)


IN_TM = MOBA_BLOCK


def _inproj_kernel(x_ref, mod_ref, w_ref, ct_ref, st_ref,
                   mq_ref, mk_ref, mv_ref, km_ref, nq_ref, nqr_ref, ks_ref, kw_ref, vs_ref, vw_ref,
                   kvc_ref, pool_ref, gq_ref, gk_ref, gv_ref, gg_ref, misc_ref, *, nt):
    tb = pl.program_id(0) % nt
    h = (x_ref[...] * (1.0 + mod_ref[0, 1:2, :]) + mod_ref[0, 0:1, :]).astype(bf16)

    def seg(name, width):
        a = _S[name]
        return jnp.dot(h, w_ref[:, a:a + width], preferred_element_type=f32)

    ct, st = ct_ref[...], st_ref[...]
    lane = lax.broadcasted_iota(i32, (IN_TM, LANES), 1)
    first8 = (lane % HEAD_DIM) < ROPE_DIM // 2
    half = lane < HEAD_DIM

    def rope128(y):
        partner = jnp.where(first8, pltpu.roll(y, LANES - 8, axis=1), pltpu.roll(y, 8, axis=1))
        return y * ct + partner * st

    def rope(y):
        return jnp.concatenate([rope128(y[:, c * LANES:(c + 1) * LANES]) for c in range(y.shape[1] // LANES)],
                               axis=1)

    def lo_half(y):
        return jnp.where(half, y, 0.0)

    def hi_half(y):
        return jnp.where(half, pltpu.roll(y, HEAD_DIM, axis=1), 0.0)

    def per_head(y):
        parts = []
        for c in range(2):
            yc = y[:, c * LANES:(c + 1) * LANES]
            parts += [lo_half(yc), hi_half(yc)]
        return jnp.concatenate(parts, axis=1)

    mq_ref[...] = per_head(rope(seg("mq", 256)) * Q_SCALE).astype(bf16)
    k = rope(seg("mk", 256))
    km_ref[0] = jnp.mean(k, axis=0, keepdims=True)
    lane4 = lax.broadcasted_iota(i32, (IN_TM, 4 * LANES), 1)
    mk_ref[...] = jnp.where((lane4 % LANES) == HEAD_DIM + tb, 1.0, per_head(k)).astype(bf16)
    mv_ref[...] = per_head(seg("mv", 256)).astype(bf16)
    q = seg("nq", 256) * Q_SCALE
    nq_ref[...] = per_head(q).astype(bf16)
    nqr_ref[...] = per_head(rope(q)).astype(bf16)
    sw = rope128(seg("sw", 128))
    row = lax.broadcasted_iota(i32, (IN_TM, LANES), 0)
    slc_id = tb * (IN_TM // SLC_BLOCK) + row // SLC_BLOCK
    ks_ref[...] = jnp.concatenate([lo_half(sw), jnp.where(lane == slc_id, 1.0, 0.0)], axis=1).astype(bf16)
    kw_ref[...] = hi_half(sw).astype(bf16)
    vsw = seg("vsw", 128)
    vs_ref[...] = lo_half(vsw).astype(bf16)
    vw_ref[...] = hi_half(vsw).astype(bf16)
    kvc_ref[...] = seg("kvc", 128).astype(bf16)
    pool_ref[...] = seg("pool", 256)
    gq_ref[...] = seg("gq", 256)
    gk_ref[...] = seg("gk", 256)
    gv_ref[...] = seg("gv", 256)
    gg_ref[...] = seg("gg", 256)
    misc_ref[...] = seg("misc", 128)


def _inproj(x2, mod, w, ct, st, B, T):
    N = B * T
    nt = T // IN_TM
    assert T % IN_TM == 0 and nt <= 32 and T // SLC_BLOCK <= LANES
    row = lambda w_, dt: (jax.ShapeDtypeStruct((N, w_), dt), pl.BlockSpec((IN_TM, w_), lambda i: (i, 0)))
    outs = [row(512, bf16), row(512, bf16), row(512, bf16),
            (jax.ShapeDtypeStruct((N // IN_TM, 1, 256), f32), pl.BlockSpec((1, 1, 256), lambda i: (i, 0, 0))),
            row(512, bf16), row(512, bf16), row(256, bf16), row(128, bf16), row(128, bf16), row(128, bf16),
            row(128, bf16), row(256, f32), row(256, f32), row(256, f32), row(256, f32), row(256, f32),
            row(128, f32)]
    return pl.pallas_call(
        functools.partial(_inproj_kernel, nt=nt), grid=(N // IN_TM,),
        in_specs=[pl.BlockSpec((IN_TM, D_MODEL), lambda i: (i, 0)),
                  pl.BlockSpec((1, 6, D_MODEL), lambda i: (i // nt, 0, 0)),
                  pl.BlockSpec((D_MODEL, IN_COLS_PACKED), lambda i: (0, 0), pipeline_mode=pl.Buffered(1)),
                  pl.BlockSpec((IN_TM, LANES), lambda i: (i % nt, 0)),
                  pl.BlockSpec((IN_TM, LANES), lambda i: (i % nt, 0))],
        out_specs=[o[1] for o in outs], out_shape=[o[0] for o in outs],
        compiler_params=_cparams("arbitrary"), name="inproj")(x2, mod, w, ct, st)


def _cmp_kernel(x_ref, wa_ref, wb_ref, pea_ref, peb_ref, w2k_ref, w2v_ref, kc_ref, vc_ref):
    x = x_ref[0].astype(f32)
    a = jnp.dot((x + pea_ref[...]).astype(bf16), wa_ref[...], preferred_element_type=f32)
    b = jnp.dot((x + peb_ref[...]).astype(bf16), wb_ref[...], preferred_element_type=f32)
    hid = a + pltpu.roll(b, x.shape[0] - 1, axis=0)
    g = jax.nn.gelu(hid).astype(bf16)
    kc_ref[0] = jnp.dot(g, w2k_ref[...], preferred_element_type=f32).astype(bf16)
    vc_ref[0] = jnp.dot(g, w2v_ref[...], preferred_element_type=f32).astype(bf16)


def _cmp(xseg, wa, wb, pea, peb, w2k, w2v):
    B, ns, wd = xseg.shape
    full = lambda a: pl.BlockSpec(a.shape, lambda b: (0,) * a.ndim)
    return pl.pallas_call(
        _cmp_kernel, grid=(B,),
        in_specs=[pl.BlockSpec((1, ns, wd), lambda b: (b, 0, 0))] + [full(a) for a in (wa, wb, pea, peb, w2k, w2v)],
        out_specs=[pl.BlockSpec((1, ns, LANES), lambda b: (b, 0, 0))] * 2,
        out_shape=[jax.ShapeDtypeStruct((B, ns, LANES), bf16)] * 2,
        compiler_params=_cparams("arbitrary"), name="nsa_compress")(xseg, wa, wb, pea, peb, w2k, w2v)


MOBA_NBLK = 32
KEY_TILE = 512
MOBA_TQ = 512


def _moba_kernel(q_ref, k_ref, vt_ref, km_ref, o_ref):
    q0 = pl.program_id(1) * MOBA_TQ
    heads = range(N_HEADS)
    hl = lambda h: slice(h * LANES, (h + 1) * LANES)
    blk = lax.broadcasted_iota(i32, (MOBA_NBLK, MOBA_TQ), 0)
    own = (q0 + lax.broadcasted_iota(i32, (1, MOBA_TQ), 1)) // MOBA_BLOCK
    past = blk < own
    zeros = lambda n: jnp.zeros((n, MOBA_TQ), f32)
    qf = []
    for h in heads:
        qh = q_ref[:, hl(h)]
        hi, lo = _split_bf16(km_ref[0, h])
        gate_t = (lax.dot_general(hi, qh, NT, preferred_element_type=f32)
                  + lax.dot_general(lo, qh, NT, preferred_element_type=f32))
        picked = _argmax_rounds(jnp.where(past, gate_t, -jnp.inf), blk, MOBA_TOPK, axis=0)
        allowed = ((picked > 0.0) & past) | (blk == own)
        bias_t = jnp.concatenate([zeros(HEAD_DIM), jnp.where(allowed, 0.0, MASKED),
                                  zeros(LANES - HEAD_DIM - MOBA_NBLK)], axis=0)
        qf.append(qh + jnp.transpose(bias_t).astype(bf16))

    def scores_t(h, off):
        return lax.dot_general(k_ref[pl.ds(off, KEY_TILE), hl(h)], qf[h], NT, preferred_element_type=f32)

    def body(p, carry):
        off = pl.multiple_of(p * KEY_TILE, KEY_TILE)
        return _softmax_steps_t([scores_t(h, off) for h in heads], carry, [vt_ref[0, p, hl(h), :] for h in heads])

    last = q0 // KEY_TILE
    carry = lax.fori_loop(0, last, body, _softmax_init_t(N_HEADS, MOBA_TQ))
    off = pl.multiple_of(last * KEY_TILE, KEY_TILE)
    kpos = off + lax.broadcasted_iota(i32, (KEY_TILE, MOBA_TQ), 0)
    qpos = q0 + lax.broadcasted_iota(i32, (KEY_TILE, MOBA_TQ), 1)
    carry = _softmax_steps_t([jnp.where(kpos <= qpos, scores_t(h, off), MASKED) for h in heads], carry,
                             [vt_ref[0, last, hl(h), :] for h in heads])
    for h in heads:
        o_ref[:, hl(h)] = jnp.transpose(carry[h][2] / carry[h][1]).astype(bf16)


def _blocks_t(v, B, T, blk):
    return v.reshape(B, T // blk, blk, v.shape[1]).transpose(0, 1, 3, 2)


def _moba(mq, mk, mv, km, B, T):
    N = B * T
    nt = T // MOBA_TQ
    wd = N_HEADS * LANES
    assert T % KEY_TILE == 0 and KEY_TILE % MOBA_TQ == 0 and T // MOBA_BLOCK <= MOBA_NBLK
    return pl.pallas_call(
        _moba_kernel, grid=(B, nt),
        in_specs=[pl.BlockSpec((MOBA_TQ, wd), lambda b, i: (b * nt + i, 0)),
                  pl.BlockSpec((T, wd), lambda b, i: (b, 0), pipeline_mode=pl.Buffered(1)),
                  pl.BlockSpec((1, T // KEY_TILE, wd, KEY_TILE), lambda b, i: (b, 0, 0, 0),
                               pipeline_mode=pl.Buffered(1)),
                  pl.BlockSpec((1, N_HEADS, MOBA_NBLK, LANES), lambda b, i: (b, 0, 0, 0))],
        out_specs=pl.BlockSpec((MOBA_TQ, wd), lambda b, i: (b * nt + i, 0)),
        out_shape=jax.ShapeDtypeStruct((N, wd), bf16),
        compiler_params=_cparams("arbitrary", "arbitrary"), name="moba")(mq, mk, _blocks_t(mv, B, T, KEY_TILE), km)


NSA_TQ = 512
NSA_KB = 256


def _stack_heads(ref):
    return jnp.concatenate([ref[:, h * LANES:(h + 1) * LANES] for h in range(N_HEADS)], axis=0)


def _nsa_select_kernel(nq_ref, kc_ref, vct_ref, ovt_ref, oc_ref, selb_ref, *, n_cmp):
    c = pl.program_id(1)
    rows = N_HEADS * NSA_TQ
    s = lax.dot_general(kc_ref[0], _stack_heads(nq_ref), NT, preferred_element_type=f32)
    n = lax.broadcasted_iota(i32, s.shape, 0)
    qpos_t = c * NSA_TQ + lax.broadcasted_iota(i32, (1, rows), 1) % NSA_TQ
    ok = (n * CMP_STRIDE + (CMP_BLOCK - 1) <= qpos_t) & (n < n_cmp)
    s = jnp.where(ok, s, -jnp.inf)
    m = jnp.max(s, axis=0, keepdims=True)
    m = jnp.where(m > -jnp.inf, m, 0.0)
    e = jnp.where(ok, jnp.exp2(s - m), 0.0)
    p_c = e / jnp.maximum(jnp.sum(e, axis=0, keepdims=True), 1e-30)
    o_c_t = jnp.dot(vct_ref[0], p_c.astype(bf16), preferred_element_type=f32)
    hq = lambda h: slice(h * NSA_TQ, (h + 1) * NSA_TQ)
    for h in range(N_HEADS):
        oc_ref[:, h * LANES:(h + 1) * LANES] = jnp.transpose(o_c_t[:, hq(h)]).astype(bf16)
    hi, lo = _split_bf16((p_c[:, hq(0)] + p_c[:, hq(1)]) + (p_c[:, hq(2)] + p_c[:, hq(3)]))
    imp = (jnp.dot(ovt_ref[...], hi, preferred_element_type=f32)
           + jnp.dot(ovt_ref[...], lo, preferred_element_type=f32))
    j = lax.broadcasted_iota(i32, imp.shape, 0)
    cur = (c * NSA_TQ + lax.broadcasted_iota(i32, (1, NSA_TQ), 1)) // SLC_BLOCK
    forced = (j == 0) | (j == cur) | (j == cur - 1)
    valid = j <= cur
    score = jnp.where(valid, jnp.where(forced, FORCE_SCORE, imp), -jnp.inf)
    chosen = (_argmax_rounds(score, j, SLC_TOPN, axis=0) > 0.0) & valid
    selb_ref[...] = jnp.transpose(jnp.where(chosen, 0.0, MASKED)).astype(bf16)


def _nsa_attend_kernel(nqr_ref, selb_ref, oc_ref, misc_ref, ks_ref, vst_ref, kw_ref, vwt_ref, o_ref):
    c = pl.program_id(1)
    rows = N_HEADS * NSA_TQ
    heads = range(N_HEADS)
    q4r = _stack_heads(nqr_ref)
    qpos_t = c * NSA_TQ + lax.broadcasted_iota(i32, (1, rows), 1) % NSA_TQ
    selb = selb_ref[...]
    lhs = [jnp.concatenate([q4r[h * NSA_TQ:(h + 1) * NSA_TQ], selb], axis=1) for h in heads]

    def scores_t(off):
        kb = ks_ref[pl.ds(off, KEY_TILE), :]
        return [lax.dot_general(kb, lhs[h], NT, preferred_element_type=f32) for h in heads]

    def body(p, carry):
        return _softmax_steps_t(scores_t(pl.multiple_of(p * KEY_TILE, KEY_TILE)), carry, [vst_ref[0, p]] * N_HEADS)

    last = (c * NSA_TQ) // KEY_TILE
    carry = lax.fori_loop(0, last, body, _softmax_init_t(N_HEADS, NSA_TQ))
    off = pl.multiple_of(last * KEY_TILE, KEY_TILE)
    kpos = off + lax.broadcasted_iota(i32, (KEY_TILE, NSA_TQ), 0)
    qpos = c * NSA_TQ + lax.broadcasted_iota(i32, (KEY_TILE, NSA_TQ), 1)
    carry = _softmax_steps_t([jnp.where(kpos <= qpos, s, MASKED) for s in scores_t(off)], carry,
                             [vst_ref[0, last]] * N_HEADS)
    o_s = jnp.concatenate([jnp.transpose(carry[h][2] / carry[h][1]) for h in heads], axis=0)

    nwb = (WIN + NSA_TQ) // NSA_KB
    sb = jnp.maximum(c * (NSA_TQ // NSA_KB) - WIN // NSA_KB, 0)
    start = pl.multiple_of(sb * NSA_KB, NSA_KB)
    s_w = lax.dot_general(kw_ref[pl.ds(start, nwb * NSA_KB), :], q4r, NT, preferred_element_type=f32)
    wpos = start + lax.broadcasted_iota(i32, s_w.shape, 0)
    s_w = jnp.where((wpos <= qpos_t) & (wpos > qpos_t - WIN), s_w, -jnp.inf)
    e_w = jnp.exp2(s_w - jnp.max(s_w, axis=0, keepdims=True))
    p_w = (e_w / jnp.sum(e_w, axis=0, keepdims=True)).astype(bf16)
    o_w_t = jnp.dot(vwt_ref[0, sb], p_w[0:NSA_KB], preferred_element_type=f32)
    for i in range(1, nwb):
        o_w_t = o_w_t + jnp.dot(vwt_ref[0, sb + i], p_w[i * NSA_KB:(i + 1) * NSA_KB], preferred_element_type=f32)
    o_w = jnp.transpose(o_w_t)

    gates = jax.nn.sigmoid(misc_ref[...])
    gl = lax.broadcasted_iota(i32, gates.shape, 1)

    def gate_col(g):
        return jnp.concatenate([jnp.sum(jnp.where(gl == MISC_GATE0 + 3 * h + g, gates, 0.0), axis=1, keepdims=True)
                                for h in range(N_HEADS)], axis=0)

    out = gate_col(0) * _stack_heads(oc_ref).astype(f32) + gate_col(1) * o_s + gate_col(2) * o_w
    pair = lambda a, b: a + pltpu.roll(b, HEAD_DIM, axis=1)
    o_ref[...] = jnp.concatenate([pair(out[0:NSA_TQ], out[NSA_TQ:2 * NSA_TQ]),
                                  pair(out[2 * NSA_TQ:3 * NSA_TQ], out[3 * NSA_TQ:])], axis=1).astype(bf16)


def _nsa(nq, nqr, misc, ks, vs, kw, vw, kc, vc, ov, B, T):
    N = B * T
    nc = T // NSA_TQ
    ns = kc.shape[1]
    assert T >= NSA_TQ + WIN and T % KEY_TILE == 0 and KEY_TILE % NSA_TQ == 0 and NSA_TQ % NSA_KB == 0
    assert WIN % NSA_KB == 0
    tok = lambda w_: pl.BlockSpec((NSA_TQ, w_), lambda b, c: (b * nc + c, 0))
    seq = lambda w_: pl.BlockSpec((T, w_), lambda b, c: (b, 0), pipeline_mode=pl.Buffered(1))
    o_c, selb = pl.pallas_call(
        functools.partial(_nsa_select_kernel, n_cmp=(T - CMP_BLOCK) // CMP_STRIDE + 1), grid=(B, nc),
        in_specs=[tok(512), pl.BlockSpec((1, ns, LANES), lambda b, c: (b, 0, 0)),
                  pl.BlockSpec((1, LANES, ns), lambda b, c: (b, 0, 0)), pl.BlockSpec((LANES, ns), lambda b, c: (0, 0))],
        out_specs=[tok(512), tok(LANES)],
        out_shape=[jax.ShapeDtypeStruct((N, 512), bf16), jax.ShapeDtypeStruct((N, LANES), bf16)],
        compiler_params=_cparams("arbitrary", "arbitrary"), name="nsa_select")(
            nq, kc, vc.transpose(0, 2, 1), ov.T)
    seq_t = lambda kb: pl.BlockSpec((1, T // kb, LANES, kb), lambda b, c: (b, 0, 0, 0), pipeline_mode=pl.Buffered(1))
    return pl.pallas_call(
        _nsa_attend_kernel, grid=(B, nc),
        in_specs=[tok(512), tok(LANES), tok(512), tok(LANES), seq(256), seq_t(KEY_TILE), seq(LANES), seq_t(NSA_KB)],
        out_specs=tok(256), out_shape=jax.ShapeDtypeStruct((N, 256), bf16),
        compiler_params=_cparams("arbitrary", "arbitrary"), name="nsa_attend")(
            nqr, selb, o_c, misc, ks, _blocks_t(vs, B, T, KEY_TILE), kw, _blocks_t(vw, B, T, NSA_KB))


POOL_TM = 512
POOL_HALO = 16


def _pool_kernel(u_ref, halo_ref, w_ref, sc_ref, o_ref):
    t = pl.program_id(1)
    halo = jnp.where(t == 0, 0.0, halo_ref[...])
    ext = jnp.concatenate([halo, u_ref[...]], axis=0)
    s2 = ext + pltpu.roll(ext, 1, axis=0)
    s4 = s2 + pltpu.roll(s2, 2, axis=0)
    s8 = s4 + pltpu.roll(s4, 4, axis=0)
    s16 = s8 + pltpu.roll(s8, 8, axis=0)
    pos1 = jnp.maximum(t * POOL_TM - POOL_HALO + 1 + lax.broadcasted_iota(i32, ext.shape, 0), 1).astype(f32)
    grp = lax.broadcasted_iota(i32, ext.shape, 1) // HEAD_DIM
    mean = jnp.where(grp == 0, s2 / jnp.minimum(pos1, 2.0),
                     jnp.where(grp == 1, s4 / jnp.minimum(pos1, 4.0),
                               jnp.where(grp == 2, s8 / jnp.minimum(pos1, 8.0), s16 / jnp.minimum(pos1, 16.0))))
    pooled = (mean - ext)[POOL_HALO:, :]
    o_ref[...] = (jnp.dot(pooled.astype(bf16), w_ref[...], preferred_element_type=f32) * sc_ref[...]).astype(bf16)


def _pool(u, wbd, scale, B, T):
    N = B * T
    tm = min(POOL_TM, T)
    assert tm == POOL_TM and T % POOL_TM == 0
    nt = T // tm
    hb = tm // POOL_HALO
    return pl.pallas_call(
        _pool_kernel, grid=(B, nt),
        in_specs=[pl.BlockSpec((tm, 256), lambda b, t: (b * nt + t, 0)),
                  pl.BlockSpec((POOL_HALO, 256), lambda b, t: (jnp.maximum((b * nt + t) * hb - 1, 0), 0)),
                  pl.BlockSpec((256, 256), lambda b, t: (0, 0)),
                  pl.BlockSpec((1, 256), lambda b, t: (0, 0))],
        out_specs=pl.BlockSpec((tm, 256), lambda b, t: (b * nt + t, 0)),
        out_shape=jax.ShapeDtypeStruct((N, 256), bf16),
        compiler_params=_cparams("arbitrary", "arbitrary"), name="pool")(u, u, wbd, scale)


GLA_TM = 256


def _gla_kernel(q_ref, k_ref, v_ref, g_ref, misc_ref, wa_ref, ba_ref, gn_ref, bd_ref, o_ref,
                st_ref, q_s, k_s, v_s, b_s, qe_s, ke_s, gam_s, o_s):
    nb = q_ref.shape[0]

    @pl.when(pl.program_id(0) == 0)
    def _():
        st_ref[...] = jnp.zeros_like(st_ref)

    r16 = lax.broadcasted_iota(i32, (GLA_TM, 256), 0) % GLA_SUB
    for bi_ in range(nb):
        x = jnp.dot(misc_ref[bi_], wa_ref[...], preferred_element_type=f32,
                    precision=lax.Precision.HIGHEST) + ba_ref[...]
        log_a = (jnp.minimum(x, 0.0) - jnp.log1p(jnp.exp(-jnp.abs(x)))) / GLA_TAU
        b = log_a
        for s in (1, 2, 4, 8):
            b = b + jnp.where(r16 >= s, pltpu.roll(b, s, axis=0), 0.0)
        b_end = jnp.where(r16 == GLA_SUB - 1, b, 0.0)
        for s in (1, 2, 4, 8):
            b_end = b_end + pltpu.roll(b_end, GLA_TM - s, axis=0)
        q = q_ref[bi_] * QK_SCALE
        k = k_ref[bi_]
        q_s[bi_] = q
        k_s[bi_] = k
        v_s[bi_] = v_ref[bi_]
        b_s[bi_] = b
        qe_s[bi_] = (q * jnp.exp(b)).astype(bf16)
        ke_s[bi_] = (k * jnp.exp(b_end - b)).astype(bf16)
        gam_s[bi_] = jnp.exp(b_end)
    bd = bd_ref[...]
    shape3 = (GLA_SUB, GLA_SUB, 256)
    causal = lax.broadcasted_iota(i32, shape3, 0) <= lax.broadcasted_iota(i32, shape3, 1)

    def block(n, _):
        r0 = pl.multiple_of(n * GLA_SUB, GLA_SUB)
        rows = pl.ds(r0, GLA_SUB)
        for bi_ in range(nb):
            qi, ki, vi, bi = q_s[bi_, rows, :], k_s[bi_, rows, :], v_s[bi_, rows, :], b_s[bi_, rows, :]
            diff = jnp.where(causal, bi[None, :, :] - bi[:, None, :], 0.0)
            w3 = jnp.where(causal, qi[None, :, :] * ki[:, None, :] * jnp.exp(diff), 0.0)
            a3 = jnp.dot(w3.reshape(GLA_SUB * GLA_SUB, 256).astype(bf16), bd, preferred_element_type=f32)
            intra = jnp.sum(a3.reshape(shape3) * vi[:, None, :], axis=0)
            st = st_ref[bi_]
            inter = lax.dot_general(qe_s[bi_, rows, :], st.astype(bf16), NT, preferred_element_type=f32)
            o_s[bi_, rows, :] = intra + inter
            upd = lax.dot_general(vi.astype(bf16), ke_s[bi_, rows, :], TN, preferred_element_type=f32)
            st_ref[bi_] = st * gam_s[bi_, pl.ds(r0, 1), :] + jnp.where(bd > 0, upd, 0.0)
        return 0

    lax.fori_loop(0, GLA_TM // GLA_SUB, block, 0)
    for bi_ in range(nb):
        o = o_s[bi_]
        ms = jnp.dot(o * o, bd.astype(f32), preferred_element_type=f32, precision=lax.Precision.HIGHEST) / HEAD_DIM
        o_ref[bi_] = (o * lax.rsqrt(ms + LN_EPS) * gn_ref[...] * _silu(g_ref[bi_])).astype(bf16)


def _gla(gq, gk, gv, gg, misc, wa, ba, gn, bd, B, T):
    N = B * T
    assert T % GLA_TM == 0
    tok = lambda w_: pl.BlockSpec((B, GLA_TM, w_), lambda t: (0, t, 0))
    full = lambda a: pl.BlockSpec(a.shape, lambda t: (0, 0))
    v = lambda dt: pltpu.VMEM((B, GLA_TM, 256), dt)
    seq = lambda a: a.reshape(B, T, a.shape[1])
    out = pl.pallas_call(
        _gla_kernel, grid=(T // GLA_TM,),
        in_specs=[tok(256), tok(256), tok(256), tok(256), tok(128), full(wa), full(ba), full(gn), full(bd)],
        out_specs=tok(256), out_shape=jax.ShapeDtypeStruct((B, T, 256), bf16),
        scratch_shapes=[pltpu.VMEM((B, 256, 256), f32), v(f32), v(f32), v(f32), v(f32), v(bf16), v(bf16), v(f32),
                        v(f32)],
        compiler_params=_cparams("arbitrary"), name="gla")(seq(gq), seq(gk), seq(gv), seq(gg), seq(misc), wa, ba, gn, bd)
    return out.reshape(N, 256)


OUT_TM = 256


def _outproj_kernel(mo_ref, no_ref, po_ref, go_ref, x_ref, mod_ref, w_ref, lng_ref, lnb_ref, wrh_ref, wrl_ref,
                    x1_ref, h2t_ref, h2b_ref, lg_ref, *, alpha):
    a = jnp.concatenate([mo_ref[...], no_ref[...], po_ref[...], go_ref[...]], axis=1)
    y = jnp.dot(a, w_ref[...], preferred_element_type=f32)
    x1 = _layer_norm(alpha * x_ref[...] + mod_ref[0, 2:3, :] * y, lng_ref[...], lnb_ref[...])
    x1_ref[...] = x1
    h2 = x1 * (1.0 + mod_ref[0, 4:5, :]) + mod_ref[0, 3:4, :]
    h2b_ref[...] = h2.astype(bf16)
    for s, w in enumerate(_pack_rows(h2)):
        h2t_ref[pl.ds(s, OUT_TM, stride=ROW_TILE), :] = w
    hi, lo = _split_bf16(h2)
    wrh = wrh_ref[...]
    lg_ref[...] = (jnp.dot(hi, wrh, preferred_element_type=f32) + jnp.dot(lo, wrh, preferred_element_type=f32)
                   + jnp.dot(hi, wrl_ref[...], preferred_element_type=f32))


def _outproj(mo, no, po, go, x2, mod, w, lng, lnb, wrh, wrl, alpha, B, T):
    N = B * T
    nt = T // OUT_TM
    tok = lambda w_: pl.BlockSpec((OUT_TM, w_), lambda i: (i, 0))
    full = lambda a: pl.BlockSpec(a.shape, lambda i: (0,) * a.ndim)
    return pl.pallas_call(
        functools.partial(_outproj_kernel, alpha=alpha), grid=(N // OUT_TM,),
        in_specs=[tok(512), tok(256), tok(256), tok(256), tok(D_MODEL),
                  pl.BlockSpec((1, 6, D_MODEL), lambda i: (i // nt, 0, 0)),
                  full(w), full(lng), full(lnb), full(wrh), full(wrl)],
        out_specs=[tok(D_MODEL), pl.BlockSpec((OUT_TM * ROW_TILE, LANES), lambda i: (i, 0)), tok(D_MODEL),
                   tok(N_EXPERTS)],
        out_shape=[jax.ShapeDtypeStruct((N, D_MODEL), f32), jax.ShapeDtypeStruct((N * ROW_TILE, LANES), u32),
                   jax.ShapeDtypeStruct((N, D_MODEL), bf16), jax.ShapeDtypeStruct((N, N_EXPERTS), f32)],
        compiler_params=_cparams("arbitrary"), name="outproj")(mo, no, po, go, x2, mod, w, lng, lnb, wrh, wrl)


ROUTE_TM = 256
GROUP_SIZE = N_EXPERTS // N_EXPERT_GROUPS


def _route_kernel(lg_ref, rb_ref, tri_ref, ei_ref, wt_ref, rk_ref, cnt_ref, base_ref):
    @pl.when(pl.program_id(0) == 0)
    def _():
        base_ref[...] = jnp.zeros_like(base_ref)

    s = jax.nn.sigmoid(jnp.transpose(lg_ref[...]))
    ssel = s + rb_ref[...]
    shape3 = (N_EXPERT_GROUPS, GROUP_SIZE, ROUTE_TM)
    x3 = ssel.reshape(shape3)
    i3 = lax.broadcasted_iota(i32, shape3, 1)
    m1 = jnp.max(x3, axis=1, keepdims=True)
    first = jnp.min(jnp.where(x3 == m1, i3, BIG_IDX), axis=1, keepdims=True)
    m2 = jnp.max(jnp.where(i3 == first, -jnp.inf, x3), axis=1, keepdims=True)
    gscore = (m1 + m2).reshape(N_EXPERT_GROUPS, ROUTE_TM)
    gid = lax.broadcasted_iota(i32, gscore.shape, 0)
    beaten = jnp.zeros(gscore.shape, i32)
    for g in range(N_EXPERT_GROUPS):
        other = gscore[g:g + 1, :]
        beaten = beaten + jnp.where((other > gscore) | ((other == gscore) & (g < gid)), 1, 0)
    keep = jnp.broadcast_to((beaten < TOPK_GROUPS)[:, None, :], shape3).reshape(ssel.shape)
    x = jnp.where(keep, ssel, -jnp.inf)
    eid = lax.broadcasted_iota(i32, x.shape, 0)
    hits, idx_rows, w_rows = [], [], []
    for _ in range(TOP_K):
        mx = jnp.max(x, axis=0, keepdims=True)
        idx = jnp.min(jnp.where(x == mx, eid, BIG_IDX), axis=0, keepdims=True)
        hit = eid == idx
        hits.append(hit)
        idx_rows.append(idx.astype(f32))
        w_rows.append(jnp.sum(jnp.where(hit, s, 0.0), axis=0, keepdims=True))
        x = jnp.where(hit, -jnp.inf, x)
    wsum = w_rows[0]
    for w in w_rows[1:]:
        wsum = wsum + w
    chosen = jnp.zeros(s.shape, f32)
    for hit in hits:
        chosen = jnp.where(hit, 1.0, chosen)
    before = base_ref[...] + jnp.dot(chosen.astype(bf16), tri_ref[...], preferred_element_type=f32)
    rk_rows = [jnp.sum(jnp.where(hit, before, 0.0), axis=0, keepdims=True) for hit in hits]
    total = base_ref[...] + jnp.sum(chosen, axis=1, keepdims=True)
    base_ref[...] = total
    cnt_ref[...] = total.astype(i32)

    def per_token(rows):
        return jnp.transpose(jnp.concatenate(rows + [jnp.zeros((LANES - TOP_K, ROUTE_TM), f32)], axis=0))

    ei_ref[...] = per_token(idx_rows).astype(i32)
    wt_ref[...] = per_token([w / wsum * ROUTED_SCALE for w in w_rows])
    rk_ref[...] = per_token(rk_rows).astype(i32)


def _route(logits, rb):
    N = logits.shape[0]
    r_, c_ = jnp.arange(ROUTE_TM)[:, None], jnp.arange(ROUTE_TM)[None, :]
    tri = (r_ < c_).astype(bf16)
    tok = pl.BlockSpec((ROUTE_TM, LANES), lambda i: (i, 0))
    col = pl.BlockSpec((N_EXPERTS, 1), lambda i: (0, 0))
    return pl.pallas_call(
        _route_kernel, grid=(N // ROUTE_TM,),
        in_specs=[pl.BlockSpec((ROUTE_TM, N_EXPERTS), lambda i: (i, 0)), col,
                  pl.BlockSpec((ROUTE_TM, ROUTE_TM), lambda i: (0, 0))],
        out_specs=[tok, tok, tok, col],
        out_shape=[jax.ShapeDtypeStruct((N, LANES), i32), jax.ShapeDtypeStruct((N, LANES), f32),
                   jax.ShapeDtypeStruct((N, LANES), i32), jax.ShapeDtypeStruct((N_EXPERTS, 1), i32)],
        scratch_shapes=[pltpu.VMEM((N_EXPERTS, 1), f32)],
        compiler_params=_cparams("arbitrary"), name="route")(logits, rb.reshape(N_EXPERTS, 1), tri)


DISP_TM = 128


def _dispatch_kernel(h_ref, ei_ref, rk_ref, ps_ref, xs_in, pos_ref, xs_hbm, pos_s, csem, rsem):
    del xs_in
    ei = ei_ref[...].astype(f32)
    lane = lax.broadcasted_iota(i32, ei.shape, 1)
    lane_e = lax.broadcasted_iota(i32, (DISP_TM, N_EXPERTS), 1).astype(f32)
    starts = ps_ref[...].astype(f32)
    pos = jnp.zeros(ei.shape, f32)
    for k in range(TOP_K):
        e_k = jnp.sum(jnp.where(lane == k, ei, 0.0), axis=1, keepdims=True)
        st_k = jnp.sum(jnp.where(lane_e == e_k, starts, 0.0), axis=1, keepdims=True)
        pos = jnp.where(lane == k, st_k, pos)
    pos_ref[...] = jnp.where(lane < TOP_K, pos.astype(i32) + rk_ref[...], 0)
    cp = pltpu.make_async_copy(pos_ref, pos_s, csem)
    cp.start()
    cp.wait()

    def tile(i):
        return pl.ds(pl.multiple_of(i * ROW_TILE, ROW_TILE), ROW_TILE)

    def row_copy(t, k, p):
        return pltpu.make_async_copy(h_ref.at[tile(t)], xs_hbm.at[tile(p)], rsem)

    def issue(t, _):
        for k in range(TOP_K):
            row_copy(t, k, pos_s[t, k]).start(priority=k % 2)
        return 0
    lax.fori_loop(0, DISP_TM, issue, 0)

    def drain(t, _):
        for k in range(TOP_K):
            row_copy(t, k, 0).wait()
        return 0
    lax.fori_loop(0, DISP_TM, drain, 0)


def _dispatch(h2t, eidx, rank, pstarts, n_rows):
    N = h2t.shape[0] // ROW_TILE
    tok = pl.BlockSpec((DISP_TM, LANES), lambda i: (i, 0))
    xs0 = jnp.zeros((n_rows * ROW_TILE, LANES), u32)
    pos, xs = pl.pallas_call(
        _dispatch_kernel, grid=(N // DISP_TM,),
        in_specs=[pl.BlockSpec((DISP_TM * ROW_TILE, LANES), lambda i: (i, 0)), tok, tok,
                  pl.BlockSpec((1, N_EXPERTS), lambda i: (0, 0)), pl.BlockSpec(memory_space=pl.ANY)],
        out_specs=[tok, pl.BlockSpec(memory_space=pl.ANY)],
        out_shape=[jax.ShapeDtypeStruct((N, LANES), i32), jax.ShapeDtypeStruct(xs0.shape, u32)],
        scratch_shapes=[pltpu.SMEM((DISP_TM, LANES), i32), pltpu.SemaphoreType.DMA, pltpu.SemaphoreType.DMA],
        input_output_aliases={4: 1},
        compiler_params=_cparams("arbitrary"), name="moe_dispatch")(h2t, eidx, rank, pstarts, xs0)
    return pos, xs


MOE_BLK = MOE_ROWS * ROW_TILE
MOE_IN_BUFS = 4
MOE_OUT_BUFS = 3


def _moe_kernel(b0_ref, nb_ref, nu_ref, xs_in, wg_ref, wu_ref, wd_ref, xs_out, xbuf, ybuf, isem, osem):
    e = pl.program_id(0)
    nu = nu_ref[0]

    def in_copy(b):
        return pltpu.make_async_copy(xs_in.at[pl.ds(pl.multiple_of(b * MOE_BLK, MOE_BLK), MOE_BLK)],
                                     xbuf.at[b % MOE_IN_BUFS], isem.at[b % MOE_IN_BUFS])

    def out_copy(b):
        return pltpu.make_async_copy(ybuf.at[b % MOE_OUT_BUFS],
                                     xs_out.at[pl.ds(pl.multiple_of(b * MOE_BLK, MOE_BLK), MOE_BLK)],
                                     osem.at[b % MOE_OUT_BUFS])

    @pl.when(e == 0)
    def _():
        for j in range(MOE_IN_BUFS - 1):
            @pl.when(j < nu)
            def _():
                in_copy(j).start()

    wg = wg_ref[0, 0].astype(bf16)
    wu = wu_ref[0, 0].astype(bf16)
    wd = wd_ref[0, 0].astype(bf16)

    def block(i, _):
        b = b0_ref[e] + i
        in_copy(b).wait()

        @pl.when(b + MOE_IN_BUFS - 1 < nu)
        def _():
            in_copy(b + MOE_IN_BUFS - 1).start()

        @pl.when(b >= MOE_OUT_BUFS)
        def _():
            out_copy(b - MOE_OUT_BUFS).wait()

        g = jnp.zeros((MOE_ROWS, D_EXPERT), f32)
        u = jnp.zeros((MOE_ROWS, D_EXPERT), f32)
        for s in range(ROW_TILE):
            word = xbuf[b % MOE_IN_BUFS, pl.ds(s, MOE_ROWS, stride=ROW_TILE), :]
            xc = jnp.concatenate(_unpack_word(word), axis=1).astype(bf16)
            rows = slice(2 * s * LANES, (2 * s + 2) * LANES)
            g = g + jnp.dot(xc, wg[rows], preferred_element_type=f32)
            u = u + jnp.dot(xc, wu[rows], preferred_element_type=f32)
        y = jnp.dot((_silu(g) * u).astype(bf16), wd, preferred_element_type=f32)
        for s, w in enumerate(_pack_rows(y)):
            ybuf[b % MOE_OUT_BUFS, pl.ds(s, MOE_ROWS, stride=ROW_TILE), :] = w
        out_copy(b).start()
        return 0

    lax.fori_loop(0, nb_ref[e], block, 0)

    @pl.when(e == pl.num_programs(0) - 1)
    def _():
        for j in range(MOE_OUT_BUFS, 0, -1):
            @pl.when(nu >= j)
            def _():
                out_copy(nu - j).wait()


def _moe(layer, blk0, nblk, n_used, xs, wg, wu, wd):
    wspec = lambda a: pl.BlockSpec((1, 1) + a.shape[2:], lambda e, b0, nb, nu: (layer, e, 0, 0))
    gs = pltpu.PrefetchScalarGridSpec(
        num_scalar_prefetch=3, grid=(N_EXPERTS,),
        in_specs=[pl.BlockSpec(memory_space=pl.ANY), wspec(wg), wspec(wu), wspec(wd)],
        out_specs=pl.BlockSpec(memory_space=pl.ANY),
        scratch_shapes=[pltpu.VMEM((MOE_IN_BUFS, MOE_BLK, LANES), u32), pltpu.VMEM((MOE_OUT_BUFS, MOE_BLK, LANES), u32),
                        pltpu.SemaphoreType.DMA((MOE_IN_BUFS,)), pltpu.SemaphoreType.DMA((MOE_OUT_BUFS,))])
    return pl.pallas_call(
        _moe_kernel, grid_spec=gs, out_shape=jax.ShapeDtypeStruct(xs.shape, u32),
        input_output_aliases={3: 0},
        compiler_params=_cparams("arbitrary"), name="moe_experts")(blk0, nblk, n_used, xs, wg, wu, wd)


FIN_TM = 128


def _tile_gather(pos_hbm, ys_hbm, pos_s, gbuf, isem, rsem, step, n_steps):
    def idx_copy(s):
        return pltpu.make_async_copy(pos_hbm.at[pl.ds(s * FIN_TM, FIN_TM)], pos_s.at[s % 2], isem.at[s % 2])

    def tile(i):
        return pl.ds(pl.multiple_of(i * ROW_TILE, ROW_TILE), ROW_TILE)

    def row_copy(slot, t, k, p):
        return pltpu.make_async_copy(ys_hbm.at[tile(p)], gbuf.at[slot, k, tile(t)], rsem.at[slot])

    def start_rows(s):
        slot = s % 2

        def body(t, _):
            for k in range(TOP_K):
                row_copy(slot, t, k, pos_s[slot, t, k]).start(priority=k % 2)
            return 0
        lax.fori_loop(0, FIN_TM, body, 0)

    @pl.when(step == 0)
    def _():
        idx_copy(step).start()
        idx_copy(step).wait()
        start_rows(step)

        @pl.when(n_steps > 1)
        def _():
            idx_copy(step + 1).start()

    @pl.when(step + 1 < n_steps)
    def _():
        idx_copy(step + 1).wait()
        start_rows(step + 1)

        @pl.when(step + 2 < n_steps)
        def _():
            idx_copy(step + 2).start()

    slot = step % 2

    def wait_body(t, _):
        for k in range(TOP_K):
            row_copy(slot, t, k, 0).wait()
        return 0
    lax.fori_loop(0, FIN_TM, wait_body, 0)


def _fin_kernel(pos_hbm, ys_hbm, h2_ref, x1_ref, wt_ref, mod_ref, sg_ref, su_ref, sd_ref, lng_ref, lnb_ref, o_ref,
                pos_s, gbuf, isem, rsem, *, alpha):
    i = pl.program_id(0)
    _tile_gather(pos_hbm, ys_hbm, pos_s, gbuf, isem, rsem, i, pl.num_programs(0))
    hb = h2_ref[...]
    g = jnp.dot(hb, sg_ref[...], preferred_element_type=f32)
    u = jnp.dot(hb, su_ref[...], preferred_element_type=f32)
    shared = jnp.dot((_silu(g) * u).astype(bf16), sd_ref[...], preferred_element_type=f32)
    wt = wt_ref[...]
    lane = lax.broadcasted_iota(i32, wt.shape, 1)
    wk = [jnp.sum(jnp.where(lane == k, wt, 0.0), axis=1, keepdims=True) for k in range(TOP_K)]
    slot = i % 2
    cols = []
    for s in range(ROW_TILE):
        sub = pl.ds(s, FIN_TM, stride=ROW_TILE)
        lo, hi = _unpack_word(gbuf[slot, 0, sub, :])
        acc_lo, acc_hi = lo * wk[0], hi * wk[0]
        for k in range(1, TOP_K):
            lo, hi = _unpack_word(gbuf[slot, k, sub, :])
            acc_lo, acc_hi = acc_lo + lo * wk[k], acc_hi + hi * wk[k]
        cols += [acc_lo, acc_hi]
    y = shared + jnp.concatenate(cols, axis=1)
    o_ref[...] = _layer_norm(alpha * x1_ref[...] + mod_ref[0, 5:6, :] * y, lng_ref[...], lnb_ref[...])


def _fin(pos, ys, h2b, x1, wts, mod, sg, su, sd, lng, lnb, alpha, B, T):
    N = B * T
    nt = T // FIN_TM
    tok = lambda w_: pl.BlockSpec((FIN_TM, w_), lambda i: (i, 0))
    full = lambda a: pl.BlockSpec(a.shape, lambda i: (0,) * a.ndim)
    return pl.pallas_call(
        functools.partial(_fin_kernel, alpha=alpha), grid=(N // FIN_TM,),
        in_specs=[pl.BlockSpec(memory_space=pl.ANY), pl.BlockSpec(memory_space=pl.ANY),
                  tok(D_MODEL), tok(D_MODEL), tok(LANES),
                  pl.BlockSpec((1, 6, D_MODEL), lambda i: (i // nt, 0, 0)),
                  full(sg), full(su), full(sd), full(lng), full(lnb)],
        out_specs=tok(D_MODEL), out_shape=jax.ShapeDtypeStruct((N, D_MODEL), f32),
        scratch_shapes=[pltpu.SMEM((2, FIN_TM, LANES), i32), pltpu.VMEM((2, TOP_K, FIN_TM * ROW_TILE, LANES), u32),
                        pltpu.SemaphoreType.DMA((2,)), pltpu.SemaphoreType.DMA((2,))],
        compiler_params=_cparams("arbitrary"), name="combine")(pos, ys, h2b, x1, wts, mod, sg, su, sd, lng, lnb)


def _pack_w_in(w):
    c = lambda name, width: w[:, _OFF[name]:_OFF[name] + width]
    cols = [c("moba_q", 256), c("moba_k", 256), c("moba_v", 256), c("nsa_q", 256),
            c("k_slc", 64), c("k_win", 64), c("v_slc", 64), c("v_win", 64), c("k_cmp", 64), c("v_cmp", 64),
            c("pool", 256), c("gla_q", 256), c("gla_k", 256), c("gla_v", 256), c("gla_g", 256),
            c("nsa_gate", 12), c("gla_a", 16), jnp.zeros((w.shape[0], LANES - 28), w.dtype)]
    return jnp.concatenate(cols, axis=1).astype(bf16)


def _rope_tables(T):
    half = ROPE_DIM // 2
    inv_freq = ROPE_THETA ** (-jnp.arange(half, dtype=f32) / half)
    ang = jnp.arange(T).astype(f32)[:, None] * inv_freq[None, :]
    cos, sin = jnp.cos(ang), jnp.sin(ang)
    one = jnp.ones((T, HEAD_DIM - ROPE_DIM), f32)
    zero = jnp.zeros((T, HEAD_DIM - ROPE_DIM), f32)
    ct = jnp.concatenate([cos, cos, one], axis=1)
    st = jnp.concatenate([-sin, sin, zero], axis=1)
    return jnp.tile(ct, (1, 2)), jnp.tile(st, (1, 2))


def _cmp_weights(pe, w1, w2):
    half = CMP_BLOCK // 2
    z = jnp.zeros((half, HEAD_DIM, CMP_HIDDEN), f32)

    def arrange(lo):
        wk = w1[0].reshape(CMP_BLOCK, HEAD_DIM, CMP_HIDDEN)[lo:lo + half]
        wv = w1[1].reshape(CMP_BLOCK, HEAD_DIM, CMP_HIDDEN)[lo:lo + half]
        top = jnp.concatenate([wk, z], axis=2)
        bot = jnp.concatenate([z, wv], axis=2)
        return jnp.concatenate([top, bot], axis=1).reshape(half * 2 * HEAD_DIM, 2 * CMP_HIDDEN).astype(bf16)

    def pe_row(lo):
        return jnp.concatenate([pe[0, lo:lo + half], pe[1, lo:lo + half]], axis=1).reshape(1, half * 2 * HEAD_DIM)

    zc = jnp.zeros((CMP_HIDDEN, LANES - HEAD_DIM), f32)
    zr = jnp.zeros((CMP_HIDDEN, LANES), f32)
    w2k = jnp.concatenate([jnp.concatenate([w2[0], zc], axis=1), zr], axis=0).astype(bf16)
    w2v = jnp.concatenate([zr, jnp.concatenate([w2[1], zc], axis=1)], axis=0).astype(bf16)
    return arrange(0), arrange(half), pe_row(0), pe_row(half), w2k, w2v


def _overlap_matrix(ns):
    n = jnp.arange(ns)[:, None] * CMP_STRIDE
    j = jnp.arange(LANES)[None, :] * SLC_BLOCK
    ov = (n < j + SLC_BLOCK) & (n + CMP_BLOCK > j) & (jnp.arange(ns)[:, None] < ns - 1)
    return ov.astype(bf16)


def _block_diag_ones():
    h = jnp.arange(256) // HEAD_DIM
    return (h[:, None] == h[None, :]).astype(bf16)


def _expert_layout(counts, n_tok):
    counts = counts.reshape(N_EXPERTS)
    padded = (counts + MOE_ROWS - 1) // MOE_ROWS * MOE_ROWS
    pstarts = (jnp.cumsum(padded) - padded).astype(i32)
    n_blocks = n_tok * TOP_K // MOE_ROWS + N_EXPERTS
    n_used = (jnp.sum(padded) // MOE_ROWS).astype(i32).reshape(1)
    return (pstarts.reshape(1, N_EXPERTS), pstarts // MOE_ROWS, (padded // MOE_ROWS).astype(i32), n_used,
            n_blocks * MOE_ROWS)


def _mixer_inputs(x2, mod, w_in, B, T):
    ct, st = _rope_tables(T)
    return _inproj(x2, mod, _pack_w_in(w_in), ct, st, B, T)


def _token_mixers(x2, mod, w_in, cmp_pe, cmp_w1, cmp_w2, pool_w, pool_scale, gla_wa, gla_ba, gla_norm, B, T):
    N = B * T
    (mq, mk, mv, km, nq, nqr, ks, kw, vs, vw, kvc, pool_u, gq, gk, gv, gg, misc) = _mixer_inputs(x2, mod, w_in, B, T)
    nt = T // MOBA_BLOCK
    kmh = km.reshape(B, nt, N_HEADS, HEAD_DIM).transpose(0, 2, 1, 3)
    mo = _moba(mq, mk, mv, jnp.pad(kmh, ((0, 0), (0, 0), (0, MOBA_NBLK - nt), (0, LANES - HEAD_DIM))), B, T)
    ns = T // CMP_STRIDE
    kc, vc = _cmp(kvc.reshape(B, ns, CMP_STRIDE * LANES), *_cmp_weights(cmp_pe, cmp_w1, cmp_w2))
    no = _nsa(nq, nqr, misc, ks, vs, kw, vw, kc, vc, _overlap_matrix(ns), B, T)
    wbd = jax.scipy.linalg.block_diag(*[pool_w[g] for g in range(len(POOL_WINDOWS))]).astype(bf16)
    po = _pool(pool_u, wbd, pool_scale.reshape(1, 256), B, T)
    wa = jnp.zeros((LANES, 256), f32).at[MISC_A0:MISC_A0 + GLA_LOWRANK].set(gla_wa)
    go = _gla(gq, gk, gv, gg, misc, wa, gla_ba.reshape(1, 256), jnp.tile(gla_norm, N_HEADS).reshape(1, 256),
              _block_diag_ones(), B, T)
    return mo, no, po, go


def _pad_w_out(w_out):
    wm = w_out[:GROUP_WIDTH].reshape(N_HEADS, HEAD_DIM, D_MODEL)
    wm = jnp.pad(wm, ((0, 0), (0, LANES - HEAD_DIM), (0, 0))).reshape(N_HEADS * LANES, D_MODEL)
    return jnp.concatenate([wm, w_out[GROUP_WIDTH:]], axis=0).astype(bf16)


def kernel(x, c, w_ada, b_ada, w_in, cmp_pe, cmp_w1, cmp_w2, pool_w, pool_scale, gla_wa, gla_ba, gla_norm, w_out,
           ln_g, ln_b, w_router, router_bias, exp_gate, exp_up, exp_down, sh_gate, sh_up, sh_down):
    B, T, D = x.shape
    N = B * T
    depth = w_ada.shape[0]
    alpha = float((2 * depth) ** 0.25)
    x2 = x.reshape(N, D)
    c8 = jnp.zeros((8, D), f32).at[:B].set(c)
    for l in range(depth):
        mod = _ada(c8, w_ada[l], b_ada[l].reshape(1, -1))[:B].reshape(B, 6, D)
        mo, no, po, go = _token_mixers(x2, mod, w_in[l], cmp_pe[l], cmp_w1[l], cmp_w2[l], pool_w[l], pool_scale[l],
                                       gla_wa[l], gla_ba[l], gla_norm[l], B, T)
        wrh, wrl = _split_bf16(w_router[l])
        x1, h2t, h2b, logits = _outproj(mo, no, po, go, x2, mod, _pad_w_out(w_out[l]), ln_g[l, 0].reshape(1, D),
                                        ln_b[l, 0].reshape(1, D), wrh, wrl, alpha, B, T)
        eidx, wts, rank, counts = _route(logits, router_bias[l].reshape(1, N_EXPERTS))
        pstarts, blk0, nblk, n_used, n_rows = _expert_layout(counts, N)
        pos, xs = _dispatch(h2t, eidx, rank, pstarts, n_rows)
        ys = _moe(l, blk0, nblk, n_used, xs, exp_gate, exp_up, exp_down)
        x2 = _fin(pos, ys, h2b, x1, wts, mod, sh_gate[l].astype(bf16), sh_up[l].astype(bf16),
                  sh_down[l].astype(bf16), ln_g[l, 1].reshape(1, D), ln_b[l, 1].reshape(1, D), alpha, B, T)
    return x2.reshape(B, T, D)
```

```python
import functools

import jax
import jax.numpy as jnp
from jax import lax
from jax.experimental import pallas as pl
from jax.experimental.pallas import tpu as pltpu

f32, bf16, i32, u32 = jnp.float32, jnp.bfloat16, jnp.int32, jnp.uint32

D_MODEL = 1024
HEAD_DIM = 64
N_HEADS = 4
GROUP_WIDTH = 256
ROPE_THETA = 500000.0
ROPE_DIM = 16
MOBA_BLOCK = 256
MOBA_TOPK = 3
CMP_BLOCK = 32
CMP_STRIDE = 16
CMP_HIDDEN = 128
SLC_BLOCK = 64
SLC_TOPN = 16
WIN = 512
FORCE_SCORE = 1e9
POOL_WINDOWS = (2, 4, 8, 16)
GLA_SUB = 16
GLA_LOWRANK = 16
GLA_TAU = 16.0
N_EXPERTS = 256
TOP_K = 8
N_EXPERT_GROUPS = 8
TOPK_GROUPS = 4
D_EXPERT = 256
ROUTED_SCALE = 2.5
LN_EPS = 1e-5
QK_SCALE = HEAD_DIM ** -0.5
LOG2E = 1.4426950408889634
Q_SCALE = QK_SCALE * LOG2E

LANES = 128
MASKED = -1e30
M_INIT = -3e38
BIG_IDX = 1 << 20
MOE_ROWS = 256
ROW_TILE = D_MODEL // (2 * LANES)
VMEM_LIMIT = 56 * 1024 * 1024

_OFF = dict(moba_q=0, moba_k=256, moba_v=512, nsa_q=768, k_cmp=1024, v_cmp=1088, k_slc=1152, v_slc=1216,
            k_win=1280, v_win=1344, nsa_gate=1408, pool=1420, gla_q=1676, gla_k=1932, gla_v=2188,
            gla_a=2444, gla_g=2460)
_S = dict(mq=0, mk=256, mv=512, nq=768, sw=1024, vsw=1152, kvc=1280, pool=1408, gq=1664, gk=1920, gv=2176,
          gg=2432, misc=2688)
IN_COLS_PACKED = 2816
MISC_GATE0 = 0
MISC_A0 = 12

NT = (((1,), (1,)), ((), ()))
TN = (((0,), (0,)), ((), ()))


def _cparams(*sem):
    return pltpu.CompilerParams(dimension_semantics=sem, vmem_limit_bytes=VMEM_LIMIT)


def _silu(x):
    return x * jax.nn.sigmoid(x)


def _split_bf16(x):
    hi = x.astype(bf16)
    lo = (x - hi.astype(f32)).astype(bf16)
    return hi, lo


def _pack_rows(x):
    bits = lambda v: lax.bitcast_convert_type(v.astype(bf16).astype(f32), u32)
    return [(bits(x[:, (2 * s) * LANES:(2 * s + 1) * LANES]) >> 16) | bits(x[:, (2 * s + 1) * LANES:(2 * s + 2) * LANES])
            for s in range(ROW_TILE)]


def _unpack_word(w):
    return (lax.bitcast_convert_type(w << 16, f32), lax.bitcast_convert_type(w & jnp.uint32(0xFFFF0000), f32))


def _layer_norm(z, g, b):
    mu = jnp.mean(z, axis=-1, keepdims=True)
    zc = z - mu
    var = jnp.mean(zc * zc, axis=-1, keepdims=True)
    return zc * lax.rsqrt(var + LN_EPS) * g + b


def _argmax_rounds(score, index, rounds, axis=1):
    picked = jnp.zeros(score.shape, f32)
    for _ in range(rounds):
        mx = jnp.max(score, axis=axis, keepdims=True)
        first = jnp.min(jnp.where(score == mx, index, BIG_IDX), axis=axis, keepdims=True)
        hit = index == first
        picked = jnp.where(hit, 1.0, picked)
        score = jnp.where(hit, -jnp.inf, score)
    return picked


def _softmax_steps_t(s_list, carries, v_list):
    m_new = [jnp.maximum(c[0], jnp.max(s, axis=0, keepdims=True)) for s, c in zip(s_list, carries)]
    p = [jnp.exp2(s - m) for s, m in zip(s_list, m_new)]
    pv = [jnp.dot(v, pi.astype(bf16), preferred_element_type=f32) for v, pi in zip(v_list, p)]
    out = []
    for c, m, pi, pvi in zip(carries, m_new, p, pv):
        alpha = jnp.exp2(c[0] - m)
        out.append((m, alpha * c[1] + jnp.sum(pi, axis=0, keepdims=True), alpha * c[2] + pvi))
    return tuple(out)


def _softmax_init_t(groups, queries):
    return tuple((jnp.full((1, queries), M_INIT, f32), jnp.zeros((1, queries), f32),
                  jnp.zeros((LANES, queries), f32)) for _ in range(groups))


def _ada_kernel(c_ref, w_ref, b_ref, o_ref):
    o_ref[...] = jnp.dot(_silu(c_ref[...]), w_ref[...], preferred_element_type=f32,
                         precision=lax.Precision.HIGHEST) + b_ref[...]


def _ada(c8, w, b):
    n = w.shape[1] // D_MODEL
    return pl.pallas_call(
        _ada_kernel, grid=(n,),
        in_specs=[pl.BlockSpec((8, D_MODEL), lambda j: (0, 0)),
                  pl.BlockSpec((D_MODEL, D_MODEL), lambda j: (0, j)),
                  pl.BlockSpec((1, D_MODEL), lambda j: (0, j))],
        out_specs=pl.BlockSpec((8, D_MODEL), lambda j: (0, j)),
        out_shape=jax.ShapeDtypeStruct((8, w.shape[1]), f32),
        compiler_params=_cparams("arbitrary"), name="ada")(c8, w, b)


IN_TM = MOBA_BLOCK


def _inproj_kernel(x_ref, mod_ref, w_ref, ct_ref, st_ref,
                   mq_ref, mk_ref, mv_ref, km_ref, nq_ref, nqr_ref, ks_ref, kw_ref, vs_ref, vw_ref,
                   kvc_ref, pool_ref, gq_ref, gk_ref, gv_ref, gg_ref, misc_ref, *, nt):
    tb = pl.program_id(0) % nt
    h = (x_ref[...] * (1.0 + mod_ref[0, 1:2, :]) + mod_ref[0, 0:1, :]).astype(bf16)

    def seg(name, width):
        a = _S[name]
        return jnp.dot(h, w_ref[:, a:a + width], preferred_element_type=f32)

    ct, st = ct_ref[...], st_ref[...]
    lane = lax.broadcasted_iota(i32, (IN_TM, LANES), 1)
    first8 = (lane % HEAD_DIM) < ROPE_DIM // 2
    half = lane < HEAD_DIM

    def rope128(y):
        partner = jnp.where(first8, pltpu.roll(y, LANES - 8, axis=1), pltpu.roll(y, 8, axis=1))
        return y * ct + partner * st

    def rope(y):
        return jnp.concatenate([rope128(y[:, c * LANES:(c + 1) * LANES]) for c in range(y.shape[1] // LANES)],
                               axis=1)

    def lo_half(y):
        return jnp.where(half, y, 0.0)

    def hi_half(y):
        return jnp.where(half, pltpu.roll(y, HEAD_DIM, axis=1), 0.0)

    def per_head(y):
        parts = []
        for c in range(2):
            yc = y[:, c * LANES:(c + 1) * LANES]
            parts += [lo_half(yc), hi_half(yc)]
        return jnp.concatenate(parts, axis=1)

    mq_ref[...] = per_head(rope(seg("mq", 256)) * Q_SCALE).astype(bf16)
    k = rope(seg("mk", 256))
    km_ref[0] = jnp.mean(k, axis=0, keepdims=True)
    lane4 = lax.broadcasted_iota(i32, (IN_TM, 4 * LANES), 1)
    mk_ref[...] = jnp.where((lane4 % LANES) == HEAD_DIM + tb, 1.0, per_head(k)).astype(bf16)
    mv_ref[...] = per_head(seg("mv", 256)).astype(bf16)
    q = seg("nq", 256) * Q_SCALE
    nq_ref[...] = per_head(q).astype(bf16)
    nqr_ref[...] = per_head(rope(q)).astype(bf16)
    sw = rope128(seg("sw", 128))
    row = lax.broadcasted_iota(i32, (IN_TM, LANES), 0)
    slc_id = tb * (IN_TM // SLC_BLOCK) + row // SLC_BLOCK
    ks_ref[...] = jnp.concatenate([lo_half(sw), jnp.where(lane == slc_id, 1.0, 0.0)], axis=1).astype(bf16)
    kw_ref[...] = hi_half(sw).astype(bf16)
    vsw = seg("vsw", 128)
    vs_ref[...] = lo_half(vsw).astype(bf16)
    vw_ref[...] = hi_half(vsw).astype(bf16)
    kvc_ref[...] = seg("kvc", 128).astype(bf16)
    pool_ref[...] = seg("pool", 256)
    gq_ref[...] = seg("gq", 256)
    gk_ref[...] = seg("gk", 256)
    gv_ref[...] = seg("gv", 256)
    gg_ref[...] = seg("gg", 256)
    misc_ref[...] = seg("misc", 128)


def _inproj(x2, mod, w, ct, st, B, T):
    N = B * T
    nt = T // IN_TM
    assert T % IN_TM == 0 and nt <= 32 and T // SLC_BLOCK <= LANES
    row = lambda w_, dt: (jax.ShapeDtypeStruct((N, w_), dt), pl.BlockSpec((IN_TM, w_), lambda i: (i, 0)))
    outs = [row(512, bf16), row(512, bf16), row(512, bf16),
            (jax.ShapeDtypeStruct((N // IN_TM, 1, 256), f32), pl.BlockSpec((1, 1, 256), lambda i: (i, 0, 0))),
            row(512, bf16), row(512, bf16), row(256, bf16), row(128, bf16), row(128, bf16), row(128, bf16),
            row(128, bf16), row(256, f32), row(256, f32), row(256, f32), row(256, f32), row(256, f32),
            row(128, f32)]
    return pl.pallas_call(
        functools.partial(_inproj_kernel, nt=nt), grid=(N // IN_TM,),
        in_specs=[pl.BlockSpec((IN_TM, D_MODEL), lambda i: (i, 0)),
                  pl.BlockSpec((1, 6, D_MODEL), lambda i: (i // nt, 0, 0)),
                  pl.BlockSpec((D_MODEL, IN_COLS_PACKED), lambda i: (0, 0), pipeline_mode=pl.Buffered(1)),
                  pl.BlockSpec((IN_TM, LANES), lambda i: (i % nt, 0)),
                  pl.BlockSpec((IN_TM, LANES), lambda i: (i % nt, 0))],
        out_specs=[o[1] for o in outs], out_shape=[o[0] for o in outs],
        compiler_params=_cparams("arbitrary"), name="inproj")(x2, mod, w, ct, st)


def _cmp_kernel(x_ref, wa_ref, wb_ref, pea_ref, peb_ref, w2k_ref, w2v_ref, kc_ref, vc_ref):
    x = x_ref[0].astype(f32)
    a = jnp.dot((x + pea_ref[...]).astype(bf16), wa_ref[...], preferred_element_type=f32)
    b = jnp.dot((x + peb_ref[...]).astype(bf16), wb_ref[...], preferred_element_type=f32)
    hid = a + pltpu.roll(b, x.shape[0] - 1, axis=0)
    g = jax.nn.gelu(hid).astype(bf16)
    kc_ref[0] = jnp.dot(g, w2k_ref[...], preferred_element_type=f32).astype(bf16)
    vc_ref[0] = jnp.dot(g, w2v_ref[...], preferred_element_type=f32).astype(bf16)


def _cmp(xseg, wa, wb, pea, peb, w2k, w2v):
    B, ns, wd = xseg.shape
    full = lambda a: pl.BlockSpec(a.shape, lambda b: (0,) * a.ndim)
    return pl.pallas_call(
        _cmp_kernel, grid=(B,),
        in_specs=[pl.BlockSpec((1, ns, wd), lambda b: (b, 0, 0))] + [full(a) for a in (wa, wb, pea, peb, w2k, w2v)],
        out_specs=[pl.BlockSpec((1, ns, LANES), lambda b: (b, 0, 0))] * 2,
        out_shape=[jax.ShapeDtypeStruct((B, ns, LANES), bf16)] * 2,
        compiler_params=_cparams("arbitrary"), name="nsa_compress")(xseg, wa, wb, pea, peb, w2k, w2v)


MOBA_NBLK = 32
KEY_TILE = 512
MOBA_TQ = 512


def _moba_kernel(q_ref, k_ref, vt_ref, km_ref, o_ref):
    q0 = pl.program_id(1) * MOBA_TQ
    heads = range(N_HEADS)
    hl = lambda h: slice(h * LANES, (h + 1) * LANES)
    blk = lax.broadcasted_iota(i32, (MOBA_NBLK, MOBA_TQ), 0)
    own = (q0 + lax.broadcasted_iota(i32, (1, MOBA_TQ), 1)) // MOBA_BLOCK
    past = blk < own
    zeros = lambda n: jnp.zeros((n, MOBA_TQ), f32)
    qf = []
    for h in heads:
        qh = q_ref[:, hl(h)]
        hi, lo = _split_bf16(km_ref[0, h])
        gate_t = (lax.dot_general(hi, qh, NT, preferred_element_type=f32)
                  + lax.dot_general(lo, qh, NT, preferred_element_type=f32))
        picked = _argmax_rounds(jnp.where(past, gate_t, -jnp.inf), blk, MOBA_TOPK, axis=0)
        allowed = ((picked > 0.0) & past) | (blk == own)
        bias_t = jnp.concatenate([zeros(HEAD_DIM), jnp.where(allowed, 0.0, MASKED),
                                  zeros(LANES - HEAD_DIM - MOBA_NBLK)], axis=0)
        qf.append(qh + jnp.transpose(bias_t).astype(bf16))

    def scores_t(h, off):
        return lax.dot_general(k_ref[pl.ds(off, KEY_TILE), hl(h)], qf[h], NT, preferred_element_type=f32)

    def body(p, carry):
        off = pl.multiple_of(p * KEY_TILE, KEY_TILE)
        return _softmax_steps_t([scores_t(h, off) for h in heads], carry, [vt_ref[0, p, hl(h), :] for h in heads])

    last = q0 // KEY_TILE
    carry = lax.fori_loop(0, last, body, _softmax_init_t(N_HEADS, MOBA_TQ))
    off = pl.multiple_of(last * KEY_TILE, KEY_TILE)
    kpos = off + lax.broadcasted_iota(i32, (KEY_TILE, MOBA_TQ), 0)
    qpos = q0 + lax.broadcasted_iota(i32, (KEY_TILE, MOBA_TQ), 1)
    carry = _softmax_steps_t([jnp.where(kpos <= qpos, scores_t(h, off), MASKED) for h in heads], carry,
                             [vt_ref[0, last, hl(h), :] for h in heads])
    for h in heads:
        o_ref[:, hl(h)] = jnp.transpose(carry[h][2] / carry[h][1]).astype(bf16)


def _blocks_t(v, B, T, blk):
    return v.reshape(B, T // blk, blk, v.shape[1]).transpose(0, 1, 3, 2)


def _moba(mq, mk, mv, km, B, T):
    N = B * T
    nt = T // MOBA_TQ
    wd = N_HEADS * LANES
    assert T % KEY_TILE == 0 and KEY_TILE % MOBA_TQ == 0 and T // MOBA_BLOCK <= MOBA_NBLK
    return pl.pallas_call(
        _moba_kernel, grid=(B, nt),
        in_specs=[pl.BlockSpec((MOBA_TQ, wd), lambda b, i: (b * nt + i, 0)),
                  pl.BlockSpec((T, wd), lambda b, i: (b, 0), pipeline_mode=pl.Buffered(1)),
                  pl.BlockSpec((1, T // KEY_TILE, wd, KEY_TILE), lambda b, i: (b, 0, 0, 0),
                               pipeline_mode=pl.Buffered(1)),
                  pl.BlockSpec((1, N_HEADS, MOBA_NBLK, LANES), lambda b, i: (b, 0, 0, 0))],
        out_specs=pl.BlockSpec((MOBA_TQ, wd), lambda b, i: (b * nt + i, 0)),
        out_shape=jax.ShapeDtypeStruct((N, wd), bf16),
        compiler_params=_cparams("arbitrary", "arbitrary"), name="moba")(mq, mk, _blocks_t(mv, B, T, KEY_TILE), km)


NSA_TQ = 512
NSA_KB = 256


def _stack_heads(ref):
    return jnp.concatenate([ref[:, h * LANES:(h + 1) * LANES] for h in range(N_HEADS)], axis=0)


def _nsa_select_kernel(nq_ref, kc_ref, vct_ref, ovt_ref, oc_ref, selb_ref, *, n_cmp):
    c = pl.program_id(1)
    rows = N_HEADS * NSA_TQ
    s = lax.dot_general(kc_ref[0], _stack_heads(nq_ref), NT, preferred_element_type=f32)
    n = lax.broadcasted_iota(i32, s.shape, 0)
    qpos_t = c * NSA_TQ + lax.broadcasted_iota(i32, (1, rows), 1) % NSA_TQ
    ok = (n * CMP_STRIDE + (CMP_BLOCK - 1) <= qpos_t) & (n < n_cmp)
    s = jnp.where(ok, s, -jnp.inf)
    m = jnp.max(s, axis=0, keepdims=True)
    m = jnp.where(m > -jnp.inf, m, 0.0)
    e = jnp.where(ok, jnp.exp2(s - m), 0.0)
    p_c = e / jnp.maximum(jnp.sum(e, axis=0, keepdims=True), 1e-30)
    o_c_t = jnp.dot(vct_ref[0], p_c.astype(bf16), preferred_element_type=f32)
    hq = lambda h: slice(h * NSA_TQ, (h + 1) * NSA_TQ)
    for h in range(N_HEADS):
        oc_ref[:, h * LANES:(h + 1) * LANES] = jnp.transpose(o_c_t[:, hq(h)]).astype(bf16)
    hi, lo = _split_bf16((p_c[:, hq(0)] + p_c[:, hq(1)]) + (p_c[:, hq(2)] + p_c[:, hq(3)]))
    imp = (jnp.dot(ovt_ref[...], hi, preferred_element_type=f32)
           + jnp.dot(ovt_ref[...], lo, preferred_element_type=f32))
    j = lax.broadcasted_iota(i32, imp.shape, 0)
    cur = (c * NSA_TQ + lax.broadcasted_iota(i32, (1, NSA_TQ), 1)) // SLC_BLOCK
    forced = (j == 0) | (j == cur) | (j == cur - 1)
    valid = j <= cur
    score = jnp.where(valid, jnp.where(forced, FORCE_SCORE, imp), -jnp.inf)
    chosen = (_argmax_rounds(score, j, SLC_TOPN, axis=0) > 0.0) & valid
    selb_ref[...] = jnp.transpose(jnp.where(chosen, 0.0, MASKED)).astype(bf16)


def _nsa_attend_kernel(nqr_ref, selb_ref, oc_ref, misc_ref, ks_ref, vst_ref, kw_ref, vwt_ref, o_ref):
    c = pl.program_id(1)
    rows = N_HEADS * NSA_TQ
    heads = range(N_HEADS)
    q4r = _stack_heads(nqr_ref)
    qpos_t = c * NSA_TQ + lax.broadcasted_iota(i32, (1, rows), 1) % NSA_TQ
    selb = selb_ref[...]
    lhs = [jnp.concatenate([q4r[h * NSA_TQ:(h + 1) * NSA_TQ], selb], axis=1) for h in heads]

    def scores_t(off):
        kb = ks_ref[pl.ds(off, KEY_TILE), :]
        return [lax.dot_general(kb, lhs[h], NT, preferred_element_type=f32) for h in heads]

    def body(p, carry):
        return _softmax_steps_t(scores_t(pl.multiple_of(p * KEY_TILE, KEY_TILE)), carry, [vst_ref[0, p]] * N_HEADS)

    last = (c * NSA_TQ) // KEY_TILE
    carry = lax.fori_loop(0, last, body, _softmax_init_t(N_HEADS, NSA_TQ))
    off = pl.multiple_of(last * KEY_TILE, KEY_TILE)
    kpos = off + lax.broadcasted_iota(i32, (KEY_TILE, NSA_TQ), 0)
    qpos = c * NSA_TQ + lax.broadcasted_iota(i32, (KEY_TILE, NSA_TQ), 1)
    carry = _softmax_steps_t([jnp.where(kpos <= qpos, s, MASKED) for s in scores_t(off)], carry,
                             [vst_ref[0, last]] * N_HEADS)
    o_s = jnp.concatenate([jnp.transpose(carry[h][2] / carry[h][1]) for h in heads], axis=0)

    nwb = (WIN + NSA_TQ) // NSA_KB
    sb = jnp.maximum(c * (NSA_TQ // NSA_KB) - WIN // NSA_KB, 0)
    start = pl.multiple_of(sb * NSA_KB, NSA_KB)
    s_w = lax.dot_general(kw_ref[pl.ds(start, nwb * NSA_KB), :], q4r, NT, preferred_element_type=f32)
    wpos = start + lax.broadcasted_iota(i32, s_w.shape, 0)
    s_w = jnp.where((wpos <= qpos_t) & (wpos > qpos_t - WIN), s_w, -jnp.inf)
    e_w = jnp.exp2(s_w - jnp.max(s_w, axis=0, keepdims=True))
    p_w = (e_w / jnp.sum(e_w, axis=0, keepdims=True)).astype(bf16)
    o_w_t = jnp.dot(vwt_ref[0, sb], p_w[0:NSA_KB], preferred_element_type=f32)
    for i in range(1, nwb):
        o_w_t = o_w_t + jnp.dot(vwt_ref[0, sb + i], p_w[i * NSA_KB:(i + 1) * NSA_KB], preferred_element_type=f32)
    o_w = jnp.transpose(o_w_t)

    gates = jax.nn.sigmoid(misc_ref[...])
    gl = lax.broadcasted_iota(i32, gates.shape, 1)

    def gate_col(g):
        return jnp.concatenate([jnp.sum(jnp.where(gl == MISC_GATE0 + 3 * h + g, gates, 0.0), axis=1, keepdims=True)
                                for h in range(N_HEADS)], axis=0)

    out = gate_col(0) * _stack_heads(oc_ref).astype(f32) + gate_col(1) * o_s + gate_col(2) * o_w
    pair = lambda a, b: a + pltpu.roll(b, HEAD_DIM, axis=1)
    o_ref[...] = jnp.concatenate([pair(out[0:NSA_TQ], out[NSA_TQ:2 * NSA_TQ]),
                                  pair(out[2 * NSA_TQ:3 * NSA_TQ], out[3 * NSA_TQ:])], axis=1).astype(bf16)


def _nsa(nq, nqr, misc, ks, vs, kw, vw, kc, vc, ov, B, T):
    N = B * T
    nc = T // NSA_TQ
    ns = kc.shape[1]
    assert T >= NSA_TQ + WIN and T % KEY_TILE == 0 and KEY_TILE % NSA_TQ == 0 and NSA_TQ % NSA_KB == 0
    assert WIN % NSA_KB == 0
    tok = lambda w_: pl.BlockSpec((NSA_TQ, w_), lambda b, c: (b * nc + c, 0))
    seq = lambda w_: pl.BlockSpec((T, w_), lambda b, c: (b, 0), pipeline_mode=pl.Buffered(1))
    o_c, selb = pl.pallas_call(
        functools.partial(_nsa_select_kernel, n_cmp=(T - CMP_BLOCK) // CMP_STRIDE + 1), grid=(B, nc),
        in_specs=[tok(512), pl.BlockSpec((1, ns, LANES), lambda b, c: (b, 0, 0)),
                  pl.BlockSpec((1, LANES, ns), lambda b, c: (b, 0, 0)), pl.BlockSpec((LANES, ns), lambda b, c: (0, 0))],
        out_specs=[tok(512), tok(LANES)],
        out_shape=[jax.ShapeDtypeStruct((N, 512), bf16), jax.ShapeDtypeStruct((N, LANES), bf16)],
        compiler_params=_cparams("arbitrary", "arbitrary"), name="nsa_select")(
            nq, kc, vc.transpose(0, 2, 1), ov.T)
    seq_t = lambda kb: pl.BlockSpec((1, T // kb, LANES, kb), lambda b, c: (b, 0, 0, 0), pipeline_mode=pl.Buffered(1))
    return pl.pallas_call(
        _nsa_attend_kernel, grid=(B, nc),
        in_specs=[tok(512), tok(LANES), tok(512), tok(LANES), seq(256), seq_t(KEY_TILE), seq(LANES), seq_t(NSA_KB)],
        out_specs=tok(256), out_shape=jax.ShapeDtypeStruct((N, 256), bf16),
        compiler_params=_cparams("arbitrary", "arbitrary"), name="nsa_attend")(
            nqr, selb, o_c, misc, ks, _blocks_t(vs, B, T, KEY_TILE), kw, _blocks_t(vw, B, T, NSA_KB))


POOL_TM = 512
POOL_HALO = 16


def _pool_kernel(u_ref, halo_ref, w_ref, sc_ref, o_ref):
    t = pl.program_id(1)
    halo = jnp.where(t == 0, 0.0, halo_ref[...])
    ext = jnp.concatenate([halo, u_ref[...]], axis=0)
    s2 = ext + pltpu.roll(ext, 1, axis=0)
    s4 = s2 + pltpu.roll(s2, 2, axis=0)
    s8 = s4 + pltpu.roll(s4, 4, axis=0)
    s16 = s8 + pltpu.roll(s8, 8, axis=0)
    pos1 = jnp.maximum(t * POOL_TM - POOL_HALO + 1 + lax.broadcasted_iota(i32, ext.shape, 0), 1).astype(f32)
    grp = lax.broadcasted_iota(i32, ext.shape, 1) // HEAD_DIM
    mean = jnp.where(grp == 0, s2 / jnp.minimum(pos1, 2.0),
                     jnp.where(grp == 1, s4 / jnp.minimum(pos1, 4.0),
                               jnp.where(grp == 2, s8 / jnp.minimum(pos1, 8.0), s16 / jnp.minimum(pos1, 16.0))))
    pooled = (mean - ext)[POOL_HALO:, :]
    o_ref[...] = (jnp.dot(pooled.astype(bf16), w_ref[...], preferred_element_type=f32) * sc_ref[...]).astype(bf16)


def _pool(u, wbd, scale, B, T):
    N = B * T
    tm = min(POOL_TM, T)
    assert tm == POOL_TM and T % POOL_TM == 0
    nt = T // tm
    hb = tm // POOL_HALO
    return pl.pallas_call(
        _pool_kernel, grid=(B, nt),
        in_specs=[pl.BlockSpec((tm, 256), lambda b, t: (b * nt + t, 0)),
                  pl.BlockSpec((POOL_HALO, 256), lambda b, t: (jnp.maximum((b * nt + t) * hb - 1, 0), 0)),
                  pl.BlockSpec((256, 256), lambda b, t: (0, 0)),
                  pl.BlockSpec((1, 256), lambda b, t: (0, 0))],
        out_specs=pl.BlockSpec((tm, 256), lambda b, t: (b * nt + t, 0)),
        out_shape=jax.ShapeDtypeStruct((N, 256), bf16),
        compiler_params=_cparams("arbitrary", "arbitrary"), name="pool")(u, u, wbd, scale)


GLA_TM = 256


def _gla_kernel(q_ref, k_ref, v_ref, g_ref, misc_ref, wa_ref, ba_ref, gn_ref, bd_ref, o_ref,
                st_ref, q_s, k_s, v_s, b_s, qe_s, ke_s, gam_s, o_s):
    nb = q_ref.shape[0]

    @pl.when(pl.program_id(0) == 0)
    def _():
        st_ref[...] = jnp.zeros_like(st_ref)

    r16 = lax.broadcasted_iota(i32, (GLA_TM, 256), 0) % GLA_SUB
    for bi_ in range(nb):
        x = jnp.dot(misc_ref[bi_], wa_ref[...], preferred_element_type=f32,
                    precision=lax.Precision.HIGHEST) + ba_ref[...]
        log_a = (jnp.minimum(x, 0.0) - jnp.log1p(jnp.exp(-jnp.abs(x)))) / GLA_TAU
        b = log_a
        for s in (1, 2, 4, 8):
            b = b + jnp.where(r16 >= s, pltpu.roll(b, s, axis=0), 0.0)
        b_end = jnp.where(r16 == GLA_SUB - 1, b, 0.0)
        for s in (1, 2, 4, 8):
            b_end = b_end + pltpu.roll(b_end, GLA_TM - s, axis=0)
        q = q_ref[bi_] * QK_SCALE
        k = k_ref[bi_]
        q_s[bi_] = q
        k_s[bi_] = k
        v_s[bi_] = v_ref[bi_]
        b_s[bi_] = b
        qe_s[bi_] = (q * jnp.exp(b)).astype(bf16)
        ke_s[bi_] = (k * jnp.exp(b_end - b)).astype(bf16)
        gam_s[bi_] = jnp.exp(b_end)
    bd = bd_ref[...]
    shape3 = (GLA_SUB, GLA_SUB, 256)
    causal = lax.broadcasted_iota(i32, shape3, 0) <= lax.broadcasted_iota(i32, shape3, 1)

    def block(n, _):
        r0 = pl.multiple_of(n * GLA_SUB, GLA_SUB)
        rows = pl.ds(r0, GLA_SUB)
        for bi_ in range(nb):
            qi, ki, vi, bi = q_s[bi_, rows, :], k_s[bi_, rows, :], v_s[bi_, rows, :], b_s[bi_, rows, :]
            diff = jnp.where(causal, bi[None, :, :] - bi[:, None, :], 0.0)
            w3 = jnp.where(causal, qi[None, :, :] * ki[:, None, :] * jnp.exp(diff), 0.0)
            a3 = jnp.dot(w3.reshape(GLA_SUB * GLA_SUB, 256).astype(bf16), bd, preferred_element_type=f32)
            intra = jnp.sum(a3.reshape(shape3) * vi[:, None, :], axis=0)
            st = st_ref[bi_]
            inter = lax.dot_general(qe_s[bi_, rows, :], st.astype(bf16), NT, preferred_element_type=f32)
            o_s[bi_, rows, :] = intra + inter
            upd = lax.dot_general(vi.astype(bf16), ke_s[bi_, rows, :], TN, preferred_element_type=f32)
            st_ref[bi_] = st * gam_s[bi_, pl.ds(r0, 1), :] + jnp.where(bd > 0, upd, 0.0)
        return 0

    lax.fori_loop(0, GLA_TM // GLA_SUB, block, 0)
    for bi_ in range(nb):
        o = o_s[bi_]
        ms = jnp.dot(o * o, bd.astype(f32), preferred_element_type=f32, precision=lax.Precision.HIGHEST) / HEAD_DIM
        o_ref[bi_] = (o * lax.rsqrt(ms + LN_EPS) * gn_ref[...] * _silu(g_ref[bi_])).astype(bf16)


def _gla(gq, gk, gv, gg, misc, wa, ba, gn, bd, B, T):
    N = B * T
    assert T % GLA_TM == 0
    tok = lambda w_: pl.BlockSpec((B, GLA_TM, w_), lambda t: (0, t, 0))
    full = lambda a: pl.BlockSpec(a.shape, lambda t: (0, 0))
    v = lambda dt: pltpu.VMEM((B, GLA_TM, 256), dt)
    seq = lambda a: a.reshape(B, T, a.shape[1])
    out = pl.pallas_call(
        _gla_kernel, grid=(T // GLA_TM,),
        in_specs=[tok(256), tok(256), tok(256), tok(256), tok(128), full(wa), full(ba), full(gn), full(bd)],
        out_specs=tok(256), out_shape=jax.ShapeDtypeStruct((B, T, 256), bf16),
        scratch_shapes=[pltpu.VMEM((B, 256, 256), f32), v(f32), v(f32), v(f32), v(f32), v(bf16), v(bf16), v(f32),
                        v(f32)],
        compiler_params=_cparams("arbitrary"), name="gla")(seq(gq), seq(gk), seq(gv), seq(gg), seq(misc), wa, ba, gn, bd)
    return out.reshape(N, 256)


OUT_TM = 256


def _outproj_kernel(mo_ref, no_ref, po_ref, go_ref, x_ref, mod_ref, w_ref, lng_ref, lnb_ref, wrh_ref, wrl_ref,
                    x1_ref, h2b_ref, lg_ref, *, alpha):
    a = jnp.concatenate([mo_ref[...], no_ref[...], po_ref[...], go_ref[...]], axis=1)
    y = jnp.dot(a, w_ref[...], preferred_element_type=f32)
    x1 = _layer_norm(alpha * x_ref[...] + mod_ref[0, 2:3, :] * y, lng_ref[...], lnb_ref[...])
    x1_ref[...] = x1
    h2 = x1 * (1.0 + mod_ref[0, 4:5, :]) + mod_ref[0, 3:4, :]
    h2b_ref[...] = h2.astype(bf16)
    hi, lo = _split_bf16(h2)
    wrh = wrh_ref[...]
    lg_ref[...] = (jnp.dot(hi, wrh, preferred_element_type=f32) + jnp.dot(lo, wrh, preferred_element_type=f32)
                   + jnp.dot(hi, wrl_ref[...], preferred_element_type=f32))


def _outproj(mo, no, po, go, x2, mod, w, lng, lnb, wrh, wrl, alpha, B, T):
    N = B * T
    nt = T // OUT_TM
    tok = lambda w_: pl.BlockSpec((OUT_TM, w_), lambda i: (i, 0))
    full = lambda a: pl.BlockSpec(a.shape, lambda i: (0,) * a.ndim)
    return pl.pallas_call(
        functools.partial(_outproj_kernel, alpha=alpha), grid=(N // OUT_TM,),
        in_specs=[tok(512), tok(256), tok(256), tok(256), tok(D_MODEL),
                  pl.BlockSpec((1, 6, D_MODEL), lambda i: (i // nt, 0, 0)),
                  full(w), full(lng), full(lnb), full(wrh), full(wrl)],
        out_specs=[tok(D_MODEL), tok(D_MODEL), tok(N_EXPERTS)],
        out_shape=[jax.ShapeDtypeStruct((N, D_MODEL), f32), jax.ShapeDtypeStruct((N, D_MODEL), bf16),
                   jax.ShapeDtypeStruct((N, N_EXPERTS), f32)],
        compiler_params=_cparams("arbitrary"), name="outproj")(mo, no, po, go, x2, mod, w, lng, lnb, wrh, wrl)


ROUTE_TM = 256
GROUP_SIZE = N_EXPERTS // N_EXPERT_GROUPS


def _per_token(rows):
    pad = jnp.zeros((LANES - len(rows), rows[0].shape[1]), f32)
    return jnp.transpose(jnp.concatenate(rows + [pad], axis=0))


def _route_kernel(lg_ref, rb_ref, ei_ref, wt_ref, cnt_ref, tb_ref, base_ref):
    @pl.when(pl.program_id(0) == 0)
    def _():
        base_ref[...] = jnp.zeros_like(base_ref)

    tb_ref[0] = base_ref[...].astype(i32)

    s = jax.nn.sigmoid(jnp.transpose(lg_ref[...]))
    ssel = s + rb_ref[...]
    shape3 = (N_EXPERT_GROUPS, GROUP_SIZE, ROUTE_TM)
    x3 = ssel.reshape(shape3)
    i3 = lax.broadcasted_iota(i32, shape3, 1)
    m1 = jnp.max(x3, axis=1, keepdims=True)
    first = jnp.min(jnp.where(x3 == m1, i3, BIG_IDX), axis=1, keepdims=True)
    m2 = jnp.max(jnp.where(i3 == first, -jnp.inf, x3), axis=1, keepdims=True)
    gscore = (m1 + m2).reshape(N_EXPERT_GROUPS, ROUTE_TM)
    gid = lax.broadcasted_iota(i32, gscore.shape, 0)
    beaten = jnp.zeros(gscore.shape, i32)
    for g in range(N_EXPERT_GROUPS):
        other = gscore[g:g + 1, :]
        beaten = beaten + jnp.where((other > gscore) | ((other == gscore) & (g < gid)), 1, 0)
    keep = jnp.broadcast_to((beaten < TOPK_GROUPS)[:, None, :], shape3).reshape(ssel.shape)
    x = jnp.where(keep, ssel, -jnp.inf)
    eid = lax.broadcasted_iota(i32, x.shape, 0)
    hits, idx_rows, w_rows = [], [], []
    for _ in range(TOP_K):
        mx = jnp.max(x, axis=0, keepdims=True)
        idx = jnp.min(jnp.where(x == mx, eid, BIG_IDX), axis=0, keepdims=True)
        hit = eid == idx
        hits.append(hit)
        idx_rows.append(idx.astype(f32))
        w_rows.append(jnp.sum(jnp.where(hit, s, 0.0), axis=0, keepdims=True))
        x = jnp.where(hit, -jnp.inf, x)
    wsum = w_rows[0]
    for w in w_rows[1:]:
        wsum = wsum + w
    chosen = jnp.zeros(s.shape, f32)
    for hit in hits:
        chosen = jnp.where(hit, 1.0, chosen)
    total = base_ref[...] + jnp.sum(chosen, axis=1, keepdims=True)
    base_ref[...] = total
    cnt_ref[...] = total.astype(i32)
    ei_ref[...] = _per_token(idx_rows).astype(i32)
    wt_ref[...] = _per_token([w / wsum * ROUTED_SCALE for w in w_rows])


def _route(logits, rb):
    N = logits.shape[0]
    tok = pl.BlockSpec((ROUTE_TM, LANES), lambda i: (i, 0))
    col = pl.BlockSpec((N_EXPERTS, 1), lambda i: (0, 0))
    return pl.pallas_call(
        _route_kernel, grid=(N // ROUTE_TM,),
        in_specs=[pl.BlockSpec((ROUTE_TM, N_EXPERTS), lambda i: (i, 0)), col],
        out_specs=[tok, tok, col, pl.BlockSpec((1, N_EXPERTS, 1), lambda i: (i, 0, 0))],
        out_shape=[jax.ShapeDtypeStruct((N, LANES), i32), jax.ShapeDtypeStruct((N, LANES), f32),
                   jax.ShapeDtypeStruct((N_EXPERTS, 1), i32), jax.ShapeDtypeStruct((N // ROUTE_TM, N_EXPERTS, 1), i32)],
        scratch_shapes=[pltpu.VMEM((N_EXPERTS, 1), f32)],
        compiler_params=_cparams("arbitrary"), name="route")(logits, rb.reshape(N_EXPERTS, 1))


DISP_TM = ROUTE_TM
DISP_SLOTS = DISP_TM * TOP_K
DISP_CHUNK = 16
DISP_PERM_ROWS = 512
DISP_MAX_CHUNKS = DISP_SLOTS // DISP_CHUNK + N_EXPERTS


def _dispatch_kernel(x_ref, ei_ref, tb_ref, ps_ref, triu_ref, tril_ref, xs_in, pos_ref, xs_hbm,
                     pbuf, tab_v, tab_s, csem, rsem):
    del xs_in
    i = pl.program_id(0)
    slot = i % 2

    @pl.when(i == 0)
    def _():
        pbuf[:, DISP_SLOTS * ROW_TILE:, :] = jnp.zeros((2, DISP_CHUNK * ROW_TILE, LANES), u32)

    ei_t = jnp.transpose(ei_ref[...].astype(f32))
    eid = lax.broadcasted_iota(i32, (N_EXPERTS, DISP_TM), 0).astype(f32)
    hits = [eid == ei_t[k:k + 1, :] for k in range(TOP_K)]
    member = jnp.zeros(eid.shape, f32)
    for hit in hits:
        member = jnp.where(hit, 1.0, member)
    prefix = jnp.dot(member.astype(bf16), triu_ref[...], preferred_element_type=f32)
    cnt = jnp.sum(member, axis=1, keepdims=True)
    nch = jnp.floor((cnt + (DISP_CHUNK - 1)) * (1.0 / DISP_CHUNK))
    lane = lax.broadcasted_iota(i32, (N_EXPERTS, LANES), 1)
    before = jnp.dot(tril_ref[...], jnp.where(lane == 0, cnt, jnp.where(lane == 1, nch, 0.0)).astype(bf16),
                     preferred_element_type=f32)
    off, cidx = before[:, 0:1], before[:, 1:2]
    dst = (ps_ref[...] + tb_ref[0]).astype(f32)
    slot_rows = [jnp.sum(jnp.where(hit, off + prefix, 0.0), axis=0, keepdims=True) for hit in hits]
    pos_ref[...] = _per_token([jnp.sum(jnp.where(hit, dst + prefix, 0.0), axis=0, keepdims=True)
                               for hit in hits]).astype(i32)
    rel = lax.broadcasted_iota(i32, (N_EXPERTS, DISP_MAX_CHUNKS), 1).astype(f32) - cidx
    mine = (rel >= 0.0) & (rel < nch)
    src_row = jnp.sum(jnp.where(mine, off + DISP_CHUNK * rel, 0.0), axis=0, keepdims=True)
    dst_row = jnp.sum(jnp.where(mine, dst + DISP_CHUNK * rel, 0.0), axis=0, keepdims=True)
    n_row = jnp.broadcast_to(jnp.sum(nch, axis=0, keepdims=True), src_row.shape)
    tab_v[...] = jnp.concatenate([src_row, dst_row, n_row, jnp.zeros((5, DISP_MAX_CHUNKS), f32)], axis=0).astype(i32)
    cp = pltpu.make_async_copy(tab_v, tab_s.at[slot], csem)
    cp.start()

    x = x_ref[...]
    for r in range(DISP_SLOTS // DISP_PERM_ROWS):
        sid = (r * DISP_PERM_ROWS + lax.broadcasted_iota(i32, (DISP_PERM_ROWS, DISP_TM), 0)).astype(f32)
        perm = jnp.zeros(sid.shape, f32)
        for srow in slot_rows:
            perm = jnp.where(sid == srow, 1.0, perm)
        rows = jnp.dot(perm.astype(bf16), x, preferred_element_type=f32)
        for s, w in enumerate(_pack_rows(rows)):
            pbuf[slot, pl.ds(r * DISP_PERM_ROWS * ROW_TILE + s, DISP_PERM_ROWS, stride=ROW_TILE), :] = w
    cp.wait()

    def chunk_copy(sl, src, dst_):
        span = lambda r0: pl.ds(pl.multiple_of(r0 * ROW_TILE, ROW_TILE), DISP_CHUNK * ROW_TILE)
        return pltpu.make_async_copy(pbuf.at[sl, span(src)], xs_hbm.at[span(dst_)], rsem.at[sl])

    def start_all(sl):
        def body(j, _):
            chunk_copy(sl, tab_s[sl, 0, j], tab_s[sl, 1, j]).start()
            return 0
        lax.fori_loop(0, tab_s[sl, 2, 0], body, 0)

    def wait_all(sl):
        def body(j, _):
            chunk_copy(sl, 0, 0).wait()
            return 0
        lax.fori_loop(0, tab_s[sl, 2, 0], body, 0)

    @pl.when(i > 0)
    def _():
        wait_all(1 - slot)
    start_all(slot)

    @pl.when(i == pl.num_programs(0) - 1)
    def _():
        wait_all(slot)


def _dispatch(h2b, eidx, tile_base, pstarts, n_rows):
    N = h2b.shape[0]
    r_, c_ = jnp.arange(DISP_TM)[:, None], jnp.arange(DISP_TM)[None, :]
    triu = (r_ < c_).astype(bf16)
    e_, f_ = jnp.arange(N_EXPERTS)[:, None], jnp.arange(N_EXPERTS)[None, :]
    tril = (f_ < e_).astype(bf16)
    tok = pl.BlockSpec((DISP_TM, LANES), lambda i: (i, 0))
    full = lambda a: pl.BlockSpec(a.shape, lambda i: (0,) * a.ndim)
    xs0 = jnp.zeros((n_rows * ROW_TILE, LANES), u32)
    pos, xs = pl.pallas_call(
        _dispatch_kernel, grid=(N // DISP_TM,),
        in_specs=[pl.BlockSpec((DISP_TM, D_MODEL), lambda i: (i, 0)), tok,
                  pl.BlockSpec((1, N_EXPERTS, 1), lambda i: (i, 0, 0)), full(pstarts), full(triu), full(tril),
                  pl.BlockSpec(memory_space=pl.ANY)],
        out_specs=[tok, pl.BlockSpec(memory_space=pl.ANY)],
        out_shape=[jax.ShapeDtypeStruct((N, LANES), i32), jax.ShapeDtypeStruct(xs0.shape, u32)],
        scratch_shapes=[pltpu.VMEM((2, (DISP_SLOTS + DISP_CHUNK) * ROW_TILE, LANES), u32),
                        pltpu.VMEM((8, DISP_MAX_CHUNKS), i32), pltpu.SMEM((2, 8, DISP_MAX_CHUNKS), i32),
                        pltpu.SemaphoreType.DMA, pltpu.SemaphoreType.DMA((2,))],
        input_output_aliases={6: 1},
        compiler_params=_cparams("arbitrary"), name="moe_dispatch")(h2b, eidx, tile_base, pstarts, triu, tril, xs0)
    return pos, xs


MOE_BLK = MOE_ROWS * ROW_TILE
MOE_IN_BUFS = 4
MOE_OUT_BUFS = 3


def _moe_kernel(b0_ref, nb_ref, nu_ref, xs_in, wg_ref, wu_ref, wd_ref, xs_out, xbuf, ybuf, isem, osem):
    e = pl.program_id(0)
    nu = nu_ref[0]

    def in_copy(b):
        return pltpu.make_async_copy(xs_in.at[pl.ds(pl.multiple_of(b * MOE_BLK, MOE_BLK), MOE_BLK)],
                                     xbuf.at[b % MOE_IN_BUFS], isem.at[b % MOE_IN_BUFS])

    def out_copy(b):
        return pltpu.make_async_copy(ybuf.at[b % MOE_OUT_BUFS],
                                     xs_out.at[pl.ds(pl.multiple_of(b * MOE_BLK, MOE_BLK), MOE_BLK)],
                                     osem.at[b % MOE_OUT_BUFS])

    @pl.when(e == 0)
    def _():
        for j in range(MOE_IN_BUFS - 1):
            @pl.when(j < nu)
            def _():
                in_copy(j).start()

    wg = wg_ref[0, 0].astype(bf16)
    wu = wu_ref[0, 0].astype(bf16)
    wd = wd_ref[0, 0].astype(bf16)

    def block(i, _):
        b = b0_ref[e] + i
        in_copy(b).wait()

        @pl.when(b + MOE_IN_BUFS - 1 < nu)
        def _():
            in_copy(b + MOE_IN_BUFS - 1).start()

        @pl.when(b >= MOE_OUT_BUFS)
        def _():
            out_copy(b - MOE_OUT_BUFS).wait()

        g = jnp.zeros((MOE_ROWS, D_EXPERT), f32)
        u = jnp.zeros((MOE_ROWS, D_EXPERT), f32)
        for s in range(ROW_TILE):
            word = xbuf[b % MOE_IN_BUFS, pl.ds(s, MOE_ROWS, stride=ROW_TILE), :]
            xc = jnp.concatenate(_unpack_word(word), axis=1).astype(bf16)
            rows = slice(2 * s * LANES, (2 * s + 2) * LANES)
            g = g + jnp.dot(xc, wg[rows], preferred_element_type=f32)
            u = u + jnp.dot(xc, wu[rows], preferred_element_type=f32)
        y = jnp.dot((_silu(g) * u).astype(bf16), wd, preferred_element_type=f32)
        for s, w in enumerate(_pack_rows(y)):
            ybuf[b % MOE_OUT_BUFS, pl.ds(s, MOE_ROWS, stride=ROW_TILE), :] = w
        out_copy(b).start()
        return 0

    lax.fori_loop(0, nb_ref[e], block, 0)

    @pl.when(e == pl.num_programs(0) - 1)
    def _():
        for j in range(MOE_OUT_BUFS, 0, -1):
            @pl.when(nu >= j)
            def _():
                out_copy(nu - j).wait()


def _moe(layer, blk0, nblk, n_used, xs, wg, wu, wd):
    wspec = lambda a: pl.BlockSpec((1, 1) + a.shape[2:], lambda e, b0, nb, nu: (layer, e, 0, 0))
    gs = pltpu.PrefetchScalarGridSpec(
        num_scalar_prefetch=3, grid=(N_EXPERTS,),
        in_specs=[pl.BlockSpec(memory_space=pl.ANY), wspec(wg), wspec(wu), wspec(wd)],
        out_specs=pl.BlockSpec(memory_space=pl.ANY),
        scratch_shapes=[pltpu.VMEM((MOE_IN_BUFS, MOE_BLK, LANES), u32), pltpu.VMEM((MOE_OUT_BUFS, MOE_BLK, LANES), u32),
                        pltpu.SemaphoreType.DMA((MOE_IN_BUFS,)), pltpu.SemaphoreType.DMA((MOE_OUT_BUFS,))])
    return pl.pallas_call(
        _moe_kernel, grid_spec=gs, out_shape=jax.ShapeDtypeStruct(xs.shape, u32),
        input_output_aliases={3: 0},
        compiler_params=_cparams("arbitrary"), name="moe_experts")(blk0, nblk, n_used, xs, wg, wu, wd)


FIN_TM = 128


def _tile_gather(pos_hbm, ys_hbm, pos_s, gbuf, isem, rsem, step, n_steps):
    def idx_copy(s):
        return pltpu.make_async_copy(pos_hbm.at[pl.ds(s * FIN_TM, FIN_TM)], pos_s.at[s % 2], isem.at[s % 2])

    def tile(i):
        return pl.ds(pl.multiple_of(i * ROW_TILE, ROW_TILE), ROW_TILE)

    def row_copy(slot, t, k, p):
        return pltpu.make_async_copy(ys_hbm.at[tile(p)], gbuf.at[slot, k, tile(t)], rsem.at[slot])

    def start_rows(s):
        slot = s % 2

        def body(t, _):
            for k in range(TOP_K):
                row_copy(slot, t, k, pos_s[slot, t, k]).start(priority=k % 2)
            return 0
        lax.fori_loop(0, FIN_TM, body, 0)

    @pl.when(step == 0)
    def _():
        idx_copy(step).start()
        idx_copy(step).wait()
        start_rows(step)

        @pl.when(n_steps > 1)
        def _():
            idx_copy(step + 1).start()

    @pl.when(step + 1 < n_steps)
    def _():
        idx_copy(step + 1).wait()
        start_rows(step + 1)

        @pl.when(step + 2 < n_steps)
        def _():
            idx_copy(step + 2).start()

    slot = step % 2

    def wait_body(t, _):
        for k in range(TOP_K):
            row_copy(slot, t, k, 0).wait()
        return 0
    lax.fori_loop(0, FIN_TM, wait_body, 0)


def _fin_kernel(pos_hbm, ys_hbm, h2_ref, x1_ref, wt_ref, mod_ref, sg_ref, su_ref, sd_ref, lng_ref, lnb_ref, o_ref,
                pos_s, gbuf, isem, rsem, *, alpha):
    i = pl.program_id(0)
    _tile_gather(pos_hbm, ys_hbm, pos_s, gbuf, isem, rsem, i, pl.num_programs(0))
    hb = h2_ref[...]
    g = jnp.dot(hb, sg_ref[...], preferred_element_type=f32)
    u = jnp.dot(hb, su_ref[...], preferred_element_type=f32)
    shared = jnp.dot((_silu(g) * u).astype(bf16), sd_ref[...], preferred_element_type=f32)
    wt = wt_ref[...]
    lane = lax.broadcasted_iota(i32, wt.shape, 1)
    wk = [jnp.sum(jnp.where(lane == k, wt, 0.0), axis=1, keepdims=True) for k in range(TOP_K)]
    slot = i % 2
    cols = []
    for s in range(ROW_TILE):
        sub = pl.ds(s, FIN_TM, stride=ROW_TILE)
        lo, hi = _unpack_word(gbuf[slot, 0, sub, :])
        acc_lo, acc_hi = lo * wk[0], hi * wk[0]
        for k in range(1, TOP_K):
            lo, hi = _unpack_word(gbuf[slot, k, sub, :])
            acc_lo, acc_hi = acc_lo + lo * wk[k], acc_hi + hi * wk[k]
        cols += [acc_lo, acc_hi]
    y = shared + jnp.concatenate(cols, axis=1)
    o_ref[...] = _layer_norm(alpha * x1_ref[...] + mod_ref[0, 5:6, :] * y, lng_ref[...], lnb_ref[...])


def _fin(pos, ys, h2b, x1, wts, mod, sg, su, sd, lng, lnb, alpha, B, T):
    N = B * T
    nt = T // FIN_TM
    tok = lambda w_: pl.BlockSpec((FIN_TM, w_), lambda i: (i, 0))
    full = lambda a: pl.BlockSpec(a.shape, lambda i: (0,) * a.ndim)
    return pl.pallas_call(
        functools.partial(_fin_kernel, alpha=alpha), grid=(N // FIN_TM,),
        in_specs=[pl.BlockSpec(memory_space=pl.ANY), pl.BlockSpec(memory_space=pl.ANY),
                  tok(D_MODEL), tok(D_MODEL), tok(LANES),
                  pl.BlockSpec((1, 6, D_MODEL), lambda i: (i // nt, 0, 0)),
                  full(sg), full(su), full(sd), full(lng), full(lnb)],
        out_specs=tok(D_MODEL), out_shape=jax.ShapeDtypeStruct((N, D_MODEL), f32),
        scratch_shapes=[pltpu.SMEM((2, FIN_TM, LANES), i32), pltpu.VMEM((2, TOP_K, FIN_TM * ROW_TILE, LANES), u32),
                        pltpu.SemaphoreType.DMA((2,)), pltpu.SemaphoreType.DMA((2,))],
        compiler_params=_cparams("arbitrary"), name="combine")(pos, ys, h2b, x1, wts, mod, sg, su, sd, lng, lnb)


def _pack_w_in(w):
    c = lambda name, width: w[:, _OFF[name]:_OFF[name] + width]
    cols = [c("moba_q", 256), c("moba_k", 256), c("moba_v", 256), c("nsa_q", 256),
            c("k_slc", 64), c("k_win", 64), c("v_slc", 64), c("v_win", 64), c("k_cmp", 64), c("v_cmp", 64),
            c("pool", 256), c("gla_q", 256), c("gla_k", 256), c("gla_v", 256), c("gla_g", 256),
            c("nsa_gate", 12), c("gla_a", 16), jnp.zeros((w.shape[0], LANES - 28), w.dtype)]
    return jnp.concatenate(cols, axis=1).astype(bf16)


def _rope_tables(T):
    half = ROPE_DIM // 2
    inv_freq = ROPE_THETA ** (-jnp.arange(half, dtype=f32) / half)
    ang = jnp.arange(T).astype(f32)[:, None] * inv_freq[None, :]
    cos, sin = jnp.cos(ang), jnp.sin(ang)
    one = jnp.ones((T, HEAD_DIM - ROPE_DIM), f32)
    zero = jnp.zeros((T, HEAD_DIM - ROPE_DIM), f32)
    ct = jnp.concatenate([cos, cos, one], axis=1)
    st = jnp.concatenate([-sin, sin, zero], axis=1)
    return jnp.tile(ct, (1, 2)), jnp.tile(st, (1, 2))


def _cmp_weights(pe, w1, w2):
    half = CMP_BLOCK // 2
    z = jnp.zeros((half, HEAD_DIM, CMP_HIDDEN), f32)

    def arrange(lo):
        wk = w1[0].reshape(CMP_BLOCK, HEAD_DIM, CMP_HIDDEN)[lo:lo + half]
        wv = w1[1].reshape(CMP_BLOCK, HEAD_DIM, CMP_HIDDEN)[lo:lo + half]
        top = jnp.concatenate([wk, z], axis=2)
        bot = jnp.concatenate([z, wv], axis=2)
        return jnp.concatenate([top, bot], axis=1).reshape(half * 2 * HEAD_DIM, 2 * CMP_HIDDEN).astype(bf16)

    def pe_row(lo):
        return jnp.concatenate([pe[0, lo:lo + half], pe[1, lo:lo + half]], axis=1).reshape(1, half * 2 * HEAD_DIM)

    zc = jnp.zeros((CMP_HIDDEN, LANES - HEAD_DIM), f32)
    zr = jnp.zeros((CMP_HIDDEN, LANES), f32)
    w2k = jnp.concatenate([jnp.concatenate([w2[0], zc], axis=1), zr], axis=0).astype(bf16)
    w2v = jnp.concatenate([zr, jnp.concatenate([w2[1], zc], axis=1)], axis=0).astype(bf16)
    return arrange(0), arrange(half), pe_row(0), pe_row(half), w2k, w2v


def _overlap_matrix(ns):
    n = jnp.arange(ns)[:, None] * CMP_STRIDE
    j = jnp.arange(LANES)[None, :] * SLC_BLOCK
    ov = (n < j + SLC_BLOCK) & (n + CMP_BLOCK > j) & (jnp.arange(ns)[:, None] < ns - 1)
    return ov.astype(bf16)


def _block_diag_ones():
    h = jnp.arange(256) // HEAD_DIM
    return (h[:, None] == h[None, :]).astype(bf16)


def _expert_layout(counts, n_tok):
    counts = counts.reshape(N_EXPERTS)
    slack = DISP_CHUNK - 1
    padded = jnp.where(counts > 0, (counts + slack + MOE_ROWS - 1) // MOE_ROWS * MOE_ROWS, 0)
    pstarts = (jnp.cumsum(padded) - padded).astype(i32)
    n_blocks = n_tok * TOP_K // MOE_ROWS + N_EXPERTS + (N_EXPERTS * slack) // MOE_ROWS + 1
    n_used = (jnp.sum(padded) // MOE_ROWS).astype(i32).reshape(1)
    return (pstarts.reshape(N_EXPERTS, 1), pstarts // MOE_ROWS, (padded // MOE_ROWS).astype(i32), n_used,
            n_blocks * MOE_ROWS)


def _mixer_inputs(x2, mod, w_in, B, T):
    ct, st = _rope_tables(T)
    return _inproj(x2, mod, _pack_w_in(w_in), ct, st, B, T)


def _token_mixers(x2, mod, w_in, cmp_pe, cmp_w1, cmp_w2, pool_w, pool_scale, gla_wa, gla_ba, gla_norm, B, T):
    N = B * T
    (mq, mk, mv, km, nq, nqr, ks, kw, vs, vw, kvc, pool_u, gq, gk, gv, gg, misc) = _mixer_inputs(x2, mod, w_in, B, T)
    nt = T // MOBA_BLOCK
    kmh = km.reshape(B, nt, N_HEADS, HEAD_DIM).transpose(0, 2, 1, 3)
    mo = _moba(mq, mk, mv, jnp.pad(kmh, ((0, 0), (0, 0), (0, MOBA_NBLK - nt), (0, LANES - HEAD_DIM))), B, T)
    ns = T // CMP_STRIDE
    kc, vc = _cmp(kvc.reshape(B, ns, CMP_STRIDE * LANES), *_cmp_weights(cmp_pe, cmp_w1, cmp_w2))
    no = _nsa(nq, nqr, misc, ks, vs, kw, vw, kc, vc, _overlap_matrix(ns), B, T)
    wbd = jax.scipy.linalg.block_diag(*[pool_w[g] for g in range(len(POOL_WINDOWS))]).astype(bf16)
    po = _pool(pool_u, wbd, pool_scale.reshape(1, 256), B, T)
    wa = jnp.zeros((LANES, 256), f32).at[MISC_A0:MISC_A0 + GLA_LOWRANK].set(gla_wa)
    go = _gla(gq, gk, gv, gg, misc, wa, gla_ba.reshape(1, 256), jnp.tile(gla_norm, N_HEADS).reshape(1, 256),
              _block_diag_ones(), B, T)
    return mo, no, po, go


def _pad_w_out(w_out):
    wm = w_out[:GROUP_WIDTH].reshape(N_HEADS, HEAD_DIM, D_MODEL)
    wm = jnp.pad(wm, ((0, 0), (0, LANES - HEAD_DIM), (0, 0))).reshape(N_HEADS * LANES, D_MODEL)
    return jnp.concatenate([wm, w_out[GROUP_WIDTH:]], axis=0).astype(bf16)


def kernel(x, c, w_ada, b_ada, w_in, cmp_pe, cmp_w1, cmp_w2, pool_w, pool_scale, gla_wa, gla_ba, gla_norm, w_out,
           ln_g, ln_b, w_router, router_bias, exp_gate, exp_up, exp_down, sh_gate, sh_up, sh_down):
    B, T, D = x.shape
    N = B * T
    depth = w_ada.shape[0]
    alpha = float((2 * depth) ** 0.25)
    x2 = x.reshape(N, D)
    c8 = jnp.zeros((8, D), f32).at[:B].set(c)
    for l in range(depth):
        mod = _ada(c8, w_ada[l], b_ada[l].reshape(1, -1))[:B].reshape(B, 6, D)
        mo, no, po, go = _token_mixers(x2, mod, w_in[l], cmp_pe[l], cmp_w1[l], cmp_w2[l], pool_w[l], pool_scale[l],
                                       gla_wa[l], gla_ba[l], gla_norm[l], B, T)
        wrh, wrl = _split_bf16(w_router[l])
        x1, h2b, logits = _outproj(mo, no, po, go, x2, mod, _pad_w_out(w_out[l]), ln_g[l, 0].reshape(1, D),
                                   ln_b[l, 0].reshape(1, D), wrh, wrl, alpha, B, T)
        eidx, wts, counts, tile_base = _route(logits, router_bias[l].reshape(1, N_EXPERTS))
        pstarts, blk0, nblk, n_used, n_rows = _expert_layout(counts, N)
        pos, xs = _dispatch(h2b, eidx, tile_base, pstarts, n_rows)
        ys = _moe(l, blk0, nblk, n_used, xs, exp_gate, exp_up, exp_down)
        x2 = _fin(pos, ys, h2b, x1, wts, mod, sh_gate[l].astype(bf16), sh_up[l].astype(bf16),
                  sh_down[l].astype(bf16), ln_g[l, 1].reshape(1, D), ln_b[l, 1].reshape(1, D), alpha, B, T)
    return x2.reshape(B, T, D)
```

```python
import functools

import jax
import jax.numpy as jnp
from jax import lax
from jax.experimental import pallas as pl
from jax.experimental.pallas import tpu as pltpu

f32, bf16, i32, u32 = jnp.float32, jnp.bfloat16, jnp.int32, jnp.uint32

D_MODEL = 1024
HEAD_DIM = 64
N_HEADS = 4
GROUP_WIDTH = 256
ROPE_THETA = 500000.0
ROPE_DIM = 16
MOBA_BLOCK = 256
MOBA_TOPK = 3
CMP_BLOCK = 32
CMP_STRIDE = 16
CMP_HIDDEN = 128
SLC_BLOCK = 64
SLC_TOPN = 16
WIN = 512
FORCE_SCORE = 1e9
POOL_WINDOWS = (2, 4, 8, 16)
GLA_SUB = 16
GLA_LOWRANK = 16
GLA_TAU = 16.0
N_EXPERTS = 256
TOP_K = 8
N_EXPERT_GROUPS = 8
TOPK_GROUPS = 4
D_EXPERT = 256
ROUTED_SCALE = 2.5
LN_EPS = 1e-5
QK_SCALE = HEAD_DIM ** -0.5
LOG2E = 1.4426950408889634
Q_SCALE = QK_SCALE * LOG2E

LANES = 128
MASKED = -1e30
M_INIT = -3e38
BIG_IDX = 1 << 20
MOE_ROWS = 256
ROW_TILE = D_MODEL // (2 * LANES)
VMEM_LIMIT = 56 * 1024 * 1024

_OFF = dict(moba_q=0, moba_k=256, moba_v=512, nsa_q=768, k_cmp=1024, v_cmp=1088, k_slc=1152, v_slc=1216,
            k_win=1280, v_win=1344, nsa_gate=1408, pool=1420, gla_q=1676, gla_k=1932, gla_v=2188,
            gla_a=2444, gla_g=2460)
_S = dict(mq=0, mk=256, mv=512, nq=768, sw=1024, vsw=1152, kvc=1280, pool=1408, gq=1664, gk=1920, gv=2176,
          gg=2432, misc=2688)
IN_COLS_PACKED = 2816
MISC_GATE0 = 0
MISC_A0 = 12

NT = (((1,), (1,)), ((), ()))
TN = (((0,), (0,)), ((), ()))


def _cparams(*sem):
    return pltpu.CompilerParams(dimension_semantics=sem, vmem_limit_bytes=VMEM_LIMIT)


def _silu(x):
    return x * jax.nn.sigmoid(x)


def _split_bf16(x):
    hi = x.astype(bf16)
    lo = (x - hi.astype(f32)).astype(bf16)
    return hi, lo


def _pack_rows(x):
    bits = lambda v: lax.bitcast_convert_type(v.astype(bf16).astype(f32), u32)
    return [(bits(x[:, (2 * s) * LANES:(2 * s + 1) * LANES]) >> 16) | bits(x[:, (2 * s + 1) * LANES:(2 * s + 2) * LANES])
            for s in range(ROW_TILE)]


def _unpack_word(w):
    return (lax.bitcast_convert_type(w << 16, f32), lax.bitcast_convert_type(w & jnp.uint32(0xFFFF0000), f32))


def _layer_norm(z, g, b):
    mu = jnp.mean(z, axis=-1, keepdims=True)
    zc = z - mu
    var = jnp.mean(zc * zc, axis=-1, keepdims=True)
    return zc * lax.rsqrt(var + LN_EPS) * g + b


def _argmax_rounds(score, index, rounds, axis=1):
    picked = jnp.zeros(score.shape, f32)
    for _ in range(rounds):
        mx = jnp.max(score, axis=axis, keepdims=True)
        first = jnp.min(jnp.where(score == mx, index, BIG_IDX), axis=axis, keepdims=True)
        hit = index == first
        picked = jnp.where(hit, 1.0, picked)
        score = jnp.where(hit, -jnp.inf, score)
    return picked


def _softmax_steps_t(s_list, carries, v_list):
    m_new = [jnp.maximum(c[0], jnp.max(s, axis=0, keepdims=True)) for s, c in zip(s_list, carries)]
    p = [jnp.exp2(s - m) for s, m in zip(s_list, m_new)]
    pv = [jnp.dot(v, pi.astype(bf16), preferred_element_type=f32) for v, pi in zip(v_list, p)]
    out = []
    for c, m, pi, pvi in zip(carries, m_new, p, pv):
        alpha = jnp.exp2(c[0] - m)
        out.append((m, alpha * c[1] + jnp.sum(pi, axis=0, keepdims=True), alpha * c[2] + pvi))
    return tuple(out)


def _softmax_init_t(groups, queries):
    return tuple((jnp.full((1, queries), M_INIT, f32), jnp.zeros((1, queries), f32),
                  jnp.zeros((LANES, queries), f32)) for _ in range(groups))


def _ada_kernel(c_ref, w_ref, b_ref, o_ref):
    o_ref[...] = jnp.dot(_silu(c_ref[...]), w_ref[...], preferred_element_type=f32,
                         precision=lax.Precision.HIGHEST) + b_ref[...]


def _ada(c8, w, b):
    n = w.shape[1] // D_MODEL
    return pl.pallas_call(
        _ada_kernel, grid=(n,),
        in_specs=[pl.BlockSpec((8, D_MODEL), lambda j: (0, 0)),
                  pl.BlockSpec((D_MODEL, D_MODEL), lambda j: (0, j)),
                  pl.BlockSpec((1, D_MODEL), lambda j: (0, j))],
        out_specs=pl.BlockSpec((8, D_MODEL), lambda j: (0, j)),
        out_shape=jax.ShapeDtypeStruct((8, w.shape[1]), f32),
        compiler_params=_cparams("arbitrary"), name="ada")(c8, w, b)


IN_TM = MOBA_BLOCK


def _inproj_kernel(x_ref, mod_ref, w_ref, ct_ref, st_ref,
                   mq_ref, mk_ref, mv_ref, km_ref, nq_ref, nqr_ref, ks_ref, kw_ref, vs_ref, vw_ref,
                   kvc_ref, pool_ref, gq_ref, gk_ref, gv_ref, gg_ref, misc_ref, *, nt):
    tb = pl.program_id(0) % nt
    h = (x_ref[...] * (1.0 + mod_ref[0, 1:2, :]) + mod_ref[0, 0:1, :]).astype(bf16)

    def seg(name, width):
        a = _S[name]
        return jnp.dot(h, w_ref[:, a:a + width], preferred_element_type=f32)

    ct, st = ct_ref[...], st_ref[...]
    lane = lax.broadcasted_iota(i32, (IN_TM, LANES), 1)
    first8 = (lane % HEAD_DIM) < ROPE_DIM // 2
    half = lane < HEAD_DIM

    def rope128(y):
        partner = jnp.where(first8, pltpu.roll(y, LANES - 8, axis=1), pltpu.roll(y, 8, axis=1))
        return y * ct + partner * st

    def rope(y):
        return jnp.concatenate([rope128(y[:, c * LANES:(c + 1) * LANES]) for c in range(y.shape[1] // LANES)],
                               axis=1)

    def lo_half(y):
        return jnp.where(half, y, 0.0)

    def hi_half(y):
        return jnp.where(half, pltpu.roll(y, HEAD_DIM, axis=1), 0.0)

    def per_head(y):
        parts = []
        for c in range(2):
            yc = y[:, c * LANES:(c + 1) * LANES]
            parts += [lo_half(yc), hi_half(yc)]
        return jnp.concatenate(parts, axis=1)

    mq_ref[...] = per_head(rope(seg("mq", 256)) * Q_SCALE).astype(bf16)
    k = rope(seg("mk", 256))
    km_ref[0] = jnp.mean(k, axis=0, keepdims=True)
    lane4 = lax.broadcasted_iota(i32, (IN_TM, 4 * LANES), 1)
    mk_ref[...] = jnp.where((lane4 % LANES) == HEAD_DIM + tb, 1.0, per_head(k)).astype(bf16)
    mv_ref[...] = per_head(seg("mv", 256)).astype(bf16)
    q = seg("nq", 256) * Q_SCALE
    nq_ref[...] = per_head(q).astype(bf16)
    nqr_ref[...] = per_head(rope(q)).astype(bf16)
    sw = rope128(seg("sw", 128))
    row = lax.broadcasted_iota(i32, (IN_TM, LANES), 0)
    slc_id = tb * (IN_TM // SLC_BLOCK) + row // SLC_BLOCK
    ks_ref[...] = jnp.concatenate([lo_half(sw), jnp.where(lane == slc_id, 1.0, 0.0)], axis=1).astype(bf16)
    kw_ref[...] = hi_half(sw).astype(bf16)
    vsw = seg("vsw", 128)
    vs_ref[...] = lo_half(vsw).astype(bf16)
    vw_ref[...] = hi_half(vsw).astype(bf16)
    kvc_ref[...] = seg("kvc", 128).astype(bf16)
    pool_ref[...] = seg("pool", 256)
    gq_ref[...] = seg("gq", 256)
    gk_ref[...] = seg("gk", 256)
    gv_ref[...] = seg("gv", 256)
    gg_ref[...] = seg("gg", 256)
    misc_ref[...] = seg("misc", 128)


def _inproj(x2, mod, w, ct, st, B, T):
    N = B * T
    nt = T // IN_TM
    assert T % IN_TM == 0 and nt <= 32 and T // SLC_BLOCK <= LANES
    row = lambda w_, dt: (jax.ShapeDtypeStruct((N, w_), dt), pl.BlockSpec((IN_TM, w_), lambda i: (i, 0)))
    outs = [row(512, bf16), row(512, bf16), row(512, bf16),
            (jax.ShapeDtypeStruct((N // IN_TM, 1, 256), f32), pl.BlockSpec((1, 1, 256), lambda i: (i, 0, 0))),
            row(512, bf16), row(512, bf16), row(256, bf16), row(128, bf16), row(128, bf16), row(128, bf16),
            row(128, bf16), row(256, f32), row(256, f32), row(256, f32), row(256, f32), row(256, f32),
            row(128, f32)]
    return pl.pallas_call(
        functools.partial(_inproj_kernel, nt=nt), grid=(N // IN_TM,),
        in_specs=[pl.BlockSpec((IN_TM, D_MODEL), lambda i: (i, 0)),
                  pl.BlockSpec((1, 6, D_MODEL), lambda i: (i // nt, 0, 0)),
                  pl.BlockSpec((D_MODEL, IN_COLS_PACKED), lambda i: (0, 0), pipeline_mode=pl.Buffered(1)),
                  pl.BlockSpec((IN_TM, LANES), lambda i: (i % nt, 0)),
                  pl.BlockSpec((IN_TM, LANES), lambda i: (i % nt, 0))],
        out_specs=[o[1] for o in outs], out_shape=[o[0] for o in outs],
        compiler_params=_cparams("arbitrary"), name="inproj")(x2, mod, w, ct, st)


def _cmp_kernel(x_ref, wa_ref, wb_ref, pea_ref, peb_ref, w2k_ref, w2v_ref, kc_ref, vc_ref):
    x = x_ref[0].astype(f32)
    a = jnp.dot((x + pea_ref[...]).astype(bf16), wa_ref[...], preferred_element_type=f32)
    b = jnp.dot((x + peb_ref[...]).astype(bf16), wb_ref[...], preferred_element_type=f32)
    hid = a + pltpu.roll(b, x.shape[0] - 1, axis=0)
    g = jax.nn.gelu(hid).astype(bf16)
    kc_ref[0] = jnp.dot(g, w2k_ref[...], preferred_element_type=f32).astype(bf16)
    vc_ref[0] = jnp.dot(g, w2v_ref[...], preferred_element_type=f32).astype(bf16)


def _cmp(xseg, wa, wb, pea, peb, w2k, w2v):
    B, ns, wd = xseg.shape
    full = lambda a: pl.BlockSpec(a.shape, lambda b: (0,) * a.ndim)
    return pl.pallas_call(
        _cmp_kernel, grid=(B,),
        in_specs=[pl.BlockSpec((1, ns, wd), lambda b: (b, 0, 0))] + [full(a) for a in (wa, wb, pea, peb, w2k, w2v)],
        out_specs=[pl.BlockSpec((1, ns, LANES), lambda b: (b, 0, 0))] * 2,
        out_shape=[jax.ShapeDtypeStruct((B, ns, LANES), bf16)] * 2,
        compiler_params=_cparams("arbitrary"), name="nsa_compress")(xseg, wa, wb, pea, peb, w2k, w2v)


MOBA_NBLK = 32
KEY_TILE = 512
MOBA_TQ = 512


def _moba_kernel(q_ref, k_ref, vt_ref, km_ref, o_ref):
    q0 = pl.program_id(1) * MOBA_TQ
    heads = range(N_HEADS)
    hl = lambda h: slice(h * LANES, (h + 1) * LANES)
    blk = lax.broadcasted_iota(i32, (MOBA_NBLK, MOBA_TQ), 0)
    own = (q0 + lax.broadcasted_iota(i32, (1, MOBA_TQ), 1)) // MOBA_BLOCK
    past = blk < own
    zeros = lambda n: jnp.zeros((n, MOBA_TQ), f32)
    qf = []
    for h in heads:
        qh = q_ref[:, hl(h)]
        hi, lo = _split_bf16(km_ref[0, h])
        gate_t = (lax.dot_general(hi, qh, NT, preferred_element_type=f32)
                  + lax.dot_general(lo, qh, NT, preferred_element_type=f32))
        picked = _argmax_rounds(jnp.where(past, gate_t, -jnp.inf), blk, MOBA_TOPK, axis=0)
        allowed = ((picked > 0.0) & past) | (blk == own)
        bias_t = jnp.concatenate([zeros(HEAD_DIM), jnp.where(allowed, 0.0, MASKED),
                                  zeros(LANES - HEAD_DIM - MOBA_NBLK)], axis=0)
        qf.append(qh + jnp.transpose(bias_t).astype(bf16))

    def scores_t(h, off):
        return lax.dot_general(k_ref[pl.ds(off, KEY_TILE), hl(h)], qf[h], NT, preferred_element_type=f32)

    def body(p, carry):
        off = pl.multiple_of(p * KEY_TILE, KEY_TILE)
        return _softmax_steps_t([scores_t(h, off) for h in heads], carry, [vt_ref[0, p, hl(h), :] for h in heads])

    last = q0 // KEY_TILE
    carry = lax.fori_loop(0, last, body, _softmax_init_t(N_HEADS, MOBA_TQ))
    off = pl.multiple_of(last * KEY_TILE, KEY_TILE)
    kpos = off + lax.broadcasted_iota(i32, (KEY_TILE, MOBA_TQ), 0)
    qpos = q0 + lax.broadcasted_iota(i32, (KEY_TILE, MOBA_TQ), 1)
    carry = _softmax_steps_t([jnp.where(kpos <= qpos, scores_t(h, off), MASKED) for h in heads], carry,
                             [vt_ref[0, last, hl(h), :] for h in heads])
    for h in heads:
        o_ref[:, hl(h)] = jnp.transpose(carry[h][2] / carry[h][1]).astype(bf16)


def _blocks_t(v, B, T, blk):
    return v.reshape(B, T // blk, blk, v.shape[1]).transpose(0, 1, 3, 2)


def _moba(mq, mk, mv, km, B, T):
    N = B * T
    nt = T // MOBA_TQ
    wd = N_HEADS * LANES
    assert T % KEY_TILE == 0 and KEY_TILE % MOBA_TQ == 0 and T // MOBA_BLOCK <= MOBA_NBLK
    return pl.pallas_call(
        _moba_kernel, grid=(B, nt),
        in_specs=[pl.BlockSpec((MOBA_TQ, wd), lambda b, i: (b * nt + i, 0)),
                  pl.BlockSpec((T, wd), lambda b, i: (b, 0), pipeline_mode=pl.Buffered(1)),
                  pl.BlockSpec((1, T // KEY_TILE, wd, KEY_TILE), lambda b, i: (b, 0, 0, 0),
                               pipeline_mode=pl.Buffered(1)),
                  pl.BlockSpec((1, N_HEADS, MOBA_NBLK, LANES), lambda b, i: (b, 0, 0, 0))],
        out_specs=pl.BlockSpec((MOBA_TQ, wd), lambda b, i: (b * nt + i, 0)),
        out_shape=jax.ShapeDtypeStruct((N, wd), bf16),
        compiler_params=_cparams("arbitrary", "arbitrary"), name="moba")(mq, mk, _blocks_t(mv, B, T, KEY_TILE), km)


NSA_TQ = 512
NSA_KB = 256


def _stack_heads(ref):
    return jnp.concatenate([ref[:, h * LANES:(h + 1) * LANES] for h in range(N_HEADS)], axis=0)


def _nsa_select_kernel(nq_ref, kc_ref, vct_ref, ovt_ref, oc_ref, selb_ref, *, n_cmp):
    c = pl.program_id(1)
    rows = N_HEADS * NSA_TQ
    s = lax.dot_general(kc_ref[0], _stack_heads(nq_ref), NT, preferred_element_type=f32)
    n = lax.broadcasted_iota(i32, s.shape, 0)
    qpos_t = c * NSA_TQ + lax.broadcasted_iota(i32, (1, rows), 1) % NSA_TQ
    ok = (n * CMP_STRIDE + (CMP_BLOCK - 1) <= qpos_t) & (n < n_cmp)
    s = jnp.where(ok, s, -jnp.inf)
    m = jnp.max(s, axis=0, keepdims=True)
    m = jnp.where(m > -jnp.inf, m, 0.0)
    e = jnp.where(ok, jnp.exp2(s - m), 0.0)
    p_c = e / jnp.maximum(jnp.sum(e, axis=0, keepdims=True), 1e-30)
    o_c_t = jnp.dot(vct_ref[0], p_c.astype(bf16), preferred_element_type=f32)
    hq = lambda h: slice(h * NSA_TQ, (h + 1) * NSA_TQ)
    for h in range(N_HEADS):
        oc_ref[:, h * LANES:(h + 1) * LANES] = jnp.transpose(o_c_t[:, hq(h)]).astype(bf16)
    hi, lo = _split_bf16((p_c[:, hq(0)] + p_c[:, hq(1)]) + (p_c[:, hq(2)] + p_c[:, hq(3)]))
    imp = (jnp.dot(ovt_ref[...], hi, preferred_element_type=f32)
           + jnp.dot(ovt_ref[...], lo, preferred_element_type=f32))
    j = lax.broadcasted_iota(i32, imp.shape, 0)
    cur = (c * NSA_TQ + lax.broadcasted_iota(i32, (1, NSA_TQ), 1)) // SLC_BLOCK
    forced = (j == 0) | (j == cur) | (j == cur - 1)
    valid = j <= cur
    score = jnp.where(valid, jnp.where(forced, FORCE_SCORE, imp), -jnp.inf)
    chosen = (_argmax_rounds(score, j, SLC_TOPN, axis=0) > 0.0) & valid
    selb_ref[...] = jnp.transpose(jnp.where(chosen, 0.0, MASKED)).astype(bf16)


def _nsa_attend_kernel(nqr_ref, selb_ref, oc_ref, misc_ref, ks_ref, vst_ref, kw_ref, vwt_ref, o_ref):
    c = pl.program_id(1)
    rows = N_HEADS * NSA_TQ
    heads = range(N_HEADS)
    q4r = _stack_heads(nqr_ref)
    qpos_t = c * NSA_TQ + lax.broadcasted_iota(i32, (1, rows), 1) % NSA_TQ
    selb = selb_ref[...]
    lhs = [jnp.concatenate([q4r[h * NSA_TQ:(h + 1) * NSA_TQ], selb], axis=1) for h in heads]

    def scores_t(off):
        kb = ks_ref[pl.ds(off, KEY_TILE), :]
        return [lax.dot_general(kb, lhs[h], NT, preferred_element_type=f32) for h in heads]

    def body(p, carry):
        return _softmax_steps_t(scores_t(pl.multiple_of(p * KEY_TILE, KEY_TILE)), carry, [vst_ref[0, p]] * N_HEADS)

    last = (c * NSA_TQ) // KEY_TILE
    carry = lax.fori_loop(0, last, body, _softmax_init_t(N_HEADS, NSA_TQ))
    off = pl.multiple_of(last * KEY_TILE, KEY_TILE)
    kpos = off + lax.broadcasted_iota(i32, (KEY_TILE, NSA_TQ), 0)
    qpos = c * NSA_TQ + lax.broadcasted_iota(i32, (KEY_TILE, NSA_TQ), 1)
    carry = _softmax_steps_t([jnp.where(kpos <= qpos, s, MASKED) for s in scores_t(off)], carry,
                             [vst_ref[0, last]] * N_HEADS)
    o_s = jnp.concatenate([jnp.transpose(carry[h][2] / carry[h][1]) for h in heads], axis=0)

    nwb = (WIN + NSA_TQ) // NSA_KB
    sb = jnp.maximum(c * (NSA_TQ // NSA_KB) - WIN // NSA_KB, 0)
    start = pl.multiple_of(sb * NSA_KB, NSA_KB)
    s_w = lax.dot_general(kw_ref[pl.ds(start, nwb * NSA_KB), :], q4r, NT, preferred_element_type=f32)
    wpos = start + lax.broadcasted_iota(i32, s_w.shape, 0)
    s_w = jnp.where((wpos <= qpos_t) & (wpos > qpos_t - WIN), s_w, -jnp.inf)
    e_w = jnp.exp2(s_w - jnp.max(s_w, axis=0, keepdims=True))
    p_w = (e_w / jnp.sum(e_w, axis=0, keepdims=True)).astype(bf16)
    o_w_t = jnp.dot(vwt_ref[0, sb], p_w[0:NSA_KB], preferred_element_type=f32)
    for i in range(1, nwb):
        o_w_t = o_w_t + jnp.dot(vwt_ref[0, sb + i], p_w[i * NSA_KB:(i + 1) * NSA_KB], preferred_element_type=f32)
    o_w = jnp.transpose(o_w_t)

    gates = jax.nn.sigmoid(misc_ref[...])
    gl = lax.broadcasted_iota(i32, gates.shape, 1)

    def gate_col(g):
        return jnp.concatenate([jnp.sum(jnp.where(gl == MISC_GATE0 + 3 * h + g, gates, 0.0), axis=1, keepdims=True)
                                for h in range(N_HEADS)], axis=0)

    out = gate_col(0) * _stack_heads(oc_ref).astype(f32) + gate_col(1) * o_s + gate_col(2) * o_w
    pair = lambda a, b: a + pltpu.roll(b, HEAD_DIM, axis=1)
    o_ref[...] = jnp.concatenate([pair(out[0:NSA_TQ], out[NSA_TQ:2 * NSA_TQ]),
                                  pair(out[2 * NSA_TQ:3 * NSA_TQ], out[3 * NSA_TQ:])], axis=1).astype(bf16)


def _nsa(nq, nqr, misc, ks, vs, kw, vw, kc, vc, ov, B, T):
    N = B * T
    nc = T // NSA_TQ
    ns = kc.shape[1]
    assert T >= NSA_TQ + WIN and T % KEY_TILE == 0 and KEY_TILE % NSA_TQ == 0 and NSA_TQ % NSA_KB == 0
    assert WIN % NSA_KB == 0
    tok = lambda w_: pl.BlockSpec((NSA_TQ, w_), lambda b, c: (b * nc + c, 0))
    seq = lambda w_: pl.BlockSpec((T, w_), lambda b, c: (b, 0), pipeline_mode=pl.Buffered(1))
    o_c, selb = pl.pallas_call(
        functools.partial(_nsa_select_kernel, n_cmp=(T - CMP_BLOCK) // CMP_STRIDE + 1), grid=(B, nc),
        in_specs=[tok(512), pl.BlockSpec((1, ns, LANES), lambda b, c: (b, 0, 0)),
                  pl.BlockSpec((1, LANES, ns), lambda b, c: (b, 0, 0)), pl.BlockSpec((LANES, ns), lambda b, c: (0, 0))],
        out_specs=[tok(512), tok(LANES)],
        out_shape=[jax.ShapeDtypeStruct((N, 512), bf16), jax.ShapeDtypeStruct((N, LANES), bf16)],
        compiler_params=_cparams("arbitrary", "arbitrary"), name="nsa_select")(
            nq, kc, vc.transpose(0, 2, 1), ov.T)
    seq_t = lambda kb: pl.BlockSpec((1, T // kb, LANES, kb), lambda b, c: (b, 0, 0, 0), pipeline_mode=pl.Buffered(1))
    return pl.pallas_call(
        _nsa_attend_kernel, grid=(B, nc),
        in_specs=[tok(512), tok(LANES), tok(512), tok(LANES), seq(256), seq_t(KEY_TILE), seq(LANES), seq_t(NSA_KB)],
        out_specs=tok(256), out_shape=jax.ShapeDtypeStruct((N, 256), bf16),
        compiler_params=_cparams("arbitrary", "arbitrary"), name="nsa_attend")(
            nqr, selb, o_c, misc, ks, _blocks_t(vs, B, T, KEY_TILE), kw, _blocks_t(vw, B, T, NSA_KB))


POOL_TM = 512
POOL_HALO = 16


def _pool_kernel(u_ref, halo_ref, w_ref, sc_ref, o_ref):
    t = pl.program_id(1)
    halo = jnp.where(t == 0, 0.0, halo_ref[...])
    ext = jnp.concatenate([halo, u_ref[...]], axis=0)
    s2 = ext + pltpu.roll(ext, 1, axis=0)
    s4 = s2 + pltpu.roll(s2, 2, axis=0)
    s8 = s4 + pltpu.roll(s4, 4, axis=0)
    s16 = s8 + pltpu.roll(s8, 8, axis=0)
    pos1 = jnp.maximum(t * POOL_TM - POOL_HALO + 1 + lax.broadcasted_iota(i32, ext.shape, 0), 1).astype(f32)
    grp = lax.broadcasted_iota(i32, ext.shape, 1) // HEAD_DIM
    mean = jnp.where(grp == 0, s2 / jnp.minimum(pos1, 2.0),
                     jnp.where(grp == 1, s4 / jnp.minimum(pos1, 4.0),
                               jnp.where(grp == 2, s8 / jnp.minimum(pos1, 8.0), s16 / jnp.minimum(pos1, 16.0))))
    pooled = (mean - ext)[POOL_HALO:, :]
    o_ref[...] = (jnp.dot(pooled.astype(bf16), w_ref[...], preferred_element_type=f32) * sc_ref[...]).astype(bf16)


def _pool(u, wbd, scale, B, T):
    N = B * T
    tm = min(POOL_TM, T)
    assert tm == POOL_TM and T % POOL_TM == 0
    nt = T // tm
    hb = tm // POOL_HALO
    return pl.pallas_call(
        _pool_kernel, grid=(B, nt),
        in_specs=[pl.BlockSpec((tm, 256), lambda b, t: (b * nt + t, 0)),
                  pl.BlockSpec((POOL_HALO, 256), lambda b, t: (jnp.maximum((b * nt + t) * hb - 1, 0), 0)),
                  pl.BlockSpec((256, 256), lambda b, t: (0, 0)),
                  pl.BlockSpec((1, 256), lambda b, t: (0, 0))],
        out_specs=pl.BlockSpec((tm, 256), lambda b, t: (b * nt + t, 0)),
        out_shape=jax.ShapeDtypeStruct((N, 256), bf16),
        compiler_params=_cparams("arbitrary", "arbitrary"), name="pool")(u, u, wbd, scale)


GLA_TM = 256


def _gla_kernel(q_ref, k_ref, v_ref, g_ref, misc_ref, wa_ref, ba_ref, gn_ref, bd_ref, o_ref,
                st_ref, q_s, k_s, v_s, b_s, qe_s, ke_s, gam_s, o_s):
    nb = q_ref.shape[0]

    @pl.when(pl.program_id(0) == 0)
    def _():
        st_ref[...] = jnp.zeros_like(st_ref)

    r16 = lax.broadcasted_iota(i32, (GLA_TM, 256), 0) % GLA_SUB
    for bi_ in range(nb):
        x = jnp.dot(misc_ref[bi_], wa_ref[...], preferred_element_type=f32,
                    precision=lax.Precision.HIGHEST) + ba_ref[...]
        log_a = (jnp.minimum(x, 0.0) - jnp.log1p(jnp.exp(-jnp.abs(x)))) / GLA_TAU
        b = log_a
        for s in (1, 2, 4, 8):
            b = b + jnp.where(r16 >= s, pltpu.roll(b, s, axis=0), 0.0)
        b_end = jnp.where(r16 == GLA_SUB - 1, b, 0.0)
        for s in (1, 2, 4, 8):
            b_end = b_end + pltpu.roll(b_end, GLA_TM - s, axis=0)
        q = q_ref[bi_] * QK_SCALE
        k = k_ref[bi_]
        q_s[bi_] = q
        k_s[bi_] = k
        v_s[bi_] = v_ref[bi_]
        b_s[bi_] = b
        qe_s[bi_] = (q * jnp.exp(b)).astype(bf16)
        ke_s[bi_] = (k * jnp.exp(b_end - b)).astype(bf16)
        gam_s[bi_] = jnp.exp(b_end)
    bd = bd_ref[...]
    shape3 = (GLA_SUB, GLA_SUB, 256)
    causal = lax.broadcasted_iota(i32, shape3, 0) <= lax.broadcasted_iota(i32, shape3, 1)

    def block(n, _):
        r0 = pl.multiple_of(n * GLA_SUB, GLA_SUB)
        rows = pl.ds(r0, GLA_SUB)
        for bi_ in range(nb):
            qi, ki, vi, bi = q_s[bi_, rows, :], k_s[bi_, rows, :], v_s[bi_, rows, :], b_s[bi_, rows, :]
            diff = jnp.where(causal, bi[None, :, :] - bi[:, None, :], 0.0)
            w3 = jnp.where(causal, qi[None, :, :] * ki[:, None, :] * jnp.exp(diff), 0.0)
            a3 = jnp.dot(w3.reshape(GLA_SUB * GLA_SUB, 256).astype(bf16), bd, preferred_element_type=f32)
            intra = jnp.sum(a3.reshape(shape3) * vi[:, None, :], axis=0)
            st = st_ref[bi_]
            inter = lax.dot_general(qe_s[bi_, rows, :], st.astype(bf16), NT, preferred_element_type=f32)
            o_s[bi_, rows, :] = intra + inter
            upd = lax.dot_general(vi.astype(bf16), ke_s[bi_, rows, :], TN, preferred_element_type=f32)
            st_ref[bi_] = st * gam_s[bi_, pl.ds(r0, 1), :] + jnp.where(bd > 0, upd, 0.0)
        return 0

    lax.fori_loop(0, GLA_TM // GLA_SUB, block, 0)
    for bi_ in range(nb):
        o = o_s[bi_]
        ms = jnp.dot(o * o, bd.astype(f32), preferred_element_type=f32, precision=lax.Precision.HIGHEST) / HEAD_DIM
        o_ref[bi_] = (o * lax.rsqrt(ms + LN_EPS) * gn_ref[...] * _silu(g_ref[bi_])).astype(bf16)


def _gla(gq, gk, gv, gg, misc, wa, ba, gn, bd, B, T):
    N = B * T
    assert T % GLA_TM == 0
    tok = lambda w_: pl.BlockSpec((B, GLA_TM, w_), lambda t: (0, t, 0))
    full = lambda a: pl.BlockSpec(a.shape, lambda t: (0, 0))
    v = lambda dt: pltpu.VMEM((B, GLA_TM, 256), dt)
    seq = lambda a: a.reshape(B, T, a.shape[1])
    out = pl.pallas_call(
        _gla_kernel, grid=(T // GLA_TM,),
        in_specs=[tok(256), tok(256), tok(256), tok(256), tok(128), full(wa), full(ba), full(gn), full(bd)],
        out_specs=tok(256), out_shape=jax.ShapeDtypeStruct((B, T, 256), bf16),
        scratch_shapes=[pltpu.VMEM((B, 256, 256), f32), v(f32), v(f32), v(f32), v(f32), v(bf16), v(bf16), v(f32),
                        v(f32)],
        compiler_params=_cparams("arbitrary"), name="gla")(seq(gq), seq(gk), seq(gv), seq(gg), seq(misc), wa, ba, gn, bd)
    return out.reshape(N, 256)


OUT_TM = 256


def _outproj_kernel(mo_ref, no_ref, po_ref, go_ref, x_ref, mod_ref, w_ref, lng_ref, lnb_ref, wrh_ref, wrl_ref,
                    x1_ref, h2b_ref, lg_ref, *, alpha):
    a = jnp.concatenate([mo_ref[...], no_ref[...], po_ref[...], go_ref[...]], axis=1)
    y = jnp.dot(a, w_ref[...], preferred_element_type=f32)
    x1 = _layer_norm(alpha * x_ref[...] + mod_ref[0, 2:3, :] * y, lng_ref[...], lnb_ref[...])
    x1_ref[...] = x1
    h2 = x1 * (1.0 + mod_ref[0, 4:5, :]) + mod_ref[0, 3:4, :]
    h2b_ref[...] = h2.astype(bf16)
    hi, lo = _split_bf16(h2)
    wrh = wrh_ref[...]
    lg_ref[...] = (jnp.dot(hi, wrh, preferred_element_type=f32) + jnp.dot(lo, wrh, preferred_element_type=f32)
                   + jnp.dot(hi, wrl_ref[...], preferred_element_type=f32))


def _outproj(mo, no, po, go, x2, mod, w, lng, lnb, wrh, wrl, alpha, B, T):
    N = B * T
    nt = T // OUT_TM
    tok = lambda w_: pl.BlockSpec((OUT_TM, w_), lambda i: (i, 0))
    full = lambda a: pl.BlockSpec(a.shape, lambda i: (0,) * a.ndim)
    return pl.pallas_call(
        functools.partial(_outproj_kernel, alpha=alpha), grid=(N // OUT_TM,),
        in_specs=[tok(512), tok(256), tok(256), tok(256), tok(D_MODEL),
                  pl.BlockSpec((1, 6, D_MODEL), lambda i: (i // nt, 0, 0)),
                  full(w), full(lng), full(lnb), full(wrh), full(wrl)],
        out_specs=[tok(D_MODEL), tok(D_MODEL), tok(N_EXPERTS)],
        out_shape=[jax.ShapeDtypeStruct((N, D_MODEL), f32), jax.ShapeDtypeStruct((N, D_MODEL), bf16),
                   jax.ShapeDtypeStruct((N, N_EXPERTS), f32)],
        compiler_params=_cparams("arbitrary"), name="outproj")(mo, no, po, go, x2, mod, w, lng, lnb, wrh, wrl)


ROUTE_TM = 256
GROUP_SIZE = N_EXPERTS // N_EXPERT_GROUPS


def _per_token(rows):
    pad = jnp.zeros((LANES - len(rows), rows[0].shape[1]), f32)
    return jnp.transpose(jnp.concatenate(rows + [pad], axis=0))


def _route_kernel(lg_ref, rb_ref, ei_ref, wt_ref, cnt_ref, tb_ref, base_ref):
    @pl.when(pl.program_id(0) == 0)
    def _():
        base_ref[...] = jnp.zeros_like(base_ref)

    tb_ref[0] = base_ref[...].astype(i32)

    s = jax.nn.sigmoid(jnp.transpose(lg_ref[...]))
    ssel = s + rb_ref[...]
    shape3 = (N_EXPERT_GROUPS, GROUP_SIZE, ROUTE_TM)
    x3 = ssel.reshape(shape3)
    i3 = lax.broadcasted_iota(i32, shape3, 1)
    m1 = jnp.max(x3, axis=1, keepdims=True)
    first = jnp.min(jnp.where(x3 == m1, i3, BIG_IDX), axis=1, keepdims=True)
    m2 = jnp.max(jnp.where(i3 == first, -jnp.inf, x3), axis=1, keepdims=True)
    gscore = (m1 + m2).reshape(N_EXPERT_GROUPS, ROUTE_TM)
    gid = lax.broadcasted_iota(i32, gscore.shape, 0)
    beaten = jnp.zeros(gscore.shape, i32)
    for g in range(N_EXPERT_GROUPS):
        other = gscore[g:g + 1, :]
        beaten = beaten + jnp.where((other > gscore) | ((other == gscore) & (g < gid)), 1, 0)
    keep = jnp.broadcast_to((beaten < TOPK_GROUPS)[:, None, :], shape3).reshape(ssel.shape)
    x = jnp.where(keep, ssel, -jnp.inf)
    eid = lax.broadcasted_iota(i32, x.shape, 0)
    hits, idx_rows, w_rows = [], [], []
    for _ in range(TOP_K):
        mx = jnp.max(x, axis=0, keepdims=True)
        idx = jnp.min(jnp.where(x == mx, eid, BIG_IDX), axis=0, keepdims=True)
        hit = eid == idx
        hits.append(hit)
        idx_rows.append(idx.astype(f32))
        w_rows.append(jnp.sum(jnp.where(hit, s, 0.0), axis=0, keepdims=True))
        x = jnp.where(hit, -jnp.inf, x)
    wsum = w_rows[0]
    for w in w_rows[1:]:
        wsum = wsum + w
    chosen = jnp.zeros(s.shape, f32)
    for hit in hits:
        chosen = jnp.where(hit, 1.0, chosen)
    total = base_ref[...] + jnp.sum(chosen, axis=1, keepdims=True)
    base_ref[...] = total
    cnt_ref[...] = total.astype(i32)
    ei_ref[...] = _per_token(idx_rows).astype(i32)
    wt_ref[...] = _per_token([w / wsum * ROUTED_SCALE for w in w_rows])


def _route(logits, rb):
    N = logits.shape[0]
    tok = pl.BlockSpec((ROUTE_TM, LANES), lambda i: (i, 0))
    col = pl.BlockSpec((N_EXPERTS, 1), lambda i: (0, 0))
    return pl.pallas_call(
        _route_kernel, grid=(N // ROUTE_TM,),
        in_specs=[pl.BlockSpec((ROUTE_TM, N_EXPERTS), lambda i: (i, 0)), col],
        out_specs=[tok, tok, col, pl.BlockSpec((1, N_EXPERTS, 1), lambda i: (i, 0, 0))],
        out_shape=[jax.ShapeDtypeStruct((N, LANES), i32), jax.ShapeDtypeStruct((N, LANES), f32),
                   jax.ShapeDtypeStruct((N_EXPERTS, 1), i32), jax.ShapeDtypeStruct((N // ROUTE_TM, N_EXPERTS, 1), i32)],
        scratch_shapes=[pltpu.VMEM((N_EXPERTS, 1), f32)],
        compiler_params=_cparams("arbitrary"), name="route")(logits, rb.reshape(N_EXPERTS, 1))


DISP_TM = ROUTE_TM
DISP_SLOTS = DISP_TM * TOP_K
DISP_CHUNK = 8
DISP_PERM_ROWS = 512
DISP_MAX_CHUNKS = DISP_SLOTS // DISP_CHUNK + N_EXPERTS


def _dispatch_kernel(x_ref, ei_ref, tb_ref, ps_ref, triu_ref, tril_ref, xs_in, pos_ref, xs_hbm,
                     pbuf, tab_v, tab_s, csem, rsem):
    del xs_in
    i = pl.program_id(0)
    slot = i % 2

    @pl.when(i == 0)
    def _():
        pbuf[:, DISP_SLOTS * ROW_TILE:, :] = jnp.zeros((2, DISP_CHUNK * ROW_TILE, LANES), u32)

    ei_t = jnp.transpose(ei_ref[...].astype(f32))
    eid = lax.broadcasted_iota(i32, (N_EXPERTS, DISP_TM), 0).astype(f32)
    hits = [eid == ei_t[k:k + 1, :] for k in range(TOP_K)]
    member = jnp.zeros(eid.shape, f32)
    for hit in hits:
        member = jnp.where(hit, 1.0, member)
    prefix = jnp.dot(member.astype(bf16), triu_ref[...], preferred_element_type=f32)
    cnt = jnp.sum(member, axis=1, keepdims=True)
    nch = jnp.floor((cnt + (DISP_CHUNK - 1)) * (1.0 / DISP_CHUNK))
    lane = lax.broadcasted_iota(i32, (N_EXPERTS, LANES), 1)
    before = jnp.dot(tril_ref[...], jnp.where(lane == 0, cnt, jnp.where(lane == 1, nch, 0.0)).astype(bf16),
                     preferred_element_type=f32)
    off, cidx = before[:, 0:1], before[:, 1:2]
    dst = (ps_ref[...] + tb_ref[0]).astype(f32)
    slot_rows = [jnp.sum(jnp.where(hit, off + prefix, 0.0), axis=0, keepdims=True) for hit in hits]
    pos_ref[...] = _per_token([jnp.sum(jnp.where(hit, dst + prefix, 0.0), axis=0, keepdims=True)
                               for hit in hits]).astype(i32)
    rel = lax.broadcasted_iota(i32, (N_EXPERTS, DISP_MAX_CHUNKS), 1).astype(f32) - cidx
    mine = (rel >= 0.0) & (rel < nch)
    src_row = jnp.sum(jnp.where(mine, off + DISP_CHUNK * rel, 0.0), axis=0, keepdims=True)
    dst_row = jnp.sum(jnp.where(mine, dst + DISP_CHUNK * rel, 0.0), axis=0, keepdims=True)
    n_row = jnp.broadcast_to(jnp.sum(nch, axis=0, keepdims=True), src_row.shape)
    tab_v[...] = jnp.concatenate([src_row, dst_row, n_row, jnp.zeros((5, DISP_MAX_CHUNKS), f32)], axis=0).astype(i32)
    cp = pltpu.make_async_copy(tab_v, tab_s.at[slot], csem)
    cp.start()

    x = x_ref[...]
    for r in range(DISP_SLOTS // DISP_PERM_ROWS):
        sid = (r * DISP_PERM_ROWS + lax.broadcasted_iota(i32, (DISP_PERM_ROWS, DISP_TM), 0)).astype(f32)
        perm = jnp.zeros(sid.shape, f32)
        for srow in slot_rows:
            perm = jnp.where(sid == srow, 1.0, perm)
        rows = jnp.dot(perm.astype(bf16), x, preferred_element_type=f32)
        for s, w in enumerate(_pack_rows(rows)):
            pbuf[slot, pl.ds(r * DISP_PERM_ROWS * ROW_TILE + s, DISP_PERM_ROWS, stride=ROW_TILE), :] = w
    cp.wait()

    def chunk_copy(sl, src, dst_):
        span = lambda r0: pl.ds(pl.multiple_of(r0 * ROW_TILE, ROW_TILE), DISP_CHUNK * ROW_TILE)
        return pltpu.make_async_copy(pbuf.at[sl, span(src)], xs_hbm.at[span(dst_)], rsem.at[sl])

    def start_all(sl):
        n = tab_s[sl, 2, 0]

        def body(jj, _):
            for q in range(2):
                j = 2 * jj + q

                @pl.when(j < n)
                def _():
                    chunk_copy(sl, tab_s[sl, 0, j], tab_s[sl, 1, j]).start(priority=q)
            return 0
        lax.fori_loop(0, (n + 1) // 2, body, 0)

    def wait_all(sl):
        def body(j, _):
            chunk_copy(sl, 0, 0).wait()
            return 0
        lax.fori_loop(0, tab_s[sl, 2, 0], body, 0)

    @pl.when(i > 0)
    def _():
        wait_all(1 - slot)
    start_all(slot)

    @pl.when(i == pl.num_programs(0) - 1)
    def _():
        wait_all(slot)


def _dispatch(h2b, eidx, tile_base, pstarts, n_rows):
    N = h2b.shape[0]
    r_, c_ = jnp.arange(DISP_TM)[:, None], jnp.arange(DISP_TM)[None, :]
    triu = (r_ < c_).astype(bf16)
    e_, f_ = jnp.arange(N_EXPERTS)[:, None], jnp.arange(N_EXPERTS)[None, :]
    tril = (f_ < e_).astype(bf16)
    tok = pl.BlockSpec((DISP_TM, LANES), lambda i: (i, 0))
    full = lambda a: pl.BlockSpec(a.shape, lambda i: (0,) * a.ndim)
    xs0 = jnp.zeros((n_rows * ROW_TILE, LANES), u32)
    pos, xs = pl.pallas_call(
        _dispatch_kernel, grid=(N // DISP_TM,),
        in_specs=[pl.BlockSpec((DISP_TM, D_MODEL), lambda i: (i, 0)), tok,
                  pl.BlockSpec((1, N_EXPERTS, 1), lambda i: (i, 0, 0)), full(pstarts), full(triu), full(tril),
                  pl.BlockSpec(memory_space=pl.ANY)],
        out_specs=[tok, pl.BlockSpec(memory_space=pl.ANY)],
        out_shape=[jax.ShapeDtypeStruct((N, LANES), i32), jax.ShapeDtypeStruct(xs0.shape, u32)],
        scratch_shapes=[pltpu.VMEM((2, (DISP_SLOTS + DISP_CHUNK) * ROW_TILE, LANES), u32),
                        pltpu.VMEM((8, DISP_MAX_CHUNKS), i32), pltpu.SMEM((2, 8, DISP_MAX_CHUNKS), i32),
                        pltpu.SemaphoreType.DMA, pltpu.SemaphoreType.DMA((2,))],
        input_output_aliases={6: 1},
        compiler_params=_cparams("arbitrary"), name="moe_dispatch")(h2b, eidx, tile_base, pstarts, triu, tril, xs0)
    return pos, xs


MOE_BLK = MOE_ROWS * ROW_TILE
MOE_IN_BUFS = 4
MOE_OUT_BUFS = 3


def _moe_kernel(b0_ref, nb_ref, nu_ref, xs_in, wg_ref, wu_ref, wd_ref, xs_out, xbuf, ybuf, isem, osem):
    e = pl.program_id(0)
    nu = nu_ref[0]

    def in_copy(b):
        return pltpu.make_async_copy(xs_in.at[pl.ds(pl.multiple_of(b * MOE_BLK, MOE_BLK), MOE_BLK)],
                                     xbuf.at[b % MOE_IN_BUFS], isem.at[b % MOE_IN_BUFS])

    def out_copy(b):
        return pltpu.make_async_copy(ybuf.at[b % MOE_OUT_BUFS],
                                     xs_out.at[pl.ds(pl.multiple_of(b * MOE_BLK, MOE_BLK), MOE_BLK)],
                                     osem.at[b % MOE_OUT_BUFS])

    @pl.when(e == 0)
    def _():
        for j in range(MOE_IN_BUFS - 1):
            @pl.when(j < nu)
            def _():
                in_copy(j).start()

    wg = wg_ref[0, 0].astype(bf16)
    wu = wu_ref[0, 0].astype(bf16)
    wd = wd_ref[0, 0].astype(bf16)

    def block(i, _):
        b = b0_ref[e] + i
        in_copy(b).wait()

        @pl.when(b + MOE_IN_BUFS - 1 < nu)
        def _():
            in_copy(b + MOE_IN_BUFS - 1).start()

        @pl.when(b >= MOE_OUT_BUFS)
        def _():
            out_copy(b - MOE_OUT_BUFS).wait()

        g = jnp.zeros((MOE_ROWS, D_EXPERT), f32)
        u = jnp.zeros((MOE_ROWS, D_EXPERT), f32)
        for s in range(ROW_TILE):
            word = xbuf[b % MOE_IN_BUFS, pl.ds(s, MOE_ROWS, stride=ROW_TILE), :]
            xc = jnp.concatenate(_unpack_word(word), axis=1).astype(bf16)
            rows = slice(2 * s * LANES, (2 * s + 2) * LANES)
            g = g + jnp.dot(xc, wg[rows], preferred_element_type=f32)
            u = u + jnp.dot(xc, wu[rows], preferred_element_type=f32)
        y = jnp.dot((_silu(g) * u).astype(bf16), wd, preferred_element_type=f32)
        for s, w in enumerate(_pack_rows(y)):
            ybuf[b % MOE_OUT_BUFS, pl.ds(s, MOE_ROWS, stride=ROW_TILE), :] = w
        out_copy(b).start()
        return 0

    lax.fori_loop(0, nb_ref[e], block, 0)

    @pl.when(e == pl.num_programs(0) - 1)
    def _():
        for j in range(MOE_OUT_BUFS, 0, -1):
            @pl.when(nu >= j)
            def _():
                out_copy(nu - j).wait()


def _moe(layer, blk0, nblk, n_used, xs, wg, wu, wd):
    wspec = lambda a: pl.BlockSpec((1, 1) + a.shape[2:], lambda e, b0, nb, nu: (layer, e, 0, 0))
    gs = pltpu.PrefetchScalarGridSpec(
        num_scalar_prefetch=3, grid=(N_EXPERTS,),
        in_specs=[pl.BlockSpec(memory_space=pl.ANY), wspec(wg), wspec(wu), wspec(wd)],
        out_specs=pl.BlockSpec(memory_space=pl.ANY),
        scratch_shapes=[pltpu.VMEM((MOE_IN_BUFS, MOE_BLK, LANES), u32), pltpu.VMEM((MOE_OUT_BUFS, MOE_BLK, LANES), u32),
                        pltpu.SemaphoreType.DMA((MOE_IN_BUFS,)), pltpu.SemaphoreType.DMA((MOE_OUT_BUFS,))])
    return pl.pallas_call(
        _moe_kernel, grid_spec=gs, out_shape=jax.ShapeDtypeStruct(xs.shape, u32),
        input_output_aliases={3: 0},
        compiler_params=_cparams("arbitrary"), name="moe_experts")(blk0, nblk, n_used, xs, wg, wu, wd)


FIN_TM = 128


def _tile_gather(pos_hbm, ys_hbm, pos_s, gbuf, isem, rsem, step, n_steps):
    def idx_copy(s):
        return pltpu.make_async_copy(pos_hbm.at[pl.ds(s * FIN_TM, FIN_TM)], pos_s.at[s % 2], isem.at[s % 2])

    def tile(i):
        return pl.ds(pl.multiple_of(i * ROW_TILE, ROW_TILE), ROW_TILE)

    def row_copy(slot, t, k, p):
        return pltpu.make_async_copy(ys_hbm.at[tile(p)], gbuf.at[slot, k, tile(t)], rsem.at[slot])

    def start_rows(s):
        slot = s % 2

        def body(t, _):
            for k in range(TOP_K):
                row_copy(slot, t, k, pos_s[slot, t, k]).start(priority=k % 2)
            return 0
        lax.fori_loop(0, FIN_TM, body, 0)

    @pl.when(step == 0)
    def _():
        idx_copy(step).start()
        idx_copy(step).wait()
        start_rows(step)

        @pl.when(n_steps > 1)
        def _():
            idx_copy(step + 1).start()

    @pl.when(step + 1 < n_steps)
    def _():
        idx_copy(step + 1).wait()
        start_rows(step + 1)

        @pl.when(step + 2 < n_steps)
        def _():
            idx_copy(step + 2).start()

    slot = step % 2

    def wait_body(t, _):
        for k in range(TOP_K):
            row_copy(slot, t, k, 0).wait()
        return 0
    lax.fori_loop(0, FIN_TM, wait_body, 0)


def _fin_kernel(pos_hbm, ys_hbm, h2_ref, x1_ref, wt_ref, mod_ref, sg_ref, su_ref, sd_ref, lng_ref, lnb_ref, o_ref,
                pos_s, gbuf, isem, rsem, *, alpha):
    i = pl.program_id(0)
    _tile_gather(pos_hbm, ys_hbm, pos_s, gbuf, isem, rsem, i, pl.num_programs(0))
    hb = h2_ref[...]
    g = jnp.dot(hb, sg_ref[...], preferred_element_type=f32)
    u = jnp.dot(hb, su_ref[...], preferred_element_type=f32)
    shared = jnp.dot((_silu(g) * u).astype(bf16), sd_ref[...], preferred_element_type=f32)
    wt = wt_ref[...]
    lane = lax.broadcasted_iota(i32, wt.shape, 1)
    wk = [jnp.sum(jnp.where(lane == k, wt, 0.0), axis=1, keepdims=True) for k in range(TOP_K)]
    slot = i % 2
    cols = []
    for s in range(ROW_TILE):
        sub = pl.ds(s, FIN_TM, stride=ROW_TILE)
        lo, hi = _unpack_word(gbuf[slot, 0, sub, :])
        acc_lo, acc_hi = lo * wk[0], hi * wk[0]
        for k in range(1, TOP_K):
            lo, hi = _unpack_word(gbuf[slot, k, sub, :])
            acc_lo, acc_hi = acc_lo + lo * wk[k], acc_hi + hi * wk[k]
        cols += [acc_lo, acc_hi]
    y = shared + jnp.concatenate(cols, axis=1)
    o_ref[...] = _layer_norm(alpha * x1_ref[...] + mod_ref[0, 5:6, :] * y, lng_ref[...], lnb_ref[...])


def _fin(pos, ys, h2b, x1, wts, mod, sg, su, sd, lng, lnb, alpha, B, T):
    N = B * T
    nt = T // FIN_TM
    tok = lambda w_: pl.BlockSpec((FIN_TM, w_), lambda i: (i, 0))
    full = lambda a: pl.BlockSpec(a.shape, lambda i: (0,) * a.ndim)
    return pl.pallas_call(
        functools.partial(_fin_kernel, alpha=alpha), grid=(N // FIN_TM,),
        in_specs=[pl.BlockSpec(memory_space=pl.ANY), pl.BlockSpec(memory_space=pl.ANY),
                  tok(D_MODEL), tok(D_MODEL), tok(LANES),
                  pl.BlockSpec((1, 6, D_MODEL), lambda i: (i // nt, 0, 0)),
                  full(sg), full(su), full(sd), full(lng), full(lnb)],
        out_specs=tok(D_MODEL), out_shape=jax.ShapeDtypeStruct((N, D_MODEL), f32),
        scratch_shapes=[pltpu.SMEM((2, FIN_TM, LANES), i32), pltpu.VMEM((2, TOP_K, FIN_TM * ROW_TILE, LANES), u32),
                        pltpu.SemaphoreType.DMA((2,)), pltpu.SemaphoreType.DMA((2,))],
        compiler_params=_cparams("arbitrary"), name="combine")(pos, ys, h2b, x1, wts, mod, sg, su, sd, lng, lnb)


def _pack_w_in(w):
    c = lambda name, width: w[:, _OFF[name]:_OFF[name] + width]
    cols = [c("moba_q", 256), c("moba_k", 256), c("moba_v", 256), c("nsa_q", 256),
            c("k_slc", 64), c("k_win", 64), c("v_slc", 64), c("v_win", 64), c("k_cmp", 64), c("v_cmp", 64),
            c("pool", 256), c("gla_q", 256), c("gla_k", 256), c("gla_v", 256), c("gla_g", 256),
            c("nsa_gate", 12), c("gla_a", 16), jnp.zeros((w.shape[0], LANES - 28), w.dtype)]
    return jnp.concatenate(cols, axis=1).astype(bf16)


def _rope_tables(T):
    half = ROPE_DIM // 2
    inv_freq = ROPE_THETA ** (-jnp.arange(half, dtype=f32) / half)
    ang = jnp.arange(T).astype(f32)[:, None] * inv_freq[None, :]
    cos, sin = jnp.cos(ang), jnp.sin(ang)
    one = jnp.ones((T, HEAD_DIM - ROPE_DIM), f32)
    zero = jnp.zeros((T, HEAD_DIM - ROPE_DIM), f32)
    ct = jnp.concatenate([cos, cos, one], axis=1)
    st = jnp.concatenate([-sin, sin, zero], axis=1)
    return jnp.tile(ct, (1, 2)), jnp.tile(st, (1, 2))


def _cmp_weights(pe, w1, w2):
    half = CMP_BLOCK // 2
    z = jnp.zeros((half, HEAD_DIM, CMP_HIDDEN), f32)

    def arrange(lo):
        wk = w1[0].reshape(CMP_BLOCK, HEAD_DIM, CMP_HIDDEN)[lo:lo + half]
        wv = w1[1].reshape(CMP_BLOCK, HEAD_DIM, CMP_HIDDEN)[lo:lo + half]
        top = jnp.concatenate([wk, z], axis=2)
        bot = jnp.concatenate([z, wv], axis=2)
        return jnp.concatenate([top, bot], axis=1).reshape(half * 2 * HEAD_DIM, 2 * CMP_HIDDEN).astype(bf16)

    def pe_row(lo):
        return jnp.concatenate([pe[0, lo:lo + half], pe[1, lo:lo + half]], axis=1).reshape(1, half * 2 * HEAD_DIM)

    zc = jnp.zeros((CMP_HIDDEN, LANES - HEAD_DIM), f32)
    zr = jnp.zeros((CMP_HIDDEN, LANES), f32)
    w2k = jnp.concatenate([jnp.concatenate([w2[0], zc], axis=1), zr], axis=0).astype(bf16)
    w2v = jnp.concatenate([zr, jnp.concatenate([w2[1], zc], axis=1)], axis=0).astype(bf16)
    return arrange(0), arrange(half), pe_row(0), pe_row(half), w2k, w2v


def _overlap_matrix(ns):
    n = jnp.arange(ns)[:, None] * CMP_STRIDE
    j = jnp.arange(LANES)[None, :] * SLC_BLOCK
    ov = (n < j + SLC_BLOCK) & (n + CMP_BLOCK > j) & (jnp.arange(ns)[:, None] < ns - 1)
    return ov.astype(bf16)


def _block_diag_ones():
    h = jnp.arange(256) // HEAD_DIM
    return (h[:, None] == h[None, :]).astype(bf16)


def _expert_layout(counts, n_tok):
    counts = counts.reshape(N_EXPERTS)
    slack = DISP_CHUNK - 1
    padded = jnp.where(counts > 0, (counts + slack + MOE_ROWS - 1) // MOE_ROWS * MOE_ROWS, 0)
    pstarts = (jnp.cumsum(padded) - padded).astype(i32)
    n_blocks = n_tok * TOP_K // MOE_ROWS + N_EXPERTS + (N_EXPERTS * slack) // MOE_ROWS + 1
    n_used = (jnp.sum(padded) // MOE_ROWS).astype(i32).reshape(1)
    return (pstarts.reshape(N_EXPERTS, 1), pstarts // MOE_ROWS, (padded // MOE_ROWS).astype(i32), n_used,
            n_blocks * MOE_ROWS)


def _mixer_inputs(x2, mod, w_in, B, T):
    ct, st = _rope_tables(T)
    return _inproj(x2, mod, _pack_w_in(w_in), ct, st, B, T)


def _token_mixers(x2, mod, w_in, cmp_pe, cmp_w1, cmp_w2, pool_w, pool_scale, gla_wa, gla_ba, gla_norm, B, T):
    N = B * T
    (mq, mk, mv, km, nq, nqr, ks, kw, vs, vw, kvc, pool_u, gq, gk, gv, gg, misc) = _mixer_inputs(x2, mod, w_in, B, T)
    nt = T // MOBA_BLOCK
    kmh = km.reshape(B, nt, N_HEADS, HEAD_DIM).transpose(0, 2, 1, 3)
    mo = _moba(mq, mk, mv, jnp.pad(kmh, ((0, 0), (0, 0), (0, MOBA_NBLK - nt), (0, LANES - HEAD_DIM))), B, T)
    ns = T // CMP_STRIDE
    kc, vc = _cmp(kvc.reshape(B, ns, CMP_STRIDE * LANES), *_cmp_weights(cmp_pe, cmp_w1, cmp_w2))
    no = _nsa(nq, nqr, misc, ks, vs, kw, vw, kc, vc, _overlap_matrix(ns), B, T)
    wbd = jax.scipy.linalg.block_diag(*[pool_w[g] for g in range(len(POOL_WINDOWS))]).astype(bf16)
    po = _pool(pool_u, wbd, pool_scale.reshape(1, 256), B, T)
    wa = jnp.zeros((LANES, 256), f32).at[MISC_A0:MISC_A0 + GLA_LOWRANK].set(gla_wa)
    go = _gla(gq, gk, gv, gg, misc, wa, gla_ba.reshape(1, 256), jnp.tile(gla_norm, N_HEADS).reshape(1, 256),
              _block_diag_ones(), B, T)
    return mo, no, po, go


def _pad_w_out(w_out):
    wm = w_out[:GROUP_WIDTH].reshape(N_HEADS, HEAD_DIM, D_MODEL)
    wm = jnp.pad(wm, ((0, 0), (0, LANES - HEAD_DIM), (0, 0))).reshape(N_HEADS * LANES, D_MODEL)
    return jnp.concatenate([wm, w_out[GROUP_WIDTH:]], axis=0).astype(bf16)


def kernel(x, c, w_ada, b_ada, w_in, cmp_pe, cmp_w1, cmp_w2, pool_w, pool_scale, gla_wa, gla_ba, gla_norm, w_out,
           ln_g, ln_b, w_router, router_bias, exp_gate, exp_up, exp_down, sh_gate, sh_up, sh_down):
    B, T, D = x.shape
    N = B * T
    depth = w_ada.shape[0]
    alpha = float((2 * depth) ** 0.25)
    x2 = x.reshape(N, D)
    c8 = jnp.zeros((8, D), f32).at[:B].set(c)
    for l in range(depth):
        mod = _ada(c8, w_ada[l], b_ada[l].reshape(1, -1))[:B].reshape(B, 6, D)
        mo, no, po, go = _token_mixers(x2, mod, w_in[l], cmp_pe[l], cmp_w1[l], cmp_w2[l], pool_w[l], pool_scale[l],
                                       gla_wa[l], gla_ba[l], gla_norm[l], B, T)
        wrh, wrl = _split_bf16(w_router[l])
        x1, h2b, logits = _outproj(mo, no, po, go, x2, mod, _pad_w_out(w_out[l]), ln_g[l, 0].reshape(1, D),
                                   ln_b[l, 0].reshape(1, D), wrh, wrl, alpha, B, T)
        eidx, wts, counts, tile_base = _route(logits, router_bias[l].reshape(1, N_EXPERTS))
        pstarts, blk0, nblk, n_used, n_rows = _expert_layout(counts, N)
        pos, xs = _dispatch(h2b, eidx, tile_base, pstarts, n_rows)
        ys = _moe(l, blk0, nblk, n_used, xs, exp_gate, exp_up, exp_down)
        x2 = _fin(pos, ys, h2b, x1, wts, mod, sh_gate[l].astype(bf16), sh_up[l].astype(bf16),
                  sh_down[l].astype(bf16), ln_g[l, 1].reshape(1, D), ln_b[l, 1].reshape(1, D), alpha, B, T)
    return x2.reshape(B, T, D)
```

```python
import functools

import jax
import jax.numpy as jnp
from jax import lax
from jax.experimental import pallas as pl
from jax.experimental.pallas import tpu as pltpu

f32, bf16, i32, u32 = jnp.float32, jnp.bfloat16, jnp.int32, jnp.uint32

D_MODEL = 1024
HEAD_DIM = 64
N_HEADS = 4
GROUP_WIDTH = 256
ROPE_THETA = 500000.0
ROPE_DIM = 16
MOBA_BLOCK = 256
MOBA_TOPK = 3
CMP_BLOCK = 32
CMP_STRIDE = 16
CMP_HIDDEN = 128
SLC_BLOCK = 64
SLC_TOPN = 16
WIN = 512
FORCE_SCORE = 1e9
POOL_WINDOWS = (2, 4, 8, 16)
GLA_SUB = 16
GLA_LOWRANK = 16
GLA_TAU = 16.0
N_EXPERTS = 256
TOP_K = 8
N_EXPERT_GROUPS = 8
TOPK_GROUPS = 4
D_EXPERT = 256
ROUTED_SCALE = 2.5
LN_EPS = 1e-5
QK_SCALE = HEAD_DIM ** -0.5
LOG2E = 1.4426950408889634
Q_SCALE = QK_SCALE * LOG2E

LANES = 128
MASKED = -1e30
M_INIT = -3e38
BIG_IDX = 1 << 20
MOE_ROWS = 256
ROW_TILE = D_MODEL // (2 * LANES)
VMEM_LIMIT = 56 * 1024 * 1024

_OFF = dict(moba_q=0, moba_k=256, moba_v=512, nsa_q=768, k_cmp=1024, v_cmp=1088, k_slc=1152, v_slc=1216,
            k_win=1280, v_win=1344, nsa_gate=1408, pool=1420, gla_q=1676, gla_k=1932, gla_v=2188,
            gla_a=2444, gla_g=2460)
_S = dict(mq=0, mk=256, mv=512, nq=768, sw=1024, vsw=1152, kvc=1280, pool=1408, gq=1664, gk=1920, gv=2176,
          gg=2432, misc=2688)
IN_COLS_PACKED = 2816
MISC_GATE0 = 0
MISC_A0 = 12

NT = (((1,), (1,)), ((), ()))
TN = (((0,), (0,)), ((), ()))


def _cparams(*sem):
    return pltpu.CompilerParams(dimension_semantics=sem, vmem_limit_bytes=VMEM_LIMIT)


def _silu(x):
    return x * jax.nn.sigmoid(x)


def _split_bf16(x):
    hi = x.astype(bf16)
    lo = (x - hi.astype(f32)).astype(bf16)
    return hi, lo


def _pack_rows(x):
    bits = lambda v: lax.bitcast_convert_type(v.astype(bf16).astype(f32), u32)
    return [(bits(x[:, (2 * s) * LANES:(2 * s + 1) * LANES]) >> 16) | bits(x[:, (2 * s + 1) * LANES:(2 * s + 2) * LANES])
            for s in range(ROW_TILE)]


def _unpack_word(w):
    return (lax.bitcast_convert_type(w << 16, f32), lax.bitcast_convert_type(w & jnp.uint32(0xFFFF0000), f32))


def _layer_norm(z, g, b):
    mu = jnp.mean(z, axis=-1, keepdims=True)
    zc = z - mu
    var = jnp.mean(zc * zc, axis=-1, keepdims=True)
    return zc * lax.rsqrt(var + LN_EPS) * g + b


def _argmax_rounds(score, index, rounds, axis=1):
    picked = jnp.zeros(score.shape, f32)
    for _ in range(rounds):
        mx = jnp.max(score, axis=axis, keepdims=True)
        first = jnp.min(jnp.where(score == mx, index, BIG_IDX), axis=axis, keepdims=True)
        hit = index == first
        picked = jnp.where(hit, 1.0, picked)
        score = jnp.where(hit, -jnp.inf, score)
    return picked


def _softmax_steps_t(s_list, carries, v_list):
    m_new = [jnp.maximum(c[0], jnp.max(s, axis=0, keepdims=True)) for s, c in zip(s_list, carries)]
    p = [jnp.exp2(s - m) for s, m in zip(s_list, m_new)]
    pv = [jnp.dot(v, pi.astype(bf16), preferred_element_type=f32) for v, pi in zip(v_list, p)]
    out = []
    for c, m, pi, pvi in zip(carries, m_new, p, pv):
        alpha = jnp.exp2(c[0] - m)
        out.append((m, alpha * c[1] + jnp.sum(pi, axis=0, keepdims=True), alpha * c[2] + pvi))
    return tuple(out)


def _softmax_init_t(groups, queries):
    return tuple((jnp.full((1, queries), M_INIT, f32), jnp.zeros((1, queries), f32),
                  jnp.zeros((LANES, queries), f32)) for _ in range(groups))


def _ada_kernel(c_ref, w_ref, b_ref, o_ref):
    o_ref[...] = jnp.dot(_silu(c_ref[...]), w_ref[...], preferred_element_type=f32,
                         precision=lax.Precision.HIGHEST) + b_ref[...]


def _ada(c8, w, b):
    n = w.shape[1] // D_MODEL
    return pl.pallas_call(
        _ada_kernel, grid=(n,),
        in_specs=[pl.BlockSpec((8, D_MODEL), lambda j: (0, 0)),
                  pl.BlockSpec((D_MODEL, D_MODEL), lambda j: (0, j)),
                  pl.BlockSpec((1, D_MODEL), lambda j: (0, j))],
        out_specs=pl.BlockSpec((8, D_MODEL), lambda j: (0, j)),
        out_shape=jax.ShapeDtypeStruct((8, w.shape[1]), f32),
        compiler_params=_cparams("arbitrary"), name="ada")(c8, w, b)


IN_TM = MOBA_BLOCK


def _inproj_kernel(x_ref, mod_ref, w_ref, ct_ref, st_ref,
                   mq_ref, mk_ref, mv_ref, km_ref, nq_ref, nqr_ref, ks_ref, kw_ref, vs_ref, vw_ref,
                   kvc_ref, pool_ref, gq_ref, gk_ref, gv_ref, gg_ref, misc_ref, *, nt):
    tb = pl.program_id(0) % nt
    h = (x_ref[...] * (1.0 + mod_ref[0, 1:2, :]) + mod_ref[0, 0:1, :]).astype(bf16)

    def seg(name, width):
        a = _S[name]
        return jnp.dot(h, w_ref[:, a:a + width], preferred_element_type=f32)

    ct, st = ct_ref[...], st_ref[...]
    lane = lax.broadcasted_iota(i32, (IN_TM, LANES), 1)
    first8 = (lane % HEAD_DIM) < ROPE_DIM // 2
    half = lane < HEAD_DIM

    def rope128(y):
        partner = jnp.where(first8, pltpu.roll(y, LANES - 8, axis=1), pltpu.roll(y, 8, axis=1))
        return y * ct + partner * st

    def rope(y):
        return jnp.concatenate([rope128(y[:, c * LANES:(c + 1) * LANES]) for c in range(y.shape[1] // LANES)],
                               axis=1)

    def lo_half(y):
        return jnp.where(half, y, 0.0)

    def hi_half(y):
        return jnp.where(half, pltpu.roll(y, HEAD_DIM, axis=1), 0.0)

    def per_head(y):
        parts = []
        for c in range(2):
            yc = y[:, c * LANES:(c + 1) * LANES]
            parts += [lo_half(yc), hi_half(yc)]
        return jnp.concatenate(parts, axis=1)

    mq_ref[...] = per_head(rope(seg("mq", 256)) * Q_SCALE).astype(bf16)
    k = rope(seg("mk", 256))
    km_ref[0] = jnp.mean(k, axis=0, keepdims=True)
    lane4 = lax.broadcasted_iota(i32, (IN_TM, 4 * LANES), 1)
    mk_ref[...] = jnp.where((lane4 % LANES) == HEAD_DIM + tb, 1.0, per_head(k)).astype(bf16)
    mv_ref[...] = per_head(seg("mv", 256)).astype(bf16)
    q = seg("nq", 256) * Q_SCALE
    nq_ref[...] = per_head(q).astype(bf16)
    nqr_ref[...] = per_head(rope(q)).astype(bf16)
    sw = rope128(seg("sw", 128))
    row = lax.broadcasted_iota(i32, (IN_TM, LANES), 0)
    slc_id = tb * (IN_TM // SLC_BLOCK) + row // SLC_BLOCK
    ks_ref[...] = jnp.concatenate([lo_half(sw), jnp.where(lane == slc_id, 1.0, 0.0)], axis=1).astype(bf16)
    kw_ref[...] = hi_half(sw).astype(bf16)
    vsw = seg("vsw", 128)
    vs_ref[...] = lo_half(vsw).astype(bf16)
    vw_ref[...] = hi_half(vsw).astype(bf16)
    kvc_ref[...] = seg("kvc", 128).astype(bf16)
    pool_ref[...] = seg("pool", 256)
    gq_ref[...] = seg("gq", 256)
    gk_ref[...] = seg("gk", 256)
    gv_ref[...] = seg("gv", 256)
    gg_ref[...] = seg("gg", 256)
    misc_ref[...] = seg("misc", 128)


def _inproj(x2, mod, w, ct, st, B, T):
    N = B * T
    nt = T // IN_TM
    assert T % IN_TM == 0 and nt <= 32 and T // SLC_BLOCK <= LANES
    row = lambda w_, dt: (jax.ShapeDtypeStruct((N, w_), dt), pl.BlockSpec((IN_TM, w_), lambda i: (i, 0)))
    outs = [row(512, bf16), row(512, bf16), row(512, bf16),
            (jax.ShapeDtypeStruct((N // IN_TM, 1, 256), f32), pl.BlockSpec((1, 1, 256), lambda i: (i, 0, 0))),
            row(512, bf16), row(512, bf16), row(256, bf16), row(128, bf16), row(128, bf16), row(128, bf16),
            row(128, bf16), row(256, f32), row(256, f32), row(256, f32), row(256, f32), row(256, f32),
            row(128, f32)]
    return pl.pallas_call(
        functools.partial(_inproj_kernel, nt=nt), grid=(N // IN_TM,),
        in_specs=[pl.BlockSpec((IN_TM, D_MODEL), lambda i: (i, 0)),
                  pl.BlockSpec((1, 6, D_MODEL), lambda i: (i // nt, 0, 0)),
                  pl.BlockSpec((D_MODEL, IN_COLS_PACKED), lambda i: (0, 0), pipeline_mode=pl.Buffered(1)),
                  pl.BlockSpec((IN_TM, LANES), lambda i: (i % nt, 0)),
                  pl.BlockSpec((IN_TM, LANES), lambda i: (i % nt, 0))],
        out_specs=[o[1] for o in outs], out_shape=[o[0] for o in outs],
        compiler_params=_cparams("arbitrary"), name="inproj")(x2, mod, w, ct, st)


def _cmp_kernel(x_ref, wa_ref, wb_ref, pea_ref, peb_ref, w2k_ref, w2v_ref, kc_ref, vc_ref):
    x = x_ref[0].astype(f32)
    a = jnp.dot((x + pea_ref[...]).astype(bf16), wa_ref[...], preferred_element_type=f32)
    b = jnp.dot((x + peb_ref[...]).astype(bf16), wb_ref[...], preferred_element_type=f32)
    hid = a + pltpu.roll(b, x.shape[0] - 1, axis=0)
    g = jax.nn.gelu(hid).astype(bf16)
    kc_ref[0] = jnp.dot(g, w2k_ref[...], preferred_element_type=f32).astype(bf16)
    vc_ref[0] = jnp.dot(g, w2v_ref[...], preferred_element_type=f32).astype(bf16)


def _cmp(xseg, wa, wb, pea, peb, w2k, w2v):
    B, ns, wd = xseg.shape
    full = lambda a: pl.BlockSpec(a.shape, lambda b: (0,) * a.ndim)
    return pl.pallas_call(
        _cmp_kernel, grid=(B,),
        in_specs=[pl.BlockSpec((1, ns, wd), lambda b: (b, 0, 0))] + [full(a) for a in (wa, wb, pea, peb, w2k, w2v)],
        out_specs=[pl.BlockSpec((1, ns, LANES), lambda b: (b, 0, 0))] * 2,
        out_shape=[jax.ShapeDtypeStruct((B, ns, LANES), bf16)] * 2,
        compiler_params=_cparams("arbitrary"), name="nsa_compress")(xseg, wa, wb, pea, peb, w2k, w2v)


MOBA_NBLK = 32
KEY_TILE = 512
MOBA_TQ = 512


def _moba_kernel(q_ref, k_ref, vt_ref, km_ref, o_ref):
    q0 = pl.program_id(1) * MOBA_TQ
    heads = range(N_HEADS)
    hl = lambda h: slice(h * LANES, (h + 1) * LANES)
    blk = lax.broadcasted_iota(i32, (MOBA_NBLK, MOBA_TQ), 0)
    own = (q0 + lax.broadcasted_iota(i32, (1, MOBA_TQ), 1)) // MOBA_BLOCK
    past = blk < own
    zeros = lambda n: jnp.zeros((n, MOBA_TQ), f32)
    qf = []
    for h in heads:
        qh = q_ref[:, hl(h)]
        hi, lo = _split_bf16(km_ref[0, h])
        gate_t = (lax.dot_general(hi, qh, NT, preferred_element_type=f32)
                  + lax.dot_general(lo, qh, NT, preferred_element_type=f32))
        picked = _argmax_rounds(jnp.where(past, gate_t, -jnp.inf), blk, MOBA_TOPK, axis=0)
        allowed = ((picked > 0.0) & past) | (blk == own)
        bias_t = jnp.concatenate([zeros(HEAD_DIM), jnp.where(allowed, 0.0, MASKED),
                                  zeros(LANES - HEAD_DIM - MOBA_NBLK)], axis=0)
        qf.append(qh + jnp.transpose(bias_t).astype(bf16))

    def scores_t(h, off):
        return lax.dot_general(k_ref[pl.ds(off, KEY_TILE), hl(h)], qf[h], NT, preferred_element_type=f32)

    def body(p, carry):
        off = pl.multiple_of(p * KEY_TILE, KEY_TILE)
        return _softmax_steps_t([scores_t(h, off) for h in heads], carry, [vt_ref[0, p, hl(h), :] for h in heads])

    last = q0 // KEY_TILE
    carry = lax.fori_loop(0, last, body, _softmax_init_t(N_HEADS, MOBA_TQ))
    off = pl.multiple_of(last * KEY_TILE, KEY_TILE)
    kpos = off + lax.broadcasted_iota(i32, (KEY_TILE, MOBA_TQ), 0)
    qpos = q0 + lax.broadcasted_iota(i32, (KEY_TILE, MOBA_TQ), 1)
    carry = _softmax_steps_t([jnp.where(kpos <= qpos, scores_t(h, off), MASKED) for h in heads], carry,
                             [vt_ref[0, last, hl(h), :] for h in heads])
    for h in heads:
        o_ref[:, hl(h)] = jnp.transpose(carry[h][2] / carry[h][1]).astype(bf16)


def _blocks_t(v, B, T, blk):
    return v.reshape(B, T // blk, blk, v.shape[1]).transpose(0, 1, 3, 2)


def _moba(mq, mk, mv, km, B, T):
    N = B * T
    nt = T // MOBA_TQ
    wd = N_HEADS * LANES
    assert T % KEY_TILE == 0 and KEY_TILE % MOBA_TQ == 0 and T // MOBA_BLOCK <= MOBA_NBLK
    return pl.pallas_call(
        _moba_kernel, grid=(B, nt),
        in_specs=[pl.BlockSpec((MOBA_TQ, wd), lambda b, i: (b * nt + i, 0)),
                  pl.BlockSpec((T, wd), lambda b, i: (b, 0), pipeline_mode=pl.Buffered(1)),
                  pl.BlockSpec((1, T // KEY_TILE, wd, KEY_TILE), lambda b, i: (b, 0, 0, 0),
                               pipeline_mode=pl.Buffered(1)),
                  pl.BlockSpec((1, N_HEADS, MOBA_NBLK, LANES), lambda b, i: (b, 0, 0, 0))],
        out_specs=pl.BlockSpec((MOBA_TQ, wd), lambda b, i: (b * nt + i, 0)),
        out_shape=jax.ShapeDtypeStruct((N, wd), bf16),
        compiler_params=_cparams("arbitrary", "arbitrary"), name="moba")(mq, mk, _blocks_t(mv, B, T, KEY_TILE), km)


NSA_TQ = 512
NSA_KB = 256


def _stack_heads(ref):
    return jnp.concatenate([ref[:, h * LANES:(h + 1) * LANES] for h in range(N_HEADS)], axis=0)


def _nsa_select_kernel(nq_ref, kc_ref, vct_ref, ovt_ref, oc_ref, selb_ref, *, n_cmp):
    c = pl.program_id(1)
    rows = N_HEADS * NSA_TQ
    s = lax.dot_general(kc_ref[0], _stack_heads(nq_ref), NT, preferred_element_type=f32)
    n = lax.broadcasted_iota(i32, s.shape, 0)
    qpos_t = c * NSA_TQ + lax.broadcasted_iota(i32, (1, rows), 1) % NSA_TQ
    ok = (n * CMP_STRIDE + (CMP_BLOCK - 1) <= qpos_t) & (n < n_cmp)
    s = jnp.where(ok, s, -jnp.inf)
    m = jnp.max(s, axis=0, keepdims=True)
    m = jnp.where(m > -jnp.inf, m, 0.0)
    e = jnp.where(ok, jnp.exp2(s - m), 0.0)
    p_c = e / jnp.maximum(jnp.sum(e, axis=0, keepdims=True), 1e-30)
    o_c_t = jnp.dot(vct_ref[0], p_c.astype(bf16), preferred_element_type=f32)
    hq = lambda h: slice(h * NSA_TQ, (h + 1) * NSA_TQ)
    for h in range(N_HEADS):
        oc_ref[:, h * LANES:(h + 1) * LANES] = jnp.transpose(o_c_t[:, hq(h)]).astype(bf16)
    hi, lo = _split_bf16((p_c[:, hq(0)] + p_c[:, hq(1)]) + (p_c[:, hq(2)] + p_c[:, hq(3)]))
    imp = (jnp.dot(ovt_ref[...], hi, preferred_element_type=f32)
           + jnp.dot(ovt_ref[...], lo, preferred_element_type=f32))
    j = lax.broadcasted_iota(i32, imp.shape, 0)
    cur = (c * NSA_TQ + lax.broadcasted_iota(i32, (1, NSA_TQ), 1)) // SLC_BLOCK
    forced = (j == 0) | (j == cur) | (j == cur - 1)
    valid = j <= cur
    score = jnp.where(valid, jnp.where(forced, FORCE_SCORE, imp), -jnp.inf)
    chosen = (_argmax_rounds(score, j, SLC_TOPN, axis=0) > 0.0) & valid
    selb_ref[...] = jnp.transpose(jnp.where(chosen, 0.0, MASKED)).astype(bf16)


def _nsa_attend_kernel(nqr_ref, selb_ref, oc_ref, misc_ref, ks_ref, vst_ref, kw_ref, vwt_ref, o_ref):
    c = pl.program_id(1)
    rows = N_HEADS * NSA_TQ
    heads = range(N_HEADS)
    q4r = _stack_heads(nqr_ref)
    qpos_t = c * NSA_TQ + lax.broadcasted_iota(i32, (1, rows), 1) % NSA_TQ
    selb = selb_ref[...]
    lhs = [jnp.concatenate([q4r[h * NSA_TQ:(h + 1) * NSA_TQ], selb], axis=1) for h in heads]

    def scores_t(off):
        kb = ks_ref[pl.ds(off, KEY_TILE), :]
        return [lax.dot_general(kb, lhs[h], NT, preferred_element_type=f32) for h in heads]

    def body(p, carry):
        return _softmax_steps_t(scores_t(pl.multiple_of(p * KEY_TILE, KEY_TILE)), carry, [vst_ref[0, p]] * N_HEADS)

    last = (c * NSA_TQ) // KEY_TILE
    carry = lax.fori_loop(0, last, body, _softmax_init_t(N_HEADS, NSA_TQ))
    off = pl.multiple_of(last * KEY_TILE, KEY_TILE)
    kpos = off + lax.broadcasted_iota(i32, (KEY_TILE, NSA_TQ), 0)
    qpos = c * NSA_TQ + lax.broadcasted_iota(i32, (KEY_TILE, NSA_TQ), 1)
    carry = _softmax_steps_t([jnp.where(kpos <= qpos, s, MASKED) for s in scores_t(off)], carry,
                             [vst_ref[0, last]] * N_HEADS)
    o_s = jnp.concatenate([jnp.transpose(carry[h][2] / carry[h][1]) for h in heads], axis=0)

    nwb = (WIN + NSA_TQ) // NSA_KB
    sb = jnp.maximum(c * (NSA_TQ // NSA_KB) - WIN // NSA_KB, 0)
    start = pl.multiple_of(sb * NSA_KB, NSA_KB)
    s_w = lax.dot_general(kw_ref[pl.ds(start, nwb * NSA_KB), :], q4r, NT, preferred_element_type=f32)
    wpos = start + lax.broadcasted_iota(i32, s_w.shape, 0)
    s_w = jnp.where((wpos <= qpos_t) & (wpos > qpos_t - WIN), s_w, -jnp.inf)
    e_w = jnp.exp2(s_w - jnp.max(s_w, axis=0, keepdims=True))
    p_w = (e_w / jnp.sum(e_w, axis=0, keepdims=True)).astype(bf16)
    o_w_t = jnp.dot(vwt_ref[0, sb], p_w[0:NSA_KB], preferred_element_type=f32)
    for i in range(1, nwb):
        o_w_t = o_w_t + jnp.dot(vwt_ref[0, sb + i], p_w[i * NSA_KB:(i + 1) * NSA_KB], preferred_element_type=f32)
    o_w = jnp.transpose(o_w_t)

    gates = jax.nn.sigmoid(misc_ref[...])
    gl = lax.broadcasted_iota(i32, gates.shape, 1)

    def gate_col(g):
        return jnp.concatenate([jnp.sum(jnp.where(gl == MISC_GATE0 + 3 * h + g, gates, 0.0), axis=1, keepdims=True)
                                for h in range(N_HEADS)], axis=0)

    out = gate_col(0) * _stack_heads(oc_ref).astype(f32) + gate_col(1) * o_s + gate_col(2) * o_w
    pair = lambda a, b: a + pltpu.roll(b, HEAD_DIM, axis=1)
    o_ref[...] = jnp.concatenate([pair(out[0:NSA_TQ], out[NSA_TQ:2 * NSA_TQ]),
                                  pair(out[2 * NSA_TQ:3 * NSA_TQ], out[3 * NSA_TQ:])], axis=1).astype(bf16)


def _nsa(nq, nqr, misc, ks, vs, kw, vw, kc, vc, ov, B, T):
    N = B * T
    nc = T // NSA_TQ
    ns = kc.shape[1]
    assert T >= NSA_TQ + WIN and T % KEY_TILE == 0 and KEY_TILE % NSA_TQ == 0 and NSA_TQ % NSA_KB == 0
    assert WIN % NSA_KB == 0
    tok = lambda w_: pl.BlockSpec((NSA_TQ, w_), lambda b, c: (b * nc + c, 0))
    seq = lambda w_: pl.BlockSpec((T, w_), lambda b, c: (b, 0), pipeline_mode=pl.Buffered(1))
    o_c, selb = pl.pallas_call(
        functools.partial(_nsa_select_kernel, n_cmp=(T - CMP_BLOCK) // CMP_STRIDE + 1), grid=(B, nc),
        in_specs=[tok(512), pl.BlockSpec((1, ns, LANES), lambda b, c: (b, 0, 0)),
                  pl.BlockSpec((1, LANES, ns), lambda b, c: (b, 0, 0)), pl.BlockSpec((LANES, ns), lambda b, c: (0, 0))],
        out_specs=[tok(512), tok(LANES)],
        out_shape=[jax.ShapeDtypeStruct((N, 512), bf16), jax.ShapeDtypeStruct((N, LANES), bf16)],
        compiler_params=_cparams("arbitrary", "arbitrary"), name="nsa_select")(
            nq, kc, vc.transpose(0, 2, 1), ov.T)
    seq_t = lambda kb: pl.BlockSpec((1, T // kb, LANES, kb), lambda b, c: (b, 0, 0, 0), pipeline_mode=pl.Buffered(1))
    return pl.pallas_call(
        _nsa_attend_kernel, grid=(B, nc),
        in_specs=[tok(512), tok(LANES), tok(512), tok(LANES), seq(256), seq_t(KEY_TILE), seq(LANES), seq_t(NSA_KB)],
        out_specs=tok(256), out_shape=jax.ShapeDtypeStruct((N, 256), bf16),
        compiler_params=_cparams("arbitrary", "arbitrary"), name="nsa_attend")(
            nqr, selb, o_c, misc, ks, _blocks_t(vs, B, T, KEY_TILE), kw, _blocks_t(vw, B, T, NSA_KB))


POOL_TM = 512
POOL_HALO = 16


def _pool_kernel(u_ref, halo_ref, w_ref, sc_ref, o_ref):
    t = pl.program_id(1)
    halo = jnp.where(t == 0, 0.0, halo_ref[...])
    ext = jnp.concatenate([halo, u_ref[...]], axis=0)
    s2 = ext + pltpu.roll(ext, 1, axis=0)
    s4 = s2 + pltpu.roll(s2, 2, axis=0)
    s8 = s4 + pltpu.roll(s4, 4, axis=0)
    s16 = s8 + pltpu.roll(s8, 8, axis=0)
    pos1 = jnp.maximum(t * POOL_TM - POOL_HALO + 1 + lax.broadcasted_iota(i32, ext.shape, 0), 1).astype(f32)
    grp = lax.broadcasted_iota(i32, ext.shape, 1) // HEAD_DIM
    mean = jnp.where(grp == 0, s2 / jnp.minimum(pos1, 2.0),
                     jnp.where(grp == 1, s4 / jnp.minimum(pos1, 4.0),
                               jnp.where(grp == 2, s8 / jnp.minimum(pos1, 8.0), s16 / jnp.minimum(pos1, 16.0))))
    pooled = (mean - ext)[POOL_HALO:, :]
    o_ref[...] = (jnp.dot(pooled.astype(bf16), w_ref[...], preferred_element_type=f32) * sc_ref[...]).astype(bf16)


def _pool(u, wbd, scale, B, T):
    N = B * T
    tm = min(POOL_TM, T)
    assert tm == POOL_TM and T % POOL_TM == 0
    nt = T // tm
    hb = tm // POOL_HALO
    return pl.pallas_call(
        _pool_kernel, grid=(B, nt),
        in_specs=[pl.BlockSpec((tm, 256), lambda b, t: (b * nt + t, 0)),
                  pl.BlockSpec((POOL_HALO, 256), lambda b, t: (jnp.maximum((b * nt + t) * hb - 1, 0), 0)),
                  pl.BlockSpec((256, 256), lambda b, t: (0, 0)),
                  pl.BlockSpec((1, 256), lambda b, t: (0, 0))],
        out_specs=pl.BlockSpec((tm, 256), lambda b, t: (b * nt + t, 0)),
        out_shape=jax.ShapeDtypeStruct((N, 256), bf16),
        compiler_params=_cparams("arbitrary", "arbitrary"), name="pool")(u, u, wbd, scale)


GLA_TM = 256


def _gla_kernel(q_ref, k_ref, v_ref, g_ref, misc_ref, wa_ref, ba_ref, gn_ref, bd_ref, o_ref,
                st_ref, q_s, k_s, v_s, b_s, qe_s, ke_s, gam_s, o_s):
    nb = q_ref.shape[0]

    @pl.when(pl.program_id(0) == 0)
    def _():
        st_ref[...] = jnp.zeros_like(st_ref)

    r16 = lax.broadcasted_iota(i32, (GLA_TM, 256), 0) % GLA_SUB
    for bi_ in range(nb):
        x = jnp.dot(misc_ref[bi_], wa_ref[...], preferred_element_type=f32,
                    precision=lax.Precision.HIGHEST) + ba_ref[...]
        log_a = (jnp.minimum(x, 0.0) - jnp.log1p(jnp.exp(-jnp.abs(x)))) / GLA_TAU
        b = log_a
        for s in (1, 2, 4, 8):
            b = b + jnp.where(r16 >= s, pltpu.roll(b, s, axis=0), 0.0)
        b_end = jnp.where(r16 == GLA_SUB - 1, b, 0.0)
        for s in (1, 2, 4, 8):
            b_end = b_end + pltpu.roll(b_end, GLA_TM - s, axis=0)
        q = q_ref[bi_] * QK_SCALE
        k = k_ref[bi_]
        q_s[bi_] = q
        k_s[bi_] = k
        v_s[bi_] = v_ref[bi_]
        b_s[bi_] = b
        qe_s[bi_] = (q * jnp.exp(b)).astype(bf16)
        ke_s[bi_] = (k * jnp.exp(b_end - b)).astype(bf16)
        gam_s[bi_] = jnp.exp(b_end)
    bd = bd_ref[...]
    shape3 = (GLA_SUB, GLA_SUB, 256)
    causal = lax.broadcasted_iota(i32, shape3, 0) <= lax.broadcasted_iota(i32, shape3, 1)

    def block(n, _):
        r0 = pl.multiple_of(n * GLA_SUB, GLA_SUB)
        rows = pl.ds(r0, GLA_SUB)
        for bi_ in range(nb):
            qi, ki, vi, bi = q_s[bi_, rows, :], k_s[bi_, rows, :], v_s[bi_, rows, :], b_s[bi_, rows, :]
            diff = jnp.where(causal, bi[None, :, :] - bi[:, None, :], 0.0)
            w3 = jnp.where(causal, qi[None, :, :] * ki[:, None, :] * jnp.exp(diff), 0.0)
            a3 = jnp.dot(w3.reshape(GLA_SUB * GLA_SUB, 256).astype(bf16), bd, preferred_element_type=f32)
            intra = jnp.sum(a3.reshape(shape3) * vi[:, None, :], axis=0)
            st = st_ref[bi_]
            inter = lax.dot_general(qe_s[bi_, rows, :], st.astype(bf16), NT, preferred_element_type=f32)
            o_s[bi_, rows, :] = intra + inter
            upd = lax.dot_general(vi.astype(bf16), ke_s[bi_, rows, :], TN, preferred_element_type=f32)
            st_ref[bi_] = st * gam_s[bi_, pl.ds(r0, 1), :] + jnp.where(bd > 0, upd, 0.0)
        return 0

    lax.fori_loop(0, GLA_TM // GLA_SUB, block, 0)
    for bi_ in range(nb):
        o = o_s[bi_]
        ms = jnp.dot(o * o, bd.astype(f32), preferred_element_type=f32, precision=lax.Precision.HIGHEST) / HEAD_DIM
        o_ref[bi_] = (o * lax.rsqrt(ms + LN_EPS) * gn_ref[...] * _silu(g_ref[bi_])).astype(bf16)


def _gla(gq, gk, gv, gg, misc, wa, ba, gn, bd, B, T):
    N = B * T
    assert T % GLA_TM == 0
    tok = lambda w_: pl.BlockSpec((B, GLA_TM, w_), lambda t: (0, t, 0))
    full = lambda a: pl.BlockSpec(a.shape, lambda t: (0, 0))
    v = lambda dt: pltpu.VMEM((B, GLA_TM, 256), dt)
    seq = lambda a: a.reshape(B, T, a.shape[1])
    out = pl.pallas_call(
        _gla_kernel, grid=(T // GLA_TM,),
        in_specs=[tok(256), tok(256), tok(256), tok(256), tok(128), full(wa), full(ba), full(gn), full(bd)],
        out_specs=tok(256), out_shape=jax.ShapeDtypeStruct((B, T, 256), bf16),
        scratch_shapes=[pltpu.VMEM((B, 256, 256), f32), v(f32), v(f32), v(f32), v(f32), v(bf16), v(bf16), v(f32),
                        v(f32)],
        compiler_params=_cparams("arbitrary"), name="gla")(seq(gq), seq(gk), seq(gv), seq(gg), seq(misc), wa, ba, gn, bd)
    return out.reshape(N, 256)


OUT_TM = 256


def _outproj_kernel(mo_ref, no_ref, po_ref, go_ref, x_ref, mod_ref, w_ref, lng_ref, lnb_ref, wrh_ref, wrl_ref,
                    x1_ref, h2b_ref, lg_ref, *, alpha):
    a = jnp.concatenate([mo_ref[...], no_ref[...], po_ref[...], go_ref[...]], axis=1)
    y = jnp.dot(a, w_ref[...], preferred_element_type=f32)
    x1 = _layer_norm(alpha * x_ref[...] + mod_ref[0, 2:3, :] * y, lng_ref[...], lnb_ref[...])
    x1_ref[...] = x1
    h2 = x1 * (1.0 + mod_ref[0, 4:5, :]) + mod_ref[0, 3:4, :]
    h2b_ref[...] = h2.astype(bf16)
    hi, lo = _split_bf16(h2)
    wrh = wrh_ref[...]
    lg_ref[...] = (jnp.dot(hi, wrh, preferred_element_type=f32) + jnp.dot(lo, wrh, preferred_element_type=f32)
                   + jnp.dot(hi, wrl_ref[...], preferred_element_type=f32))


def _outproj(mo, no, po, go, x2, mod, w, lng, lnb, wrh, wrl, alpha, B, T):
    N = B * T
    nt = T // OUT_TM
    tok = lambda w_: pl.BlockSpec((OUT_TM, w_), lambda i: (i, 0))
    full = lambda a: pl.BlockSpec(a.shape, lambda i: (0,) * a.ndim)
    return pl.pallas_call(
        functools.partial(_outproj_kernel, alpha=alpha), grid=(N // OUT_TM,),
        in_specs=[tok(512), tok(256), tok(256), tok(256), tok(D_MODEL),
                  pl.BlockSpec((1, 6, D_MODEL), lambda i: (i // nt, 0, 0)),
                  full(w), full(lng), full(lnb), full(wrh), full(wrl)],
        out_specs=[tok(D_MODEL), tok(D_MODEL), tok(N_EXPERTS)],
        out_shape=[jax.ShapeDtypeStruct((N, D_MODEL), f32), jax.ShapeDtypeStruct((N, D_MODEL), bf16),
                   jax.ShapeDtypeStruct((N, N_EXPERTS), f32)],
        compiler_params=_cparams("arbitrary"), name="outproj")(mo, no, po, go, x2, mod, w, lng, lnb, wrh, wrl)


ROUTE_TM = 256
GROUP_SIZE = N_EXPERTS // N_EXPERT_GROUPS


def _per_token(rows):
    pad = jnp.zeros((LANES - len(rows), rows[0].shape[1]), f32)
    return jnp.transpose(jnp.concatenate(rows + [pad], axis=0))


def _route_kernel(lg_ref, rb_ref, ei_ref, wt_ref, cnt_ref, tb_ref, base_ref):
    @pl.when(pl.program_id(0) == 0)
    def _():
        base_ref[...] = jnp.zeros_like(base_ref)

    tb_ref[0] = base_ref[...].astype(i32)

    s = jax.nn.sigmoid(jnp.transpose(lg_ref[...]))
    ssel = s + rb_ref[...]
    shape3 = (N_EXPERT_GROUPS, GROUP_SIZE, ROUTE_TM)
    x3 = ssel.reshape(shape3)
    i3 = lax.broadcasted_iota(i32, shape3, 1)
    m1 = jnp.max(x3, axis=1, keepdims=True)
    first = jnp.min(jnp.where(x3 == m1, i3, BIG_IDX), axis=1, keepdims=True)
    m2 = jnp.max(jnp.where(i3 == first, -jnp.inf, x3), axis=1, keepdims=True)
    gscore = (m1 + m2).reshape(N_EXPERT_GROUPS, ROUTE_TM)
    gid = lax.broadcasted_iota(i32, gscore.shape, 0)
    beaten = jnp.zeros(gscore.shape, i32)
    for g in range(N_EXPERT_GROUPS):
        other = gscore[g:g + 1, :]
        beaten = beaten + jnp.where((other > gscore) | ((other == gscore) & (g < gid)), 1, 0)
    keep = jnp.broadcast_to((beaten < TOPK_GROUPS)[:, None, :], shape3).reshape(ssel.shape)
    x = jnp.where(keep, ssel, -jnp.inf)
    eid = lax.broadcasted_iota(i32, x.shape, 0)
    hits, idx_rows, w_rows = [], [], []
    for _ in range(TOP_K):
        mx = jnp.max(x, axis=0, keepdims=True)
        idx = jnp.min(jnp.where(x == mx, eid, BIG_IDX), axis=0, keepdims=True)
        hit = eid == idx
        hits.append(hit)
        idx_rows.append(idx.astype(f32))
        w_rows.append(jnp.sum(jnp.where(hit, s, 0.0), axis=0, keepdims=True))
        x = jnp.where(hit, -jnp.inf, x)
    wsum = w_rows[0]
    for w in w_rows[1:]:
        wsum = wsum + w
    chosen = jnp.zeros(s.shape, f32)
    for hit in hits:
        chosen = jnp.where(hit, 1.0, chosen)
    total = base_ref[...] + jnp.sum(chosen, axis=1, keepdims=True)
    base_ref[...] = total
    cnt_ref[...] = total.astype(i32)
    ei_ref[...] = _per_token(idx_rows).astype(i32)
    wt_ref[...] = _per_token([w / wsum * ROUTED_SCALE for w in w_rows])


def _route(logits, rb):
    N = logits.shape[0]
    tok = pl.BlockSpec((ROUTE_TM, LANES), lambda i: (i, 0))
    col = pl.BlockSpec((N_EXPERTS, 1), lambda i: (0, 0))
    return pl.pallas_call(
        _route_kernel, grid=(N // ROUTE_TM,),
        in_specs=[pl.BlockSpec((ROUTE_TM, N_EXPERTS), lambda i: (i, 0)), col],
        out_specs=[tok, tok, col, pl.BlockSpec((1, N_EXPERTS, 1), lambda i: (i, 0, 0))],
        out_shape=[jax.ShapeDtypeStruct((N, LANES), i32), jax.ShapeDtypeStruct((N, LANES), f32),
                   jax.ShapeDtypeStruct((N_EXPERTS, 1), i32), jax.ShapeDtypeStruct((N // ROUTE_TM, N_EXPERTS, 1), i32)],
        scratch_shapes=[pltpu.VMEM((N_EXPERTS, 1), f32)],
        compiler_params=_cparams("arbitrary"), name="route")(logits, rb.reshape(N_EXPERTS, 1))


DISP_TM = ROUTE_TM
DISP_SLOTS = DISP_TM * TOP_K
DISP_CHUNK = 16
DISP_PERM_ROWS = 512
DISP_MAX_CHUNKS = DISP_SLOTS // DISP_CHUNK + N_EXPERTS


def _dispatch_kernel(x_ref, ei_ref, tb_ref, ps_ref, triu_ref, tril_ref, xs_in, pos_ref, xs_hbm,
                     pbuf, tab_v, tab_s, csem, rsem):
    del xs_in
    i = pl.program_id(0)
    slot = i % 2

    @pl.when(i == 0)
    def _():
        pbuf[:, DISP_SLOTS * ROW_TILE:, :] = jnp.zeros((2, DISP_CHUNK * ROW_TILE, LANES), u32)

    ei_t = jnp.transpose(ei_ref[...].astype(f32))
    eid = lax.broadcasted_iota(i32, (N_EXPERTS, DISP_TM), 0).astype(f32)
    hits = [eid == ei_t[k:k + 1, :] for k in range(TOP_K)]
    member = jnp.zeros(eid.shape, f32)
    for hit in hits:
        member = jnp.where(hit, 1.0, member)
    prefix = jnp.dot(member.astype(bf16), triu_ref[...], preferred_element_type=f32)
    cnt = jnp.sum(member, axis=1, keepdims=True)
    nch = jnp.floor((cnt + (DISP_CHUNK - 1)) * (1.0 / DISP_CHUNK))
    lane = lax.broadcasted_iota(i32, (N_EXPERTS, LANES), 1)
    before = jnp.dot(tril_ref[...], jnp.where(lane == 0, cnt, jnp.where(lane == 1, nch, 0.0)).astype(bf16),
                     preferred_element_type=f32)
    off, cidx = before[:, 0:1], before[:, 1:2]
    dst = (ps_ref[...] + tb_ref[0]).astype(f32)
    slot_rows = [jnp.sum(jnp.where(hit, off + prefix, 0.0), axis=0, keepdims=True) for hit in hits]
    pos_ref[...] = _per_token([jnp.sum(jnp.where(hit, dst + prefix, 0.0), axis=0, keepdims=True)
                               for hit in hits]).astype(i32)
    rel = lax.broadcasted_iota(i32, (N_EXPERTS, DISP_MAX_CHUNKS), 1).astype(f32) - cidx
    mine = (rel >= 0.0) & (rel < nch)
    src_row = jnp.sum(jnp.where(mine, off + DISP_CHUNK * rel, 0.0), axis=0, keepdims=True)
    dst_row = jnp.sum(jnp.where(mine, dst + DISP_CHUNK * rel, 0.0), axis=0, keepdims=True)
    n_row = jnp.broadcast_to(jnp.sum(nch, axis=0, keepdims=True), src_row.shape)
    tab_v[...] = jnp.concatenate([src_row, dst_row, n_row, jnp.zeros((5, DISP_MAX_CHUNKS), f32)], axis=0).astype(i32)
    cp = pltpu.make_async_copy(tab_v, tab_s.at[slot], csem)
    cp.start()

    x = x_ref[...]
    for r in range(DISP_SLOTS // DISP_PERM_ROWS):
        sid = (r * DISP_PERM_ROWS + lax.broadcasted_iota(i32, (DISP_PERM_ROWS, DISP_TM), 0)).astype(f32)
        perm = jnp.zeros(sid.shape, f32)
        for srow in slot_rows:
            perm = jnp.where(sid == srow, 1.0, perm)
        rows = jnp.dot(perm.astype(bf16), x, preferred_element_type=f32)
        for s, w in enumerate(_pack_rows(rows)):
            pbuf[slot, pl.ds(r * DISP_PERM_ROWS * ROW_TILE + s, DISP_PERM_ROWS, stride=ROW_TILE), :] = w
    cp.wait()

    def chunk_copy(sl, src, dst_):
        span = lambda r0: pl.ds(pl.multiple_of(r0 * ROW_TILE, ROW_TILE), DISP_CHUNK * ROW_TILE)
        return pltpu.make_async_copy(pbuf.at[sl, span(src)], xs_hbm.at[span(dst_)], rsem.at[sl])

    def start_all(sl):
        n = tab_s[sl, 2, 0]
        for q in range(2):
            def body(jj, _):
                j = 2 * jj + q
                chunk_copy(sl, tab_s[sl, 0, j], tab_s[sl, 1, j]).start(priority=q)
                return 0
            lax.fori_loop(0, (n + 1 - q) // 2, body, 0)

    def wait_all(sl):
        def body(j, _):
            chunk_copy(sl, 0, 0).wait()
            return 0
        lax.fori_loop(0, tab_s[sl, 2, 0], body, 0)

    @pl.when(i > 0)
    def _():
        wait_all(1 - slot)
    start_all(slot)

    @pl.when(i == pl.num_programs(0) - 1)
    def _():
        wait_all(slot)


def _dispatch(h2b, eidx, tile_base, pstarts, n_rows):
    N = h2b.shape[0]
    r_, c_ = jnp.arange(DISP_TM)[:, None], jnp.arange(DISP_TM)[None, :]
    triu = (r_ < c_).astype(bf16)
    e_, f_ = jnp.arange(N_EXPERTS)[:, None], jnp.arange(N_EXPERTS)[None, :]
    tril = (f_ < e_).astype(bf16)
    tok = pl.BlockSpec((DISP_TM, LANES), lambda i: (i, 0))
    full = lambda a: pl.BlockSpec(a.shape, lambda i: (0,) * a.ndim)
    xs0 = jnp.zeros((n_rows * ROW_TILE, LANES), u32)
    pos, xs = pl.pallas_call(
        _dispatch_kernel, grid=(N // DISP_TM,),
        in_specs=[pl.BlockSpec((DISP_TM, D_MODEL), lambda i: (i, 0)), tok,
                  pl.BlockSpec((1, N_EXPERTS, 1), lambda i: (i, 0, 0)), full(pstarts), full(triu), full(tril),
                  pl.BlockSpec(memory_space=pl.ANY)],
        out_specs=[tok, pl.BlockSpec(memory_space=pl.ANY)],
        out_shape=[jax.ShapeDtypeStruct((N, LANES), i32), jax.ShapeDtypeStruct(xs0.shape, u32)],
        scratch_shapes=[pltpu.VMEM((2, (DISP_SLOTS + DISP_CHUNK) * ROW_TILE, LANES), u32),
                        pltpu.VMEM((8, DISP_MAX_CHUNKS), i32), pltpu.SMEM((2, 8, DISP_MAX_CHUNKS), i32),
                        pltpu.SemaphoreType.DMA, pltpu.SemaphoreType.DMA((2,))],
        input_output_aliases={6: 1},
        compiler_params=_cparams("arbitrary"), name="moe_dispatch")(h2b, eidx, tile_base, pstarts, triu, tril, xs0)
    return pos, xs


MOE_BLK = MOE_ROWS * ROW_TILE
MOE_IN_BUFS = 4
MOE_OUT_BUFS = 3


def _moe_kernel(b0_ref, nb_ref, nu_ref, xs_in, wg_ref, wu_ref, wd_ref, xs_out, xbuf, ybuf, isem, osem):
    e = pl.program_id(0)
    nu = nu_ref[0]

    def in_copy(b):
        return pltpu.make_async_copy(xs_in.at[pl.ds(pl.multiple_of(b * MOE_BLK, MOE_BLK), MOE_BLK)],
                                     xbuf.at[b % MOE_IN_BUFS], isem.at[b % MOE_IN_BUFS])

    def out_copy(b):
        return pltpu.make_async_copy(ybuf.at[b % MOE_OUT_BUFS],
                                     xs_out.at[pl.ds(pl.multiple_of(b * MOE_BLK, MOE_BLK), MOE_BLK)],
                                     osem.at[b % MOE_OUT_BUFS])

    @pl.when(e == 0)
    def _():
        for j in range(MOE_IN_BUFS - 1):
            @pl.when(j < nu)
            def _():
                in_copy(j).start()

    wg = wg_ref[0, 0].astype(bf16)
    wu = wu_ref[0, 0].astype(bf16)
    wd = wd_ref[0, 0].astype(bf16)

    def block(i, _):
        b = b0_ref[e] + i
        in_copy(b).wait()

        @pl.when(b + MOE_IN_BUFS - 1 < nu)
        def _():
            in_copy(b + MOE_IN_BUFS - 1).start()

        @pl.when(b >= MOE_OUT_BUFS)
        def _():
            out_copy(b - MOE_OUT_BUFS).wait()

        g = jnp.zeros((MOE_ROWS, D_EXPERT), f32)
        u = jnp.zeros((MOE_ROWS, D_EXPERT), f32)
        for s in range(ROW_TILE):
            word = xbuf[b % MOE_IN_BUFS, pl.ds(s, MOE_ROWS, stride=ROW_TILE), :]
            xc = jnp.concatenate(_unpack_word(word), axis=1).astype(bf16)
            rows = slice(2 * s * LANES, (2 * s + 2) * LANES)
            g = g + jnp.dot(xc, wg[rows], preferred_element_type=f32)
            u = u + jnp.dot(xc, wu[rows], preferred_element_type=f32)
        y = jnp.dot((_silu(g) * u).astype(bf16), wd, preferred_element_type=f32)
        for s, w in enumerate(_pack_rows(y)):
            ybuf[b % MOE_OUT_BUFS, pl.ds(s, MOE_ROWS, stride=ROW_TILE), :] = w
        out_copy(b).start()
        return 0

    lax.fori_loop(0, nb_ref[e], block, 0)

    @pl.when(e == pl.num_programs(0) - 1)
    def _():
        for j in range(MOE_OUT_BUFS, 0, -1):
            @pl.when(nu >= j)
            def _():
                out_copy(nu - j).wait()


def _moe(layer, blk0, nblk, n_used, xs, wg, wu, wd):
    wspec = lambda a: pl.BlockSpec((1, 1) + a.shape[2:], lambda e, b0, nb, nu: (layer, e, 0, 0))
    gs = pltpu.PrefetchScalarGridSpec(
        num_scalar_prefetch=3, grid=(N_EXPERTS,),
        in_specs=[pl.BlockSpec(memory_space=pl.ANY), wspec(wg), wspec(wu), wspec(wd)],
        out_specs=pl.BlockSpec(memory_space=pl.ANY),
        scratch_shapes=[pltpu.VMEM((MOE_IN_BUFS, MOE_BLK, LANES), u32), pltpu.VMEM((MOE_OUT_BUFS, MOE_BLK, LANES), u32),
                        pltpu.SemaphoreType.DMA((MOE_IN_BUFS,)), pltpu.SemaphoreType.DMA((MOE_OUT_BUFS,))])
    return pl.pallas_call(
        _moe_kernel, grid_spec=gs, out_shape=jax.ShapeDtypeStruct(xs.shape, u32),
        input_output_aliases={3: 0},
        compiler_params=_cparams("arbitrary"), name="moe_experts")(blk0, nblk, n_used, xs, wg, wu, wd)


FIN_TM = 128


def _tile_gather(pos_hbm, ys_hbm, pos_s, gbuf, isem, rsem, step, n_steps):
    def idx_copy(s):
        return pltpu.make_async_copy(pos_hbm.at[pl.ds(s * FIN_TM, FIN_TM)], pos_s.at[s % 2], isem.at[s % 2])

    def tile(i):
        return pl.ds(pl.multiple_of(i * ROW_TILE, ROW_TILE), ROW_TILE)

    def row_copy(slot, t, k, p):
        return pltpu.make_async_copy(ys_hbm.at[tile(p)], gbuf.at[slot, k, tile(t)], rsem.at[slot])

    def start_rows(s):
        slot = s % 2

        def body(t, _):
            for k in range(TOP_K):
                row_copy(slot, t, k, pos_s[slot, t, k]).start(priority=k % 2)
            return 0
        lax.fori_loop(0, FIN_TM, body, 0)

    @pl.when(step == 0)
    def _():
        idx_copy(step).start()
        idx_copy(step).wait()
        start_rows(step)

        @pl.when(n_steps > 1)
        def _():
            idx_copy(step + 1).start()

    @pl.when(step + 1 < n_steps)
    def _():
        idx_copy(step + 1).wait()
        start_rows(step + 1)

        @pl.when(step + 2 < n_steps)
        def _():
            idx_copy(step + 2).start()

    slot = step % 2

    def wait_body(t, _):
        for k in range(TOP_K):
            row_copy(slot, t, k, 0).wait()
        return 0
    lax.fori_loop(0, FIN_TM, wait_body, 0)


def _fin_kernel(pos_hbm, ys_hbm, h2_ref, x1_ref, wt_ref, mod_ref, sg_ref, su_ref, sd_ref, lng_ref, lnb_ref, o_ref,
                pos_s, gbuf, isem, rsem, *, alpha):
    i = pl.program_id(0)
    _tile_gather(pos_hbm, ys_hbm, pos_s, gbuf, isem, rsem, i, pl.num_programs(0))
    hb = h2_ref[...]
    g = jnp.dot(hb, sg_ref[...], preferred_element_type=f32)
    u = jnp.dot(hb, su_ref[...], preferred_element_type=f32)
    shared = jnp.dot((_silu(g) * u).astype(bf16), sd_ref[...], preferred_element_type=f32)
    wt = wt_ref[...]
    lane = lax.broadcasted_iota(i32, wt.shape, 1)
    wk = [jnp.sum(jnp.where(lane == k, wt, 0.0), axis=1, keepdims=True) for k in range(TOP_K)]
    slot = i % 2
    cols = []
    for s in range(ROW_TILE):
        sub = pl.ds(s, FIN_TM, stride=ROW_TILE)
        lo, hi = _unpack_word(gbuf[slot, 0, sub, :])
        acc_lo, acc_hi = lo * wk[0], hi * wk[0]
        for k in range(1, TOP_K):
            lo, hi = _unpack_word(gbuf[slot, k, sub, :])
            acc_lo, acc_hi = acc_lo + lo * wk[k], acc_hi + hi * wk[k]
        cols += [acc_lo, acc_hi]
    y = shared + jnp.concatenate(cols, axis=1)
    o_ref[...] = _layer_norm(alpha * x1_ref[...] + mod_ref[0, 5:6, :] * y, lng_ref[...], lnb_ref[...])


def _fin(pos, ys, h2b, x1, wts, mod, sg, su, sd, lng, lnb, alpha, B, T):
    N = B * T
    nt = T // FIN_TM
    tok = lambda w_: pl.BlockSpec((FIN_TM, w_), lambda i: (i, 0))
    full = lambda a: pl.BlockSpec(a.shape, lambda i: (0,) * a.ndim)
    return pl.pallas_call(
        functools.partial(_fin_kernel, alpha=alpha), grid=(N // FIN_TM,),
        in_specs=[pl.BlockSpec(memory_space=pl.ANY), pl.BlockSpec(memory_space=pl.ANY),
                  tok(D_MODEL), tok(D_MODEL), tok(LANES),
                  pl.BlockSpec((1, 6, D_MODEL), lambda i: (i // nt, 0, 0)),
                  full(sg), full(su), full(sd), full(lng), full(lnb)],
        out_specs=tok(D_MODEL), out_shape=jax.ShapeDtypeStruct((N, D_MODEL), f32),
        scratch_shapes=[pltpu.SMEM((2, FIN_TM, LANES), i32), pltpu.VMEM((2, TOP_K, FIN_TM * ROW_TILE, LANES), u32),
                        pltpu.SemaphoreType.DMA((2,)), pltpu.SemaphoreType.DMA((2,))],
        compiler_params=_cparams("arbitrary"), name="combine")(pos, ys, h2b, x1, wts, mod, sg, su, sd, lng, lnb)


def _pack_w_in(w):
    c = lambda name, width: w[:, _OFF[name]:_OFF[name] + width]
    cols = [c("moba_q", 256), c("moba_k", 256), c("moba_v", 256), c("nsa_q", 256),
            c("k_slc", 64), c("k_win", 64), c("v_slc", 64), c("v_win", 64), c("k_cmp", 64), c("v_cmp", 64),
            c("pool", 256), c("gla_q", 256), c("gla_k", 256), c("gla_v", 256), c("gla_g", 256),
            c("nsa_gate", 12), c("gla_a", 16), jnp.zeros((w.shape[0], LANES - 28), w.dtype)]
    return jnp.concatenate(cols, axis=1).astype(bf16)


def _rope_tables(T):
    half = ROPE_DIM // 2
    inv_freq = ROPE_THETA ** (-jnp.arange(half, dtype=f32) / half)
    ang = jnp.arange(T).astype(f32)[:, None] * inv_freq[None, :]
    cos, sin = jnp.cos(ang), jnp.sin(ang)
    one = jnp.ones((T, HEAD_DIM - ROPE_DIM), f32)
    zero = jnp.zeros((T, HEAD_DIM - ROPE_DIM), f32)
    ct = jnp.concatenate([cos, cos, one], axis=1)
    st = jnp.concatenate([-sin, sin, zero], axis=1)
    return jnp.tile(ct, (1, 2)), jnp.tile(st, (1, 2))


def _cmp_weights(pe, w1, w2):
    half = CMP_BLOCK // 2
    z = jnp.zeros((half, HEAD_DIM, CMP_HIDDEN), f32)

    def arrange(lo):
        wk = w1[0].reshape(CMP_BLOCK, HEAD_DIM, CMP_HIDDEN)[lo:lo + half]
        wv = w1[1].reshape(CMP_BLOCK, HEAD_DIM, CMP_HIDDEN)[lo:lo + half]
        top = jnp.concatenate([wk, z], axis=2)
        bot = jnp.concatenate([z, wv], axis=2)
        return jnp.concatenate([top, bot], axis=1).reshape(half * 2 * HEAD_DIM, 2 * CMP_HIDDEN).astype(bf16)

    def pe_row(lo):
        return jnp.concatenate([pe[0, lo:lo + half], pe[1, lo:lo + half]], axis=1).reshape(1, half * 2 * HEAD_DIM)

    zc = jnp.zeros((CMP_HIDDEN, LANES - HEAD_DIM), f32)
    zr = jnp.zeros((CMP_HIDDEN, LANES), f32)
    w2k = jnp.concatenate([jnp.concatenate([w2[0], zc], axis=1), zr], axis=0).astype(bf16)
    w2v = jnp.concatenate([zr, jnp.concatenate([w2[1], zc], axis=1)], axis=0).astype(bf16)
    return arrange(0), arrange(half), pe_row(0), pe_row(half), w2k, w2v


def _overlap_matrix(ns):
    n = jnp.arange(ns)[:, None] * CMP_STRIDE
    j = jnp.arange(LANES)[None, :] * SLC_BLOCK
    ov = (n < j + SLC_BLOCK) & (n + CMP_BLOCK > j) & (jnp.arange(ns)[:, None] < ns - 1)
    return ov.astype(bf16)


def _block_diag_ones():
    h = jnp.arange(256) // HEAD_DIM
    return (h[:, None] == h[None, :]).astype(bf16)


def _expert_layout(counts, n_tok):
    counts = counts.reshape(N_EXPERTS)
    slack = DISP_CHUNK - 1
    padded = jnp.where(counts > 0, (counts + slack + MOE_ROWS - 1) // MOE_ROWS * MOE_ROWS, 0)
    pstarts = (jnp.cumsum(padded) - padded).astype(i32)
    n_blocks = n_tok * TOP_K // MOE_ROWS + N_EXPERTS + (N_EXPERTS * slack) // MOE_ROWS + 1
    n_used = (jnp.sum(padded) // MOE_ROWS).astype(i32).reshape(1)
    return (pstarts.reshape(N_EXPERTS, 1), pstarts // MOE_ROWS, (padded // MOE_ROWS).astype(i32), n_used,
            n_blocks * MOE_ROWS)


def _mixer_inputs(x2, mod, w_in, B, T):
    ct, st = _rope_tables(T)
    return _inproj(x2, mod, _pack_w_in(w_in), ct, st, B, T)


def _token_mixers(x2, mod, w_in, cmp_pe, cmp_w1, cmp_w2, pool_w, pool_scale, gla_wa, gla_ba, gla_norm, B, T):
    N = B * T
    (mq, mk, mv, km, nq, nqr, ks, kw, vs, vw, kvc, pool_u, gq, gk, gv, gg, misc) = _mixer_inputs(x2, mod, w_in, B, T)
    nt = T // MOBA_BLOCK
    kmh = km.reshape(B, nt, N_HEADS, HEAD_DIM).transpose(0, 2, 1, 3)
    mo = _moba(mq, mk, mv, jnp.pad(kmh, ((0, 0), (0, 0), (0, MOBA_NBLK - nt), (0, LANES - HEAD_DIM))), B, T)
    ns = T // CMP_STRIDE
    kc, vc = _cmp(kvc.reshape(B, ns, CMP_STRIDE * LANES), *_cmp_weights(cmp_pe, cmp_w1, cmp_w2))
    no = _nsa(nq, nqr, misc, ks, vs, kw, vw, kc, vc, _overlap_matrix(ns), B, T)
    wbd = jax.scipy.linalg.block_diag(*[pool_w[g] for g in range(len(POOL_WINDOWS))]).astype(bf16)
    po = _pool(pool_u, wbd, pool_scale.reshape(1, 256), B, T)
    wa = jnp.zeros((LANES, 256), f32).at[MISC_A0:MISC_A0 + GLA_LOWRANK].set(gla_wa)
    go = _gla(gq, gk, gv, gg, misc, wa, gla_ba.reshape(1, 256), jnp.tile(gla_norm, N_HEADS).reshape(1, 256),
              _block_diag_ones(), B, T)
    return mo, no, po, go


def _pad_w_out(w_out):
    wm = w_out[:GROUP_WIDTH].reshape(N_HEADS, HEAD_DIM, D_MODEL)
    wm = jnp.pad(wm, ((0, 0), (0, LANES - HEAD_DIM), (0, 0))).reshape(N_HEADS * LANES, D_MODEL)
    return jnp.concatenate([wm, w_out[GROUP_WIDTH:]], axis=0).astype(bf16)


def kernel(x, c, w_ada, b_ada, w_in, cmp_pe, cmp_w1, cmp_w2, pool_w, pool_scale, gla_wa, gla_ba, gla_norm, w_out,
           ln_g, ln_b, w_router, router_bias, exp_gate, exp_up, exp_down, sh_gate, sh_up, sh_down):
    B, T, D = x.shape
    N = B * T
    depth = w_ada.shape[0]
    alpha = float((2 * depth) ** 0.25)
    x2 = x.reshape(N, D)
    c8 = jnp.zeros((8, D), f32).at[:B].set(c)
    for l in range(depth):
        mod = _ada(c8, w_ada[l], b_ada[l].reshape(1, -1))[:B].reshape(B, 6, D)
        mo, no, po, go = _token_mixers(x2, mod, w_in[l], cmp_pe[l], cmp_w1[l], cmp_w2[l], pool_w[l], pool_scale[l],
                                       gla_wa[l], gla_ba[l], gla_norm[l], B, T)
        wrh, wrl = _split_bf16(w_router[l])
        x1, h2b, logits = _outproj(mo, no, po, go, x2, mod, _pad_w_out(w_out[l]), ln_g[l, 0].reshape(1, D),
                                   ln_b[l, 0].reshape(1, D), wrh, wrl, alpha, B, T)
        eidx, wts, counts, tile_base = _route(logits, router_bias[l].reshape(1, N_EXPERTS))
        pstarts, blk0, nblk, n_used, n_rows = _expert_layout(counts, N)
        pos, xs = _dispatch(h2b, eidx, tile_base, pstarts, n_rows)
        ys = _moe(l, blk0, nblk, n_used, xs, exp_gate, exp_up, exp_down)
        x2 = _fin(pos, ys, h2b, x1, wts, mod, sh_gate[l].astype(bf16), sh_up[l].astype(bf16),
                  sh_down[l].astype(bf16), ln_g[l, 1].reshape(1, D), ln_b[l, 1].reshape(1, D), alpha, B, T)
    return x2.reshape(B, T, D)
```

```python
import functools

import jax
import jax.numpy as jnp
from jax import lax
from jax.experimental import pallas as pl
from jax.experimental.pallas import tpu as pltpu

f32, bf16, i32, u32 = jnp.float32, jnp.bfloat16, jnp.int32, jnp.uint32

D_MODEL = 1024
HEAD_DIM = 64
N_HEADS = 4
GROUP_WIDTH = 256
ROPE_THETA = 500000.0
ROPE_DIM = 16
MOBA_BLOCK = 256
MOBA_TOPK = 3
CMP_BLOCK = 32
CMP_STRIDE = 16
CMP_HIDDEN = 128
SLC_BLOCK = 64
SLC_TOPN = 16
WIN = 512
FORCE_SCORE = 1e9
POOL_WINDOWS = (2, 4, 8, 16)
GLA_SUB = 16
GLA_LOWRANK = 16
GLA_TAU = 16.0
N_EXPERTS = 256
TOP_K = 8
N_EXPERT_GROUPS = 8
TOPK_GROUPS = 4
D_EXPERT = 256
ROUTED_SCALE = 2.5
LN_EPS = 1e-5
QK_SCALE = HEAD_DIM ** -0.5
LOG2E = 1.4426950408889634
Q_SCALE = QK_SCALE * LOG2E

LANES = 128
MASKED = -1e30
M_INIT = -3e38
BIG_IDX = 1 << 20
MOE_ROWS = 256
ROW_TILE = D_MODEL // (2 * LANES)
VMEM_LIMIT = 56 * 1024 * 1024

_OFF = dict(moba_q=0, moba_k=256, moba_v=512, nsa_q=768, k_cmp=1024, v_cmp=1088, k_slc=1152, v_slc=1216,
            k_win=1280, v_win=1344, nsa_gate=1408, pool=1420, gla_q=1676, gla_k=1932, gla_v=2188,
            gla_a=2444, gla_g=2460)
_S = dict(mq=0, mk=256, mv=512, nq=768, sw=1024, vsw=1152, kvc=1280, pool=1408, gq=1664, gk=1920, gv=2176,
          gg=2432, misc=2688)
IN_COLS_PACKED = 2816
MISC_GATE0 = 0
MISC_A0 = 12

NT = (((1,), (1,)), ((), ()))
TN = (((0,), (0,)), ((), ()))


def _cparams(*sem):
    return pltpu.CompilerParams(dimension_semantics=sem, vmem_limit_bytes=VMEM_LIMIT)


def _silu(x):
    return x * jax.nn.sigmoid(x)


def _split_bf16(x):
    hi = x.astype(bf16)
    lo = (x - hi.astype(f32)).astype(bf16)
    return hi, lo


def _pack_rows(x):
    bits = lambda v: lax.bitcast_convert_type(v.astype(bf16).astype(f32), u32)
    return [(bits(x[:, (2 * s) * LANES:(2 * s + 1) * LANES]) >> 16) | bits(x[:, (2 * s + 1) * LANES:(2 * s + 2) * LANES])
            for s in range(ROW_TILE)]


def _unpack_word(w):
    return (lax.bitcast_convert_type(w << 16, f32), lax.bitcast_convert_type(w & jnp.uint32(0xFFFF0000), f32))


def _layer_norm(z, g, b):
    mu = jnp.mean(z, axis=-1, keepdims=True)
    zc = z - mu
    var = jnp.mean(zc * zc, axis=-1, keepdims=True)
    return zc * lax.rsqrt(var + LN_EPS) * g + b


def _argmax_rounds(score, index, rounds, axis=1):
    picked = jnp.zeros(score.shape, f32)
    for _ in range(rounds):
        mx = jnp.max(score, axis=axis, keepdims=True)
        first = jnp.min(jnp.where(score == mx, index, BIG_IDX), axis=axis, keepdims=True)
        hit = index == first
        picked = jnp.where(hit, 1.0, picked)
        score = jnp.where(hit, -jnp.inf, score)
    return picked


def _softmax_steps_t(s_list, carries, v_list):
    m_new = [jnp.maximum(c[0], jnp.max(s, axis=0, keepdims=True)) for s, c in zip(s_list, carries)]
    p = [jnp.exp2(s - m) for s, m in zip(s_list, m_new)]
    pv = [jnp.dot(v, pi.astype(bf16), preferred_element_type=f32) for v, pi in zip(v_list, p)]
    out = []
    for c, m, pi, pvi in zip(carries, m_new, p, pv):
        alpha = jnp.exp2(c[0] - m)
        out.append((m, alpha * c[1] + jnp.sum(pi, axis=0, keepdims=True), alpha * c[2] + pvi))
    return tuple(out)


def _softmax_init_t(groups, queries):
    return tuple((jnp.full((1, queries), M_INIT, f32), jnp.zeros((1, queries), f32),
                  jnp.zeros((LANES, queries), f32)) for _ in range(groups))


def _ada_kernel(c_ref, w_ref, b_ref, o_ref):
    o_ref[...] = jnp.dot(_silu(c_ref[...]), w_ref[...], preferred_element_type=f32,
                         precision=lax.Precision.HIGHEST) + b_ref[...]


def _ada(c8, w, b):
    n = w.shape[1] // D_MODEL
    return pl.pallas_call(
        _ada_kernel, grid=(n,),
        in_specs=[pl.BlockSpec((8, D_MODEL), lambda j: (0, 0)),
                  pl.BlockSpec((D_MODEL, D_MODEL), lambda j: (0, j)),
                  pl.BlockSpec((1, D_MODEL), lambda j: (0, j))],
        out_specs=pl.BlockSpec((8, D_MODEL), lambda j: (0, j)),
        out_shape=jax.ShapeDtypeStruct((8, w.shape[1]), f32),
        compiler_params=_cparams("arbitrary"), name="ada")(c8, w, b)


IN_TM = MOBA_BLOCK


def _inproj_kernel(x_ref, mod_ref, w_ref, ct_ref, st_ref,
                   mq_ref, mk_ref, mv_ref, km_ref, nq_ref, nqr_ref, ks_ref, kw_ref, vs_ref, vw_ref,
                   kvc_ref, pool_ref, gq_ref, gk_ref, gv_ref, gg_ref, misc_ref, *, nt):
    tb = pl.program_id(0) % nt
    h = (x_ref[...] * (1.0 + mod_ref[0, 1:2, :]) + mod_ref[0, 0:1, :]).astype(bf16)

    def seg(name, width):
        a = _S[name]
        return jnp.dot(h, w_ref[:, a:a + width], preferred_element_type=f32)

    ct, st = ct_ref[...], st_ref[...]
    lane = lax.broadcasted_iota(i32, (IN_TM, LANES), 1)
    first8 = (lane % HEAD_DIM) < ROPE_DIM // 2
    half = lane < HEAD_DIM

    def rope128(y):
        partner = jnp.where(first8, pltpu.roll(y, LANES - 8, axis=1), pltpu.roll(y, 8, axis=1))
        return y * ct + partner * st

    def rope(y):
        return jnp.concatenate([rope128(y[:, c * LANES:(c + 1) * LANES]) for c in range(y.shape[1] // LANES)],
                               axis=1)

    def lo_half(y):
        return jnp.where(half, y, 0.0)

    def hi_half(y):
        return jnp.where(half, pltpu.roll(y, HEAD_DIM, axis=1), 0.0)

    def per_head(y):
        parts = []
        for c in range(2):
            yc = y[:, c * LANES:(c + 1) * LANES]
            parts += [lo_half(yc), hi_half(yc)]
        return jnp.concatenate(parts, axis=1)

    mq_ref[...] = per_head(rope(seg("mq", 256)) * Q_SCALE).astype(bf16)
    k = rope(seg("mk", 256))
    km_ref[0] = jnp.mean(k, axis=0, keepdims=True)
    lane4 = lax.broadcasted_iota(i32, (IN_TM, 4 * LANES), 1)
    mk_ref[...] = jnp.where((lane4 % LANES) == HEAD_DIM + tb, 1.0, per_head(k)).astype(bf16)
    mv_ref[...] = per_head(seg("mv", 256)).astype(bf16)
    q = seg("nq", 256) * Q_SCALE
    nq_ref[...] = per_head(q).astype(bf16)
    nqr_ref[...] = per_head(rope(q)).astype(bf16)
    sw = rope128(seg("sw", 128))
    row = lax.broadcasted_iota(i32, (IN_TM, LANES), 0)
    slc_id = tb * (IN_TM // SLC_BLOCK) + row // SLC_BLOCK
    ks_ref[...] = jnp.concatenate([lo_half(sw), jnp.where(lane == slc_id, 1.0, 0.0)], axis=1).astype(bf16)
    kw_ref[...] = hi_half(sw).astype(bf16)
    vsw = seg("vsw", 128)
    vs_ref[...] = lo_half(vsw).astype(bf16)
    vw_ref[...] = hi_half(vsw).astype(bf16)
    kvc_ref[...] = seg("kvc", 128).astype(bf16)
    pool_ref[...] = seg("pool", 256)
    gq_ref[...] = seg("gq", 256)
    gk_ref[...] = seg("gk", 256)
    gv_ref[...] = seg("gv", 256)
    gg_ref[...] = seg("gg", 256)
    misc_ref[...] = seg("misc", 128)


def _inproj(x2, mod, w, ct, st, B, T):
    N = B * T
    nt = T // IN_TM
    assert T % IN_TM == 0 and nt <= 32 and T // SLC_BLOCK <= LANES
    row = lambda w_, dt: (jax.ShapeDtypeStruct((N, w_), dt), pl.BlockSpec((IN_TM, w_), lambda i: (i, 0)))
    outs = [row(512, bf16), row(512, bf16), row(512, bf16),
            (jax.ShapeDtypeStruct((N // IN_TM, 1, 256), f32), pl.BlockSpec((1, 1, 256), lambda i: (i, 0, 0))),
            row(512, bf16), row(512, bf16), row(256, bf16), row(128, bf16), row(128, bf16), row(128, bf16),
            row(128, bf16), row(256, f32), row(256, f32), row(256, f32), row(256, f32), row(256, f32),
            row(128, f32)]
    return pl.pallas_call(
        functools.partial(_inproj_kernel, nt=nt), grid=(N // IN_TM,),
        in_specs=[pl.BlockSpec((IN_TM, D_MODEL), lambda i: (i, 0)),
                  pl.BlockSpec((1, 6, D_MODEL), lambda i: (i // nt, 0, 0)),
                  pl.BlockSpec((D_MODEL, IN_COLS_PACKED), lambda i: (0, 0), pipeline_mode=pl.Buffered(1)),
                  pl.BlockSpec((IN_TM, LANES), lambda i: (i % nt, 0)),
                  pl.BlockSpec((IN_TM, LANES), lambda i: (i % nt, 0))],
        out_specs=[o[1] for o in outs], out_shape=[o[0] for o in outs],
        compiler_params=_cparams("arbitrary"), name="inproj")(x2, mod, w, ct, st)


def _cmp_kernel(x_ref, wa_ref, wb_ref, pea_ref, peb_ref, w2k_ref, w2v_ref, kc_ref, vc_ref):
    x = x_ref[0].astype(f32)
    a = jnp.dot((x + pea_ref[...]).astype(bf16), wa_ref[...], preferred_element_type=f32)
    b = jnp.dot((x + peb_ref[...]).astype(bf16), wb_ref[...], preferred_element_type=f32)
    hid = a + pltpu.roll(b, x.shape[0] - 1, axis=0)
    g = jax.nn.gelu(hid).astype(bf16)
    kc_ref[0] = jnp.dot(g, w2k_ref[...], preferred_element_type=f32).astype(bf16)
    vc_ref[0] = jnp.dot(g, w2v_ref[...], preferred_element_type=f32).astype(bf16)


def _cmp(xseg, wa, wb, pea, peb, w2k, w2v):
    B, ns, wd = xseg.shape
    full = lambda a: pl.BlockSpec(a.shape, lambda b: (0,) * a.ndim)
    return pl.pallas_call(
        _cmp_kernel, grid=(B,),
        in_specs=[pl.BlockSpec((1, ns, wd), lambda b: (b, 0, 0))] + [full(a) for a in (wa, wb, pea, peb, w2k, w2v)],
        out_specs=[pl.BlockSpec((1, ns, LANES), lambda b: (b, 0, 0))] * 2,
        out_shape=[jax.ShapeDtypeStruct((B, ns, LANES), bf16)] * 2,
        compiler_params=_cparams("arbitrary"), name="nsa_compress")(xseg, wa, wb, pea, peb, w2k, w2v)


MOBA_NBLK = 32
KEY_TILE = 512
MOBA_TQ = 512


def _moba_kernel(q_ref, k_ref, vt_ref, km_ref, o_ref):
    q0 = pl.program_id(1) * MOBA_TQ
    heads = range(N_HEADS)
    hl = lambda h: slice(h * LANES, (h + 1) * LANES)
    blk = lax.broadcasted_iota(i32, (MOBA_NBLK, MOBA_TQ), 0)
    own = (q0 + lax.broadcasted_iota(i32, (1, MOBA_TQ), 1)) // MOBA_BLOCK
    past = blk < own
    zeros = lambda n: jnp.zeros((n, MOBA_TQ), f32)
    qf = []
    for h in heads:
        qh = q_ref[:, hl(h)]
        hi, lo = _split_bf16(km_ref[0, h])
        gate_t = (lax.dot_general(hi, qh, NT, preferred_element_type=f32)
                  + lax.dot_general(lo, qh, NT, preferred_element_type=f32))
        picked = _argmax_rounds(jnp.where(past, gate_t, -jnp.inf), blk, MOBA_TOPK, axis=0)
        allowed = ((picked > 0.0) & past) | (blk == own)
        bias_t = jnp.concatenate([zeros(HEAD_DIM), jnp.where(allowed, 0.0, MASKED),
                                  zeros(LANES - HEAD_DIM - MOBA_NBLK)], axis=0)
        qf.append(qh + jnp.transpose(bias_t).astype(bf16))

    def scores_t(h, off):
        return lax.dot_general(k_ref[pl.ds(off, KEY_TILE), hl(h)], qf[h], NT, preferred_element_type=f32)

    def body(p, carry):
        off = pl.multiple_of(p * KEY_TILE, KEY_TILE)
        return _softmax_steps_t([scores_t(h, off) for h in heads], carry, [vt_ref[0, p, hl(h), :] for h in heads])

    last = q0 // KEY_TILE
    carry = lax.fori_loop(0, last, body, _softmax_init_t(N_HEADS, MOBA_TQ))
    off = pl.multiple_of(last * KEY_TILE, KEY_TILE)
    kpos = off + lax.broadcasted_iota(i32, (KEY_TILE, MOBA_TQ), 0)
    qpos = q0 + lax.broadcasted_iota(i32, (KEY_TILE, MOBA_TQ), 1)
    carry = _softmax_steps_t([jnp.where(kpos <= qpos, scores_t(h, off), MASKED) for h in heads], carry,
                             [vt_ref[0, last, hl(h), :] for h in heads])
    for h in heads:
        o_ref[:, hl(h)] = jnp.transpose(carry[h][2] / carry[h][1]).astype(bf16)


def _blocks_t(v, B, T, blk):
    return v.reshape(B, T // blk, blk, v.shape[1]).transpose(0, 1, 3, 2)


def _moba(mq, mk, mv, km, B, T):
    N = B * T
    nt = T // MOBA_TQ
    wd = N_HEADS * LANES
    assert T % KEY_TILE == 0 and KEY_TILE % MOBA_TQ == 0 and T // MOBA_BLOCK <= MOBA_NBLK
    return pl.pallas_call(
        _moba_kernel, grid=(B, nt),
        in_specs=[pl.BlockSpec((MOBA_TQ, wd), lambda b, i: (b * nt + i, 0)),
                  pl.BlockSpec((T, wd), lambda b, i: (b, 0), pipeline_mode=pl.Buffered(1)),
                  pl.BlockSpec((1, T // KEY_TILE, wd, KEY_TILE), lambda b, i: (b, 0, 0, 0),
                               pipeline_mode=pl.Buffered(1)),
                  pl.BlockSpec((1, N_HEADS, MOBA_NBLK, LANES), lambda b, i: (b, 0, 0, 0))],
        out_specs=pl.BlockSpec((MOBA_TQ, wd), lambda b, i: (b * nt + i, 0)),
        out_shape=jax.ShapeDtypeStruct((N, wd), bf16),
        compiler_params=_cparams("arbitrary", "arbitrary"), name="moba")(mq, mk, _blocks_t(mv, B, T, KEY_TILE), km)


NSA_TQ = 512
NSA_KB = 256


def _stack_heads(ref):
    return jnp.concatenate([ref[:, h * LANES:(h + 1) * LANES] for h in range(N_HEADS)], axis=0)


def _nsa_select_kernel(nq_ref, kc_ref, vct_ref, ovt_ref, oc_ref, selb_ref, *, n_cmp):
    c = pl.program_id(1)
    rows = N_HEADS * NSA_TQ
    s = lax.dot_general(kc_ref[0], _stack_heads(nq_ref), NT, preferred_element_type=f32)
    n = lax.broadcasted_iota(i32, s.shape, 0)
    qpos_t = c * NSA_TQ + lax.broadcasted_iota(i32, (1, rows), 1) % NSA_TQ
    ok = (n * CMP_STRIDE + (CMP_BLOCK - 1) <= qpos_t) & (n < n_cmp)
    s = jnp.where(ok, s, -jnp.inf)
    m = jnp.max(s, axis=0, keepdims=True)
    m = jnp.where(m > -jnp.inf, m, 0.0)
    e = jnp.where(ok, jnp.exp2(s - m), 0.0)
    p_c = e / jnp.maximum(jnp.sum(e, axis=0, keepdims=True), 1e-30)
    o_c_t = jnp.dot(vct_ref[0], p_c.astype(bf16), preferred_element_type=f32)
    hq = lambda h: slice(h * NSA_TQ, (h + 1) * NSA_TQ)
    for h in range(N_HEADS):
        oc_ref[:, h * LANES:(h + 1) * LANES] = jnp.transpose(o_c_t[:, hq(h)]).astype(bf16)
    hi, lo = _split_bf16((p_c[:, hq(0)] + p_c[:, hq(1)]) + (p_c[:, hq(2)] + p_c[:, hq(3)]))
    imp = (jnp.dot(ovt_ref[...], hi, preferred_element_type=f32)
           + jnp.dot(ovt_ref[...], lo, preferred_element_type=f32))
    j = lax.broadcasted_iota(i32, imp.shape, 0)
    cur = (c * NSA_TQ + lax.broadcasted_iota(i32, (1, NSA_TQ), 1)) // SLC_BLOCK
    forced = (j == 0) | (j == cur) | (j == cur - 1)
    valid = j <= cur
    score = jnp.where(valid, jnp.where(forced, FORCE_SCORE, imp), -jnp.inf)
    chosen = (_argmax_rounds(score, j, SLC_TOPN, axis=0) > 0.0) & valid
    selb_ref[...] = jnp.transpose(jnp.where(chosen, 0.0, MASKED)).astype(bf16)


def _nsa_attend_kernel(nqr_ref, selb_ref, oc_ref, misc_ref, ks_ref, vst_ref, kw_ref, vwt_ref, o_ref):
    c = pl.program_id(1)
    rows = N_HEADS * NSA_TQ
    heads = range(N_HEADS)
    q4r = _stack_heads(nqr_ref)
    qpos_t = c * NSA_TQ + lax.broadcasted_iota(i32, (1, rows), 1) % NSA_TQ
    selb = selb_ref[...]
    lhs = [jnp.concatenate([q4r[h * NSA_TQ:(h + 1) * NSA_TQ], selb], axis=1) for h in heads]

    def scores_t(off):
        kb = ks_ref[pl.ds(off, KEY_TILE), :]
        return [lax.dot_general(kb, lhs[h], NT, preferred_element_type=f32) for h in heads]

    def body(p, carry):
        return _softmax_steps_t(scores_t(pl.multiple_of(p * KEY_TILE, KEY_TILE)), carry, [vst_ref[0, p]] * N_HEADS)

    last = (c * NSA_TQ) // KEY_TILE
    carry = lax.fori_loop(0, last, body, _softmax_init_t(N_HEADS, NSA_TQ))
    off = pl.multiple_of(last * KEY_TILE, KEY_TILE)
    kpos = off + lax.broadcasted_iota(i32, (KEY_TILE, NSA_TQ), 0)
    qpos = c * NSA_TQ + lax.broadcasted_iota(i32, (KEY_TILE, NSA_TQ), 1)
    carry = _softmax_steps_t([jnp.where(kpos <= qpos, s, MASKED) for s in scores_t(off)], carry,
                             [vst_ref[0, last]] * N_HEADS)
    o_s = jnp.concatenate([jnp.transpose(carry[h][2] / carry[h][1]) for h in heads], axis=0)

    nwb = (WIN + NSA_TQ) // NSA_KB
    sb = jnp.maximum(c * (NSA_TQ // NSA_KB) - WIN // NSA_KB, 0)
    start = pl.multiple_of(sb * NSA_KB, NSA_KB)
    s_w = lax.dot_general(kw_ref[pl.ds(start, nwb * NSA_KB), :], q4r, NT, preferred_element_type=f32)
    wpos = start + lax.broadcasted_iota(i32, s_w.shape, 0)
    s_w = jnp.where((wpos <= qpos_t) & (wpos > qpos_t - WIN), s_w, -jnp.inf)
    e_w = jnp.exp2(s_w - jnp.max(s_w, axis=0, keepdims=True))
    p_w = (e_w / jnp.sum(e_w, axis=0, keepdims=True)).astype(bf16)
    o_w_t = jnp.dot(vwt_ref[0, sb], p_w[0:NSA_KB], preferred_element_type=f32)
    for i in range(1, nwb):
        o_w_t = o_w_t + jnp.dot(vwt_ref[0, sb + i], p_w[i * NSA_KB:(i + 1) * NSA_KB], preferred_element_type=f32)
    o_w = jnp.transpose(o_w_t)

    gates = jax.nn.sigmoid(misc_ref[...])
    gl = lax.broadcasted_iota(i32, gates.shape, 1)

    def gate_col(g):
        return jnp.concatenate([jnp.sum(jnp.where(gl == MISC_GATE0 + 3 * h + g, gates, 0.0), axis=1, keepdims=True)
                                for h in range(N_HEADS)], axis=0)

    out = gate_col(0) * _stack_heads(oc_ref).astype(f32) + gate_col(1) * o_s + gate_col(2) * o_w
    pair = lambda a, b: a + pltpu.roll(b, HEAD_DIM, axis=1)
    o_ref[...] = jnp.concatenate([pair(out[0:NSA_TQ], out[NSA_TQ:2 * NSA_TQ]),
                                  pair(out[2 * NSA_TQ:3 * NSA_TQ], out[3 * NSA_TQ:])], axis=1).astype(bf16)


def _nsa(nq, nqr, misc, ks, vs, kw, vw, kc, vc, ov, B, T):
    N = B * T
    nc = T // NSA_TQ
    ns = kc.shape[1]
    assert T >= NSA_TQ + WIN and T % KEY_TILE == 0 and KEY_TILE % NSA_TQ == 0 and NSA_TQ % NSA_KB == 0
    assert WIN % NSA_KB == 0
    tok = lambda w_: pl.BlockSpec((NSA_TQ, w_), lambda b, c: (b * nc + c, 0))
    seq = lambda w_: pl.BlockSpec((T, w_), lambda b, c: (b, 0), pipeline_mode=pl.Buffered(1))
    o_c, selb = pl.pallas_call(
        functools.partial(_nsa_select_kernel, n_cmp=(T - CMP_BLOCK) // CMP_STRIDE + 1), grid=(B, nc),
        in_specs=[tok(512), pl.BlockSpec((1, ns, LANES), lambda b, c: (b, 0, 0)),
                  pl.BlockSpec((1, LANES, ns), lambda b, c: (b, 0, 0)), pl.BlockSpec((LANES, ns), lambda b, c: (0, 0))],
        out_specs=[tok(512), tok(LANES)],
        out_shape=[jax.ShapeDtypeStruct((N, 512), bf16), jax.ShapeDtypeStruct((N, LANES), bf16)],
        compiler_params=_cparams("arbitrary", "arbitrary"), name="nsa_select")(
            nq, kc, vc.transpose(0, 2, 1), ov.T)
    seq_t = lambda kb: pl.BlockSpec((1, T // kb, LANES, kb), lambda b, c: (b, 0, 0, 0), pipeline_mode=pl.Buffered(1))
    return pl.pallas_call(
        _nsa_attend_kernel, grid=(B, nc),
        in_specs=[tok(512), tok(LANES), tok(512), tok(LANES), seq(256), seq_t(KEY_TILE), seq(LANES), seq_t(NSA_KB)],
        out_specs=tok(256), out_shape=jax.ShapeDtypeStruct((N, 256), bf16),
        compiler_params=_cparams("arbitrary", "arbitrary"), name="nsa_attend")(
            nqr, selb, o_c, misc, ks, _blocks_t(vs, B, T, KEY_TILE), kw, _blocks_t(vw, B, T, NSA_KB))


POOL_TM = 512
POOL_HALO = 16


def _pool_kernel(u_ref, halo_ref, w_ref, sc_ref, o_ref):
    t = pl.program_id(1)
    halo = jnp.where(t == 0, 0.0, halo_ref[...])
    ext = jnp.concatenate([halo, u_ref[...]], axis=0)
    s2 = ext + pltpu.roll(ext, 1, axis=0)
    s4 = s2 + pltpu.roll(s2, 2, axis=0)
    s8 = s4 + pltpu.roll(s4, 4, axis=0)
    s16 = s8 + pltpu.roll(s8, 8, axis=0)
    pos1 = jnp.maximum(t * POOL_TM - POOL_HALO + 1 + lax.broadcasted_iota(i32, ext.shape, 0), 1).astype(f32)
    grp = lax.broadcasted_iota(i32, ext.shape, 1) // HEAD_DIM
    mean = jnp.where(grp == 0, s2 / jnp.minimum(pos1, 2.0),
                     jnp.where(grp == 1, s4 / jnp.minimum(pos1, 4.0),
                               jnp.where(grp == 2, s8 / jnp.minimum(pos1, 8.0), s16 / jnp.minimum(pos1, 16.0))))
    pooled = (mean - ext)[POOL_HALO:, :]
    o_ref[...] = (jnp.dot(pooled.astype(bf16), w_ref[...], preferred_element_type=f32) * sc_ref[...]).astype(bf16)


def _pool(u, wbd, scale, B, T):
    N = B * T
    tm = min(POOL_TM, T)
    assert tm == POOL_TM and T % POOL_TM == 0
    nt = T // tm
    hb = tm // POOL_HALO
    return pl.pallas_call(
        _pool_kernel, grid=(B, nt),
        in_specs=[pl.BlockSpec((tm, 256), lambda b, t: (b * nt + t, 0)),
                  pl.BlockSpec((POOL_HALO, 256), lambda b, t: (jnp.maximum((b * nt + t) * hb - 1, 0), 0)),
                  pl.BlockSpec((256, 256), lambda b, t: (0, 0)),
                  pl.BlockSpec((1, 256), lambda b, t: (0, 0))],
        out_specs=pl.BlockSpec((tm, 256), lambda b, t: (b * nt + t, 0)),
        out_shape=jax.ShapeDtypeStruct((N, 256), bf16),
        compiler_params=_cparams("arbitrary", "arbitrary"), name="pool")(u, u, wbd, scale)


GLA_TM = 256


def _gla_kernel(q_ref, k_ref, v_ref, g_ref, misc_ref, wa_ref, ba_ref, gn_ref, bd_ref, o_ref,
                st_ref, q_s, k_s, v_s, b_s, qe_s, ke_s, gam_s, o_s):
    nb = q_ref.shape[0]

    @pl.when(pl.program_id(0) == 0)
    def _():
        st_ref[...] = jnp.zeros_like(st_ref)

    r16 = lax.broadcasted_iota(i32, (GLA_TM, 256), 0) % GLA_SUB
    for bi_ in range(nb):
        x = jnp.dot(misc_ref[bi_], wa_ref[...], preferred_element_type=f32,
                    precision=lax.Precision.HIGHEST) + ba_ref[...]
        log_a = (jnp.minimum(x, 0.0) - jnp.log1p(jnp.exp(-jnp.abs(x)))) / GLA_TAU
        b = log_a
        for s in (1, 2, 4, 8):
            b = b + jnp.where(r16 >= s, pltpu.roll(b, s, axis=0), 0.0)
        b_end = jnp.where(r16 == GLA_SUB - 1, b, 0.0)
        for s in (1, 2, 4, 8):
            b_end = b_end + pltpu.roll(b_end, GLA_TM - s, axis=0)
        q = q_ref[bi_] * QK_SCALE
        k = k_ref[bi_]
        q_s[bi_] = q
        k_s[bi_] = k
        v_s[bi_] = v_ref[bi_]
        b_s[bi_] = b
        qe_s[bi_] = (q * jnp.exp(b)).astype(bf16)
        ke_s[bi_] = (k * jnp.exp(b_end - b)).astype(bf16)
        gam_s[bi_] = jnp.exp(b_end)
    bd = bd_ref[...]
    shape3 = (GLA_SUB, GLA_SUB, 256)
    causal = lax.broadcasted_iota(i32, shape3, 0) <= lax.broadcasted_iota(i32, shape3, 1)

    def block(n, _):
        r0 = pl.multiple_of(n * GLA_SUB, GLA_SUB)
        rows = pl.ds(r0, GLA_SUB)
        for bi_ in range(nb):
            qi, ki, vi, bi = q_s[bi_, rows, :], k_s[bi_, rows, :], v_s[bi_, rows, :], b_s[bi_, rows, :]
            diff = jnp.where(causal, bi[None, :, :] - bi[:, None, :], 0.0)
            w3 = jnp.where(causal, qi[None, :, :] * ki[:, None, :] * jnp.exp(diff), 0.0)
            a3 = jnp.dot(w3.reshape(GLA_SUB * GLA_SUB, 256).astype(bf16), bd, preferred_element_type=f32)
            intra = jnp.sum(a3.reshape(shape3) * vi[:, None, :], axis=0)
            st = st_ref[bi_]
            inter = lax.dot_general(qe_s[bi_, rows, :], st.astype(bf16), NT, preferred_element_type=f32)
            o_s[bi_, rows, :] = intra + inter
            upd = lax.dot_general(vi.astype(bf16), ke_s[bi_, rows, :], TN, preferred_element_type=f32)
            st_ref[bi_] = st * gam_s[bi_, pl.ds(r0, 1), :] + jnp.where(bd > 0, upd, 0.0)
        return 0

    lax.fori_loop(0, GLA_TM // GLA_SUB, block, 0)
    for bi_ in range(nb):
        o = o_s[bi_]
        ms = jnp.dot(o * o, bd.astype(f32), preferred_element_type=f32, precision=lax.Precision.HIGHEST) / HEAD_DIM
        o_ref[bi_] = (o * lax.rsqrt(ms + LN_EPS) * gn_ref[...] * _silu(g_ref[bi_])).astype(bf16)


def _gla(gq, gk, gv, gg, misc, wa, ba, gn, bd, B, T):
    N = B * T
    assert T % GLA_TM == 0
    tok = lambda w_: pl.BlockSpec((B, GLA_TM, w_), lambda t: (0, t, 0))
    full = lambda a: pl.BlockSpec(a.shape, lambda t: (0, 0))
    v = lambda dt: pltpu.VMEM((B, GLA_TM, 256), dt)
    seq = lambda a: a.reshape(B, T, a.shape[1])
    out = pl.pallas_call(
        _gla_kernel, grid=(T // GLA_TM,),
        in_specs=[tok(256), tok(256), tok(256), tok(256), tok(128), full(wa), full(ba), full(gn), full(bd)],
        out_specs=tok(256), out_shape=jax.ShapeDtypeStruct((B, T, 256), bf16),
        scratch_shapes=[pltpu.VMEM((B, 256, 256), f32), v(f32), v(f32), v(f32), v(f32), v(bf16), v(bf16), v(f32),
                        v(f32)],
        compiler_params=_cparams("arbitrary"), name="gla")(seq(gq), seq(gk), seq(gv), seq(gg), seq(misc), wa, ba, gn, bd)
    return out.reshape(N, 256)


OUT_TM = 256


def _outproj_kernel(mo_ref, no_ref, po_ref, go_ref, x_ref, mod_ref, w_ref, lng_ref, lnb_ref, wrh_ref, wrl_ref,
                    x1_ref, h2b_ref, lg_ref, *, alpha):
    a = jnp.concatenate([mo_ref[...], no_ref[...], po_ref[...], go_ref[...]], axis=1)
    y = jnp.dot(a, w_ref[...], preferred_element_type=f32)
    x1 = _layer_norm(alpha * x_ref[...] + mod_ref[0, 2:3, :] * y, lng_ref[...], lnb_ref[...])
    x1_ref[...] = x1
    h2 = x1 * (1.0 + mod_ref[0, 4:5, :]) + mod_ref[0, 3:4, :]
    h2b_ref[...] = h2.astype(bf16)
    hi, lo = _split_bf16(h2)
    wrh = wrh_ref[...]
    lg_ref[...] = (jnp.dot(hi, wrh, preferred_element_type=f32) + jnp.dot(lo, wrh, preferred_element_type=f32)
                   + jnp.dot(hi, wrl_ref[...], preferred_element_type=f32))


def _outproj(mo, no, po, go, x2, mod, w, lng, lnb, wrh, wrl, alpha, B, T):
    N = B * T
    nt = T // OUT_TM
    tok = lambda w_: pl.BlockSpec((OUT_TM, w_), lambda i: (i, 0))
    full = lambda a: pl.BlockSpec(a.shape, lambda i: (0,) * a.ndim)
    return pl.pallas_call(
        functools.partial(_outproj_kernel, alpha=alpha), grid=(N // OUT_TM,),
        in_specs=[tok(512), tok(256), tok(256), tok(256), tok(D_MODEL),
                  pl.BlockSpec((1, 6, D_MODEL), lambda i: (i // nt, 0, 0)),
                  full(w), full(lng), full(lnb), full(wrh), full(wrl)],
        out_specs=[tok(D_MODEL), tok(D_MODEL), tok(N_EXPERTS)],
        out_shape=[jax.ShapeDtypeStruct((N, D_MODEL), f32), jax.ShapeDtypeStruct((N, D_MODEL), bf16),
                   jax.ShapeDtypeStruct((N, N_EXPERTS), f32)],
        compiler_params=_cparams("arbitrary"), name="outproj")(mo, no, po, go, x2, mod, w, lng, lnb, wrh, wrl)


ROUTE_TM = 256
GROUP_SIZE = N_EXPERTS // N_EXPERT_GROUPS


def _per_token(rows):
    pad = jnp.zeros((LANES - len(rows), rows[0].shape[1]), f32)
    return jnp.transpose(jnp.concatenate(rows + [pad], axis=0))


def _route_kernel(lg_ref, rb_ref, ei_ref, wt_ref, cnt_ref, tb_ref, base_ref):
    @pl.when(pl.program_id(0) == 0)
    def _():
        base_ref[...] = jnp.zeros_like(base_ref)

    tb_ref[0] = base_ref[...].astype(i32)

    s = jax.nn.sigmoid(jnp.transpose(lg_ref[...]))
    ssel = s + rb_ref[...]
    shape3 = (N_EXPERT_GROUPS, GROUP_SIZE, ROUTE_TM)
    x3 = ssel.reshape(shape3)
    i3 = lax.broadcasted_iota(i32, shape3, 1)
    m1 = jnp.max(x3, axis=1, keepdims=True)
    first = jnp.min(jnp.where(x3 == m1, i3, BIG_IDX), axis=1, keepdims=True)
    m2 = jnp.max(jnp.where(i3 == first, -jnp.inf, x3), axis=1, keepdims=True)
    gscore = (m1 + m2).reshape(N_EXPERT_GROUPS, ROUTE_TM)
    gid = lax.broadcasted_iota(i32, gscore.shape, 0)
    beaten = jnp.zeros(gscore.shape, i32)
    for g in range(N_EXPERT_GROUPS):
        other = gscore[g:g + 1, :]
        beaten = beaten + jnp.where((other > gscore) | ((other == gscore) & (g < gid)), 1, 0)
    keep = jnp.broadcast_to((beaten < TOPK_GROUPS)[:, None, :], shape3).reshape(ssel.shape)
    x = jnp.where(keep, ssel, -jnp.inf)
    eid = lax.broadcasted_iota(i32, x.shape, 0)
    hits, idx_rows, w_rows = [], [], []
    for _ in range(TOP_K):
        mx = jnp.max(x, axis=0, keepdims=True)
        idx = jnp.min(jnp.where(x == mx, eid, BIG_IDX), axis=0, keepdims=True)
        hit = eid == idx
        hits.append(hit)
        idx_rows.append(idx.astype(f32))
        w_rows.append(jnp.sum(jnp.where(hit, s, 0.0), axis=0, keepdims=True))
        x = jnp.where(hit, -jnp.inf, x)
    wsum = w_rows[0]
    for w in w_rows[1:]:
        wsum = wsum + w
    chosen = jnp.zeros(s.shape, f32)
    for hit in hits:
        chosen = jnp.where(hit, 1.0, chosen)
    total = base_ref[...] + jnp.sum(chosen, axis=1, keepdims=True)
    base_ref[...] = total
    cnt_ref[...] = total.astype(i32)
    ei_ref[...] = _per_token(idx_rows).astype(i32)
    wt_ref[...] = _per_token([w / wsum * ROUTED_SCALE for w in w_rows])


def _route(logits, rb):
    N = logits.shape[0]
    tok = pl.BlockSpec((ROUTE_TM, LANES), lambda i: (i, 0))
    col = pl.BlockSpec((N_EXPERTS, 1), lambda i: (0, 0))
    return pl.pallas_call(
        _route_kernel, grid=(N // ROUTE_TM,),
        in_specs=[pl.BlockSpec((ROUTE_TM, N_EXPERTS), lambda i: (i, 0)), col],
        out_specs=[tok, tok, col, pl.BlockSpec((1, N_EXPERTS, 1), lambda i: (i, 0, 0))],
        out_shape=[jax.ShapeDtypeStruct((N, LANES), i32), jax.ShapeDtypeStruct((N, LANES), f32),
                   jax.ShapeDtypeStruct((N_EXPERTS, 1), i32), jax.ShapeDtypeStruct((N // ROUTE_TM, N_EXPERTS, 1), i32)],
        scratch_shapes=[pltpu.VMEM((N_EXPERTS, 1), f32)],
        compiler_params=_cparams("arbitrary"), name="route")(logits, rb.reshape(N_EXPERTS, 1))


DISP_TM = ROUTE_TM
DISP_SLOTS = DISP_TM * TOP_K
DISP_CHUNK = 16
DISP_PERM_ROWS = 512
DISP_MAX_CHUNKS = DISP_SLOTS // DISP_CHUNK + N_EXPERTS


def _dispatch_kernel(x_ref, ei_ref, tb_ref, ps_ref, triu_ref, tril_ref, xs_in, pos_ref, xs_hbm,
                     pbuf, tab_v, tab_s, csem, rsem):
    del xs_in
    i = pl.program_id(0)
    slot = i % 2

    @pl.when(i == 0)
    def _():
        pbuf[:, DISP_SLOTS * ROW_TILE:, :] = jnp.zeros((2, DISP_CHUNK * ROW_TILE, LANES), u32)

    ei_t = jnp.transpose(ei_ref[...].astype(f32))
    eid = lax.broadcasted_iota(i32, (N_EXPERTS, DISP_TM), 0).astype(f32)
    hits = [eid == ei_t[k:k + 1, :] for k in range(TOP_K)]
    member = jnp.zeros(eid.shape, f32)
    for hit in hits:
        member = jnp.where(hit, 1.0, member)
    prefix = jnp.dot(member.astype(bf16), triu_ref[...], preferred_element_type=f32)
    cnt = jnp.sum(member, axis=1, keepdims=True)
    nch = jnp.floor((cnt + (DISP_CHUNK - 1)) * (1.0 / DISP_CHUNK))
    lane = lax.broadcasted_iota(i32, (N_EXPERTS, LANES), 1)
    before = jnp.dot(tril_ref[...], jnp.where(lane == 0, cnt, jnp.where(lane == 1, nch, 0.0)).astype(bf16),
                     preferred_element_type=f32)
    off, cidx = before[:, 0:1], before[:, 1:2]
    dst = (ps_ref[...] + tb_ref[0]).astype(f32)
    slot_rows = [jnp.sum(jnp.where(hit, off + prefix, 0.0), axis=0, keepdims=True) for hit in hits]
    pos_ref[...] = _per_token([jnp.sum(jnp.where(hit, dst + prefix, 0.0), axis=0, keepdims=True)
                               for hit in hits]).astype(i32)
    rel = lax.broadcasted_iota(i32, (N_EXPERTS, DISP_MAX_CHUNKS), 1).astype(f32) - cidx
    mine = (rel >= 0.0) & (rel < nch)
    src_row = jnp.sum(jnp.where(mine, off + DISP_CHUNK * rel, 0.0), axis=0, keepdims=True)
    dst_row = jnp.sum(jnp.where(mine, dst + DISP_CHUNK * rel, 0.0), axis=0, keepdims=True)
    n_row = jnp.broadcast_to(jnp.sum(nch, axis=0, keepdims=True), src_row.shape)
    tab_v[...] = jnp.concatenate([src_row, dst_row, n_row, jnp.zeros((5, DISP_MAX_CHUNKS), f32)], axis=0).astype(i32)
    cp = pltpu.make_async_copy(tab_v, tab_s.at[slot], csem)
    cp.start()

    x = x_ref[...]
    for r in range(DISP_SLOTS // DISP_PERM_ROWS):
        sid = (r * DISP_PERM_ROWS + lax.broadcasted_iota(i32, (DISP_PERM_ROWS, DISP_TM), 0)).astype(f32)
        perm = jnp.zeros(sid.shape, f32)
        for srow in slot_rows:
            perm = jnp.where(sid == srow, 1.0, perm)
        rows = jnp.dot(perm.astype(bf16), x, preferred_element_type=f32)
        for s, w in enumerate(_pack_rows(rows)):
            pbuf[slot, pl.ds(r * DISP_PERM_ROWS * ROW_TILE + s, DISP_PERM_ROWS, stride=ROW_TILE), :] = w
    cp.wait()

    def chunk_copy(sl, src, dst_):
        span = lambda r0: pl.ds(pl.multiple_of(r0 * ROW_TILE, ROW_TILE), DISP_CHUNK * ROW_TILE)
        return pltpu.make_async_copy(pbuf.at[sl, span(src)], xs_hbm.at[span(dst_)], rsem.at[sl])

    def start_all(sl):
        def body(j, _):
            chunk_copy(sl, tab_s[sl, 0, j], tab_s[sl, 1, j]).start()
            return 0
        lax.fori_loop(0, tab_s[sl, 2, 0], body, 0)

    def wait_all(sl):
        def body(j, _):
            chunk_copy(sl, 0, 0).wait()
            return 0
        lax.fori_loop(0, tab_s[sl, 2, 0], body, 0)

    @pl.when(i > 0)
    def _():
        wait_all(1 - slot)
    start_all(slot)

    @pl.when(i == pl.num_programs(0) - 1)
    def _():
        wait_all(slot)


def _dispatch(h2b, eidx, tile_base, pstarts, n_rows):
    N = h2b.shape[0]
    r_, c_ = jnp.arange(DISP_TM)[:, None], jnp.arange(DISP_TM)[None, :]
    triu = (r_ < c_).astype(bf16)
    e_, f_ = jnp.arange(N_EXPERTS)[:, None], jnp.arange(N_EXPERTS)[None, :]
    tril = (f_ < e_).astype(bf16)
    tok = pl.BlockSpec((DISP_TM, LANES), lambda i: (i, 0))
    full = lambda a: pl.BlockSpec(a.shape, lambda i: (0,) * a.ndim)
    xs0 = jnp.zeros((n_rows * ROW_TILE, LANES), u32)
    pos, xs = pl.pallas_call(
        _dispatch_kernel, grid=(N // DISP_TM,),
        in_specs=[pl.BlockSpec((DISP_TM, D_MODEL), lambda i: (i, 0)), tok,
                  pl.BlockSpec((1, N_EXPERTS, 1), lambda i: (i, 0, 0)), full(pstarts), full(triu), full(tril),
                  pl.BlockSpec(memory_space=pl.ANY)],
        out_specs=[tok, pl.BlockSpec(memory_space=pl.ANY)],
        out_shape=[jax.ShapeDtypeStruct((N, LANES), i32), jax.ShapeDtypeStruct(xs0.shape, u32)],
        scratch_shapes=[pltpu.VMEM((2, (DISP_SLOTS + DISP_CHUNK) * ROW_TILE, LANES), u32),
                        pltpu.VMEM((8, DISP_MAX_CHUNKS), i32), pltpu.SMEM((2, 8, DISP_MAX_CHUNKS), i32),
                        pltpu.SemaphoreType.DMA, pltpu.SemaphoreType.DMA((2,))],
        input_output_aliases={6: 1},
        compiler_params=_cparams("arbitrary"), name="moe_dispatch")(h2b, eidx, tile_base, pstarts, triu, tril, xs0)
    return pos, xs


MOE_BLK = MOE_ROWS * ROW_TILE
MOE_IN_BUFS = 6
MOE_OUT_BUFS = 4


def _moe_kernel(b0_ref, nb_ref, nu_ref, xs_in, wg_ref, wu_ref, wd_ref, xs_out, xbuf, ybuf, isem, osem):
    e = pl.program_id(0)
    nu = nu_ref[0]

    def in_copy(b):
        return pltpu.make_async_copy(xs_in.at[pl.ds(pl.multiple_of(b * MOE_BLK, MOE_BLK), MOE_BLK)],
                                     xbuf.at[b % MOE_IN_BUFS], isem.at[b % MOE_IN_BUFS])

    def out_copy(b):
        return pltpu.make_async_copy(ybuf.at[b % MOE_OUT_BUFS],
                                     xs_out.at[pl.ds(pl.multiple_of(b * MOE_BLK, MOE_BLK), MOE_BLK)],
                                     osem.at[b % MOE_OUT_BUFS])

    @pl.when(e == 0)
    def _():
        for j in range(MOE_IN_BUFS - 1):
            @pl.when(j < nu)
            def _():
                in_copy(j).start()

    wg = wg_ref[0, 0].astype(bf16)
    wu = wu_ref[0, 0].astype(bf16)
    wd = wd_ref[0, 0].astype(bf16)

    def block(i, _):
        b = b0_ref[e] + i
        in_copy(b).wait()

        @pl.when(b + MOE_IN_BUFS - 1 < nu)
        def _():
            in_copy(b + MOE_IN_BUFS - 1).start()

        @pl.when(b >= MOE_OUT_BUFS)
        def _():
            out_copy(b - MOE_OUT_BUFS).wait()

        g = jnp.zeros((MOE_ROWS, D_EXPERT), f32)
        u = jnp.zeros((MOE_ROWS, D_EXPERT), f32)
        for s in range(ROW_TILE):
            word = xbuf[b % MOE_IN_BUFS, pl.ds(s, MOE_ROWS, stride=ROW_TILE), :]
            xc = jnp.concatenate(_unpack_word(word), axis=1).astype(bf16)
            rows = slice(2 * s * LANES, (2 * s + 2) * LANES)
            g = g + jnp.dot(xc, wg[rows], preferred_element_type=f32)
            u = u + jnp.dot(xc, wu[rows], preferred_element_type=f32)
        y = jnp.dot((_silu(g) * u).astype(bf16), wd, preferred_element_type=f32)
        for s, w in enumerate(_pack_rows(y)):
            ybuf[b % MOE_OUT_BUFS, pl.ds(s, MOE_ROWS, stride=ROW_TILE), :] = w
        out_copy(b).start()
        return 0

    lax.fori_loop(0, nb_ref[e], block, 0)

    @pl.when(e == pl.num_programs(0) - 1)
    def _():
        for j in range(MOE_OUT_BUFS, 0, -1):
            @pl.when(nu >= j)
            def _():
                out_copy(nu - j).wait()


def _moe(layer, blk0, nblk, n_used, xs, wg, wu, wd):
    wspec = lambda a: pl.BlockSpec((1, 1) + a.shape[2:], lambda e, b0, nb, nu: (layer, e, 0, 0))
    gs = pltpu.PrefetchScalarGridSpec(
        num_scalar_prefetch=3, grid=(N_EXPERTS,),
        in_specs=[pl.BlockSpec(memory_space=pl.ANY), wspec(wg), wspec(wu), wspec(wd)],
        out_specs=pl.BlockSpec(memory_space=pl.ANY),
        scratch_shapes=[pltpu.VMEM((MOE_IN_BUFS, MOE_BLK, LANES), u32), pltpu.VMEM((MOE_OUT_BUFS, MOE_BLK, LANES), u32),
                        pltpu.SemaphoreType.DMA((MOE_IN_BUFS,)), pltpu.SemaphoreType.DMA((MOE_OUT_BUFS,))])
    return pl.pallas_call(
        _moe_kernel, grid_spec=gs, out_shape=jax.ShapeDtypeStruct(xs.shape, u32),
        input_output_aliases={3: 0},
        compiler_params=_cparams("arbitrary"), name="moe_experts")(blk0, nblk, n_used, xs, wg, wu, wd)


FIN_TM = 256


def _tile_gather(pos_hbm, ys_hbm, pos_s, gbuf, isem, rsem, step, n_steps):
    def idx_copy(s):
        return pltpu.make_async_copy(pos_hbm.at[pl.ds(s * FIN_TM, FIN_TM)], pos_s.at[s % 2], isem.at[s % 2])

    def tile(i):
        return pl.ds(pl.multiple_of(i * ROW_TILE, ROW_TILE), ROW_TILE)

    def row_copy(slot, t, k, p):
        return pltpu.make_async_copy(ys_hbm.at[tile(p)], gbuf.at[slot, k, tile(t)], rsem.at[slot])

    def start_rows(s):
        slot = s % 2

        def body(t, _):
            for k in range(TOP_K):
                row_copy(slot, t, k, pos_s[slot, t, k]).start(priority=k % 2)
            return 0
        lax.fori_loop(0, FIN_TM, body, 0)

    @pl.when(step == 0)
    def _():
        idx_copy(step).start()
        idx_copy(step).wait()
        start_rows(step)

        @pl.when(n_steps > 1)
        def _():
            idx_copy(step + 1).start()

    @pl.when(step + 1 < n_steps)
    def _():
        idx_copy(step + 1).wait()
        start_rows(step + 1)

        @pl.when(step + 2 < n_steps)
        def _():
            idx_copy(step + 2).start()

    slot = step % 2

    def wait_body(t, _):
        for k in range(TOP_K):
            row_copy(slot, t, k, 0).wait()
        return 0
    lax.fori_loop(0, FIN_TM, wait_body, 0)


def _fin_kernel(pos_hbm, ys_hbm, h2_ref, x1_ref, wt_ref, mod_ref, sg_ref, su_ref, sd_ref, lng_ref, lnb_ref, o_ref,
                pos_s, gbuf, isem, rsem, *, alpha):
    i = pl.program_id(0)
    _tile_gather(pos_hbm, ys_hbm, pos_s, gbuf, isem, rsem, i, pl.num_programs(0))
    hb = h2_ref[...]
    g = jnp.dot(hb, sg_ref[...], preferred_element_type=f32)
    u = jnp.dot(hb, su_ref[...], preferred_element_type=f32)
    shared = jnp.dot((_silu(g) * u).astype(bf16), sd_ref[...], preferred_element_type=f32)
    wt = wt_ref[...]
    lane = lax.broadcasted_iota(i32, wt.shape, 1)
    wk = [jnp.sum(jnp.where(lane == k, wt, 0.0), axis=1, keepdims=True) for k in range(TOP_K)]
    slot = i % 2
    cols = []
    for s in range(ROW_TILE):
        sub = pl.ds(s, FIN_TM, stride=ROW_TILE)
        lo, hi = _unpack_word(gbuf[slot, 0, sub, :])
        acc_lo, acc_hi = lo * wk[0], hi * wk[0]
        for k in range(1, TOP_K):
            lo, hi = _unpack_word(gbuf[slot, k, sub, :])
            acc_lo, acc_hi = acc_lo + lo * wk[k], acc_hi + hi * wk[k]
        cols += [acc_lo, acc_hi]
    y = shared + jnp.concatenate(cols, axis=1)
    o_ref[...] = _layer_norm(alpha * x1_ref[...] + mod_ref[0, 5:6, :] * y, lng_ref[...], lnb_ref[...])


def _fin(pos, ys, h2b, x1, wts, mod, sg, su, sd, lng, lnb, alpha, B, T):
    N = B * T
    nt = T // FIN_TM
    tok = lambda w_: pl.BlockSpec((FIN_TM, w_), lambda i: (i, 0))
    full = lambda a: pl.BlockSpec(a.shape, lambda i: (0,) * a.ndim)
    return pl.pallas_call(
        functools.partial(_fin_kernel, alpha=alpha), grid=(N // FIN_TM,),
        in_specs=[pl.BlockSpec(memory_space=pl.ANY), pl.BlockSpec(memory_space=pl.ANY),
                  tok(D_MODEL), tok(D_MODEL), tok(LANES),
                  pl.BlockSpec((1, 6, D_MODEL), lambda i: (i // nt, 0, 0)),
                  full(sg), full(su), full(sd), full(lng), full(lnb)],
        out_specs=tok(D_MODEL), out_shape=jax.ShapeDtypeStruct((N, D_MODEL), f32),
        scratch_shapes=[pltpu.SMEM((2, FIN_TM, LANES), i32), pltpu.VMEM((2, TOP_K, FIN_TM * ROW_TILE, LANES), u32),
                        pltpu.SemaphoreType.DMA((2,)), pltpu.SemaphoreType.DMA((2,))],
        compiler_params=_cparams("arbitrary"), name="combine")(pos, ys, h2b, x1, wts, mod, sg, su, sd, lng, lnb)


def _pack_w_in(w):
    c = lambda name, width: w[:, _OFF[name]:_OFF[name] + width]
    cols = [c("moba_q", 256), c("moba_k", 256), c("moba_v", 256), c("nsa_q", 256),
            c("k_slc", 64), c("k_win", 64), c("v_slc", 64), c("v_win", 64), c("k_cmp", 64), c("v_cmp", 64),
            c("pool", 256), c("gla_q", 256), c("gla_k", 256), c("gla_v", 256), c("gla_g", 256),
            c("nsa_gate", 12), c("gla_a", 16), jnp.zeros((w.shape[0], LANES - 28), w.dtype)]
    return jnp.concatenate(cols, axis=1).astype(bf16)


def _rope_tables(T):
    half = ROPE_DIM // 2
    inv_freq = ROPE_THETA ** (-jnp.arange(half, dtype=f32) / half)
    ang = jnp.arange(T).astype(f32)[:, None] * inv_freq[None, :]
    cos, sin = jnp.cos(ang), jnp.sin(ang)
    one = jnp.ones((T, HEAD_DIM - ROPE_DIM), f32)
    zero = jnp.zeros((T, HEAD_DIM - ROPE_DIM), f32)
    ct = jnp.concatenate([cos, cos, one], axis=1)
    st = jnp.concatenate([-sin, sin, zero], axis=1)
    return jnp.tile(ct, (1, 2)), jnp.tile(st, (1, 2))


def _cmp_weights(pe, w1, w2):
    half = CMP_BLOCK // 2
    z = jnp.zeros((half, HEAD_DIM, CMP_HIDDEN), f32)

    def arrange(lo):
        wk = w1[0].reshape(CMP_BLOCK, HEAD_DIM, CMP_HIDDEN)[lo:lo + half]
        wv = w1[1].reshape(CMP_BLOCK, HEAD_DIM, CMP_HIDDEN)[lo:lo + half]
        top = jnp.concatenate([wk, z], axis=2)
        bot = jnp.concatenate([z, wv], axis=2)
        return jnp.concatenate([top, bot], axis=1).reshape(half * 2 * HEAD_DIM, 2 * CMP_HIDDEN).astype(bf16)

    def pe_row(lo):
        return jnp.concatenate([pe[0, lo:lo + half], pe[1, lo:lo + half]], axis=1).reshape(1, half * 2 * HEAD_DIM)

    zc = jnp.zeros((CMP_HIDDEN, LANES - HEAD_DIM), f32)
    zr = jnp.zeros((CMP_HIDDEN, LANES), f32)
    w2k = jnp.concatenate([jnp.concatenate([w2[0], zc], axis=1), zr], axis=0).astype(bf16)
    w2v = jnp.concatenate([zr, jnp.concatenate([w2[1], zc], axis=1)], axis=0).astype(bf16)
    return arrange(0), arrange(half), pe_row(0), pe_row(half), w2k, w2v


def _overlap_matrix(ns):
    n = jnp.arange(ns)[:, None] * CMP_STRIDE
    j = jnp.arange(LANES)[None, :] * SLC_BLOCK
    ov = (n < j + SLC_BLOCK) & (n + CMP_BLOCK > j) & (jnp.arange(ns)[:, None] < ns - 1)
    return ov.astype(bf16)


def _block_diag_ones():
    h = jnp.arange(256) // HEAD_DIM
    return (h[:, None] == h[None, :]).astype(bf16)


def _expert_layout(counts, n_tok):
    counts = counts.reshape(N_EXPERTS)
    slack = DISP_CHUNK - 1
    padded = jnp.where(counts > 0, (counts + slack + MOE_ROWS - 1) // MOE_ROWS * MOE_ROWS, 0)
    pstarts = (jnp.cumsum(padded) - padded).astype(i32)
    n_blocks = n_tok * TOP_K // MOE_ROWS + N_EXPERTS + (N_EXPERTS * slack) // MOE_ROWS + 1
    n_used = (jnp.sum(padded) // MOE_ROWS).astype(i32).reshape(1)
    return (pstarts.reshape(N_EXPERTS, 1), pstarts // MOE_ROWS, (padded // MOE_ROWS).astype(i32), n_used,
            n_blocks * MOE_ROWS)


def _mixer_inputs(x2, mod, w_in, B, T):
    ct, st = _rope_tables(T)
    return _inproj(x2, mod, _pack_w_in(w_in), ct, st, B, T)


def _token_mixers(x2, mod, w_in, cmp_pe, cmp_w1, cmp_w2, pool_w, pool_scale, gla_wa, gla_ba, gla_norm, B, T):
    N = B * T
    (mq, mk, mv, km, nq, nqr, ks, kw, vs, vw, kvc, pool_u, gq, gk, gv, gg, misc) = _mixer_inputs(x2, mod, w_in, B, T)
    nt = T // MOBA_BLOCK
    kmh = km.reshape(B, nt, N_HEADS, HEAD_DIM).transpose(0, 2, 1, 3)
    mo = _moba(mq, mk, mv, jnp.pad(kmh, ((0, 0), (0, 0), (0, MOBA_NBLK - nt), (0, LANES - HEAD_DIM))), B, T)
    ns = T // CMP_STRIDE
    kc, vc = _cmp(kvc.reshape(B, ns, CMP_STRIDE * LANES), *_cmp_weights(cmp_pe, cmp_w1, cmp_w2))
    no = _nsa(nq, nqr, misc, ks, vs, kw, vw, kc, vc, _overlap_matrix(ns), B, T)
    wbd = jax.scipy.linalg.block_diag(*[pool_w[g] for g in range(len(POOL_WINDOWS))]).astype(bf16)
    po = _pool(pool_u, wbd, pool_scale.reshape(1, 256), B, T)
    wa = jnp.zeros((LANES, 256), f32).at[MISC_A0:MISC_A0 + GLA_LOWRANK].set(gla_wa)
    go = _gla(gq, gk, gv, gg, misc, wa, gla_ba.reshape(1, 256), jnp.tile(gla_norm, N_HEADS).reshape(1, 256),
              _block_diag_ones(), B, T)
    return mo, no, po, go


def _pad_w_out(w_out):
    wm = w_out[:GROUP_WIDTH].reshape(N_HEADS, HEAD_DIM, D_MODEL)
    wm = jnp.pad(wm, ((0, 0), (0, LANES - HEAD_DIM), (0, 0))).reshape(N_HEADS * LANES, D_MODEL)
    return jnp.concatenate([wm, w_out[GROUP_WIDTH:]], axis=0).astype(bf16)


def kernel(x, c, w_ada, b_ada, w_in, cmp_pe, cmp_w1, cmp_w2, pool_w, pool_scale, gla_wa, gla_ba, gla_norm, w_out,
           ln_g, ln_b, w_router, router_bias, exp_gate, exp_up, exp_down, sh_gate, sh_up, sh_down):
    B, T, D = x.shape
    N = B * T
    depth = w_ada.shape[0]
    alpha = float((2 * depth) ** 0.25)
    x2 = x.reshape(N, D)
    c8 = jnp.zeros((8, D), f32).at[:B].set(c)
    for l in range(depth):
        mod = _ada(c8, w_ada[l], b_ada[l].reshape(1, -1))[:B].reshape(B, 6, D)
        mo, no, po, go = _token_mixers(x2, mod, w_in[l], cmp_pe[l], cmp_w1[l], cmp_w2[l], pool_w[l], pool_scale[l],
                                       gla_wa[l], gla_ba[l], gla_norm[l], B, T)
        wrh, wrl = _split_bf16(w_router[l])
        x1, h2b, logits = _outproj(mo, no, po, go, x2, mod, _pad_w_out(w_out[l]), ln_g[l, 0].reshape(1, D),
                                   ln_b[l, 0].reshape(1, D), wrh, wrl, alpha, B, T)
        eidx, wts, counts, tile_base = _route(logits, router_bias[l].reshape(1, N_EXPERTS))
        pstarts, blk0, nblk, n_used, n_rows = _expert_layout(counts, N)
        pos, xs = _dispatch(h2b, eidx, tile_base, pstarts, n_rows)
        ys = _moe(l, blk0, nblk, n_used, xs, exp_gate, exp_up, exp_down)
        x2 = _fin(pos, ys, h2b, x1, wts, mod, sh_gate[l].astype(bf16), sh_up[l].astype(bf16),
                  sh_down[l].astype(bf16), ln_g[l, 1].reshape(1, D), ln_b[l, 1].reshape(1, D), alpha, B, T)
    return x2.reshape(B, T, D)
```

```python
import functools

import jax
import jax.numpy as jnp
from jax import lax
from jax.experimental import pallas as pl
from jax.experimental.pallas import tpu as pltpu

f32, bf16, i32, u32 = jnp.float32, jnp.bfloat16, jnp.int32, jnp.uint32

D_MODEL = 1024
HEAD_DIM = 64
N_HEADS = 4
GROUP_WIDTH = 256
ROPE_THETA = 500000.0
ROPE_DIM = 16
MOBA_BLOCK = 256
MOBA_TOPK = 3
CMP_BLOCK = 32
CMP_STRIDE = 16
CMP_HIDDEN = 128
SLC_BLOCK = 64
SLC_TOPN = 16
WIN = 512
FORCE_SCORE = 1e9
POOL_WINDOWS = (2, 4, 8, 16)
GLA_SUB = 16
GLA_LOWRANK = 16
GLA_TAU = 16.0
N_EXPERTS = 256
TOP_K = 8
N_EXPERT_GROUPS = 8
TOPK_GROUPS = 4
D_EXPERT = 256
ROUTED_SCALE = 2.5
LN_EPS = 1e-5
QK_SCALE = HEAD_DIM ** -0.5
LOG2E = 1.4426950408889634
Q_SCALE = QK_SCALE * LOG2E

LANES = 128
MASKED = -1e30
M_INIT = -3e38
BIG_IDX = 1 << 20
MOE_ROWS = 256
ROW_TILE = D_MODEL // (2 * LANES)
VMEM_LIMIT = 56 * 1024 * 1024

_OFF = dict(moba_q=0, moba_k=256, moba_v=512, nsa_q=768, k_cmp=1024, v_cmp=1088, k_slc=1152, v_slc=1216,
            k_win=1280, v_win=1344, nsa_gate=1408, pool=1420, gla_q=1676, gla_k=1932, gla_v=2188,
            gla_a=2444, gla_g=2460)
_S = dict(mq=0, mk=256, mv=512, nq=768, sw=1024, vsw=1152, kvc=1280, pool=1408, gq=1664, gk=1920, gv=2176,
          gg=2432, misc=2688)
IN_COLS_PACKED = 2816
MISC_GATE0 = 0
MISC_A0 = 12

NT = (((1,), (1,)), ((), ()))
TN = (((0,), (0,)), ((), ()))


def _cparams(*sem):
    return pltpu.CompilerParams(dimension_semantics=sem, vmem_limit_bytes=VMEM_LIMIT)


def _silu(x):
    return x * jax.nn.sigmoid(x)


def _split_bf16(x):
    hi = x.astype(bf16)
    lo = (x - hi.astype(f32)).astype(bf16)
    return hi, lo


def _pack_rows(x):
    bits = lambda v: lax.bitcast_convert_type(v.astype(bf16).astype(f32), u32)
    return [(bits(x[:, (2 * s) * LANES:(2 * s + 1) * LANES]) >> 16) | bits(x[:, (2 * s + 1) * LANES:(2 * s + 2) * LANES])
            for s in range(ROW_TILE)]


def _unpack_word(w):
    return (lax.bitcast_convert_type(w << 16, f32), lax.bitcast_convert_type(w & jnp.uint32(0xFFFF0000), f32))


def _layer_norm(z, g, b):
    mu = jnp.mean(z, axis=-1, keepdims=True)
    zc = z - mu
    var = jnp.mean(zc * zc, axis=-1, keepdims=True)
    return zc * lax.rsqrt(var + LN_EPS) * g + b


def _argmax_rounds(score, index, rounds, axis=1):
    picked = jnp.zeros(score.shape, f32)
    for _ in range(rounds):
        mx = jnp.max(score, axis=axis, keepdims=True)
        first = jnp.min(jnp.where(score == mx, index, BIG_IDX), axis=axis, keepdims=True)
        hit = index == first
        picked = jnp.where(hit, 1.0, picked)
        score = jnp.where(hit, -jnp.inf, score)
    return picked


def _softmax_steps_t(s_list, carries, v_list):
    m_new = [jnp.maximum(c[0], jnp.max(s, axis=0, keepdims=True)) for s, c in zip(s_list, carries)]
    p = [jnp.exp2(s - m) for s, m in zip(s_list, m_new)]
    pv = [jnp.dot(v, pi.astype(bf16), preferred_element_type=f32) for v, pi in zip(v_list, p)]
    out = []
    for c, m, pi, pvi in zip(carries, m_new, p, pv):
        alpha = jnp.exp2(c[0] - m)
        out.append((m, alpha * c[1] + jnp.sum(pi, axis=0, keepdims=True), alpha * c[2] + pvi))
    return tuple(out)


def _softmax_init_t(groups, queries):
    return tuple((jnp.full((1, queries), M_INIT, f32), jnp.zeros((1, queries), f32),
                  jnp.zeros((LANES, queries), f32)) for _ in range(groups))


def _ada_kernel(c_ref, w_ref, b_ref, o_ref):
    o_ref[...] = jnp.dot(_silu(c_ref[...]), w_ref[...], preferred_element_type=f32,
                         precision=lax.Precision.HIGHEST) + b_ref[...]


def _ada(c8, w, b):
    n = w.shape[1] // D_MODEL
    return pl.pallas_call(
        _ada_kernel, grid=(n,),
        in_specs=[pl.BlockSpec((8, D_MODEL), lambda j: (0, 0)),
                  pl.BlockSpec((D_MODEL, D_MODEL), lambda j: (0, j)),
                  pl.BlockSpec((1, D_MODEL), lambda j: (0, j))],
        out_specs=pl.BlockSpec((8, D_MODEL), lambda j: (0, j)),
        out_shape=jax.ShapeDtypeStruct((8, w.shape[1]), f32),
        compiler_params=_cparams("arbitrary"), name="ada")(c8, w, b)


IN_TM = MOBA_BLOCK


def _inproj_kernel(x_ref, mod_ref, w_ref, ct_ref, st_ref,
                   mq_ref, mk_ref, mv_ref, km_ref, nq_ref, nqr_ref, ks_ref, kw_ref, vs_ref, vw_ref,
                   kvc_ref, pool_ref, gq_ref, gk_ref, gv_ref, gg_ref, misc_ref, *, nt):
    tb = pl.program_id(0) % nt
    h = (x_ref[...] * (1.0 + mod_ref[0, 1:2, :]) + mod_ref[0, 0:1, :]).astype(bf16)

    def seg(name, width):
        a = _S[name]
        return jnp.dot(h, w_ref[:, a:a + width], preferred_element_type=f32)

    ct, st = ct_ref[...], st_ref[...]
    lane = lax.broadcasted_iota(i32, (IN_TM, LANES), 1)
    first8 = (lane % HEAD_DIM) < ROPE_DIM // 2
    half = lane < HEAD_DIM

    def rope128(y):
        partner = jnp.where(first8, pltpu.roll(y, LANES - 8, axis=1), pltpu.roll(y, 8, axis=1))
        return y * ct + partner * st

    def rope(y):
        return jnp.concatenate([rope128(y[:, c * LANES:(c + 1) * LANES]) for c in range(y.shape[1] // LANES)],
                               axis=1)

    def lo_half(y):
        return jnp.where(half, y, 0.0)

    def hi_half(y):
        return jnp.where(half, pltpu.roll(y, HEAD_DIM, axis=1), 0.0)

    def per_head(y):
        parts = []
        for c in range(2):
            yc = y[:, c * LANES:(c + 1) * LANES]
            parts += [lo_half(yc), hi_half(yc)]
        return jnp.concatenate(parts, axis=1)

    mq_ref[...] = per_head(rope(seg("mq", 256)) * Q_SCALE).astype(bf16)
    k = rope(seg("mk", 256))
    km_ref[0] = jnp.mean(k, axis=0, keepdims=True)
    lane4 = lax.broadcasted_iota(i32, (IN_TM, 4 * LANES), 1)
    mk_ref[...] = jnp.where((lane4 % LANES) == HEAD_DIM + tb, 1.0, per_head(k)).astype(bf16)
    mv_ref[...] = per_head(seg("mv", 256)).astype(bf16)
    q = seg("nq", 256) * Q_SCALE
    nq_ref[...] = per_head(q).astype(bf16)
    nqr_ref[...] = per_head(rope(q)).astype(bf16)
    sw = rope128(seg("sw", 128))
    row = lax.broadcasted_iota(i32, (IN_TM, LANES), 0)
    slc_id = tb * (IN_TM // SLC_BLOCK) + row // SLC_BLOCK
    ks_ref[...] = jnp.concatenate([lo_half(sw), jnp.where(lane == slc_id, 1.0, 0.0)], axis=1).astype(bf16)
    kw_ref[...] = hi_half(sw).astype(bf16)
    vsw = seg("vsw", 128)
    vs_ref[...] = lo_half(vsw).astype(bf16)
    vw_ref[...] = hi_half(vsw).astype(bf16)
    kvc_ref[...] = seg("kvc", 128).astype(bf16)
    pool_ref[...] = seg("pool", 256)
    gq_ref[...] = seg("gq", 256)
    gk_ref[...] = seg("gk", 256)
    gv_ref[...] = seg("gv", 256)
    gg_ref[...] = seg("gg", 256)
    misc_ref[...] = seg("misc", 128)


def _inproj(x2, mod, w, ct, st, B, T):
    N = B * T
    nt = T // IN_TM
    assert T % IN_TM == 0 and nt <= 32 and T // SLC_BLOCK <= LANES
    row = lambda w_, dt: (jax.ShapeDtypeStruct((N, w_), dt), pl.BlockSpec((IN_TM, w_), lambda i: (i, 0)))
    outs = [row(512, bf16), row(512, bf16), row(512, bf16),
            (jax.ShapeDtypeStruct((N // IN_TM, 1, 256), f32), pl.BlockSpec((1, 1, 256), lambda i: (i, 0, 0))),
            row(512, bf16), row(512, bf16), row(256, bf16), row(128, bf16), row(128, bf16), row(128, bf16),
            row(128, bf16), row(256, f32), row(256, f32), row(256, f32), row(256, f32), row(256, f32),
            row(128, f32)]
    return pl.pallas_call(
        functools.partial(_inproj_kernel, nt=nt), grid=(N // IN_TM,),
        in_specs=[pl.BlockSpec((IN_TM, D_MODEL), lambda i: (i, 0)),
                  pl.BlockSpec((1, 6, D_MODEL), lambda i: (i // nt, 0, 0)),
                  pl.BlockSpec((D_MODEL, IN_COLS_PACKED), lambda i: (0, 0), pipeline_mode=pl.Buffered(1)),
                  pl.BlockSpec((IN_TM, LANES), lambda i: (i % nt, 0)),
                  pl.BlockSpec((IN_TM, LANES), lambda i: (i % nt, 0))],
        out_specs=[o[1] for o in outs], out_shape=[o[0] for o in outs],
        compiler_params=_cparams("arbitrary"), name="inproj")(x2, mod, w, ct, st)


def _cmp_kernel(x_ref, wa_ref, wb_ref, pea_ref, peb_ref, w2k_ref, w2v_ref, kc_ref, vc_ref):
    x = x_ref[0].astype(f32)
    a = jnp.dot((x + pea_ref[...]).astype(bf16), wa_ref[...], preferred_element_type=f32)
    b = jnp.dot((x + peb_ref[...]).astype(bf16), wb_ref[...], preferred_element_type=f32)
    hid = a + pltpu.roll(b, x.shape[0] - 1, axis=0)
    g = jax.nn.gelu(hid).astype(bf16)
    kc_ref[0] = jnp.dot(g, w2k_ref[...], preferred_element_type=f32).astype(bf16)
    vc_ref[0] = jnp.dot(g, w2v_ref[...], preferred_element_type=f32).astype(bf16)


def _cmp(xseg, wa, wb, pea, peb, w2k, w2v):
    B, ns, wd = xseg.shape
    full = lambda a: pl.BlockSpec(a.shape, lambda b: (0,) * a.ndim)
    return pl.pallas_call(
        _cmp_kernel, grid=(B,),
        in_specs=[pl.BlockSpec((1, ns, wd), lambda b: (b, 0, 0))] + [full(a) for a in (wa, wb, pea, peb, w2k, w2v)],
        out_specs=[pl.BlockSpec((1, ns, LANES), lambda b: (b, 0, 0))] * 2,
        out_shape=[jax.ShapeDtypeStruct((B, ns, LANES), bf16)] * 2,
        compiler_params=_cparams("arbitrary"), name="nsa_compress")(xseg, wa, wb, pea, peb, w2k, w2v)


MOBA_NBLK = 32
KEY_TILE = 512
MOBA_TQ = 512


def _moba_kernel(q_ref, k_ref, vt_ref, km_ref, o_ref):
    q0 = pl.program_id(1) * MOBA_TQ
    heads = range(N_HEADS)
    hl = lambda h: slice(h * LANES, (h + 1) * LANES)
    blk = lax.broadcasted_iota(i32, (MOBA_NBLK, MOBA_TQ), 0)
    own = (q0 + lax.broadcasted_iota(i32, (1, MOBA_TQ), 1)) // MOBA_BLOCK
    past = blk < own
    zeros = lambda n: jnp.zeros((n, MOBA_TQ), f32)
    qf = []
    for h in heads:
        qh = q_ref[:, hl(h)]
        hi, lo = _split_bf16(km_ref[0, h])
        gate_t = (lax.dot_general(hi, qh, NT, preferred_element_type=f32)
                  + lax.dot_general(lo, qh, NT, preferred_element_type=f32))
        picked = _argmax_rounds(jnp.where(past, gate_t, -jnp.inf), blk, MOBA_TOPK, axis=0)
        allowed = ((picked > 0.0) & past) | (blk == own)
        bias_t = jnp.concatenate([zeros(HEAD_DIM), jnp.where(allowed, 0.0, MASKED),
                                  zeros(LANES - HEAD_DIM - MOBA_NBLK)], axis=0)
        qf.append(qh + jnp.transpose(bias_t).astype(bf16))

    def scores_t(h, off):
        return lax.dot_general(k_ref[pl.ds(off, KEY_TILE), hl(h)], qf[h], NT, preferred_element_type=f32)

    def body(p, carry):
        off = pl.multiple_of(p * KEY_TILE, KEY_TILE)
        return _softmax_steps_t([scores_t(h, off) for h in heads], carry, [vt_ref[0, p, hl(h), :] for h in heads])

    last = q0 // KEY_TILE
    carry = lax.fori_loop(0, last, body, _softmax_init_t(N_HEADS, MOBA_TQ))
    off = pl.multiple_of(last * KEY_TILE, KEY_TILE)
    kpos = off + lax.broadcasted_iota(i32, (KEY_TILE, MOBA_TQ), 0)
    qpos = q0 + lax.broadcasted_iota(i32, (KEY_TILE, MOBA_TQ), 1)
    carry = _softmax_steps_t([jnp.where(kpos <= qpos, scores_t(h, off), MASKED) for h in heads], carry,
                             [vt_ref[0, last, hl(h), :] for h in heads])
    for h in heads:
        o_ref[:, hl(h)] = jnp.transpose(carry[h][2] / carry[h][1]).astype(bf16)


def _blocks_t(v, B, T, blk):
    return v.reshape(B, T // blk, blk, v.shape[1]).transpose(0, 1, 3, 2)


def _moba(mq, mk, mv, km, B, T):
    N = B * T
    nt = T // MOBA_TQ
    wd = N_HEADS * LANES
    assert T % KEY_TILE == 0 and KEY_TILE % MOBA_TQ == 0 and T // MOBA_BLOCK <= MOBA_NBLK
    return pl.pallas_call(
        _moba_kernel, grid=(B, nt),
        in_specs=[pl.BlockSpec((MOBA_TQ, wd), lambda b, i: (b * nt + i, 0)),
                  pl.BlockSpec((T, wd), lambda b, i: (b, 0), pipeline_mode=pl.Buffered(1)),
                  pl.BlockSpec((1, T // KEY_TILE, wd, KEY_TILE), lambda b, i: (b, 0, 0, 0),
                               pipeline_mode=pl.Buffered(1)),
                  pl.BlockSpec((1, N_HEADS, MOBA_NBLK, LANES), lambda b, i: (b, 0, 0, 0))],
        out_specs=pl.BlockSpec((MOBA_TQ, wd), lambda b, i: (b * nt + i, 0)),
        out_shape=jax.ShapeDtypeStruct((N, wd), bf16),
        compiler_params=_cparams("arbitrary", "arbitrary"), name="moba")(mq, mk, _blocks_t(mv, B, T, KEY_TILE), km)


NSA_TQ = 512
NSA_KB = 256


def _stack_heads(ref):
    return jnp.concatenate([ref[:, h * LANES:(h + 1) * LANES] for h in range(N_HEADS)], axis=0)


def _nsa_select_kernel(nq_ref, kc_ref, vct_ref, ovt_ref, oc_ref, selb_ref, *, n_cmp):
    c = pl.program_id(1)
    rows = N_HEADS * NSA_TQ
    s = lax.dot_general(kc_ref[0], _stack_heads(nq_ref), NT, preferred_element_type=f32)
    n = lax.broadcasted_iota(i32, s.shape, 0)
    qpos_t = c * NSA_TQ + lax.broadcasted_iota(i32, (1, rows), 1) % NSA_TQ
    ok = (n * CMP_STRIDE + (CMP_BLOCK - 1) <= qpos_t) & (n < n_cmp)
    s = jnp.where(ok, s, -jnp.inf)
    m = jnp.max(s, axis=0, keepdims=True)
    m = jnp.where(m > -jnp.inf, m, 0.0)
    e = jnp.where(ok, jnp.exp2(s - m), 0.0)
    p_c = e / jnp.maximum(jnp.sum(e, axis=0, keepdims=True), 1e-30)
    o_c_t = jnp.dot(vct_ref[0], p_c.astype(bf16), preferred_element_type=f32)
    hq = lambda h: slice(h * NSA_TQ, (h + 1) * NSA_TQ)
    for h in range(N_HEADS):
        oc_ref[:, h * LANES:(h + 1) * LANES] = jnp.transpose(o_c_t[:, hq(h)]).astype(bf16)
    hi, lo = _split_bf16((p_c[:, hq(0)] + p_c[:, hq(1)]) + (p_c[:, hq(2)] + p_c[:, hq(3)]))
    imp = (jnp.dot(ovt_ref[...], hi, preferred_element_type=f32)
           + jnp.dot(ovt_ref[...], lo, preferred_element_type=f32))
    j = lax.broadcasted_iota(i32, imp.shape, 0)
    cur = (c * NSA_TQ + lax.broadcasted_iota(i32, (1, NSA_TQ), 1)) // SLC_BLOCK
    forced = (j == 0) | (j == cur) | (j == cur - 1)
    valid = j <= cur
    score = jnp.where(valid, jnp.where(forced, FORCE_SCORE, imp), -jnp.inf)
    chosen = (_argmax_rounds(score, j, SLC_TOPN, axis=0) > 0.0) & valid
    selb_ref[...] = jnp.transpose(jnp.where(chosen, 0.0, MASKED)).astype(bf16)


def _nsa_attend_kernel(nqr_ref, selb_ref, oc_ref, misc_ref, ks_ref, vst_ref, kw_ref, vwt_ref, o_ref):
    c = pl.program_id(1)
    rows = N_HEADS * NSA_TQ
    heads = range(N_HEADS)
    q4r = _stack_heads(nqr_ref)
    qpos_t = c * NSA_TQ + lax.broadcasted_iota(i32, (1, rows), 1) % NSA_TQ
    selb = selb_ref[...]
    lhs = [jnp.concatenate([q4r[h * NSA_TQ:(h + 1) * NSA_TQ], selb], axis=1) for h in heads]

    def scores_t(off):
        kb = ks_ref[pl.ds(off, KEY_TILE), :]
        return [lax.dot_general(kb, lhs[h], NT, preferred_element_type=f32) for h in heads]

    def body(p, carry):
        return _softmax_steps_t(scores_t(pl.multiple_of(p * KEY_TILE, KEY_TILE)), carry, [vst_ref[0, p]] * N_HEADS)

    last = (c * NSA_TQ) // KEY_TILE
    carry = lax.fori_loop(0, last, body, _softmax_init_t(N_HEADS, NSA_TQ))
    off = pl.multiple_of(last * KEY_TILE, KEY_TILE)
    kpos = off + lax.broadcasted_iota(i32, (KEY_TILE, NSA_TQ), 0)
    qpos = c * NSA_TQ + lax.broadcasted_iota(i32, (KEY_TILE, NSA_TQ), 1)
    carry = _softmax_steps_t([jnp.where(kpos <= qpos, s, MASKED) for s in scores_t(off)], carry,
                             [vst_ref[0, last]] * N_HEADS)
    o_s = jnp.concatenate([jnp.transpose(carry[h][2] / carry[h][1]) for h in heads], axis=0)

    nwb = (WIN + NSA_TQ) // NSA_KB
    sb = jnp.maximum(c * (NSA_TQ // NSA_KB) - WIN // NSA_KB, 0)
    start = pl.multiple_of(sb * NSA_KB, NSA_KB)
    s_w = lax.dot_general(kw_ref[pl.ds(start, nwb * NSA_KB), :], q4r, NT, preferred_element_type=f32)
    wpos = start + lax.broadcasted_iota(i32, s_w.shape, 0)
    s_w = jnp.where((wpos <= qpos_t) & (wpos > qpos_t - WIN), s_w, -jnp.inf)
    e_w = jnp.exp2(s_w - jnp.max(s_w, axis=0, keepdims=True))
    p_w = (e_w / jnp.sum(e_w, axis=0, keepdims=True)).astype(bf16)
    o_w_t = jnp.dot(vwt_ref[0, sb], p_w[0:NSA_KB], preferred_element_type=f32)
    for i in range(1, nwb):
        o_w_t = o_w_t + jnp.dot(vwt_ref[0, sb + i], p_w[i * NSA_KB:(i + 1) * NSA_KB], preferred_element_type=f32)
    o_w = jnp.transpose(o_w_t)

    gates = jax.nn.sigmoid(misc_ref[...])
    gl = lax.broadcasted_iota(i32, gates.shape, 1)

    def gate_col(g):
        return jnp.concatenate([jnp.sum(jnp.where(gl == MISC_GATE0 + 3 * h + g, gates, 0.0), axis=1, keepdims=True)
                                for h in range(N_HEADS)], axis=0)

    out = gate_col(0) * _stack_heads(oc_ref).astype(f32) + gate_col(1) * o_s + gate_col(2) * o_w
    pair = lambda a, b: a + pltpu.roll(b, HEAD_DIM, axis=1)
    o_ref[...] = jnp.concatenate([pair(out[0:NSA_TQ], out[NSA_TQ:2 * NSA_TQ]),
                                  pair(out[2 * NSA_TQ:3 * NSA_TQ], out[3 * NSA_TQ:])], axis=1).astype(bf16)


def _nsa(nq, nqr, misc, ks, vs, kw, vw, kc, vc, ov, B, T):
    N = B * T
    nc = T // NSA_TQ
    ns = kc.shape[1]
    assert T >= NSA_TQ + WIN and T % KEY_TILE == 0 and KEY_TILE % NSA_TQ == 0 and NSA_TQ % NSA_KB == 0
    assert WIN % NSA_KB == 0
    tok = lambda w_: pl.BlockSpec((NSA_TQ, w_), lambda b, c: (b * nc + c, 0))
    seq = lambda w_: pl.BlockSpec((T, w_), lambda b, c: (b, 0), pipeline_mode=pl.Buffered(1))
    o_c, selb = pl.pallas_call(
        functools.partial(_nsa_select_kernel, n_cmp=(T - CMP_BLOCK) // CMP_STRIDE + 1), grid=(B, nc),
        in_specs=[tok(512), pl.BlockSpec((1, ns, LANES), lambda b, c: (b, 0, 0)),
                  pl.BlockSpec((1, LANES, ns), lambda b, c: (b, 0, 0)), pl.BlockSpec((LANES, ns), lambda b, c: (0, 0))],
        out_specs=[tok(512), tok(LANES)],
        out_shape=[jax.ShapeDtypeStruct((N, 512), bf16), jax.ShapeDtypeStruct((N, LANES), bf16)],
        compiler_params=_cparams("arbitrary", "arbitrary"), name="nsa_select")(
            nq, kc, vc.transpose(0, 2, 1), ov.T)
    seq_t = lambda kb: pl.BlockSpec((1, T // kb, LANES, kb), lambda b, c: (b, 0, 0, 0), pipeline_mode=pl.Buffered(1))
    return pl.pallas_call(
        _nsa_attend_kernel, grid=(B, nc),
        in_specs=[tok(512), tok(LANES), tok(512), tok(LANES), seq(256), seq_t(KEY_TILE), seq(LANES), seq_t(NSA_KB)],
        out_specs=tok(256), out_shape=jax.ShapeDtypeStruct((N, 256), bf16),
        compiler_params=_cparams("arbitrary", "arbitrary"), name="nsa_attend")(
            nqr, selb, o_c, misc, ks, _blocks_t(vs, B, T, KEY_TILE), kw, _blocks_t(vw, B, T, NSA_KB))


POOL_TM = 512
POOL_HALO = 16


def _pool_kernel(u_ref, halo_ref, w_ref, sc_ref, o_ref):
    t = pl.program_id(1)
    halo = jnp.where(t == 0, 0.0, halo_ref[...])
    ext = jnp.concatenate([halo, u_ref[...]], axis=0)
    s2 = ext + pltpu.roll(ext, 1, axis=0)
    s4 = s2 + pltpu.roll(s2, 2, axis=0)
    s8 = s4 + pltpu.roll(s4, 4, axis=0)
    s16 = s8 + pltpu.roll(s8, 8, axis=0)
    pos1 = jnp.maximum(t * POOL_TM - POOL_HALO + 1 + lax.broadcasted_iota(i32, ext.shape, 0), 1).astype(f32)
    grp = lax.broadcasted_iota(i32, ext.shape, 1) // HEAD_DIM
    mean = jnp.where(grp == 0, s2 / jnp.minimum(pos1, 2.0),
                     jnp.where(grp == 1, s4 / jnp.minimum(pos1, 4.0),
                               jnp.where(grp == 2, s8 / jnp.minimum(pos1, 8.0), s16 / jnp.minimum(pos1, 16.0))))
    pooled = (mean - ext)[POOL_HALO:, :]
    o_ref[...] = (jnp.dot(pooled.astype(bf16), w_ref[...], preferred_element_type=f32) * sc_ref[...]).astype(bf16)


def _pool(u, wbd, scale, B, T):
    N = B * T
    tm = min(POOL_TM, T)
    assert tm == POOL_TM and T % POOL_TM == 0
    nt = T // tm
    hb = tm // POOL_HALO
    return pl.pallas_call(
        _pool_kernel, grid=(B, nt),
        in_specs=[pl.BlockSpec((tm, 256), lambda b, t: (b * nt + t, 0)),
                  pl.BlockSpec((POOL_HALO, 256), lambda b, t: (jnp.maximum((b * nt + t) * hb - 1, 0), 0)),
                  pl.BlockSpec((256, 256), lambda b, t: (0, 0)),
                  pl.BlockSpec((1, 256), lambda b, t: (0, 0))],
        out_specs=pl.BlockSpec((tm, 256), lambda b, t: (b * nt + t, 0)),
        out_shape=jax.ShapeDtypeStruct((N, 256), bf16),
        compiler_params=_cparams("arbitrary", "arbitrary"), name="pool")(u, u, wbd, scale)


GLA_TM = 256


def _gla_kernel(q_ref, k_ref, v_ref, g_ref, misc_ref, wa_ref, ba_ref, gn_ref, bd_ref, o_ref,
                st_ref, q_s, k_s, v_s, b_s, qe_s, ke_s, gam_s, o_s):
    nb = q_ref.shape[0]

    @pl.when(pl.program_id(0) == 0)
    def _():
        st_ref[...] = jnp.zeros_like(st_ref)

    r16 = lax.broadcasted_iota(i32, (GLA_TM, 256), 0) % GLA_SUB
    for bi_ in range(nb):
        x = jnp.dot(misc_ref[bi_], wa_ref[...], preferred_element_type=f32,
                    precision=lax.Precision.HIGHEST) + ba_ref[...]
        log_a = (jnp.minimum(x, 0.0) - jnp.log1p(jnp.exp(-jnp.abs(x)))) / GLA_TAU
        b = log_a
        for s in (1, 2, 4, 8):
            b = b + jnp.where(r16 >= s, pltpu.roll(b, s, axis=0), 0.0)
        b_end = jnp.where(r16 == GLA_SUB - 1, b, 0.0)
        for s in (1, 2, 4, 8):
            b_end = b_end + pltpu.roll(b_end, GLA_TM - s, axis=0)
        q = q_ref[bi_] * QK_SCALE
        k = k_ref[bi_]
        q_s[bi_] = q
        k_s[bi_] = k
        v_s[bi_] = v_ref[bi_]
        b_s[bi_] = b
        qe_s[bi_] = (q * jnp.exp(b)).astype(bf16)
        ke_s[bi_] = (k * jnp.exp(b_end - b)).astype(bf16)
        gam_s[bi_] = jnp.exp(b_end)
    bd = bd_ref[...]
    shape3 = (GLA_SUB, GLA_SUB, 256)
    causal = lax.broadcasted_iota(i32, shape3, 0) <= lax.broadcasted_iota(i32, shape3, 1)

    def block(n, _):
        r0 = pl.multiple_of(n * GLA_SUB, GLA_SUB)
        rows = pl.ds(r0, GLA_SUB)
        for bi_ in range(nb):
            qi, ki, vi, bi = q_s[bi_, rows, :], k_s[bi_, rows, :], v_s[bi_, rows, :], b_s[bi_, rows, :]
            diff = jnp.where(causal, bi[None, :, :] - bi[:, None, :], 0.0)
            w3 = jnp.where(causal, qi[None, :, :] * ki[:, None, :] * jnp.exp(diff), 0.0)
            a3 = jnp.dot(w3.reshape(GLA_SUB * GLA_SUB, 256).astype(bf16), bd, preferred_element_type=f32)
            intra = jnp.sum(a3.reshape(shape3) * vi[:, None, :], axis=0)
            st = st_ref[bi_]
            inter = lax.dot_general(qe_s[bi_, rows, :], st.astype(bf16), NT, preferred_element_type=f32)
            o_s[bi_, rows, :] = intra + inter
            upd = lax.dot_general(vi.astype(bf16), ke_s[bi_, rows, :], TN, preferred_element_type=f32)
            st_ref[bi_] = st * gam_s[bi_, pl.ds(r0, 1), :] + jnp.where(bd > 0, upd, 0.0)
        return 0

    lax.fori_loop(0, GLA_TM // GLA_SUB, block, 0)
    for bi_ in range(nb):
        o = o_s[bi_]
        ms = jnp.dot(o * o, bd.astype(f32), preferred_element_type=f32, precision=lax.Precision.HIGHEST) / HEAD_DIM
        o_ref[bi_] = (o * lax.rsqrt(ms + LN_EPS) * gn_ref[...] * _silu(g_ref[bi_])).astype(bf16)


def _gla(gq, gk, gv, gg, misc, wa, ba, gn, bd, B, T):
    N = B * T
    assert T % GLA_TM == 0
    tok = lambda w_: pl.BlockSpec((B, GLA_TM, w_), lambda t: (0, t, 0))
    full = lambda a: pl.BlockSpec(a.shape, lambda t: (0, 0))
    v = lambda dt: pltpu.VMEM((B, GLA_TM, 256), dt)
    seq = lambda a: a.reshape(B, T, a.shape[1])
    out = pl.pallas_call(
        _gla_kernel, grid=(T // GLA_TM,),
        in_specs=[tok(256), tok(256), tok(256), tok(256), tok(128), full(wa), full(ba), full(gn), full(bd)],
        out_specs=tok(256), out_shape=jax.ShapeDtypeStruct((B, T, 256), bf16),
        scratch_shapes=[pltpu.VMEM((B, 256, 256), f32), v(f32), v(f32), v(f32), v(f32), v(bf16), v(bf16), v(f32),
                        v(f32)],
        compiler_params=_cparams("arbitrary"), name="gla")(seq(gq), seq(gk), seq(gv), seq(gg), seq(misc), wa, ba, gn, bd)
    return out.reshape(N, 256)


OUT_TM = 256


def _outproj_kernel(mo_ref, no_ref, po_ref, go_ref, x_ref, mod_ref, w_ref, lng_ref, lnb_ref, wrh_ref, wrl_ref,
                    x1_ref, h2b_ref, lg_ref, *, alpha):
    a = jnp.concatenate([mo_ref[...], no_ref[...], po_ref[...], go_ref[...]], axis=1)
    y = jnp.dot(a, w_ref[...], preferred_element_type=f32)
    x1 = _layer_norm(alpha * x_ref[...] + mod_ref[0, 2:3, :] * y, lng_ref[...], lnb_ref[...])
    x1_ref[...] = x1
    h2 = x1 * (1.0 + mod_ref[0, 4:5, :]) + mod_ref[0, 3:4, :]
    h2b_ref[...] = h2.astype(bf16)
    hi, lo = _split_bf16(h2)
    wrh = wrh_ref[...]
    lg_ref[...] = (jnp.dot(hi, wrh, preferred_element_type=f32) + jnp.dot(lo, wrh, preferred_element_type=f32)
                   + jnp.dot(hi, wrl_ref[...], preferred_element_type=f32))


def _outproj(mo, no, po, go, x2, mod, w, lng, lnb, wrh, wrl, alpha, B, T):
    N = B * T
    nt = T // OUT_TM
    tok = lambda w_: pl.BlockSpec((OUT_TM, w_), lambda i: (i, 0))
    full = lambda a: pl.BlockSpec(a.shape, lambda i: (0,) * a.ndim)
    return pl.pallas_call(
        functools.partial(_outproj_kernel, alpha=alpha), grid=(N // OUT_TM,),
        in_specs=[tok(512), tok(256), tok(256), tok(256), tok(D_MODEL),
                  pl.BlockSpec((1, 6, D_MODEL), lambda i: (i // nt, 0, 0)),
                  full(w), full(lng), full(lnb), full(wrh), full(wrl)],
        out_specs=[tok(D_MODEL), tok(D_MODEL), tok(N_EXPERTS)],
        out_shape=[jax.ShapeDtypeStruct((N, D_MODEL), f32), jax.ShapeDtypeStruct((N, D_MODEL), bf16),
                   jax.ShapeDtypeStruct((N, N_EXPERTS), f32)],
        compiler_params=_cparams("arbitrary"), name="outproj")(mo, no, po, go, x2, mod, w, lng, lnb, wrh, wrl)


ROUTE_TM = 256
GROUP_SIZE = N_EXPERTS // N_EXPERT_GROUPS


def _per_token(rows):
    pad = jnp.zeros((LANES - len(rows), rows[0].shape[1]), f32)
    return jnp.transpose(jnp.concatenate(rows + [pad], axis=0))


def _route_kernel(lg_ref, rb_ref, ei_ref, wt_ref, cnt_ref, tb_ref, base_ref):
    @pl.when(pl.program_id(0) == 0)
    def _():
        base_ref[...] = jnp.zeros_like(base_ref)

    tb_ref[0] = base_ref[...].astype(i32)

    s = jax.nn.sigmoid(jnp.transpose(lg_ref[...]))
    ssel = s + rb_ref[...]
    shape3 = (N_EXPERT_GROUPS, GROUP_SIZE, ROUTE_TM)
    x3 = ssel.reshape(shape3)
    i3 = lax.broadcasted_iota(i32, shape3, 1)
    m1 = jnp.max(x3, axis=1, keepdims=True)
    first = jnp.min(jnp.where(x3 == m1, i3, BIG_IDX), axis=1, keepdims=True)
    m2 = jnp.max(jnp.where(i3 == first, -jnp.inf, x3), axis=1, keepdims=True)
    gscore = (m1 + m2).reshape(N_EXPERT_GROUPS, ROUTE_TM)
    gid = lax.broadcasted_iota(i32, gscore.shape, 0)
    beaten = jnp.zeros(gscore.shape, i32)
    for g in range(N_EXPERT_GROUPS):
        other = gscore[g:g + 1, :]
        beaten = beaten + jnp.where((other > gscore) | ((other == gscore) & (g < gid)), 1, 0)
    keep = jnp.broadcast_to((beaten < TOPK_GROUPS)[:, None, :], shape3).reshape(ssel.shape)
    x = jnp.where(keep, ssel, -jnp.inf)
    eid = lax.broadcasted_iota(i32, x.shape, 0)
    hits, idx_rows, w_rows = [], [], []
    for _ in range(TOP_K):
        mx = jnp.max(x, axis=0, keepdims=True)
        idx = jnp.min(jnp.where(x == mx, eid, BIG_IDX), axis=0, keepdims=True)
        hit = eid == idx
        hits.append(hit)
        idx_rows.append(idx.astype(f32))
        w_rows.append(jnp.sum(jnp.where(hit, s, 0.0), axis=0, keepdims=True))
        x = jnp.where(hit, -jnp.inf, x)
    wsum = w_rows[0]
    for w in w_rows[1:]:
        wsum = wsum + w
    chosen = jnp.zeros(s.shape, f32)
    for hit in hits:
        chosen = jnp.where(hit, 1.0, chosen)
    total = base_ref[...] + jnp.sum(chosen, axis=1, keepdims=True)
    base_ref[...] = total
    cnt_ref[...] = total.astype(i32)
    ei_ref[...] = _per_token(idx_rows).astype(i32)
    wt_ref[...] = _per_token([w / wsum * ROUTED_SCALE for w in w_rows])


def _route(logits, rb):
    N = logits.shape[0]
    tok = pl.BlockSpec((ROUTE_TM, LANES), lambda i: (i, 0))
    col = pl.BlockSpec((N_EXPERTS, 1), lambda i: (0, 0))
    return pl.pallas_call(
        _route_kernel, grid=(N // ROUTE_TM,),
        in_specs=[pl.BlockSpec((ROUTE_TM, N_EXPERTS), lambda i: (i, 0)), col],
        out_specs=[tok, tok, col, pl.BlockSpec((1, N_EXPERTS, 1), lambda i: (i, 0, 0))],
        out_shape=[jax.ShapeDtypeStruct((N, LANES), i32), jax.ShapeDtypeStruct((N, LANES), f32),
                   jax.ShapeDtypeStruct((N_EXPERTS, 1), i32), jax.ShapeDtypeStruct((N // ROUTE_TM, N_EXPERTS, 1), i32)],
        scratch_shapes=[pltpu.VMEM((N_EXPERTS, 1), f32)],
        compiler_params=_cparams("arbitrary"), name="route")(logits, rb.reshape(N_EXPERTS, 1))


DISP_TM = ROUTE_TM
DISP_SLOTS = DISP_TM * TOP_K
DISP_CHUNK = 16
DISP_PERM_ROWS = 512
DISP_MAX_CHUNKS = DISP_SLOTS // DISP_CHUNK + N_EXPERTS


def _dispatch_kernel(x_ref, ei_ref, tb_ref, ps_ref, triu_ref, tril_ref, xs_in, pos_ref, xs_hbm,
                     pbuf, tab_v, tab_s, csem, rsem):
    del xs_in
    i = pl.program_id(0)
    slot = i % 2

    @pl.when(i == 0)
    def _():
        pbuf[:, DISP_SLOTS * ROW_TILE:, :] = jnp.zeros((2, DISP_CHUNK * ROW_TILE, LANES), u32)

    ei_t = jnp.transpose(ei_ref[...].astype(f32))
    eid = lax.broadcasted_iota(i32, (N_EXPERTS, DISP_TM), 0).astype(f32)
    hits = [eid == ei_t[k:k + 1, :] for k in range(TOP_K)]
    member = jnp.zeros(eid.shape, f32)
    for hit in hits:
        member = jnp.where(hit, 1.0, member)
    prefix = jnp.dot(member.astype(bf16), triu_ref[...], preferred_element_type=f32)
    cnt = jnp.sum(member, axis=1, keepdims=True)
    nch = jnp.floor((cnt + (DISP_CHUNK - 1)) * (1.0 / DISP_CHUNK))
    lane = lax.broadcasted_iota(i32, (N_EXPERTS, LANES), 1)
    before = jnp.dot(tril_ref[...], jnp.where(lane == 0, cnt, jnp.where(lane == 1, nch, 0.0)).astype(bf16),
                     preferred_element_type=f32)
    off, cidx = before[:, 0:1], before[:, 1:2]
    dst = (ps_ref[...] + tb_ref[0]).astype(f32)
    slot_rows = [jnp.sum(jnp.where(hit, off + prefix, 0.0), axis=0, keepdims=True) for hit in hits]
    pos_ref[...] = _per_token([jnp.sum(jnp.where(hit, dst + prefix, 0.0), axis=0, keepdims=True)
                               for hit in hits]).astype(i32)
    rel = lax.broadcasted_iota(i32, (N_EXPERTS, DISP_MAX_CHUNKS), 1).astype(f32) - cidx
    mine = (rel >= 0.0) & (rel < nch)
    src_row = jnp.sum(jnp.where(mine, off + DISP_CHUNK * rel, 0.0), axis=0, keepdims=True)
    dst_row = jnp.sum(jnp.where(mine, dst + DISP_CHUNK * rel, 0.0), axis=0, keepdims=True)
    n_row = jnp.broadcast_to(jnp.sum(nch, axis=0, keepdims=True), src_row.shape)
    tab_v[...] = jnp.concatenate([src_row, dst_row, n_row, jnp.zeros((5, DISP_MAX_CHUNKS), f32)], axis=0).astype(i32)
    cp = pltpu.make_async_copy(tab_v, tab_s.at[slot], csem)
    cp.start()

    x = x_ref[...]
    for r in range(DISP_SLOTS // DISP_PERM_ROWS):
        sid = (r * DISP_PERM_ROWS + lax.broadcasted_iota(i32, (DISP_PERM_ROWS, DISP_TM), 0)).astype(f32)
        perm = jnp.zeros(sid.shape, f32)
        for srow in slot_rows:
            perm = jnp.where(sid == srow, 1.0, perm)
        rows = jnp.dot(perm.astype(bf16), x, preferred_element_type=f32)
        for s, w in enumerate(_pack_rows(rows)):
            pbuf[slot, pl.ds(r * DISP_PERM_ROWS * ROW_TILE + s, DISP_PERM_ROWS, stride=ROW_TILE), :] = w
    cp.wait()

    def chunk_copy(sl, src, dst_):
        span = lambda r0: pl.ds(pl.multiple_of(r0 * ROW_TILE, ROW_TILE), DISP_CHUNK * ROW_TILE)
        return pltpu.make_async_copy(pbuf.at[sl, span(src)], xs_hbm.at[span(dst_)], rsem.at[sl])

    def start_all(sl):
        def body(j, _):
            chunk_copy(sl, tab_s[sl, 0, j], tab_s[sl, 1, j]).start()
            return 0
        lax.fori_loop(0, tab_s[sl, 2, 0], body, 0)

    def wait_all(sl):
        def body(j, _):
            chunk_copy(sl, 0, 0).wait()
            return 0
        lax.fori_loop(0, tab_s[sl, 2, 0], body, 0)

    @pl.when(i > 0)
    def _():
        wait_all(1 - slot)
    start_all(slot)

    @pl.when(i == pl.num_programs(0) - 1)
    def _():
        wait_all(slot)


def _dispatch(h2b, eidx, tile_base, pstarts, n_rows):
    N = h2b.shape[0]
    r_, c_ = jnp.arange(DISP_TM)[:, None], jnp.arange(DISP_TM)[None, :]
    triu = (r_ < c_).astype(bf16)
    e_, f_ = jnp.arange(N_EXPERTS)[:, None], jnp.arange(N_EXPERTS)[None, :]
    tril = (f_ < e_).astype(bf16)
    tok = pl.BlockSpec((DISP_TM, LANES), lambda i: (i, 0))
    full = lambda a: pl.BlockSpec(a.shape, lambda i: (0,) * a.ndim)
    xs0 = jnp.zeros((n_rows * ROW_TILE, LANES), u32)
    pos, xs = pl.pallas_call(
        _dispatch_kernel, grid=(N // DISP_TM,),
        in_specs=[pl.BlockSpec((DISP_TM, D_MODEL), lambda i: (i, 0)), tok,
                  pl.BlockSpec((1, N_EXPERTS, 1), lambda i: (i, 0, 0)), full(pstarts), full(triu), full(tril),
                  pl.BlockSpec(memory_space=pl.ANY)],
        out_specs=[tok, pl.BlockSpec(memory_space=pl.ANY)],
        out_shape=[jax.ShapeDtypeStruct((N, LANES), i32), jax.ShapeDtypeStruct(xs0.shape, u32)],
        scratch_shapes=[pltpu.VMEM((2, (DISP_SLOTS + DISP_CHUNK) * ROW_TILE, LANES), u32),
                        pltpu.VMEM((8, DISP_MAX_CHUNKS), i32), pltpu.SMEM((2, 8, DISP_MAX_CHUNKS), i32),
                        pltpu.SemaphoreType.DMA, pltpu.SemaphoreType.DMA((2,))],
        input_output_aliases={6: 1},
        compiler_params=_cparams("arbitrary"), name="moe_dispatch")(h2b, eidx, tile_base, pstarts, triu, tril, xs0)
    return pos, xs


MOE_BLK = MOE_ROWS * ROW_TILE
MOE_IN_BUFS = 6
MOE_OUT_BUFS = 4


def _moe_kernel(b0_ref, nb_ref, nu_ref, xs_in, wg_ref, wu_ref, wd_ref, xs_out, xbuf, ybuf, isem, osem):
    e = pl.program_id(0)
    nu = nu_ref[0]

    def in_copy(b):
        return pltpu.make_async_copy(xs_in.at[pl.ds(pl.multiple_of(b * MOE_BLK, MOE_BLK), MOE_BLK)],
                                     xbuf.at[b % MOE_IN_BUFS], isem.at[b % MOE_IN_BUFS])

    def out_copy(b):
        return pltpu.make_async_copy(ybuf.at[b % MOE_OUT_BUFS],
                                     xs_out.at[pl.ds(pl.multiple_of(b * MOE_BLK, MOE_BLK), MOE_BLK)],
                                     osem.at[b % MOE_OUT_BUFS])

    @pl.when(e == 0)
    def _():
        for j in range(MOE_IN_BUFS - 1):
            @pl.when(j < nu)
            def _():
                in_copy(j).start()

    wg = wg_ref[0, 0].astype(bf16)
    wu = wu_ref[0, 0].astype(bf16)
    wd = wd_ref[0, 0].astype(bf16)

    def block(i, _):
        b = b0_ref[e] + i
        in_copy(b).wait()

        @pl.when(b + MOE_IN_BUFS - 1 < nu)
        def _():
            in_copy(b + MOE_IN_BUFS - 1).start()

        @pl.when(b >= MOE_OUT_BUFS)
        def _():
            out_copy(b - MOE_OUT_BUFS).wait()

        g = jnp.zeros((MOE_ROWS, D_EXPERT), f32)
        u = jnp.zeros((MOE_ROWS, D_EXPERT), f32)
        for s in range(ROW_TILE):
            word = xbuf[b % MOE_IN_BUFS, pl.ds(s, MOE_ROWS, stride=ROW_TILE), :]
            xc = jnp.concatenate(_unpack_word(word), axis=1).astype(bf16)
            rows = slice(2 * s * LANES, (2 * s + 2) * LANES)
            g = g + jnp.dot(xc, wg[rows], preferred_element_type=f32)
            u = u + jnp.dot(xc, wu[rows], preferred_element_type=f32)
        y = jnp.dot((_silu(g) * u).astype(bf16), wd, preferred_element_type=f32)
        for s, w in enumerate(_pack_rows(y)):
            ybuf[b % MOE_OUT_BUFS, pl.ds(s, MOE_ROWS, stride=ROW_TILE), :] = w
        out_copy(b).start()
        return 0

    lax.fori_loop(0, nb_ref[e], block, 0)

    @pl.when(e == pl.num_programs(0) - 1)
    def _():
        for j in range(MOE_OUT_BUFS, 0, -1):
            @pl.when(nu >= j)
            def _():
                out_copy(nu - j).wait()


def _moe(layer, blk0, nblk, n_used, xs, wg, wu, wd):
    wspec = lambda a: pl.BlockSpec((1, 1) + a.shape[2:], lambda e, b0, nb, nu: (layer, e, 0, 0))
    gs = pltpu.PrefetchScalarGridSpec(
        num_scalar_prefetch=3, grid=(N_EXPERTS,),
        in_specs=[pl.BlockSpec(memory_space=pl.ANY), wspec(wg), wspec(wu), wspec(wd)],
        out_specs=pl.BlockSpec(memory_space=pl.ANY),
        scratch_shapes=[pltpu.VMEM((MOE_IN_BUFS, MOE_BLK, LANES), u32), pltpu.VMEM((MOE_OUT_BUFS, MOE_BLK, LANES), u32),
                        pltpu.SemaphoreType.DMA((MOE_IN_BUFS,)), pltpu.SemaphoreType.DMA((MOE_OUT_BUFS,))])
    return pl.pallas_call(
        _moe_kernel, grid_spec=gs, out_shape=jax.ShapeDtypeStruct(xs.shape, u32),
        input_output_aliases={3: 0},
        compiler_params=_cparams("arbitrary"), name="moe_experts")(blk0, nblk, n_used, xs, wg, wu, wd)


FIN_TM = 256


def _tile_gather(pos_hbm, ys_hbm, pos_s, gbuf, isem, rsem, step, n_steps, independent_work):
    def idx_copy(s):
        return pltpu.make_async_copy(pos_hbm.at[pl.ds(s * FIN_TM, FIN_TM)], pos_s.at[s % 2], isem.at[s % 2])

    def tile(i):
        return pl.ds(pl.multiple_of(i * ROW_TILE, ROW_TILE), ROW_TILE)

    def row_copy(slot, t, k, p):
        return pltpu.make_async_copy(ys_hbm.at[tile(p)], gbuf.at[slot, k, tile(t)], rsem.at[slot])

    def start_rows(s):
        slot = s % 2

        def body(t, _):
            for k in range(TOP_K):
                row_copy(slot, t, k, pos_s[slot, t, k]).start(priority=k % 2)
            return 0
        lax.fori_loop(0, FIN_TM, body, 0)

    @pl.when(step == 0)
    def _():
        idx_copy(step).start()
        idx_copy(step).wait()
        start_rows(step)

        @pl.when(n_steps > 1)
        def _():
            idx_copy(step + 1).start()

    @pl.when(step + 1 < n_steps)
    def _():
        idx_copy(step + 1).wait()
        start_rows(step + 1)

        @pl.when(step + 2 < n_steps)
        def _():
            idx_copy(step + 2).start()

    slot = step % 2
    result = independent_work()

    def wait_body(t, _):
        for k in range(TOP_K):
            row_copy(slot, t, k, 0).wait()
        return 0
    lax.fori_loop(0, FIN_TM, wait_body, 0)
    return result


def _fin_kernel(pos_hbm, ys_hbm, h2_ref, x1_ref, wt_ref, mod_ref, sg_ref, su_ref, sd_ref, lng_ref, lnb_ref, o_ref,
                pos_s, gbuf, isem, rsem, *, alpha):
    i = pl.program_id(0)

    def shared_expert():
        hb = h2_ref[...]
        g = jnp.dot(hb, sg_ref[...], preferred_element_type=f32)
        u = jnp.dot(hb, su_ref[...], preferred_element_type=f32)
        return jnp.dot((_silu(g) * u).astype(bf16), sd_ref[...], preferred_element_type=f32)

    shared = _tile_gather(pos_hbm, ys_hbm, pos_s, gbuf, isem, rsem, i, pl.num_programs(0), shared_expert)
    wt = wt_ref[...]
    lane = lax.broadcasted_iota(i32, wt.shape, 1)
    wk = [jnp.sum(jnp.where(lane == k, wt, 0.0), axis=1, keepdims=True) for k in range(TOP_K)]
    slot = i % 2
    cols = []
    for s in range(ROW_TILE):
        sub = pl.ds(s, FIN_TM, stride=ROW_TILE)
        lo, hi = _unpack_word(gbuf[slot, 0, sub, :])
        acc_lo, acc_hi = lo * wk[0], hi * wk[0]
        for k in range(1, TOP_K):
            lo, hi = _unpack_word(gbuf[slot, k, sub, :])
            acc_lo, acc_hi = acc_lo + lo * wk[k], acc_hi + hi * wk[k]
        cols += [acc_lo, acc_hi]
    y = shared + jnp.concatenate(cols, axis=1)
    o_ref[...] = _layer_norm(alpha * x1_ref[...] + mod_ref[0, 5:6, :] * y, lng_ref[...], lnb_ref[...])


def _fin(pos, ys, h2b, x1, wts, mod, sg, su, sd, lng, lnb, alpha, B, T):
    N = B * T
    nt = T // FIN_TM
    tok = lambda w_: pl.BlockSpec((FIN_TM, w_), lambda i: (i, 0))
    full = lambda a: pl.BlockSpec(a.shape, lambda i: (0,) * a.ndim)
    return pl.pallas_call(
        functools.partial(_fin_kernel, alpha=alpha), grid=(N // FIN_TM,),
        in_specs=[pl.BlockSpec(memory_space=pl.ANY), pl.BlockSpec(memory_space=pl.ANY),
                  tok(D_MODEL), tok(D_MODEL), tok(LANES),
                  pl.BlockSpec((1, 6, D_MODEL), lambda i: (i // nt, 0, 0)),
                  full(sg), full(su), full(sd), full(lng), full(lnb)],
        out_specs=tok(D_MODEL), out_shape=jax.ShapeDtypeStruct((N, D_MODEL), f32),
        scratch_shapes=[pltpu.SMEM((2, FIN_TM, LANES), i32), pltpu.VMEM((2, TOP_K, FIN_TM * ROW_TILE, LANES), u32),
                        pltpu.SemaphoreType.DMA((2,)), pltpu.SemaphoreType.DMA((2,))],
        compiler_params=_cparams("arbitrary"), name="combine")(pos, ys, h2b, x1, wts, mod, sg, su, sd, lng, lnb)


def _pack_w_in(w):
    c = lambda name, width: w[:, _OFF[name]:_OFF[name] + width]
    cols = [c("moba_q", 256), c("moba_k", 256), c("moba_v", 256), c("nsa_q", 256),
            c("k_slc", 64), c("k_win", 64), c("v_slc", 64), c("v_win", 64), c("k_cmp", 64), c("v_cmp", 64),
            c("pool", 256), c("gla_q", 256), c("gla_k", 256), c("gla_v", 256), c("gla_g", 256),
            c("nsa_gate", 12), c("gla_a", 16), jnp.zeros((w.shape[0], LANES - 28), w.dtype)]
    return jnp.concatenate(cols, axis=1).astype(bf16)


def _rope_tables(T):
    half = ROPE_DIM // 2
    inv_freq = ROPE_THETA ** (-jnp.arange(half, dtype=f32) / half)
    ang = jnp.arange(T).astype(f32)[:, None] * inv_freq[None, :]
    cos, sin = jnp.cos(ang), jnp.sin(ang)
    one = jnp.ones((T, HEAD_DIM - ROPE_DIM), f32)
    zero = jnp.zeros((T, HEAD_DIM - ROPE_DIM), f32)
    ct = jnp.concatenate([cos, cos, one], axis=1)
    st = jnp.concatenate([-sin, sin, zero], axis=1)
    return jnp.tile(ct, (1, 2)), jnp.tile(st, (1, 2))


def _cmp_weights(pe, w1, w2):
    half = CMP_BLOCK // 2
    z = jnp.zeros((half, HEAD_DIM, CMP_HIDDEN), f32)

    def arrange(lo):
        wk = w1[0].reshape(CMP_BLOCK, HEAD_DIM, CMP_HIDDEN)[lo:lo + half]
        wv = w1[1].reshape(CMP_BLOCK, HEAD_DIM, CMP_HIDDEN)[lo:lo + half]
        top = jnp.concatenate([wk, z], axis=2)
        bot = jnp.concatenate([z, wv], axis=2)
        return jnp.concatenate([top, bot], axis=1).reshape(half * 2 * HEAD_DIM, 2 * CMP_HIDDEN).astype(bf16)

    def pe_row(lo):
        return jnp.concatenate([pe[0, lo:lo + half], pe[1, lo:lo + half]], axis=1).reshape(1, half * 2 * HEAD_DIM)

    zc = jnp.zeros((CMP_HIDDEN, LANES - HEAD_DIM), f32)
    zr = jnp.zeros((CMP_HIDDEN, LANES), f32)
    w2k = jnp.concatenate([jnp.concatenate([w2[0], zc], axis=1), zr], axis=0).astype(bf16)
    w2v = jnp.concatenate([zr, jnp.concatenate([w2[1], zc], axis=1)], axis=0).astype(bf16)
    return arrange(0), arrange(half), pe_row(0), pe_row(half), w2k, w2v


def _overlap_matrix(ns):
    n = jnp.arange(ns)[:, None] * CMP_STRIDE
    j = jnp.arange(LANES)[None, :] * SLC_BLOCK
    ov = (n < j + SLC_BLOCK) & (n + CMP_BLOCK > j) & (jnp.arange(ns)[:, None] < ns - 1)
    return ov.astype(bf16)


def _block_diag_ones():
    h = jnp.arange(256) // HEAD_DIM
    return (h[:, None] == h[None, :]).astype(bf16)


def _expert_layout(counts, n_tok):
    counts = counts.reshape(N_EXPERTS)
    slack = DISP_CHUNK - 1
    padded = jnp.where(counts > 0, (counts + slack + MOE_ROWS - 1) // MOE_ROWS * MOE_ROWS, 0)
    pstarts = (jnp.cumsum(padded) - padded).astype(i32)
    n_blocks = n_tok * TOP_K // MOE_ROWS + N_EXPERTS + (N_EXPERTS * slack) // MOE_ROWS + 1
    n_used = (jnp.sum(padded) // MOE_ROWS).astype(i32).reshape(1)
    return (pstarts.reshape(N_EXPERTS, 1), pstarts // MOE_ROWS, (padded // MOE_ROWS).astype(i32), n_used,
            n_blocks * MOE_ROWS)


def _mixer_inputs(x2, mod, w_in, B, T):
    ct, st = _rope_tables(T)
    return _inproj(x2, mod, _pack_w_in(w_in), ct, st, B, T)


def _token_mixers(x2, mod, w_in, cmp_pe, cmp_w1, cmp_w2, pool_w, pool_scale, gla_wa, gla_ba, gla_norm, B, T):
    N = B * T
    (mq, mk, mv, km, nq, nqr, ks, kw, vs, vw, kvc, pool_u, gq, gk, gv, gg, misc) = _mixer_inputs(x2, mod, w_in, B, T)
    nt = T // MOBA_BLOCK
    kmh = km.reshape(B, nt, N_HEADS, HEAD_DIM).transpose(0, 2, 1, 3)
    mo = _moba(mq, mk, mv, jnp.pad(kmh, ((0, 0), (0, 0), (0, MOBA_NBLK - nt), (0, LANES - HEAD_DIM))), B, T)
    ns = T // CMP_STRIDE
    kc, vc = _cmp(kvc.reshape(B, ns, CMP_STRIDE * LANES), *_cmp_weights(cmp_pe, cmp_w1, cmp_w2))
    no = _nsa(nq, nqr, misc, ks, vs, kw, vw, kc, vc, _overlap_matrix(ns), B, T)
    wbd = jax.scipy.linalg.block_diag(*[pool_w[g] for g in range(len(POOL_WINDOWS))]).astype(bf16)
    po = _pool(pool_u, wbd, pool_scale.reshape(1, 256), B, T)
    wa = jnp.zeros((LANES, 256), f32).at[MISC_A0:MISC_A0 + GLA_LOWRANK].set(gla_wa)
    go = _gla(gq, gk, gv, gg, misc, wa, gla_ba.reshape(1, 256), jnp.tile(gla_norm, N_HEADS).reshape(1, 256),
              _block_diag_ones(), B, T)
    return mo, no, po, go


def _pad_w_out(w_out):
    wm = w_out[:GROUP_WIDTH].reshape(N_HEADS, HEAD_DIM, D_MODEL)
    wm = jnp.pad(wm, ((0, 0), (0, LANES - HEAD_DIM), (0, 0))).reshape(N_HEADS * LANES, D_MODEL)
    return jnp.concatenate([wm, w_out[GROUP_WIDTH:]], axis=0).astype(bf16)


def kernel(x, c, w_ada, b_ada, w_in, cmp_pe, cmp_w1, cmp_w2, pool_w, pool_scale, gla_wa, gla_ba, gla_norm, w_out,
           ln_g, ln_b, w_router, router_bias, exp_gate, exp_up, exp_down, sh_gate, sh_up, sh_down):
    B, T, D = x.shape
    N = B * T
    depth = w_ada.shape[0]
    alpha = float((2 * depth) ** 0.25)
    x2 = x.reshape(N, D)
    c8 = jnp.zeros((8, D), f32).at[:B].set(c)
    for l in range(depth):
        mod = _ada(c8, w_ada[l], b_ada[l].reshape(1, -1))[:B].reshape(B, 6, D)
        mo, no, po, go = _token_mixers(x2, mod, w_in[l], cmp_pe[l], cmp_w1[l], cmp_w2[l], pool_w[l], pool_scale[l],
                                       gla_wa[l], gla_ba[l], gla_norm[l], B, T)
        wrh, wrl = _split_bf16(w_router[l])
        x1, h2b, logits = _outproj(mo, no, po, go, x2, mod, _pad_w_out(w_out[l]), ln_g[l, 0].reshape(1, D),
                                   ln_b[l, 0].reshape(1, D), wrh, wrl, alpha, B, T)
        eidx, wts, counts, tile_base = _route(logits, router_bias[l].reshape(1, N_EXPERTS))
        pstarts, blk0, nblk, n_used, n_rows = _expert_layout(counts, N)
        pos, xs = _dispatch(h2b, eidx, tile_base, pstarts, n_rows)
        ys = _moe(l, blk0, nblk, n_used, xs, exp_gate, exp_up, exp_down)
        x2 = _fin(pos, ys, h2b, x1, wts, mod, sh_gate[l].astype(bf16), sh_up[l].astype(bf16),
                  sh_down[l].astype(bf16), ln_g[l, 1].reshape(1, D), ln_b[l, 1].reshape(1, D), alpha, B, T)
    return x2.reshape(B, T, D)
```
